```python
import jax, jax.numpy as jnp
from jax import lax
import numpy as np

D_MODEL = 1024
BATCH = 16
SEQ = 256
DEPTH = 4
DEC_BATCH = 2
DEC_SEQ = 2048
PAST_LEN = 256

GRID_W = 64
HEAD_DIM = 64
NA_HEADS = 4
NA_WIN_H = 8
NA_WIN_W = 16
MLA_HEADS = 6
MLA_Q_RANK = 256
MLA_KV_RANK = 128
MLA_NOPE_DIM = 64
MLA_ROPE_DIM = 32
MLA_V_DIM = 64
MLA_QK_DIM = MLA_NOPE_DIM + MLA_ROPE_DIM
SWA_HEADS = 6
SWA_KV_HEADS = 2
SWA_WINDOW = 128
ROPE_BASE = 10000.0
Q_BLOCK = 128
N_EXPERTS = 16
N_EXPERT_GROUPS = 4
EXPERTS_PER_GROUP = N_EXPERTS // N_EXPERT_GROUPS
TOP_K = 2
D_EXPERT = 512
RMS_EPS = 1e-6
N_MOD = 6

NA_IN = 3 * NA_HEADS * HEAD_DIM
MLA_IN = MLA_Q_RANK + MLA_KV_RANK + MLA_ROPE_DIM
SWA_IN = (SWA_HEADS + 2 * SWA_KV_HEADS) * HEAD_DIM
IN_WIDTH = NA_IN + MLA_IN + SWA_IN
MIX_WIDTH = NA_HEADS * HEAD_DIM + MLA_HEADS * MLA_V_DIM + SWA_HEADS * HEAD_DIM

kernel_name = "hybrid_prefix_flow_trunk_step"


def rms_norm(x, g):
    xf = x.astype(jnp.float32)
    y = xf * lax.rsqrt(jnp.mean(xf * xf, axis=-1, keepdims=True) + RMS_EPS)
    return (y * g.astype(jnp.float32)).astype(x.dtype)


def ada_mod(cond, w_ada_l, b_ada_l):
    m = jax.nn.silu(cond) @ w_ada_l + b_ada_l
    m = m.reshape(cond.shape[0], 1, N_MOD, D_MODEL)
    return [m[:, :, i] for i in range(N_MOD)]


def axial_rope(T, rot_dim):
    t = jnp.arange(T, dtype=jnp.int32)
    row = (t // GRID_W).astype(jnp.float32)
    col = (t % GRID_W).astype(jnp.float32)
    per_axis = rot_dim // 2
    inv = ROPE_BASE ** (-jnp.arange(0, per_axis, 2, dtype=jnp.float32) / per_axis)
    ang = jnp.concatenate([row[:, None] * inv, col[:, None] * inv], axis=-1)
    return jnp.cos(ang), jnp.sin(ang)


def apply_rope(x, cos, sin):
    half = x.shape[-1] // 2
    x1, x2 = x[..., :half], x[..., half:]
    cs = cos[None, :, None, :].astype(x.dtype)
    sn = sin[None, :, None, :].astype(x.dtype)
    return jnp.concatenate([x1 * cs - x2 * sn, x2 * cs + x1 * sn], axis=-1)


def block_attention(q, k, v, sink=None):
    B, Tq, H, dk = q.shape
    Hk = k.shape[2]
    G = H // Hk
    dv = v.shape[-1]
    nb = Tq // Q_BLOCK
    qb = q.reshape(B, nb, Q_BLOCK, Hk, G, dk).transpose(1, 0, 2, 3, 4, 5)
    scale = dk ** -0.5

    def one_block(qblk):
        s = jnp.einsum('bqkgd,bskd->bkgqs', qblk, k, preferred_element_type=jnp.float32) * scale
        if sink is not None:
            s_sink = jnp.broadcast_to(sink.astype(jnp.float32).reshape(1, Hk, G, 1, 1), s.shape[:-1] + (1,))
            s = jnp.concatenate([s, s_sink], axis=-1)
        p = jax.nn.softmax(s, axis=-1)
        if sink is not None:
            p = p[..., :-1]
        return jnp.einsum('bkgqs,bskd->bqkgd', p.astype(v.dtype), v)

    o = lax.map(one_block, qb)
    return o.transpose(1, 0, 2, 3, 4, 5).reshape(B, Tq, H, dv)


def neighborhood_attention(q, k, v, ck, cv, rpb):
    B, T, H, D = q.shape
    rows = T // GRID_W
    kh = min(NA_WIN_H, rows)
    W = GRID_W
    kw = NA_WIN_W
    qg = q.reshape(B, rows, W, H, D)
    kg = k.reshape(B, rows, W, H, D)
    vg = v.reshape(B, rows, W, H, D)
    r = jnp.arange(rows)
    row_start = jnp.clip(r - kh // 2, 0, rows - kh)
    row_idx = row_start[:, None] + jnp.arange(kh)[None, :]
    k_nb = kg[:, row_idx]
    v_nb = vg[:, row_idx]
    cidx = jnp.arange(W)
    col_start = jnp.clip(cidx - kw // 2, 0, W - kw)
    col_ok = (cidx[None, :] >= col_start[:, None]) & (cidx[None, :] < col_start[:, None] + kw)
    dr = row_idx - r[:, None] + (NA_WIN_H - 1)
    dc = jnp.clip(cidx[None, :] - cidx[:, None] + (kw - 1), 0, 2 * kw - 2)
    bias = rpb[:, dr[:, :, None, None], dc[None, None]]
    bias = bias.transpose(0, 1, 3, 2, 4).astype(jnp.float32)
    scale = D ** -0.5
    s_nb = jnp.einsum('brqhd,brkwhd->bhrqkw', qg, k_nb, preferred_element_type=jnp.float32) * scale + bias[None]
    s_nb = jnp.where(col_ok[None, None, None, :, None, :], s_nb, -jnp.inf)
    s_nb = s_nb.reshape(B, H, rows, W, kh * W)
    s_ctx = jnp.einsum('brqhd,blhd->bhrql', qg, ck, preferred_element_type=jnp.float32) * scale
    p = jax.nn.softmax(jnp.concatenate([s_nb, s_ctx], axis=-1), axis=-1)
    p_nb = p[..., :kh * W].reshape(B, H, rows, W, kh, W).astype(v.dtype)
    p_ctx = p[..., kh * W:].astype(cv.dtype)
    o = jnp.einsum('bhrqkw,brkwhd->brqhd', p_nb, v_nb) + jnp.einsum('bhrql,blhd->brqhd', p_ctx, cv)
    return o.reshape(B, T, H, D)


def banded_window_attention(q, k, v, ck, cv, sink):
    B, T, H, D = q.shape
    Hk = k.shape[2]
    G = H // Hk
    Bk = SWA_WINDOW
    nb = T // Bk
    L = ck.shape[1]
    qb = q.reshape(B, nb, Bk, Hk, G, D)
    pad = ((0, 0), (Bk, Bk), (0, 0), (0, 0))
    kp = jnp.pad(k, pad).reshape(B, nb + 2, Bk, Hk, D)
    vp = jnp.pad(v, pad).reshape(B, nb + 2, Bk, Hk, D)
    kb = jnp.concatenate([kp[:, :-2], kp[:, 1:-1], kp[:, 2:]], axis=2)
    vb = jnp.concatenate([vp[:, :-2], vp[:, 1:-1], vp[:, 2:]], axis=2)
    qi = jnp.arange(Bk)
    kj = jnp.arange(3 * Bk)
    rel = kj[None, :] - qi[:, None]
    band = (rel >= 0) & (rel <= 2 * Bk)
    kpos = jnp.arange(nb)[:, None] * Bk - Bk + kj[None, :]
    inside = (kpos >= 0) & (kpos < T)
    mask = band[None] & inside[:, None, :]
    scale = D ** -0.5
    s_loc = jnp.einsum('bnqkgd,bnskd->bnkgqs', qb, kb, preferred_element_type=jnp.float32) * scale
    s_loc = jnp.where(mask[None, :, None, None], s_loc, -jnp.inf)
    s_ctx = jnp.einsum('bnqkgd,blkd->bnkgql', qb, ck, preferred_element_type=jnp.float32) * scale
    s_sink = jnp.broadcast_to(sink.astype(jnp.float32).reshape(1, 1, Hk, G, 1, 1), s_loc.shape[:-1] + (1,))
    p = jax.nn.softmax(jnp.concatenate([s_loc, s_ctx, s_sink], axis=-1), axis=-1)
    p_loc = p[..., :3 * Bk].astype(v.dtype)
    p_ctx = p[..., 3 * Bk:3 * Bk + L].astype(cv.dtype)
    o = jnp.einsum('bnkgqs,bnskd->bnqkgd', p_loc, vb) + jnp.einsum('bnkgql,blkd->bnqkgd', p_ctx, cv)
    return o.reshape(B, T, H, D)


def project(h, w_in_l, g_q, w_qb, g_kv):
    B, T, _ = h.shape
    z = h @ w_in_l
    na = z[..., :NA_IN].reshape(B, T, 3, NA_HEADS, HEAD_DIM)
    o = NA_IN
    cq = z[..., o:o + MLA_Q_RANK]
    o += MLA_Q_RANK
    ckv = z[..., o:o + MLA_KV_RANK]
    o += MLA_KV_RANK
    kpe = z[..., o:o + MLA_ROPE_DIM]
    o += MLA_ROPE_DIM
    sw = z[..., o:].reshape(B, T, SWA_HEADS + 2 * SWA_KV_HEADS, HEAD_DIM)
    mla_q = (rms_norm(cq, g_q) @ w_qb).reshape(B, T, MLA_HEADS, MLA_QK_DIM)
    ckv = rms_norm(ckv, g_kv)
    return (na[:, :, 0], na[:, :, 1], na[:, :, 2], mla_q, ckv, kpe,
            sw[:, :, :SWA_HEADS], sw[:, :, SWA_HEADS:SWA_HEADS + SWA_KV_HEADS], sw[:, :, SWA_HEADS + SWA_KV_HEADS:])


def mla_expand(ckv, kpe, w_kvb):
    B, T, _ = ckv.shape
    kv = (ckv @ w_kvb).reshape(B, T, MLA_HEADS, MLA_NOPE_DIM + MLA_V_DIM)
    k_nope, v = kv[..., :MLA_NOPE_DIM], kv[..., MLA_NOPE_DIM:]
    k = jnp.concatenate([k_nope, jnp.broadcast_to(kpe[:, :, None, :], (B, T, MLA_HEADS, MLA_ROPE_DIM))], axis=-1)
    return k, v


def merge_heads(o_na, o_mla, o_swa, w_out_l):
    B, T = o_na.shape[:2]
    o = jnp.concatenate([o_na.reshape(B, T, -1), o_mla.reshape(B, T, -1), o_swa.reshape(B, T, -1)], axis=-1)
    return o @ w_out_l


def moe(h, w_router, b_router, w_gate, w_up, w_down):
    B, T, Dm = h.shape
    t = h.reshape(B * T, Dm)
    scores = jax.nn.sigmoid(jnp.einsum('nd,de->ne', t, w_router, preferred_element_type=jnp.float32))
    sel = (scores + b_router.astype(jnp.float32)).reshape(-1, N_EXPERT_GROUPS, EXPERTS_PER_GROUP)
    group_score = jnp.sum(lax.top_k(sel, TOP_K)[0], axis=-1)
    best = jnp.argmax(group_score, axis=-1)
    in_group = jnp.take_along_axis(sel, best[:, None, None], axis=1)[:, 0]
    _, local = lax.top_k(in_group, TOP_K)
    idx = best[:, None] * EXPERTS_PER_GROUP + local
    w = jnp.take_along_axis(scores, idx, axis=1)
    w = w / jnp.sum(w, axis=-1, keepdims=True)
    gate = jnp.sum(jax.nn.one_hot(idx, N_EXPERTS, dtype=jnp.float32) * w[..., None], axis=1)
    hg = jnp.einsum('nd,edf->nef', t, w_gate)
    hu = jnp.einsum('nd,edf->nef', t, w_up)
    a = jax.nn.silu(hg) * hu * gate[..., None].astype(t.dtype)
    y = jnp.einsum('nef,efd->nd', a, w_down)
    return y.reshape(B, T, Dm)


def setup_inputs(seed: int = 0) -> dict:
    key = jax.random.key(seed)
    ks = jax.random.split(key, 32)
    f32 = jnp.float32

    def nrm(k, shape, s):
        return jax.random.normal(k, shape, f32) * s

    return {
        'x_prompt': nrm(ks[0], (BATCH, SEQ, D_MODEL), 1.0),
        'x_sample': nrm(ks[1], (DEC_BATCH, DEC_SEQ, D_MODEL), 1.0),
        'c': nrm(ks[2], (DEC_BATCH, D_MODEL), 1.0),
        'cache_na_k': nrm(ks[3], (DEC_BATCH, DEPTH, PAST_LEN, NA_HEADS, HEAD_DIM), 1.0),
        'cache_na_v': nrm(ks[4], (DEC_BATCH, DEPTH, PAST_LEN, NA_HEADS, HEAD_DIM), 1.0),
        'cache_mla_ckv': nrm(ks[5], (DEC_BATCH, DEPTH, PAST_LEN, MLA_KV_RANK), 1.0),
        'cache_mla_kpe': nrm(ks[6], (DEC_BATCH, DEPTH, PAST_LEN, MLA_ROPE_DIM), 1.0),
        'cache_swa_k': nrm(ks[7], (DEC_BATCH, DEPTH, PAST_LEN, SWA_KV_HEADS, HEAD_DIM), 1.0),
        'cache_swa_v': nrm(ks[8], (DEC_BATCH, DEPTH, PAST_LEN, SWA_KV_HEADS, HEAD_DIM), 1.0),
        'c_ctx': nrm(ks[9], (D_MODEL,), 1.0),
        'w_ada': nrm(ks[10], (DEPTH, D_MODEL, N_MOD * D_MODEL), 0.5 * D_MODEL ** -0.5),
        'b_ada': nrm(ks[11], (DEPTH, N_MOD * D_MODEL), 0.01),
        'g_attn': 1.0 + nrm(ks[12], (DEPTH, D_MODEL), 0.05),
        'w_in': nrm(ks[13], (DEPTH, D_MODEL, IN_WIDTH), D_MODEL ** -0.5),
        'g_mla_q': 1.0 + nrm(ks[14], (DEPTH, MLA_Q_RANK), 0.05),
        'w_mla_qb': nrm(ks[15], (DEPTH, MLA_Q_RANK, MLA_HEADS * MLA_QK_DIM), MLA_Q_RANK ** -0.5),
        'g_mla_kv': 1.0 + nrm(ks[16], (DEPTH, MLA_KV_RANK), 0.05),
        'w_mla_kvb': nrm(ks[17], (DEPTH, MLA_KV_RANK, MLA_HEADS * (MLA_NOPE_DIM + MLA_V_DIM)), MLA_KV_RANK ** -0.5),
        'na_rpb': nrm(ks[18], (DEPTH, NA_HEADS, 2 * NA_WIN_H - 1, 2 * NA_WIN_W - 1), 0.2),
        'swa_sink': nrm(ks[19], (DEPTH, SWA_HEADS), 0.5),
        'w_out': nrm(ks[20], (DEPTH, MIX_WIDTH, D_MODEL), MIX_WIDTH ** -0.5),
        'g_ffn': 1.0 + nrm(ks[21], (DEPTH, D_MODEL), 0.05),
        'w_router': nrm(ks[22], (D_MODEL, N_EXPERTS), D_MODEL ** -0.5),
        'b_router': nrm(ks[23], (N_EXPERTS,), 0.01),
        'w_gate': nrm(ks[24], (DEPTH, N_EXPERTS, D_MODEL, D_EXPERT), D_MODEL ** -0.5),
        'w_up': nrm(ks[25], (DEPTH, N_EXPERTS, D_MODEL, D_EXPERT), D_MODEL ** -0.5),
        'w_down': nrm(ks[26], (DEPTH, N_EXPERTS, D_EXPERT, D_MODEL), D_EXPERT ** -0.5),
        'g_final': 1.0 + nrm(ks[27], (D_MODEL,), 0.05),
    }


def reference(x_prompt, x_sample, c, cache_na_k, cache_na_v, cache_mla_ckv, cache_mla_kpe, cache_swa_k, cache_swa_v,
              c_ctx, w_ada, b_ada, g_attn, w_in, g_mla_q, w_mla_qb, g_mla_kv, w_mla_kvb, na_rpb, swa_sink, w_out,
              g_ffn, w_router, b_router, w_gate, w_up, w_down, g_final):
    y = x_prompt
    na_ks, na_vs, ckvs, kpes, sw_ks, sw_vs = [], [], [], [], [], []
    for l in range(DEPTH):
        sh1, sc1, gt1, sh2, sc2, gt2 = ada_mod(c_ctx[None], w_ada[l], b_ada[l])
        h = rms_norm(y, g_attn[l]) * (1 + sc1) + sh1
        na_q, na_k, na_v, mla_q, ckv, kpe, sw_q, sw_k, sw_v = project(h, w_in[l], g_mla_q[l], w_mla_qb[l], g_mla_kv[l])
        mla_k, mla_v = mla_expand(ckv, kpe, w_mla_kvb[l])
        o = merge_heads(block_attention(na_q, na_k, na_v),
                        block_attention(mla_q, mla_k, mla_v),
                        block_attention(sw_q, sw_k, sw_v, swa_sink[l]), w_out[l])
        y = y + gt1 * o
        h = rms_norm(y, g_ffn[l]) * (1 + sc2) + sh2
        y = y + gt2 * moe(h, w_router, b_router, w_gate[l], w_up[l], w_down[l])
        na_ks.append(na_k)
        na_vs.append(na_v)
        ckvs.append(ckv)
        kpes.append(kpe)
        sw_ks.append(sw_k)
        sw_vs.append(sw_v)
    y_prompt = rms_norm(y, g_final)
    new_na_k = jnp.stack(na_ks, axis=1)
    new_na_v = jnp.stack(na_vs, axis=1)
    new_mla_ckv = jnp.stack(ckvs, axis=1)
    new_mla_kpe = jnp.stack(kpes, axis=1)
    new_swa_k = jnp.stack(sw_ks, axis=1)
    new_swa_v = jnp.stack(sw_vs, axis=1)

    T = x_sample.shape[1]
    cos_m, sin_m = axial_rope(T, MLA_ROPE_DIM)
    cos_s, sin_s = axial_rope(T, HEAD_DIM)
    z = x_sample
    for l in range(DEPTH):
        sh1, sc1, gt1, sh2, sc2, gt2 = ada_mod(c, w_ada[l], b_ada[l])
        h = rms_norm(z, g_attn[l]) * (1 + sc1) + sh1
        na_q, na_k, na_v, mla_q, ckv, kpe, sw_q, sw_k, sw_v = project(h, w_in[l], g_mla_q[l], w_mla_qb[l], g_mla_kv[l])
        o_na = neighborhood_attention(na_q, na_k, na_v, cache_na_k[:, l], cache_na_v[:, l], na_rpb[l])
        mla_q = jnp.concatenate([mla_q[..., :MLA_NOPE_DIM], apply_rope(mla_q[..., MLA_NOPE_DIM:], cos_m, sin_m)], axis=-1)
        kpe = apply_rope(kpe[:, :, None, :], cos_m, sin_m)[:, :, 0]
        lat_k, lat_v = mla_expand(ckv, kpe, w_mla_kvb[l])
        ctx_k, ctx_v = mla_expand(cache_mla_ckv[:, l], cache_mla_kpe[:, l], w_mla_kvb[l])
        o_mla = block_attention(mla_q, jnp.concatenate([lat_k, ctx_k], axis=1), jnp.concatenate([lat_v, ctx_v], axis=1))
        o_swa = banded_window_attention(apply_rope(sw_q, cos_s, sin_s), apply_rope(sw_k, cos_s, sin_s), sw_v,
                                        cache_swa_k[:, l], cache_swa_v[:, l], swa_sink[l])
        z = z + gt1 * merge_heads(o_na, o_mla, o_swa, w_out[l])
        h = rms_norm(z, g_ffn[l]) * (1 + sc2) + sh2
        z = z + gt2 * moe(h, w_router, b_router, w_gate[l], w_up[l], w_down[l])
    y_sample = rms_norm(z, g_final)
    return (y_prompt, y_sample, new_na_k, new_na_v, new_mla_ckv, new_mla_kpe, new_swa_k, new_swa_v)
```

```python
import functools

import jax
import jax.numpy as jnp
from jax import lax
from jax.experimental import pallas as pl
from jax.experimental.pallas import tpu as pltpu

D_MODEL = 1024
BATCH = 16
SEQ = 256
DEPTH = 4
DEC_BATCH = 2
DEC_SEQ = 2048
PAST_LEN = 256
GRID_W = 64
HEAD_DIM = 64
NA_HEADS = 4
NA_WIN_H = 8
NA_WIN_W = 16
MLA_HEADS = 6
MLA_Q_RANK = 256
MLA_KV_RANK = 128
MLA_NOPE_DIM = 64
MLA_ROPE_DIM = 32
MLA_V_DIM = 64
MLA_QK_DIM = MLA_NOPE_DIM + MLA_ROPE_DIM
SWA_HEADS = 6
SWA_KV_HEADS = 2
SWA_WINDOW = 128
ROPE_BASE = 10000.0
N_EXPERTS = 16
N_EXPERT_GROUPS = 4
EXPERTS_PER_GROUP = 4
D_EXPERT = 512
RMS_EPS = 1e-6
N_MOD = 6

NA_IN = 3 * NA_HEADS * HEAD_DIM
MLA_IN = MLA_Q_RANK + MLA_KV_RANK + MLA_ROPE_DIM
SWA_IN = (SWA_HEADS + 2 * SWA_KV_HEADS) * HEAD_DIM
IN_WIDTH = NA_IN + MLA_IN + SWA_IN
NA_OUT = NA_HEADS * HEAD_DIM
MLA_OUT = MLA_HEADS * MLA_V_DIM
SWA_OUT = SWA_HEADS * HEAD_DIM

LANES = 128
N_CTX = BATCH * SEQ
N_DEC = DEC_BATCH * DEC_SEQ
N_TOK = N_CTX + N_DEC
TM = 256
N_TILES = N_TOK // TM
CTX_TILES = N_CTX // TM
DEC_TILES_PER_BATCH = DEC_SEQ // TM
COND_ROWS = 8
KPE_OFF = NA_IN + MLA_Q_RANK + MLA_KV_RANK
SW_OFF = KPE_OFF + LANES
IN_PAD = SW_OFF + SWA_IN
MQ_NOPE = MLA_HEADS * MLA_NOPE_DIM
MQ_W = 640
NA_ROWS_PER_STEP = TM // GRID_W
NA_KEY_ROWS = 12
NA_KEYS = NA_KEY_ROWS * GRID_W
SWA_KEYS = 512
NEG = -1e30
MOE_TOK = 2048
MOE_SUB = 512

F32 = jnp.float32
BF16 = jnp.bfloat16
VMEM_LIMIT = 56 * 1024 * 1024


def _cparams(sem):
    return pltpu.CompilerParams(dimension_semantics=sem, vmem_limit_bytes=VMEM_LIMIT)


def _cond_row(i):
    return jnp.where(i < CTX_TILES, 0, 1 + (i - CTX_TILES) // DEC_TILES_PER_BATCH)


def _rope_blk(i):
    return jnp.where(i < CTX_TILES, 0, (i - CTX_TILES) % DEC_TILES_PER_BATCH)


def _rms(x, g):
    ms = jnp.mean(x * x, axis=-1, keepdims=True)
    return x * lax.rsqrt(ms + RMS_EPS) * g


def _dot(a, b):
    return jnp.dot(a.astype(BF16), b.astype(BF16), preferred_element_type=F32)


def _dot_nt(a, b):
    return lax.dot_general(a.astype(BF16), b.astype(BF16), (((1,), (1,)), ((), ())),
                           preferred_element_type=F32)


def _ada_kernel(c_ref, w_ref, b_ref, o_ref):
    c = c_ref[...]
    s = c * (1.0 / (1.0 + jnp.exp(-c)))
    o_ref[...] = _dot(s, w_ref[...]) + b_ref[...]


def _ada(cond, w_ada, b_ada):
    tn = 1536
    n = N_MOD * D_MODEL
    return pl.pallas_call(
        _ada_kernel,
        grid=(DEPTH, n // tn),
        in_specs=[pl.BlockSpec((COND_ROWS, D_MODEL), lambda l, j: (0, 0)),
                  pl.BlockSpec((None, D_MODEL, tn), lambda l, j: (l, 0, j)),
                  pl.BlockSpec((None, 1, tn), lambda l, j: (l, 0, j))],
        out_specs=pl.BlockSpec((None, COND_ROWS, tn), lambda l, j: (l, 0, j)),
        out_shape=jax.ShapeDtypeStruct((DEPTH, COND_ROWS, n), F32),
        compiler_params=_cparams(("arbitrary", "arbitrary")),
        name="ada_mod",
    )(cond, w_ada, b_ada.reshape(DEPTH, 1, n))


def _bias_kernel(rpb_ref, o_ref):
    g = pl.program_id(0)
    base = g * ((2 * NA_WIN_H - 1) * (2 * NA_WIN_W - 1))
    qc = lax.broadcasted_iota(jnp.int32, (GRID_W, GRID_W), 0)
    kc = lax.broadcasted_iota(jnp.int32, (GRID_W, GRID_W), 1)
    dc = jnp.clip(kc - qc + (NA_WIN_W - 1), 0, 2 * NA_WIN_W - 2)
    cs = jnp.clip(qc - NA_WIN_W // 2, 0, GRID_W - NA_WIN_W)
    col_ok = (kc >= cs) & (kc < cs + NA_WIN_W)
    neg = jnp.full((GRID_W, GRID_W), NEG, F32)
    tabs = []
    for a in range(2 * NA_WIN_H - 1):
        t = jnp.zeros((GRID_W, GRID_W), F32)
        for b in range(2 * NA_WIN_W - 1):
            t = jnp.where(dc == b, rpb_ref[base + a * (2 * NA_WIN_W - 1) + b], t)
        tabs.append(jnp.where(col_ok, t, NEG))
    for p in range(3):
        for qi in range(NA_ROWS_PER_STEP):
            for kj in range(NA_KEY_ROWS):
                if p == 0:
                    ok, dr = kj < NA_WIN_H, kj - qi + 7
                elif p == 1:
                    ok, dr = qi <= kj < qi + NA_WIN_H, kj - qi + 3
                else:
                    ok, dr = kj >= NA_KEY_ROWS - NA_WIN_H, kj - qi - 1
                blk = tabs[dr] if ok else neg
                o_ref[p, qi * GRID_W:(qi + 1) * GRID_W, kj * GRID_W:(kj + 1) * GRID_W] = blk


def _na_bias(na_rpb):
    return pl.pallas_call(
        _bias_kernel,
        grid=(DEPTH * NA_HEADS,),
        in_specs=[pl.BlockSpec(memory_space=pltpu.SMEM)],
        out_specs=pl.BlockSpec((None, 3, None, TM, NA_KEYS),
                               lambda g: (g // NA_HEADS, 0, g % NA_HEADS, 0, 0)),
        out_shape=jax.ShapeDtypeStruct((DEPTH, 3, NA_HEADS, TM, NA_KEYS), F32),
        compiler_params=_cparams(("arbitrary",)),
        name="na_bias",
    )(na_rpb.reshape(-1))


def _rope128(x, t_ref, half):
    return (x * t_ref[0] + pltpu.roll(x, half, 1) * t_ref[1]
            + pltpu.roll(x, LANES - half, 1) * t_ref[2])


def _pre_kernel(first, *refs):
    if first:
        (x_ref, mod_ref, g_ref, win_ref, gq_ref, wqb_ref, gkv_ref, tm_ref, ts_ref,
         na_ref, mq_ref, ckv_ref, kpe_ref, sw_ref, wbf, wqbf) = refs
        x = x_ref[...]
    else:
        (y_ref, moe_ref, modp_ref, mod_ref, g_ref, win_ref, gq_ref, wqb_ref, gkv_ref, tm_ref, ts_ref,
         xo_ref, na_ref, mq_ref, ckv_ref, kpe_ref, sw_ref, wbf, wqbf) = refs
        x = y_ref[...] + modp_ref[5:6, :] * moe_ref[...]
        xo_ref[...] = x
    i = pl.program_id(0)

    @pl.when(i == 0)
    def _():
        wbf[...] = win_ref[...].astype(BF16)
        wqbf[...] = wqb_ref[...].astype(BF16)

    h = _rms(x, g_ref[...]) * (1.0 + mod_ref[1:2, :]) + mod_ref[0:1, :]
    z = jnp.dot(h.astype(BF16), wbf[...], preferred_element_type=F32)
    na_ref[...] = z[:, :NA_IN]
    cq = _rms(z[:, NA_IN:NA_IN + MLA_Q_RANK], gq_ref[...])
    ckv_ref[...] = _rms(z[:, NA_IN + MLA_Q_RANK:KPE_OFF], gkv_ref[...])
    mq = jnp.dot(cq.astype(BF16), wqbf[...], preferred_element_type=F32)
    kpe = z[:, KPE_OFF:SW_OFF]
    sw = z[:, SW_OFF:IN_PAD]
    mq_ref[:, :MQ_NOPE] = mq[:, :MQ_NOPE]
    sw_ref[:, SWA_OUT + LANES:] = sw[:, SWA_OUT + LANES:]

    @pl.when(i < CTX_TILES)
    def _():
        mq_ref[:, MQ_NOPE:] = mq[:, MQ_NOPE:]
        kpe_ref[...] = kpe
        sw_ref[:, :SWA_OUT + LANES] = sw[:, :SWA_OUT + LANES]

    @pl.when(i >= CTX_TILES)
    def _():
        for c in range(MQ_NOPE // LANES, MQ_W // LANES):
            mq_ref[:, c * LANES:(c + 1) * LANES] = _rope128(mq[:, c * LANES:(c + 1) * LANES], tm_ref,
                                                            MLA_ROPE_DIM // 2)
        kpe_ref[...] = _rope128(kpe, tm_ref, MLA_ROPE_DIM // 2)
        for c in range((SWA_OUT + LANES) // LANES):
            sw_ref[:, c * LANES:(c + 1) * LANES] = _rope128(sw[:, c * LANES:(c + 1) * LANES], ts_ref,
                                                            HEAD_DIM // 2)


def _pre(first, l, xs, mods, g_attn, w_in_p, g_mla_q, w_qb_p, g_mla_kv, t_mla, t_swa):
    tile = lambda w: pl.BlockSpec((TM, w), lambda i: (i, 0))
    mod_spec = lambda ll: pl.BlockSpec((None, None, N_MOD, D_MODEL), lambda i: (ll, _cond_row(i), 0, 0))
    vec = lambda w: pl.BlockSpec((None, 1, w), lambda i: (l, 0, 0))
    in_specs = ([tile(D_MODEL)] if first else [tile(D_MODEL), tile(D_MODEL), mod_spec(l - 1)])
    in_specs += [mod_spec(l), vec(D_MODEL),
                 pl.BlockSpec((None, D_MODEL, IN_PAD), lambda i: (l, 0, 0)),
                 vec(MLA_Q_RANK),
                 pl.BlockSpec((None, MLA_Q_RANK, MQ_W), lambda i: (l, 0, 0)),
                 vec(MLA_KV_RANK),
                 pl.BlockSpec((3, TM, LANES), lambda i: (0, _rope_blk(i), 0)),
                 pl.BlockSpec((3, TM, LANES), lambda i: (0, _rope_blk(i), 0))]
    outs = [(D_MODEL, True)] if not first else []
    outs += [(NA_IN, False), (MQ_W, False), (MLA_KV_RANK, False), (LANES, False), (SWA_IN, False)]
    args = list(xs) + ([mods] if not first else []) + [
        mods, g_attn.reshape(DEPTH, 1, D_MODEL), w_in_p, g_mla_q.reshape(DEPTH, 1, MLA_Q_RANK), w_qb_p,
        g_mla_kv.reshape(DEPTH, 1, MLA_KV_RANK), t_mla, t_swa]
    return pl.pallas_call(
        functools.partial(_pre_kernel, first),
        grid=(N_TILES,),
        in_specs=in_specs,
        out_specs=[tile(w) for w, _ in outs],
        out_shape=[jax.ShapeDtypeStruct((N_TOK, w), F32) for w, _ in outs],
        scratch_shapes=[pltpu.VMEM((D_MODEL, IN_PAD), BF16), pltpu.VMEM((MLA_Q_RANK, MQ_W), BF16)],
        compiler_params=_cparams(("arbitrary",)),
        name="pre_attn",
    )(*args)


def _softmax_parts(parts, sink=None):
    m = parts[0].max(axis=-1, keepdims=True)
    for s in parts[1:]:
        m = jnp.maximum(m, s.max(axis=-1, keepdims=True))
    if sink is not None:
        m = jnp.maximum(m, sink)
    ps = [jnp.exp(s - m) for s in parts]
    den = ps[0].sum(axis=-1, keepdims=True)
    for p in ps[1:]:
        den = den + p.sum(axis=-1, keepdims=True)
    if sink is not None:
        den = den + jnp.exp(sink - m)
    return ps, 1.0 / den


def _mla_head(h, mq, wkvb, keys):
    scale = MLA_QK_DIM ** -0.5
    wk = wkvb[:, h * 2 * HEAD_DIM:h * 2 * HEAD_DIM + MLA_NOPE_DIM]
    wv = wkvb[:, h * 2 * HEAD_DIM + MLA_NOPE_DIM:(h + 1) * 2 * HEAD_DIM]
    qa = _dot_nt(mq[:, h * MLA_NOPE_DIM:(h + 1) * MLA_NOPE_DIM], wk)
    qr = mq[:, MQ_NOPE + h * MLA_ROPE_DIM:MQ_NOPE + (h + 1) * MLA_ROPE_DIM]
    parts = [(_dot_nt(qa, ckv) + _dot_nt(qr, kpe)) * scale for ckv, kpe in keys]
    ps, inv = _softmax_parts(parts)
    lat = _dot(ps[0], keys[0][0])
    for p, (ckv, _) in zip(ps[1:], keys[1:]):
        lat = lat + _dot(p, ckv)
    return _dot(lat * inv, wv)


def _ctx_attn_kernel(l, sink_ref, na_ref, mq_ref, ckv_ref, kpe_ref, sw_ref, wkvb_ref, o_ref):
    scale = HEAD_DIM ** -0.5
    mq = mq_ref[...]
    wkvb = wkvb_ref[...]
    keys = [(ckv_ref[...], kpe_ref[:, :MLA_ROPE_DIM])]
    for h in range(MLA_HEADS):
        o_ref[:, h * MLA_V_DIM:(h + 1) * MLA_V_DIM] = _mla_head(h, mq, wkvb, keys)
    for h in range(SWA_HEADS):
        kh = h // (SWA_HEADS // SWA_KV_HEADS)
        q = sw_ref[:, h * HEAD_DIM:(h + 1) * HEAD_DIM]
        k = sw_ref[:, SWA_OUT + kh * HEAD_DIM:SWA_OUT + (kh + 1) * HEAD_DIM]
        v = sw_ref[:, SWA_OUT + LANES + kh * HEAD_DIM:SWA_OUT + LANES + (kh + 1) * HEAD_DIM]
        (p,), inv = _softmax_parts([_dot_nt(q, k) * scale], sink_ref[l, h])
        o_ref[:, MLA_OUT + h * HEAD_DIM:MLA_OUT + (h + 1) * HEAD_DIM] = _dot(p, v) * inv
    for h in range(NA_HEADS):
        q = na_ref[:, h * HEAD_DIM:(h + 1) * HEAD_DIM]
        k = na_ref[:, NA_OUT + h * HEAD_DIM:NA_OUT + (h + 1) * HEAD_DIM]
        v = na_ref[:, 2 * NA_OUT + h * HEAD_DIM:2 * NA_OUT + (h + 1) * HEAD_DIM]
        (p,), inv = _softmax_parts([_dot_nt(q, k) * scale])
        o_ref[:, MLA_OUT + SWA_OUT + h * HEAD_DIM:MLA_OUT + SWA_OUT + (h + 1) * HEAD_DIM] = _dot(p, v) * inv


def _ctx_attn(l, sink, na, mq, ckv, kpe, sw, w_kvb):
    tile = lambda w: pl.BlockSpec((SEQ, w), lambda b: (b, 0))
    return pl.pallas_call(
        functools.partial(_ctx_attn_kernel, l),
        grid=(BATCH,),
        in_specs=[pl.BlockSpec(memory_space=pltpu.SMEM), tile(NA_IN), tile(MQ_W), tile(MLA_KV_RANK), tile(LANES),
                  tile(SWA_IN), pl.BlockSpec((None, MLA_KV_RANK, MLA_HEADS * 2 * HEAD_DIM), lambda b: (l, 0, 0))],
        out_specs=tile(D_MODEL),
        out_shape=jax.ShapeDtypeStruct((N_TOK, D_MODEL), F32),
        compiler_params=_cparams(("arbitrary",)),
        name="ctx_attn",
    )(sink, na, mq, ckv, kpe, sw, w_kvb)


def _dec_row(b, j):
    return CTX_TILES + b * DEC_TILES_PER_BATCH + j


def _na_dec_kernel(q_ref, k_ref, v_ref, ck_ref, cv_ref, bias_ref, oin_ref, o_ref):
    del oin_ref
    scale = HEAD_DIM ** -0.5
    j = pl.program_id(1)
    w0 = jnp.clip(j * NA_ROWS_PER_STEP - NA_WIN_H // 2, 0, DEC_SEQ // GRID_W - NA_KEY_ROWS)
    start = pl.multiple_of(w0 * GRID_W, GRID_W)
    for h in range(NA_HEADS):
        sl = slice(h * HEAD_DIM, (h + 1) * HEAD_DIM)
        q = q_ref[:, sl]
        k = k_ref[pl.ds(start, NA_KEYS), sl]
        v = v_ref[pl.ds(start, NA_KEYS), sl]
        s_nb = _dot_nt(q, k) * scale + bias_ref[h]
        s_ctx = _dot_nt(q, ck_ref[:, sl]) * scale
        (p_nb, p_ctx), inv = _softmax_parts([s_nb, s_ctx])
        o_ref[:, sl] = (_dot(p_nb, v) + _dot(p_ctx, cv_ref[:, sl])) * inv


def _na_dec(l, na, ck, cv, bias, o):
    pat = lambda j: jnp.where(j == 0, 0, jnp.where(j == DEC_TILES_PER_BATCH - 1, 2, 1))
    return pl.pallas_call(
        _na_dec_kernel,
        grid=(DEC_BATCH, DEC_TILES_PER_BATCH),
        in_specs=[pl.BlockSpec((TM, NA_OUT), lambda b, j: (_dec_row(b, j), 0)),
                  pl.BlockSpec((DEC_SEQ, NA_OUT), lambda b, j: (N_CTX // DEC_SEQ + b, 1)),
                  pl.BlockSpec((DEC_SEQ, NA_OUT), lambda b, j: (N_CTX // DEC_SEQ + b, 2)),
                  pl.BlockSpec((None, None, PAST_LEN, NA_OUT), lambda b, j: (b, l, 0, 0)),
                  pl.BlockSpec((None, None, PAST_LEN, NA_OUT), lambda b, j: (b, l, 0, 0)),
                  pl.BlockSpec((None, None, NA_HEADS, TM, NA_KEYS), lambda b, j: (l, pat(j), 0, 0, 0)),
                  pl.BlockSpec(memory_space=pl.ANY)],
        out_specs=pl.BlockSpec((TM, NA_OUT), lambda b, j: (_dec_row(b, j), (MLA_OUT + SWA_OUT) // NA_OUT)),
        out_shape=jax.ShapeDtypeStruct((N_TOK, D_MODEL), F32),
        input_output_aliases={6: 0},
        compiler_params=_cparams(("arbitrary", "arbitrary")),
        name="na_dec",
    )(na, na, na, ck, cv, bias, o)


def _mla_dec_kernel(mq_ref, ckv_ref, kpe_ref, cckv_ref, ckpe_ref, wkvb_ref, oin_ref, o_ref):
    del oin_ref
    mq = mq_ref[...]
    wkvb = wkvb_ref[...]
    keys = [(ckv_ref[...], kpe_ref[:, :MLA_ROPE_DIM]), (cckv_ref[...], ckpe_ref[...])]
    for h in range(MLA_HEADS):
        o_ref[:, h * MLA_V_DIM:(h + 1) * MLA_V_DIM] = _mla_head(h, mq, wkvb, keys)


def _mla_dec(l, mq, ckv, kpe, cckv, ckpe, w_kvb, o):
    return pl.pallas_call(
        _mla_dec_kernel,
        grid=(DEC_BATCH, DEC_TILES_PER_BATCH),
        in_specs=[pl.BlockSpec((TM, MQ_W), lambda b, j: (_dec_row(b, j), 0)),
                  pl.BlockSpec((DEC_SEQ, MLA_KV_RANK), lambda b, j: (N_CTX // DEC_SEQ + b, 0)),
                  pl.BlockSpec((DEC_SEQ, LANES), lambda b, j: (N_CTX // DEC_SEQ + b, 0)),
                  pl.BlockSpec((None, None, PAST_LEN, MLA_KV_RANK), lambda b, j: (b, l, 0, 0)),
                  pl.BlockSpec((None, None, PAST_LEN, MLA_ROPE_DIM), lambda b, j: (b, l, 0, 0)),
                  pl.BlockSpec((None, MLA_KV_RANK, MLA_HEADS * 2 * HEAD_DIM), lambda b, j: (l, 0, 0)),
                  pl.BlockSpec(memory_space=pl.ANY)],
        out_specs=pl.BlockSpec((TM, MLA_OUT), lambda b, j: (_dec_row(b, j), 0)),
        out_shape=jax.ShapeDtypeStruct((N_TOK, D_MODEL), F32),
        input_output_aliases={6: 0},
        compiler_params=_cparams(("arbitrary", "arbitrary")),
        name="mla_dec",
    )(mq, ckv, kpe, cckv, ckpe, w_kvb, o)


def _swa_dec_kernel(l, sink_ref, q_ref, k_ref, v_ref, ck_ref, cv_ref, oin_ref, o_ref):
    del oin_ref
    scale = HEAD_DIM ** -0.5
    j = pl.program_id(1)
    start = pl.multiple_of(jnp.clip(j * TM - SWA_WINDOW, 0, DEC_SEQ - SWA_KEYS), SWA_WINDOW)
    qpos = j * TM + lax.broadcasted_iota(jnp.int32, (TM, SWA_KEYS), 0)
    kpos = start + lax.broadcasted_iota(jnp.int32, (TM, SWA_KEYS), 1)
    band = jnp.abs(qpos - kpos) <= SWA_WINDOW
    for h in range(SWA_HEADS):
        kh = h // (SWA_HEADS // SWA_KV_HEADS)
        sl = slice(kh * HEAD_DIM, (kh + 1) * HEAD_DIM)
        q = q_ref[:, h * HEAD_DIM:(h + 1) * HEAD_DIM]
        s_loc = jnp.where(band, _dot_nt(q, k_ref[pl.ds(start, SWA_KEYS), sl]) * scale, NEG)
        s_ctx = _dot_nt(q, ck_ref[:, sl]) * scale
        (p_loc, p_ctx), inv = _softmax_parts([s_loc, s_ctx], sink_ref[l, h])
        o_ref[:, h * HEAD_DIM:(h + 1) * HEAD_DIM] = (
            _dot(p_loc, v_ref[pl.ds(start, SWA_KEYS), sl]) + _dot(p_ctx, cv_ref[:, sl])) * inv


def _swa_dec(l, sink, sw, ck, cv, o):
    return pl.pallas_call(
        functools.partial(_swa_dec_kernel, l),
        grid=(DEC_BATCH, DEC_TILES_PER_BATCH),
        in_specs=[pl.BlockSpec(memory_space=pltpu.SMEM),
                  pl.BlockSpec((TM, SWA_OUT), lambda b, j: (_dec_row(b, j), 0)),
                  pl.BlockSpec((DEC_SEQ, LANES), lambda b, j: (N_CTX // DEC_SEQ + b, SWA_OUT // LANES)),
                  pl.BlockSpec((DEC_SEQ, LANES), lambda b, j: (N_CTX // DEC_SEQ + b, SWA_OUT // LANES + 1)),
                  pl.BlockSpec((None, None, PAST_LEN, LANES), lambda b, j: (b, l, 0, 0)),
                  pl.BlockSpec((None, None, PAST_LEN, LANES), lambda b, j: (b, l, 0, 0)),
                  pl.BlockSpec(memory_space=pl.ANY)],
        out_specs=pl.BlockSpec((TM, SWA_OUT), lambda b, j: (_dec_row(b, j), 1)),
        out_shape=jax.ShapeDtypeStruct((N_TOK, D_MODEL), F32),
        input_output_aliases={6: 0},
        compiler_params=_cparams(("arbitrary", "arbitrary")),
        name="swa_dec",
    )(sink, sw, sw, sw, ck, cv, o)


def _route(sc, sel):
    rows = [sel[e:e + 1, :] for e in range(N_EXPERTS)]

    def beats(a, ia, b, ib):
        return (a > b) | ((a == b) & (ia < ib)) if ia < ib else (a > b)

    in_top = []
    gscore = []
    for g in range(N_EXPERT_GROUPS):
        mem = list(range(g * EXPERTS_PER_GROUP, (g + 1) * EXPERTS_PER_GROUP))
        acc = None
        for e in mem:
            rank = sum(beats(rows[o], o, rows[e], e).astype(jnp.int32) for o in mem if o != e)
            top = rank < 2
            in_top.append(top)
            term = jnp.where(top, rows[e], 0.0)
            acc = term if acc is None else acc + term
        gscore.append(acc)
    gates = []
    for g in range(N_EXPERT_GROUPS):
        lost = sum(beats(gscore[o], o, gscore[g], g).astype(jnp.int32) for o in range(N_EXPERT_GROUPS) if o != g)
        best = lost == 0
        for e in range(g * EXPERTS_PER_GROUP, (g + 1) * EXPERTS_PER_GROUP):
            gates.append(jnp.where(best & in_top[e], sc[e:e + 1, :], 0.0))
    gate = jnp.concatenate(gates, axis=0)
    return gate / gate.sum(axis=0, keepdims=True)


def _post_kernel(x_ref, o_ref, mod_ref, wout_ref, g_ref, wr_ref, br_ref, y_ref, h2_ref, gate_ref, wobf):
    @pl.when(pl.program_id(0) == 0)
    def _():
        wobf[:MLA_OUT, :] = wout_ref[NA_OUT:NA_OUT + MLA_OUT, :].astype(BF16)
        wobf[MLA_OUT:MLA_OUT + SWA_OUT, :] = wout_ref[NA_OUT + MLA_OUT:, :].astype(BF16)
        wobf[MLA_OUT + SWA_OUT:, :] = wout_ref[:NA_OUT, :].astype(BF16)

    attn = jnp.dot(o_ref[...].astype(BF16), wobf[...], preferred_element_type=F32)
    y = x_ref[...] + mod_ref[2:3, :] * attn
    y_ref[...] = y
    h2 = _rms(y, g_ref[...]) * (1.0 + mod_ref[4:5, :]) + mod_ref[3:4, :]
    h2_ref[...] = h2.astype(BF16)
    logits = lax.dot_general(wr_ref[...], h2, (((1,), (1,)), ((), ())),
                             precision=lax.Precision.HIGHEST, preferred_element_type=F32)
    sc = 1.0 / (1.0 + jnp.exp(-logits))
    gate_ref[...] = _route(sc, sc + br_ref[...]).T


def _post(l, x, o, mods, w_out, g_ffn, w_router_t, b_router):
    tile = lambda w: pl.BlockSpec((TM, w), lambda i: (i, 0))
    return pl.pallas_call(
        _post_kernel,
        grid=(N_TILES,),
        in_specs=[tile(D_MODEL), tile(D_MODEL),
                  pl.BlockSpec((None, None, N_MOD, D_MODEL), lambda i: (l, _cond_row(i), 0, 0)),
                  pl.BlockSpec((None, D_MODEL, D_MODEL), lambda i: (l, 0, 0)),
                  pl.BlockSpec((None, 1, D_MODEL), lambda i: (l, 0, 0)),
                  pl.BlockSpec((N_EXPERTS, D_MODEL), lambda i: (0, 0)),
                  pl.BlockSpec((N_EXPERTS, 1), lambda i: (0, 0))],
        out_specs=[tile(D_MODEL), tile(D_MODEL), tile(N_EXPERTS)],
        out_shape=[jax.ShapeDtypeStruct((N_TOK, D_MODEL), F32), jax.ShapeDtypeStruct((N_TOK, D_MODEL), BF16),
                   jax.ShapeDtypeStruct((N_TOK, N_EXPERTS), F32)],
        scratch_shapes=[pltpu.VMEM((D_MODEL, D_MODEL), BF16)],
        compiler_params=_cparams(("arbitrary",)),
        name="post_attn",
    )(x, o, mods, w_out, g_ffn.reshape(DEPTH, 1, D_MODEL), w_router_t, b_router.reshape(N_EXPERTS, 1))


def _moe_kernel(h_ref, gate_ref, wg_ref, wu_ref, wd_ref, o_ref, wgb, wub, wdb):
    e = pl.program_id(1)
    wgb[...] = wg_ref[...].astype(BF16)
    wub[...] = wu_ref[...].astype(BF16)
    wdb[...] = wd_ref[...].astype(BF16)

    @pl.when(e == 0)
    def _():
        o_ref[...] = jnp.zeros_like(o_ref)

    lane = lax.broadcasted_iota(jnp.int32, (MOE_SUB, N_EXPERTS), 1)
    for r in range(MOE_TOK // MOE_SUB):
        rows = slice(r * MOE_SUB, (r + 1) * MOE_SUB)
        x = h_ref[rows, :]
        gcol = jnp.sum(jnp.where(lane == e, gate_ref[rows, :], 0.0), axis=1, keepdims=True)
        hg = jnp.dot(x, wgb[...], preferred_element_type=F32)
        hu = jnp.dot(x, wub[...], preferred_element_type=F32)
        a = hg * (1.0 / (1.0 + jnp.exp(-hg))) * hu * gcol
        o_ref[rows, :] += jnp.dot(a.astype(BF16), wdb[...], preferred_element_type=F32)


def _moe(l, h2, gate, w_gate, w_up, w_down):
    return pl.pallas_call(
        _moe_kernel,
        grid=(N_TOK // MOE_TOK, N_EXPERTS),
        in_specs=[pl.BlockSpec((MOE_TOK, D_MODEL), lambda t, e: (t, 0)),
                  pl.BlockSpec((MOE_TOK, N_EXPERTS), lambda t, e: (t, 0)),
                  pl.BlockSpec((None, None, D_MODEL, D_EXPERT), lambda t, e: (l, e, 0, 0)),
                  pl.BlockSpec((None, None, D_MODEL, D_EXPERT), lambda t, e: (l, e, 0, 0)),
                  pl.BlockSpec((None, None, D_EXPERT, D_MODEL), lambda t, e: (l, e, 0, 0))],
        out_specs=pl.BlockSpec((MOE_TOK, D_MODEL), lambda t, e: (t, 0)),
        out_shape=jax.ShapeDtypeStruct((N_TOK, D_MODEL), F32),
        scratch_shapes=[pltpu.VMEM((D_MODEL, D_EXPERT), BF16), pltpu.VMEM((D_MODEL, D_EXPERT), BF16),
                        pltpu.VMEM((D_EXPERT, D_MODEL), BF16)],
        compiler_params=_cparams(("arbitrary", "arbitrary")),
        name="moe",
    )(h2, gate, w_gate, w_up, w_down)


def _final_kernel(y_ref, moe_ref, mod_ref, g_ref, o_ref):
    o_ref[...] = _rms(y_ref[...] + mod_ref[5:6, :] * moe_ref[...], g_ref[...])


def _final(y, moe, mods, g_final):
    tile = pl.BlockSpec((TM, D_MODEL), lambda i: (i, 0))
    return pl.pallas_call(
        _final_kernel,
        grid=(N_TILES,),
        in_specs=[tile, tile,
                  pl.BlockSpec((None, None, N_MOD, D_MODEL), lambda i: (DEPTH - 1, _cond_row(i), 0, 0)),
                  pl.BlockSpec((1, D_MODEL), lambda i: (0, 0))],
        out_specs=tile,
        out_shape=jax.ShapeDtypeStruct((N_TOK, D_MODEL), F32),
        compiler_params=_cparams(("arbitrary",)),
        name="final_norm",
    )(y, moe, mods, g_final.reshape(1, D_MODEL))


def _rope_tables(rot_dim):
    t = jnp.arange(DEC_SEQ, dtype=jnp.int32)
    row = (t // GRID_W).astype(F32)
    col = (t % GRID_W).astype(F32)
    per_axis = rot_dim // 2
    inv = ROPE_BASE ** (-jnp.arange(0, per_axis, 2, dtype=F32) / per_axis)
    ang = jnp.concatenate([row[:, None] * inv, col[:, None] * inv], axis=-1)
    cos, sin = jnp.cos(ang), jnp.sin(ang)
    zero = jnp.zeros_like(sin)
    rep = LANES // rot_dim
    tabs = [jnp.concatenate([cos, cos], -1), jnp.concatenate([zero, sin], -1), jnp.concatenate([-sin, zero], -1)]
    return jnp.stack([jnp.tile(a, (1, rep)) for a in tabs])


def kernel(x_prompt, x_sample, c, cache_na_k, cache_na_v, cache_mla_ckv, cache_mla_kpe, cache_swa_k, cache_swa_v,
           c_ctx, w_ada, b_ada, g_attn, w_in, g_mla_q, w_mla_qb, g_mla_kv, w_mla_kvb, na_rpb, swa_sink, w_out,
           g_ffn, w_router, b_router, w_gate, w_up, w_down, g_final):
    cond = jnp.concatenate([c_ctx[None], c, jnp.zeros((COND_ROWS - 1 - DEC_BATCH, D_MODEL), F32)], axis=0)
    mods = _ada(cond, w_ada, b_ada).reshape(DEPTH, COND_ROWS, N_MOD, D_MODEL)
    bias = _na_bias(na_rpb)
    t_mla = _rope_tables(MLA_ROPE_DIM)
    t_swa = _rope_tables(HEAD_DIM)

    w_in_p = jnp.concatenate([w_in[..., :KPE_OFF + MLA_ROPE_DIM],
                              jnp.zeros((DEPTH, D_MODEL, LANES - MLA_ROPE_DIM), F32),
                              w_in[..., KPE_OFF + MLA_ROPE_DIM:]], axis=-1)
    wq = w_mla_qb.reshape(DEPTH, MLA_Q_RANK, MLA_HEADS, MLA_QK_DIM)
    w_qb_p = jnp.concatenate([wq[..., :MLA_NOPE_DIM].reshape(DEPTH, MLA_Q_RANK, MQ_NOPE),
                              wq[..., MLA_NOPE_DIM:].reshape(DEPTH, MLA_Q_RANK, MLA_HEADS * MLA_ROPE_DIM),
                              jnp.zeros((DEPTH, MLA_Q_RANK, MQ_W - MLA_HEADS * MLA_QK_DIM), F32)], axis=-1)
    w_router_t = w_router.T

    c_na_k = cache_na_k.reshape(DEC_BATCH, DEPTH, PAST_LEN, NA_OUT)
    c_na_v = cache_na_v.reshape(DEC_BATCH, DEPTH, PAST_LEN, NA_OUT)
    c_sw_k = cache_swa_k.reshape(DEC_BATCH, DEPTH, PAST_LEN, LANES)
    c_sw_v = cache_swa_v.reshape(DEC_BATCH, DEPTH, PAST_LEN, LANES)

    x = jnp.concatenate([x_prompt.reshape(N_CTX, D_MODEL), x_sample.reshape(N_DEC, D_MODEL)], axis=0)
    y = moe = None
    caches = [[] for _ in range(6)]
    for l in range(DEPTH):
        if l == 0:
            na, mq, ckv, kpe, sw = _pre(True, l, [x], mods, g_attn, w_in_p, g_mla_q, w_qb_p, g_mla_kv, t_mla, t_swa)
        else:
            x, na, mq, ckv, kpe, sw = _pre(False, l, [y, moe], mods, g_attn, w_in_p, g_mla_q, w_qb_p, g_mla_kv,
                                           t_mla, t_swa)
        o = _ctx_attn(l, swa_sink, na, mq, ckv, kpe, sw, w_mla_kvb)
        o = _mla_dec(l, mq, ckv, kpe, cache_mla_ckv, cache_mla_kpe, w_mla_kvb, o)
        o = _swa_dec(l, swa_sink, sw, c_sw_k, c_sw_v, o)
        o = _na_dec(l, na, c_na_k, c_na_v, bias, o)
        y, h2, gate = _post(l, x, o, mods, w_out, g_ffn, w_router_t, b_router)
        moe = _moe(l, h2, gate, w_gate, w_up, w_down)
        caches[0].append(na[:N_CTX, NA_OUT:2 * NA_OUT])
        caches[1].append(na[:N_CTX, 2 * NA_OUT:])
        caches[2].append(ckv[:N_CTX])
        caches[3].append(kpe[:N_CTX, :MLA_ROPE_DIM])
        caches[4].append(sw[:N_CTX, SWA_OUT:SWA_OUT + LANES])
        caches[5].append(sw[:N_CTX, SWA_OUT + LANES:])
    out = _final(y, moe, mods, g_final)

    def stack(parts, tail):
        return jnp.stack([p.reshape((BATCH, SEQ) + tail) for p in parts], axis=1)

    return (out[:N_CTX].reshape(BATCH, SEQ, D_MODEL), out[N_CTX:].reshape(DEC_BATCH, DEC_SEQ, D_MODEL),
            stack(caches[0], (NA_HEADS, HEAD_DIM)), stack(caches[1], (NA_HEADS, HEAD_DIM)),
            stack(caches[2], (MLA_KV_RANK,)), stack(caches[3], (MLA_ROPE_DIM,)),
            stack(caches[4], (SWA_KV_HEADS, HEAD_DIM)), stack(caches[5], (SWA_KV_HEADS, HEAD_DIM)))
```

```python
import functools

import jax
import jax.numpy as jnp
from jax import lax
from jax.experimental import pallas as pl
from jax.experimental.pallas import tpu as pltpu

D_MODEL = 1024
BATCH = 16
SEQ = 256
DEPTH = 4
DEC_BATCH = 2
DEC_SEQ = 2048
PAST_LEN = 256
GRID_W = 64
HEAD_DIM = 64
NA_HEADS = 4
NA_WIN_H = 8
NA_WIN_W = 16
MLA_HEADS = 6
MLA_Q_RANK = 256
MLA_KV_RANK = 128
MLA_NOPE_DIM = 64
MLA_ROPE_DIM = 32
MLA_V_DIM = 64
MLA_QK_DIM = MLA_NOPE_DIM + MLA_ROPE_DIM
SWA_HEADS = 6
SWA_KV_HEADS = 2
SWA_WINDOW = 128
ROPE_BASE = 10000.0
N_EXPERTS = 16
N_EXPERT_GROUPS = 4
EXPERTS_PER_GROUP = 4
D_EXPERT = 512
RMS_EPS = 1e-6
N_MOD = 6

NA_IN = 3 * NA_HEADS * HEAD_DIM
MLA_IN = MLA_Q_RANK + MLA_KV_RANK + MLA_ROPE_DIM
SWA_IN = (SWA_HEADS + 2 * SWA_KV_HEADS) * HEAD_DIM
IN_WIDTH = NA_IN + MLA_IN + SWA_IN
NA_OUT = NA_HEADS * HEAD_DIM
MLA_OUT = MLA_HEADS * MLA_V_DIM
SWA_OUT = SWA_HEADS * HEAD_DIM

LANES = 128
N_CTX = BATCH * SEQ
N_DEC = DEC_BATCH * DEC_SEQ
N_TOK = N_CTX + N_DEC
TM = 256
N_TILES = N_TOK // TM
CTX_TILES = N_CTX // TM
DEC_TILES_PER_BATCH = DEC_SEQ // TM
COND_ROWS = 8
KPE_OFF = NA_IN + MLA_Q_RANK + MLA_KV_RANK
SW_OFF = KPE_OFF + LANES
IN_PAD = SW_OFF + SWA_IN
MQ_NOPE = MLA_HEADS * MLA_NOPE_DIM
MQ_W = 640
NA_ROWS_PER_STEP = TM // GRID_W
NA_KEY_ROWS = 12
NA_KEYS = NA_KEY_ROWS * GRID_W
SWA_KEYS = 512
NEG = -1e30
TR = 256
N_ASSIGN = 2 * N_TOK
NT = (N_ASSIGN + N_EXPERTS * (TR - 1) + TR - 1) // TR
HX = D_MODEL + LANES

F32 = jnp.float32
BF16 = jnp.bfloat16
VMEM_LIMIT = 56 * 1024 * 1024


def _cparams(sem):
    return pltpu.CompilerParams(dimension_semantics=sem, vmem_limit_bytes=VMEM_LIMIT)


def _cond_row(i):
    return jnp.where(i < CTX_TILES, 0, 1 + (i - CTX_TILES) // DEC_TILES_PER_BATCH)


def _rope_blk(i):
    return jnp.where(i < CTX_TILES, 0, (i - CTX_TILES) % DEC_TILES_PER_BATCH)


def _rms(x, g):
    ms = jnp.mean(x * x, axis=-1, keepdims=True)
    return x * lax.rsqrt(ms + RMS_EPS) * g


def _dot(a, b):
    return jnp.dot(a.astype(BF16), b.astype(BF16), preferred_element_type=F32)


def _dot_nt(a, b):
    return lax.dot_general(a.astype(BF16), b.astype(BF16), (((1,), (1,)), ((), ())),
                           preferred_element_type=F32)


def _ada_kernel(c_ref, w_ref, b_ref, o_ref):
    c = c_ref[...]
    s = c * (1.0 / (1.0 + jnp.exp(-c)))
    o_ref[...] = _dot(s, w_ref[...]) + b_ref[...]


def _ada(cond, w_ada, b_ada):
    tn = 1536
    n = N_MOD * D_MODEL
    return pl.pallas_call(
        _ada_kernel,
        grid=(DEPTH, n // tn),
        in_specs=[pl.BlockSpec((COND_ROWS, D_MODEL), lambda l, j: (0, 0)),
                  pl.BlockSpec((None, D_MODEL, tn), lambda l, j: (l, 0, j)),
                  pl.BlockSpec((None, 1, tn), lambda l, j: (l, 0, j))],
        out_specs=pl.BlockSpec((None, COND_ROWS, tn), lambda l, j: (l, 0, j)),
        out_shape=jax.ShapeDtypeStruct((DEPTH, COND_ROWS, n), F32),
        compiler_params=_cparams(("arbitrary", "arbitrary")),
        name="ada_mod",
    )(cond, w_ada, b_ada.reshape(DEPTH, 1, n))


def _bias_kernel(rpb_ref, o_ref):
    g = pl.program_id(0)
    base = g * ((2 * NA_WIN_H - 1) * (2 * NA_WIN_W - 1))
    qc = lax.broadcasted_iota(jnp.int32, (GRID_W, GRID_W), 0)
    kc = lax.broadcasted_iota(jnp.int32, (GRID_W, GRID_W), 1)
    dc = jnp.clip(kc - qc + (NA_WIN_W - 1), 0, 2 * NA_WIN_W - 2)
    cs = jnp.clip(qc - NA_WIN_W // 2, 0, GRID_W - NA_WIN_W)
    col_ok = (kc >= cs) & (kc < cs + NA_WIN_W)
    neg = jnp.full((GRID_W, GRID_W), NEG, F32)
    tabs = []
    for a in range(2 * NA_WIN_H - 1):
        t = jnp.zeros((GRID_W, GRID_W), F32)
        for b in range(2 * NA_WIN_W - 1):
            t = jnp.where(dc == b, rpb_ref[base + a * (2 * NA_WIN_W - 1) + b], t)
        tabs.append(jnp.where(col_ok, t, NEG))
    for p in range(3):
        for qi in range(NA_ROWS_PER_STEP):
            for kj in range(NA_KEY_ROWS):
                if p == 0:
                    ok, dr = kj < NA_WIN_H, kj - qi + 7
                elif p == 1:
                    ok, dr = qi <= kj < qi + NA_WIN_H, kj - qi + 3
                else:
                    ok, dr = kj >= NA_KEY_ROWS - NA_WIN_H, kj - qi - 1
                blk = tabs[dr] if ok else neg
                o_ref[p, qi * GRID_W:(qi + 1) * GRID_W, kj * GRID_W:(kj + 1) * GRID_W] = blk


def _na_bias(na_rpb):
    return pl.pallas_call(
        _bias_kernel,
        grid=(DEPTH * NA_HEADS,),
        in_specs=[pl.BlockSpec(memory_space=pltpu.SMEM)],
        out_specs=pl.BlockSpec((None, 3, None, TM, NA_KEYS),
                               lambda g: (g // NA_HEADS, 0, g % NA_HEADS, 0, 0)),
        out_shape=jax.ShapeDtypeStruct((DEPTH, 3, NA_HEADS, TM, NA_KEYS), F32),
        compiler_params=_cparams(("arbitrary",)),
        name="na_bias",
    )(na_rpb.reshape(-1))


def _rope128(x, t_ref, half):
    return (x * t_ref[0] + pltpu.roll(x, half, 1) * t_ref[1]
            + pltpu.roll(x, LANES - half, 1) * t_ref[2])


def _combine_start(i, slot, pos_ref, ys_hbm, cbuf, sem):
    def body(r, carry):
        t = i * TM + r
        for k in range(2):
            pltpu.make_async_copy(ys_hbm.at[pl.ds(pos_ref[k * N_TOK + t], 1)],
                                  cbuf.at[slot, k, pl.ds(r, 1)], sem.at[slot]).start()
        return carry

    lax.fori_loop(0, TM, body, 0, unroll=8)


def _combined_moe(pos_ref, ys_hbm, cbuf, sem):
    i = pl.program_id(0)
    slot = i % 2

    @pl.when(i == 0)
    def _():
        _combine_start(0, 0, pos_ref, ys_hbm, cbuf, sem)

    @pl.when(i + 1 < N_TILES)
    def _():
        _combine_start(i + 1, 1 - slot, pos_ref, ys_hbm, cbuf, sem)

    for k in range(2):
        pltpu.make_async_copy(ys_hbm.at[pl.ds(0, TM)], cbuf.at[slot, k], sem.at[slot]).wait()
    return cbuf[slot, 0] + cbuf[slot, 1]


def _pre_kernel(first, *refs):
    if first:
        (x_ref, mod_ref, g_ref, win_ref, gq_ref, wqb_ref, gkv_ref, tm_ref, ts_ref,
         na_ref, mq_ref, ckv_ref, kpe_ref, sw_ref, wbf, wqbf) = refs
        x = x_ref[...]
    else:
        (pos_ref, y_ref, ys_hbm, modp_ref, mod_ref, g_ref, win_ref, gq_ref, wqb_ref, gkv_ref, tm_ref, ts_ref,
         xo_ref, na_ref, mq_ref, ckv_ref, kpe_ref, sw_ref, wbf, wqbf, cbuf, sem) = refs
        x = y_ref[...] + modp_ref[5:6, :] * _combined_moe(pos_ref, ys_hbm, cbuf, sem)
        xo_ref[...] = x
    i = pl.program_id(0)

    @pl.when(i == 0)
    def _():
        wbf[...] = win_ref[...].astype(BF16)
        wqbf[...] = wqb_ref[...].astype(BF16)

    h = _rms(x, g_ref[...]) * (1.0 + mod_ref[1:2, :]) + mod_ref[0:1, :]
    z = jnp.dot(h.astype(BF16), wbf[...], preferred_element_type=F32)
    na_ref[...] = z[:, :NA_IN]
    cq = _rms(z[:, NA_IN:NA_IN + MLA_Q_RANK], gq_ref[...])
    ckv_ref[...] = _rms(z[:, NA_IN + MLA_Q_RANK:KPE_OFF], gkv_ref[...])
    mq = jnp.dot(cq.astype(BF16), wqbf[...], preferred_element_type=F32)
    kpe = z[:, KPE_OFF:SW_OFF]
    sw = z[:, SW_OFF:IN_PAD]
    mq_ref[:, :MQ_NOPE] = mq[:, :MQ_NOPE]
    sw_ref[:, SWA_OUT + LANES:] = sw[:, SWA_OUT + LANES:]

    @pl.when(i < CTX_TILES)
    def _():
        mq_ref[:, MQ_NOPE:] = mq[:, MQ_NOPE:]
        kpe_ref[...] = kpe
        sw_ref[:, :SWA_OUT + LANES] = sw[:, :SWA_OUT + LANES]

    @pl.when(i >= CTX_TILES)
    def _():
        for c in range(MQ_NOPE // LANES, MQ_W // LANES):
            mq_ref[:, c * LANES:(c + 1) * LANES] = _rope128(mq[:, c * LANES:(c + 1) * LANES], tm_ref,
                                                            MLA_ROPE_DIM // 2)
        kpe_ref[...] = _rope128(kpe, tm_ref, MLA_ROPE_DIM // 2)
        for c in range((SWA_OUT + LANES) // LANES):
            sw_ref[:, c * LANES:(c + 1) * LANES] = _rope128(sw[:, c * LANES:(c + 1) * LANES], ts_ref,
                                                            HEAD_DIM // 2)


def _combine_scratch():
    return [pltpu.VMEM((2, 2, TM, D_MODEL), F32), pltpu.SemaphoreType.DMA((2,))]


def _pre(first, l, xs, mods, g_attn, w_in_p, g_mla_q, w_qb_p, g_mla_kv, t_mla, t_swa):
    tile = lambda w: pl.BlockSpec((TM, w), lambda i, *_: (i, 0))
    mod_spec = lambda ll: pl.BlockSpec((None, None, N_MOD, D_MODEL), lambda i, *_: (ll, _cond_row(i), 0, 0))
    vec = lambda w: pl.BlockSpec((None, 1, w), lambda i, *_: (l, 0, 0))
    in_specs = ([tile(D_MODEL)] if first else
                [tile(D_MODEL), pl.BlockSpec(memory_space=pl.ANY), mod_spec(l - 1)])
    in_specs += [mod_spec(l), vec(D_MODEL),
                 pl.BlockSpec((None, D_MODEL, IN_PAD), lambda i, *_: (l, 0, 0)),
                 vec(MLA_Q_RANK),
                 pl.BlockSpec((None, MLA_Q_RANK, MQ_W), lambda i, *_: (l, 0, 0)),
                 vec(MLA_KV_RANK),
                 pl.BlockSpec((3, TM, LANES), lambda i, *_: (0, _rope_blk(i), 0)),
                 pl.BlockSpec((3, TM, LANES), lambda i, *_: (0, _rope_blk(i), 0))]
    widths = ([] if first else [D_MODEL]) + [NA_IN, MQ_W, MLA_KV_RANK, LANES, SWA_IN]
    args = list(xs) + ([mods] if not first else []) + [
        mods, g_attn.reshape(DEPTH, 1, D_MODEL), w_in_p, g_mla_q.reshape(DEPTH, 1, MLA_Q_RANK), w_qb_p,
        g_mla_kv.reshape(DEPTH, 1, MLA_KV_RANK), t_mla, t_swa]
    scratch = [pltpu.VMEM((D_MODEL, IN_PAD), BF16), pltpu.VMEM((MLA_Q_RANK, MQ_W), BF16)]
    return pl.pallas_call(
        functools.partial(_pre_kernel, first),
        grid_spec=pltpu.PrefetchScalarGridSpec(
            num_scalar_prefetch=0 if first else 1,
            grid=(N_TILES,),
            in_specs=in_specs,
            out_specs=[tile(w) for w in widths],
            scratch_shapes=scratch + ([] if first else _combine_scratch())),
        out_shape=[jax.ShapeDtypeStruct((N_TOK, w), F32) for w in widths],
        compiler_params=_cparams(("arbitrary",)),
        name="pre_attn",
    )(*args)


def _softmax_parts(parts, sink=None):
    m = parts[0].max(axis=-1, keepdims=True)
    for s in parts[1:]:
        m = jnp.maximum(m, s.max(axis=-1, keepdims=True))
    if sink is not None:
        m = jnp.maximum(m, sink)
    ps = [jnp.exp(s - m) for s in parts]
    den = ps[0].sum(axis=-1, keepdims=True)
    for p in ps[1:]:
        den = den + p.sum(axis=-1, keepdims=True)
    if sink is not None:
        den = den + jnp.exp(sink - m)
    return ps, 1.0 / den


def _mla_head(h, mq, wkvb, keys):
    scale = MLA_QK_DIM ** -0.5
    wk = wkvb[:, h * 2 * HEAD_DIM:h * 2 * HEAD_DIM + MLA_NOPE_DIM]
    wv = wkvb[:, h * 2 * HEAD_DIM + MLA_NOPE_DIM:(h + 1) * 2 * HEAD_DIM]
    qa = _dot_nt(mq[:, h * MLA_NOPE_DIM:(h + 1) * MLA_NOPE_DIM], wk)
    qr = mq[:, MQ_NOPE + h * MLA_ROPE_DIM:MQ_NOPE + (h + 1) * MLA_ROPE_DIM]
    parts = [(_dot_nt(qa, ckv) + _dot_nt(qr, kpe)) * scale for ckv, kpe in keys]
    ps, inv = _softmax_parts(parts)
    lat = _dot(ps[0], keys[0][0])
    for p, (ckv, _) in zip(ps[1:], keys[1:]):
        lat = lat + _dot(p, ckv)
    return _dot(lat * inv, wv)


def _ctx_attn_kernel(l, sink_ref, na_ref, mq_ref, ckv_ref, kpe_ref, sw_ref, wkvb_ref, o_ref):
    @pl.when(pl.program_id(0) >= BATCH)
    def _():
        o_ref[...] = jnp.zeros_like(o_ref)

    @pl.when(pl.program_id(0) < BATCH)
    def _():
        _ctx_attn_body(l, sink_ref, na_ref, mq_ref, ckv_ref, kpe_ref, sw_ref, wkvb_ref, o_ref)


def _ctx_attn_body(l, sink_ref, na_ref, mq_ref, ckv_ref, kpe_ref, sw_ref, wkvb_ref, o_ref):
    scale = HEAD_DIM ** -0.5
    mq = mq_ref[...]
    wkvb = wkvb_ref[...]
    keys = [(ckv_ref[...], kpe_ref[:, :MLA_ROPE_DIM])]
    for h in range(MLA_HEADS):
        o_ref[:, h * MLA_V_DIM:(h + 1) * MLA_V_DIM] = _mla_head(h, mq, wkvb, keys)
    for h in range(SWA_HEADS):
        kh = h // (SWA_HEADS // SWA_KV_HEADS)
        q = sw_ref[:, h * HEAD_DIM:(h + 1) * HEAD_DIM]
        k = sw_ref[:, SWA_OUT + kh * HEAD_DIM:SWA_OUT + (kh + 1) * HEAD_DIM]
        v = sw_ref[:, SWA_OUT + LANES + kh * HEAD_DIM:SWA_OUT + LANES + (kh + 1) * HEAD_DIM]
        (p,), inv = _softmax_parts([_dot_nt(q, k) * scale], sink_ref[l, h])
        o_ref[:, MLA_OUT + h * HEAD_DIM:MLA_OUT + (h + 1) * HEAD_DIM] = _dot(p, v) * inv
    for h in range(NA_HEADS):
        q = na_ref[:, h * HEAD_DIM:(h + 1) * HEAD_DIM]
        k = na_ref[:, NA_OUT + h * HEAD_DIM:NA_OUT + (h + 1) * HEAD_DIM]
        v = na_ref[:, 2 * NA_OUT + h * HEAD_DIM:2 * NA_OUT + (h + 1) * HEAD_DIM]
        (p,), inv = _softmax_parts([_dot_nt(q, k) * scale])
        o_ref[:, MLA_OUT + SWA_OUT + h * HEAD_DIM:MLA_OUT + SWA_OUT + (h + 1) * HEAD_DIM] = _dot(p, v) * inv


def _ctx_attn(l, sink, na, mq, ckv, kpe, sw, w_kvb):
    tile = lambda w: pl.BlockSpec((SEQ, w), lambda b: (jnp.minimum(b, BATCH - 1), 0))
    return pl.pallas_call(
        functools.partial(_ctx_attn_kernel, l),
        grid=(N_TOK // SEQ,),
        in_specs=[pl.BlockSpec(memory_space=pltpu.SMEM), tile(NA_IN), tile(MQ_W), tile(MLA_KV_RANK), tile(LANES),
                  tile(SWA_IN), pl.BlockSpec((None, MLA_KV_RANK, MLA_HEADS * 2 * HEAD_DIM), lambda b: (l, 0, 0))],
        out_specs=pl.BlockSpec((SEQ, D_MODEL), lambda b: (b, 0)),
        out_shape=jax.ShapeDtypeStruct((N_TOK, D_MODEL), F32),
        compiler_params=_cparams(("arbitrary",)),
        name="ctx_attn",
    )(sink, na, mq, ckv, kpe, sw, w_kvb)


def _dec_row(b, j):
    return CTX_TILES + b * DEC_TILES_PER_BATCH + j


def _na_dec_kernel(q_ref, k_ref, v_ref, ck_ref, cv_ref, bias_ref, oin_ref, o_ref):
    del oin_ref
    scale = HEAD_DIM ** -0.5
    j = pl.program_id(1)
    w0 = jnp.clip(j * NA_ROWS_PER_STEP - NA_WIN_H // 2, 0, DEC_SEQ // GRID_W - NA_KEY_ROWS)
    start = pl.multiple_of(w0 * GRID_W, GRID_W)
    for h in range(NA_HEADS):
        sl = slice(h * HEAD_DIM, (h + 1) * HEAD_DIM)
        q = q_ref[:, sl]
        k = k_ref[pl.ds(start, NA_KEYS), sl]
        v = v_ref[pl.ds(start, NA_KEYS), sl]
        s_nb = _dot_nt(q, k) * scale + bias_ref[h]
        s_ctx = _dot_nt(q, ck_ref[:, sl]) * scale
        (p_nb, p_ctx), inv = _softmax_parts([s_nb, s_ctx])
        o_ref[:, sl] = (_dot(p_nb, v) + _dot(p_ctx, cv_ref[:, sl])) * inv


def _na_dec(l, na, ck, cv, bias, o):
    pat = lambda j: jnp.where(j == 0, 0, jnp.where(j == DEC_TILES_PER_BATCH - 1, 2, 1))
    return pl.pallas_call(
        _na_dec_kernel,
        grid=(DEC_BATCH, DEC_TILES_PER_BATCH),
        in_specs=[pl.BlockSpec((TM, NA_OUT), lambda b, j: (_dec_row(b, j), 0)),
                  pl.BlockSpec((DEC_SEQ, NA_OUT), lambda b, j: (N_CTX // DEC_SEQ + b, 1)),
                  pl.BlockSpec((DEC_SEQ, NA_OUT), lambda b, j: (N_CTX // DEC_SEQ + b, 2)),
                  pl.BlockSpec((None, None, PAST_LEN, NA_OUT), lambda b, j: (b, l, 0, 0)),
                  pl.BlockSpec((None, None, PAST_LEN, NA_OUT), lambda b, j: (b, l, 0, 0)),
                  pl.BlockSpec((None, None, NA_HEADS, TM, NA_KEYS), lambda b, j: (l, pat(j), 0, 0, 0)),
                  pl.BlockSpec(memory_space=pl.ANY)],
        out_specs=pl.BlockSpec((TM, NA_OUT), lambda b, j: (_dec_row(b, j), (MLA_OUT + SWA_OUT) // NA_OUT)),
        out_shape=jax.ShapeDtypeStruct((N_TOK, D_MODEL), F32),
        input_output_aliases={6: 0},
        compiler_params=_cparams(("arbitrary", "arbitrary")),
        name="na_dec",
    )(na, na, na, ck, cv, bias, o)


def _mla_dec_kernel(mq_ref, ckv_ref, kpe_ref, cckv_ref, ckpe_ref, wkvb_ref, oin_ref, o_ref):
    del oin_ref
    mq = mq_ref[...]
    wkvb = wkvb_ref[...]
    keys = [(ckv_ref[...], kpe_ref[:, :MLA_ROPE_DIM]), (cckv_ref[...], ckpe_ref[...])]
    for h in range(MLA_HEADS):
        o_ref[:, h * MLA_V_DIM:(h + 1) * MLA_V_DIM] = _mla_head(h, mq, wkvb, keys)


def _mla_dec(l, mq, ckv, kpe, cckv, ckpe, w_kvb, o):
    return pl.pallas_call(
        _mla_dec_kernel,
        grid=(DEC_BATCH, DEC_TILES_PER_BATCH),
        in_specs=[pl.BlockSpec((TM, MQ_W), lambda b, j: (_dec_row(b, j), 0)),
                  pl.BlockSpec((DEC_SEQ, MLA_KV_RANK), lambda b, j: (N_CTX // DEC_SEQ + b, 0)),
                  pl.BlockSpec((DEC_SEQ, LANES), lambda b, j: (N_CTX // DEC_SEQ + b, 0)),
                  pl.BlockSpec((None, None, PAST_LEN, MLA_KV_RANK), lambda b, j: (b, l, 0, 0)),
                  pl.BlockSpec((None, None, PAST_LEN, MLA_ROPE_DIM), lambda b, j: (b, l, 0, 0)),
                  pl.BlockSpec((None, MLA_KV_RANK, MLA_HEADS * 2 * HEAD_DIM), lambda b, j: (l, 0, 0)),
                  pl.BlockSpec(memory_space=pl.ANY)],
        out_specs=pl.BlockSpec((TM, MLA_OUT), lambda b, j: (_dec_row(b, j), 0)),
        out_shape=jax.ShapeDtypeStruct((N_TOK, D_MODEL), F32),
        input_output_aliases={6: 0},
        compiler_params=_cparams(("arbitrary", "arbitrary")),
        name="mla_dec",
    )(mq, ckv, kpe, cckv, ckpe, w_kvb, o)


def _swa_dec_kernel(l, sink_ref, q_ref, k_ref, v_ref, ck_ref, cv_ref, oin_ref, o_ref):
    del oin_ref
    scale = HEAD_DIM ** -0.5
    j = pl.program_id(1)
    start = pl.multiple_of(jnp.clip(j * TM - SWA_WINDOW, 0, DEC_SEQ - SWA_KEYS), SWA_WINDOW)
    qpos = j * TM + lax.broadcasted_iota(jnp.int32, (TM, SWA_KEYS), 0)
    kpos = start + lax.broadcasted_iota(jnp.int32, (TM, SWA_KEYS), 1)
    band = jnp.abs(qpos - kpos) <= SWA_WINDOW
    for h in range(SWA_HEADS):
        kh = h // (SWA_HEADS // SWA_KV_HEADS)
        sl = slice(kh * HEAD_DIM, (kh + 1) * HEAD_DIM)
        q = q_ref[:, h * HEAD_DIM:(h + 1) * HEAD_DIM]
        s_loc = jnp.where(band, _dot_nt(q, k_ref[pl.ds(start, SWA_KEYS), sl]) * scale, NEG)
        s_ctx = _dot_nt(q, ck_ref[:, sl]) * scale
        (p_loc, p_ctx), inv = _softmax_parts([s_loc, s_ctx], sink_ref[l, h])
        o_ref[:, h * HEAD_DIM:(h + 1) * HEAD_DIM] = (
            _dot(p_loc, v_ref[pl.ds(start, SWA_KEYS), sl]) + _dot(p_ctx, cv_ref[:, sl])) * inv


def _swa_dec(l, sink, sw, ck, cv, o):
    return pl.pallas_call(
        functools.partial(_swa_dec_kernel, l),
        grid=(DEC_BATCH, DEC_TILES_PER_BATCH),
        in_specs=[pl.BlockSpec(memory_space=pltpu.SMEM),
                  pl.BlockSpec((TM, SWA_OUT), lambda b, j: (_dec_row(b, j), 0)),
                  pl.BlockSpec((DEC_SEQ, LANES), lambda b, j: (N_CTX // DEC_SEQ + b, SWA_OUT // LANES)),
                  pl.BlockSpec((DEC_SEQ, LANES), lambda b, j: (N_CTX // DEC_SEQ + b, SWA_OUT // LANES + 1)),
                  pl.BlockSpec((None, None, PAST_LEN, LANES), lambda b, j: (b, l, 0, 0)),
                  pl.BlockSpec((None, None, PAST_LEN, LANES), lambda b, j: (b, l, 0, 0)),
                  pl.BlockSpec(memory_space=pl.ANY)],
        out_specs=pl.BlockSpec((TM, SWA_OUT), lambda b, j: (_dec_row(b, j), 1)),
        out_shape=jax.ShapeDtypeStruct((N_TOK, D_MODEL), F32),
        input_output_aliases={6: 0},
        compiler_params=_cparams(("arbitrary", "arbitrary")),
        name="swa_dec",
    )(sink, sw, sw, sw, ck, cv, o)


def _route(sc, sel):
    rows = [sel[e:e + 1, :] for e in range(N_EXPERTS)]

    def beats(a, ia, b, ib):
        return (a > b) | ((a == b) & (ia < ib)) if ia < ib else (a > b)

    in_top = []
    gscore = []
    for g in range(N_EXPERT_GROUPS):
        mem = list(range(g * EXPERTS_PER_GROUP, (g + 1) * EXPERTS_PER_GROUP))
        acc = None
        for e in mem:
            rank = sum(beats(rows[o], o, rows[e], e).astype(jnp.int32) for o in mem if o != e)
            top = rank < 2
            in_top.append(top)
            term = jnp.where(top, rows[e], 0.0)
            acc = term if acc is None else acc + term
        gscore.append(acc)
    gates, chosen = [], []
    for g in range(N_EXPERT_GROUPS):
        lost = sum(beats(gscore[o], o, gscore[g], g).astype(jnp.int32) for o in range(N_EXPERT_GROUPS) if o != g)
        best = lost == 0
        for e in range(g * EXPERTS_PER_GROUP, (g + 1) * EXPERTS_PER_GROUP):
            pick = best & in_top[e]
            chosen.append(pick.astype(F32))
            gates.append(jnp.where(pick, sc[e:e + 1, :], 0.0))
    gate = jnp.concatenate(gates, axis=0)
    return gate / gate.sum(axis=0, keepdims=True), jnp.concatenate(chosen, axis=0)


def _post_kernel(x_ref, o_ref, mod_ref, wout_ref, g_ref, wr_ref, br_ref, y_ref, hx_ref, wobf):
    @pl.when(pl.program_id(0) == 0)
    def _():
        wobf[:MLA_OUT, :] = wout_ref[NA_OUT:NA_OUT + MLA_OUT, :].astype(BF16)
        wobf[MLA_OUT:MLA_OUT + SWA_OUT, :] = wout_ref[NA_OUT + MLA_OUT:, :].astype(BF16)
        wobf[MLA_OUT + SWA_OUT:, :] = wout_ref[:NA_OUT, :].astype(BF16)

    attn = jnp.dot(o_ref[...].astype(BF16), wobf[...], preferred_element_type=F32)
    y = x_ref[...] + mod_ref[2:3, :] * attn
    y_ref[...] = y
    h2 = _rms(y, g_ref[...]) * (1.0 + mod_ref[4:5, :]) + mod_ref[3:4, :]
    logits = lax.dot_general(wr_ref[...], h2, (((1,), (1,)), ((), ())),
                             precision=lax.Precision.HIGHEST, preferred_element_type=F32)
    sc = 1.0 / (1.0 + jnp.exp(-logits))
    gate, chosen = _route(sc, sc + br_ref[...])
    hx_ref[:, :D_MODEL] = h2
    hx_ref[:, D_MODEL:] = jnp.concatenate(
        [gate, chosen, jnp.zeros((LANES - 2 * N_EXPERTS, TM), F32)], axis=0).T


def _post(l, x, o, mods, w_out, g_ffn, w_router_t, b_router):
    tile = lambda w: pl.BlockSpec((TM, w), lambda i: (i, 0))
    return pl.pallas_call(
        _post_kernel,
        grid=(N_TILES,),
        in_specs=[tile(D_MODEL), tile(D_MODEL),
                  pl.BlockSpec((None, None, N_MOD, D_MODEL), lambda i: (l, _cond_row(i), 0, 0)),
                  pl.BlockSpec((None, D_MODEL, D_MODEL), lambda i: (l, 0, 0)),
                  pl.BlockSpec((None, 1, D_MODEL), lambda i: (l, 0, 0)),
                  pl.BlockSpec((N_EXPERTS, D_MODEL), lambda i: (0, 0)),
                  pl.BlockSpec((N_EXPERTS, 1), lambda i: (0, 0))],
        out_specs=[tile(D_MODEL), tile(HX)],
        out_shape=[jax.ShapeDtypeStruct((N_TOK, D_MODEL), F32), jax.ShapeDtypeStruct((N_TOK, HX), F32)],
        scratch_shapes=[pltpu.VMEM((D_MODEL, D_MODEL), BF16)],
        compiler_params=_cparams(("arbitrary",)),
        name="post_attn",
    )(x, o, mods, w_out, g_ffn.reshape(DEPTH, 1, D_MODEL), w_router_t, b_router.reshape(N_EXPERTS, 1))


def _plan(hx):
    i32 = jnp.int32
    sel = hx[:, D_MODEL + N_EXPERTS:D_MODEL + 2 * N_EXPERTS] > 0.5
    m = sel.astype(i32)
    cnt = m.sum(0)
    ntile = (cnt + TR - 1) // TR
    tile_end = jnp.cumsum(ntile)
    first_tile = tile_end - ntile
    seg = jnp.cumsum(cnt) - cnt
    pos = first_tile[None, :] * TR + jnp.cumsum(m, axis=0) - m
    pos0 = jnp.min(jnp.where(sel, pos, 2 ** 30), axis=1)
    pos1 = jnp.max(jnp.where(sel, pos, -1), axis=1)
    e_idx = jnp.arange(N_EXPERTS, dtype=i32)
    e0 = jnp.min(jnp.where(sel, e_idx, N_EXPERTS), axis=1)
    e1 = jnp.max(jnp.where(sel, e_idx, -1), axis=1)
    t_idx = jnp.arange(N_TOK, dtype=i32)
    stok = jnp.sort(jnp.concatenate([e0 * N_TOK + t_idx, e1 * N_TOK + t_idx])) % N_TOK
    j = jnp.arange(NT, dtype=i32)
    last = tile_end[-1] - 1
    texp = jnp.sum(jnp.minimum(j, last)[:, None] >= tile_end[None, :], axis=1).astype(i32)
    off = j - first_tile[texp]
    tbase = seg[texp] + off * TR
    tnv = jnp.where(j <= last, jnp.clip(cnt[texp] - off * TR, 0, TR), 0)
    return (jnp.concatenate([pos0, pos1]).astype(i32), stok.astype(i32), texp, tbase.astype(i32),
            tnv.astype(i32))


def _experts_kernel(stok, texp, tbase, tnv, hx_hbm, wg_ref, wu_ref, wd_ref, ys_ref, xbuf, sem, wgb, wub, wdb):
    t = pl.program_id(0)
    slot = t % 2

    def start(tt, sl):
        base = tbase[tt]

        def body(r, carry):
            tok = stok[jnp.minimum(base + r, N_ASSIGN - 1)]
            pltpu.make_async_copy(hx_hbm.at[pl.ds(tok, 1)], xbuf.at[sl, pl.ds(r, 1)], sem.at[sl]).start()
            return carry

        lax.fori_loop(0, TR, body, 0, unroll=8)

    @pl.when(t == 0)
    def _():
        start(0, 0)

    nxt = jnp.minimum(t + 1, NT - 1)

    @pl.when((t + 1 < NT) & (tnv[nxt] > 0))
    def _():
        start(nxt, 1 - slot)

    @pl.when(tnv[t] == 0)
    def _():
        ys_ref[...] = jnp.zeros_like(ys_ref)

    @pl.when(tnv[t] > 0)
    def _():
        e = texp[t]

        @pl.when((t == 0) | (e != texp[jnp.maximum(t - 1, 0)]))
        def _():
            wgb[...] = wg_ref[...].astype(BF16)
            wub[...] = wu_ref[...].astype(BF16)
            wdb[...] = wd_ref[...].astype(BF16)

        pltpu.make_async_copy(hx_hbm.at[pl.ds(0, TR)], xbuf.at[slot], sem.at[slot]).wait()
        x = xbuf[slot, :, :D_MODEL].astype(BF16)
        ext = xbuf[slot, :, D_MODEL:]
        lane = lax.broadcasted_iota(jnp.int32, (TR, LANES), 1)
        row = lax.broadcasted_iota(jnp.int32, (TR, 1), 0)
        gcol = jnp.sum(jnp.where(lane == e, ext, 0.0), axis=1, keepdims=True)
        gcol = jnp.where(row < tnv[t], gcol, 0.0)
        hg = jnp.dot(x, wgb[...], preferred_element_type=F32)
        hu = jnp.dot(x, wub[...], preferred_element_type=F32)
        a = hg * (1.0 / (1.0 + jnp.exp(-hg))) * hu * gcol
        ys_ref[...] = jnp.dot(a.astype(BF16), wdb[...], preferred_element_type=F32)


def _experts(l, plan, hx, w_gate, w_up, w_down):
    _, stok, texp, tbase, tnv = plan
    wspec = lambda a, b: pl.BlockSpec((None, None, a, b), lambda t, stok, texp, tbase, tnv: (l, texp[t], 0, 0))
    return pl.pallas_call(
        _experts_kernel,
        grid_spec=pltpu.PrefetchScalarGridSpec(
            num_scalar_prefetch=4,
            grid=(NT,),
            in_specs=[pl.BlockSpec(memory_space=pl.ANY), wspec(D_MODEL, D_EXPERT), wspec(D_MODEL, D_EXPERT),
                      wspec(D_EXPERT, D_MODEL)],
            out_specs=pl.BlockSpec((TR, D_MODEL), lambda t, *_: (t, 0)),
            scratch_shapes=[pltpu.VMEM((2, TR, HX), F32), pltpu.SemaphoreType.DMA((2,)),
                            pltpu.VMEM((D_MODEL, D_EXPERT), BF16), pltpu.VMEM((D_MODEL, D_EXPERT), BF16),
                            pltpu.VMEM((D_EXPERT, D_MODEL), BF16)]),
        out_shape=jax.ShapeDtypeStruct((NT * TR, D_MODEL), F32),
        compiler_params=_cparams(("arbitrary",)),
        name="experts",
    )(stok, texp, tbase, tnv, hx, w_gate, w_up, w_down)


def _final_kernel(pos_ref, y_ref, ys_hbm, mod_ref, g_ref, o_ref, cbuf, sem):
    o_ref[...] = _rms(y_ref[...] + mod_ref[5:6, :] * _combined_moe(pos_ref, ys_hbm, cbuf, sem), g_ref[...])


def _final(pos, y, ys, mods, g_final):
    tile = pl.BlockSpec((TM, D_MODEL), lambda i, *_: (i, 0))
    return pl.pallas_call(
        _final_kernel,
        grid_spec=pltpu.PrefetchScalarGridSpec(
            num_scalar_prefetch=1,
            grid=(N_TILES,),
            in_specs=[tile, pl.BlockSpec(memory_space=pl.ANY),
                      pl.BlockSpec((None, None, N_MOD, D_MODEL), lambda i, *_: (DEPTH - 1, _cond_row(i), 0, 0)),
                      pl.BlockSpec((1, D_MODEL), lambda i, *_: (0, 0))],
            out_specs=tile,
            scratch_shapes=_combine_scratch()),
        out_shape=jax.ShapeDtypeStruct((N_TOK, D_MODEL), F32),
        compiler_params=_cparams(("arbitrary",)),
        name="final_norm",
    )(pos, y, ys, mods, g_final.reshape(1, D_MODEL))


def _rope_tables(rot_dim):
    t = jnp.arange(DEC_SEQ, dtype=jnp.int32)
    row = (t // GRID_W).astype(F32)
    col = (t % GRID_W).astype(F32)
    per_axis = rot_dim // 2
    inv = ROPE_BASE ** (-jnp.arange(0, per_axis, 2, dtype=F32) / per_axis)
    ang = jnp.concatenate([row[:, None] * inv, col[:, None] * inv], axis=-1)
    cos, sin = jnp.cos(ang), jnp.sin(ang)
    zero = jnp.zeros_like(sin)
    rep = LANES // rot_dim
    tabs = [jnp.concatenate([cos, cos], -1), jnp.concatenate([zero, sin], -1), jnp.concatenate([-sin, zero], -1)]
    return jnp.stack([jnp.tile(a, (1, rep)) for a in tabs])


def kernel(x_prompt, x_sample, c, cache_na_k, cache_na_v, cache_mla_ckv, cache_mla_kpe, cache_swa_k, cache_swa_v,
           c_ctx, w_ada, b_ada, g_attn, w_in, g_mla_q, w_mla_qb, g_mla_kv, w_mla_kvb, na_rpb, swa_sink, w_out,
           g_ffn, w_router, b_router, w_gate, w_up, w_down, g_final):
    cond = jnp.concatenate([c_ctx[None], c, jnp.zeros((COND_ROWS - 1 - DEC_BATCH, D_MODEL), F32)], axis=0)
    mods = _ada(cond, w_ada, b_ada).reshape(DEPTH, COND_ROWS, N_MOD, D_MODEL)
    bias = _na_bias(na_rpb)
    t_mla = _rope_tables(MLA_ROPE_DIM)
    t_swa = _rope_tables(HEAD_DIM)

    w_in_p = jnp.concatenate([w_in[..., :KPE_OFF + MLA_ROPE_DIM],
                              jnp.zeros((DEPTH, D_MODEL, LANES - MLA_ROPE_DIM), F32),
                              w_in[..., KPE_OFF + MLA_ROPE_DIM:]], axis=-1)
    wq = w_mla_qb.reshape(DEPTH, MLA_Q_RANK, MLA_HEADS, MLA_QK_DIM)
    w_qb_p = jnp.concatenate([wq[..., :MLA_NOPE_DIM].reshape(DEPTH, MLA_Q_RANK, MQ_NOPE),
                              wq[..., MLA_NOPE_DIM:].reshape(DEPTH, MLA_Q_RANK, MLA_HEADS * MLA_ROPE_DIM),
                              jnp.zeros((DEPTH, MLA_Q_RANK, MQ_W - MLA_HEADS * MLA_QK_DIM), F32)], axis=-1)
    w_router_t = w_router.T

    c_na_k = cache_na_k.reshape(DEC_BATCH, DEPTH, PAST_LEN, NA_OUT)
    c_na_v = cache_na_v.reshape(DEC_BATCH, DEPTH, PAST_LEN, NA_OUT)
    c_sw_k = cache_swa_k.reshape(DEC_BATCH, DEPTH, PAST_LEN, LANES)
    c_sw_v = cache_swa_v.reshape(DEC_BATCH, DEPTH, PAST_LEN, LANES)

    x = jnp.concatenate([x_prompt.reshape(N_CTX, D_MODEL), x_sample.reshape(N_DEC, D_MODEL)], axis=0)
    y = ys = plan = None
    caches = [[] for _ in range(6)]
    for l in range(DEPTH):
        if l == 0:
            na, mq, ckv, kpe, sw = _pre(True, l, [x], mods, g_attn, w_in_p, g_mla_q, w_qb_p, g_mla_kv, t_mla, t_swa)
        else:
            x, na, mq, ckv, kpe, sw = _pre(False, l, [plan[0], y, ys], mods, g_attn, w_in_p, g_mla_q, w_qb_p,
                                           g_mla_kv, t_mla, t_swa)
        o = _ctx_attn(l, swa_sink, na, mq, ckv, kpe, sw, w_mla_kvb)
        o = _mla_dec(l, mq, ckv, kpe, cache_mla_ckv, cache_mla_kpe, w_mla_kvb, o)
        o = _swa_dec(l, swa_sink, sw, c_sw_k, c_sw_v, o)
        o = _na_dec(l, na, c_na_k, c_na_v, bias, o)
        y, hx = _post(l, x, o, mods, w_out, g_ffn, w_router_t, b_router)
        plan = _plan(hx)
        ys = _experts(l, plan, hx, w_gate, w_up, w_down)
        caches[0].append(na[:N_CTX, NA_OUT:2 * NA_OUT])
        caches[1].append(na[:N_CTX, 2 * NA_OUT:])
        caches[2].append(ckv[:N_CTX])
        caches[3].append(kpe[:N_CTX, :MLA_ROPE_DIM])
        caches[4].append(sw[:N_CTX, SWA_OUT:SWA_OUT + LANES])
        caches[5].append(sw[:N_CTX, SWA_OUT + LANES:])
    out = _final(plan[0], y, ys, mods, g_final)

    def stack(parts, tail):
        return jnp.stack([p.reshape((BATCH, SEQ) + tail) for p in parts], axis=1)

    return (out[:N_CTX].reshape(BATCH, SEQ, D_MODEL), out[N_CTX:].reshape(DEC_BATCH, DEC_SEQ, D_MODEL),
            stack(caches[0], (NA_HEADS, HEAD_DIM)), stack(caches[1], (NA_HEADS, HEAD_DIM)),
            stack(caches[2], (MLA_KV_RANK,)), stack(caches[3], (MLA_ROPE_DIM,)),
            stack(caches[4], (SWA_KV_HEADS, HEAD_DIM)), stack(caches[5], (SWA_KV_HEADS, HEAD_DIM)))
```

```python
import functools

import jax
import jax.numpy as jnp
from jax import lax
from jax.experimental import pallas as pl
from jax.experimental.pallas import tpu as pltpu

D_MODEL = 1024
BATCH = 16
SEQ = 256
DEPTH = 4
DEC_BATCH = 2
DEC_SEQ = 2048
PAST_LEN = 256
GRID_W = 64
HEAD_DIM = 64
NA_HEADS = 4
NA_WIN_H = 8
NA_WIN_W = 16
MLA_HEADS = 6
MLA_Q_RANK = 256
MLA_KV_RANK = 128
MLA_NOPE_DIM = 64
MLA_ROPE_DIM = 32
MLA_V_DIM = 64
MLA_QK_DIM = MLA_NOPE_DIM + MLA_ROPE_DIM
SWA_HEADS = 6
SWA_KV_HEADS = 2
SWA_WINDOW = 128
ROPE_BASE = 10000.0
N_EXPERTS = 16
N_EXPERT_GROUPS = 4
EXPERTS_PER_GROUP = 4
D_EXPERT = 512
RMS_EPS = 1e-6
N_MOD = 6

NA_IN = 3 * NA_HEADS * HEAD_DIM
MLA_IN = MLA_Q_RANK + MLA_KV_RANK + MLA_ROPE_DIM
SWA_IN = (SWA_HEADS + 2 * SWA_KV_HEADS) * HEAD_DIM
IN_WIDTH = NA_IN + MLA_IN + SWA_IN
NA_OUT = NA_HEADS * HEAD_DIM
MLA_OUT = MLA_HEADS * MLA_V_DIM
SWA_OUT = SWA_HEADS * HEAD_DIM

LANES = 128
N_CTX = BATCH * SEQ
N_DEC = DEC_BATCH * DEC_SEQ
N_TOK = N_CTX + N_DEC
TM = 256
N_TILES = N_TOK // TM
CTX_TILES = N_CTX // TM
DEC_TILES_PER_BATCH = DEC_SEQ // TM
COND_ROWS = 8
KPE_OFF = NA_IN + MLA_Q_RANK + MLA_KV_RANK
SW_OFF = KPE_OFF + LANES
IN_PAD = SW_OFF + SWA_IN
MQ_NOPE = MLA_HEADS * MLA_NOPE_DIM
MQ_W = 640
NA_ROWS_PER_STEP = TM // GRID_W
NA_KEY_ROWS = 12
NA_KEYS = NA_KEY_ROWS * GRID_W
SWA_KEYS = 512
NEG = -1e30
TR = 256
N_ASSIGN = 2 * N_TOK
RUN_ALIGN = 16
WIN = 64
N_RUNS = N_TILES * N_EXPERTS
NT = -(-(N_ASSIGN + N_RUNS * (RUN_ALIGN - 1) + N_EXPERTS * (TR - 1) + WIN) // TR)

F32 = jnp.float32
BF16 = jnp.bfloat16
VMEM_LIMIT = 56 * 1024 * 1024


def _cparams(sem):
    return pltpu.CompilerParams(dimension_semantics=sem, vmem_limit_bytes=VMEM_LIMIT)


def _cond_row(i):
    return jnp.where(i < CTX_TILES, 0, 1 + (i - CTX_TILES) // DEC_TILES_PER_BATCH)


def _rope_blk(i):
    return jnp.where(i < CTX_TILES, 0, (i - CTX_TILES) % DEC_TILES_PER_BATCH)


def _rms(x, g):
    ms = jnp.mean(x * x, axis=-1, keepdims=True)
    return x * lax.rsqrt(ms + RMS_EPS) * g


def _dot(a, b):
    return jnp.dot(a.astype(BF16), b.astype(BF16), preferred_element_type=F32)


def _dot_nt(a, b):
    return lax.dot_general(a.astype(BF16), b.astype(BF16), (((1,), (1,)), ((), ())),
                           preferred_element_type=F32)


def _ada_kernel(c_ref, w_ref, b_ref, o_ref):
    c = c_ref[...]
    s = c * (1.0 / (1.0 + jnp.exp(-c)))
    o_ref[...] = _dot(s, w_ref[...]) + b_ref[...]


def _ada(cond, w_ada, b_ada):
    tn = 1536
    n = N_MOD * D_MODEL
    return pl.pallas_call(
        _ada_kernel,
        grid=(DEPTH, n // tn),
        in_specs=[pl.BlockSpec((COND_ROWS, D_MODEL), lambda l, j: (0, 0)),
                  pl.BlockSpec((None, D_MODEL, tn), lambda l, j: (l, 0, j)),
                  pl.BlockSpec((None, 1, tn), lambda l, j: (l, 0, j))],
        out_specs=pl.BlockSpec((None, COND_ROWS, tn), lambda l, j: (l, 0, j)),
        out_shape=jax.ShapeDtypeStruct((DEPTH, COND_ROWS, n), F32),
        compiler_params=_cparams(("arbitrary", "arbitrary")),
        name="ada_mod",
    )(cond, w_ada, b_ada.reshape(DEPTH, 1, n))


def _bias_kernel(rpb_ref, o_ref):
    g = pl.program_id(0)
    base = g * ((2 * NA_WIN_H - 1) * (2 * NA_WIN_W - 1))
    qc = lax.broadcasted_iota(jnp.int32, (GRID_W, GRID_W), 0)
    kc = lax.broadcasted_iota(jnp.int32, (GRID_W, GRID_W), 1)
    dc = jnp.clip(kc - qc + (NA_WIN_W - 1), 0, 2 * NA_WIN_W - 2)
    cs = jnp.clip(qc - NA_WIN_W // 2, 0, GRID_W - NA_WIN_W)
    col_ok = (kc >= cs) & (kc < cs + NA_WIN_W)
    neg = jnp.full((GRID_W, GRID_W), NEG, F32)
    tabs = []
    for a in range(2 * NA_WIN_H - 1):
        t = jnp.zeros((GRID_W, GRID_W), F32)
        for b in range(2 * NA_WIN_W - 1):
            t = jnp.where(dc == b, rpb_ref[base + a * (2 * NA_WIN_W - 1) + b], t)
        tabs.append(jnp.where(col_ok, t, NEG))
    for p in range(3):
        for qi in range(NA_ROWS_PER_STEP):
            for kj in range(NA_KEY_ROWS):
                if p == 0:
                    ok, dr = kj < NA_WIN_H, kj - qi + 7
                elif p == 1:
                    ok, dr = qi <= kj < qi + NA_WIN_H, kj - qi + 3
                else:
                    ok, dr = kj >= NA_KEY_ROWS - NA_WIN_H, kj - qi - 1
                blk = tabs[dr] if ok else neg
                o_ref[p, qi * GRID_W:(qi + 1) * GRID_W, kj * GRID_W:(kj + 1) * GRID_W] = blk


def _na_bias(na_rpb):
    return pl.pallas_call(
        _bias_kernel,
        grid=(DEPTH * NA_HEADS,),
        in_specs=[pl.BlockSpec(memory_space=pltpu.SMEM)],
        out_specs=pl.BlockSpec((None, 3, None, TM, NA_KEYS),
                               lambda g: (g // NA_HEADS, 0, g % NA_HEADS, 0, 0)),
        out_shape=jax.ShapeDtypeStruct((DEPTH, 3, NA_HEADS, TM, NA_KEYS), F32),
        compiler_params=_cparams(("arbitrary",)),
        name="na_bias",
    )(na_rpb.reshape(-1))


def _rope128(x, t_ref, half):
    return (x * t_ref[0] + pltpu.roll(x, half, 1) * t_ref[1]
            + pltpu.roll(x, LANES - half, 1) * t_ref[2])


def _run_ranks(sel_t):
    earlier = (lax.broadcasted_iota(jnp.int32, (TM, TM), 0) < lax.broadcasted_iota(jnp.int32, (TM, TM), 1))
    return jnp.dot(sel_t.astype(BF16), earlier.astype(BF16), preferred_element_type=F32)


def _window_matrix(rank_t, sel_t, val_t, p):
    jj = (lax.broadcasted_iota(jnp.int32, (WIN, TM), 0) + p * WIN).astype(F32)
    blocks = []
    for e in range(N_EXPERTS):
        hit = (rank_t[e:e + 1, :] == jj) & (sel_t[e:e + 1, :] > 0.5)
        blocks.append(jnp.where(hit, 1.0 if val_t is None else val_t[e:e + 1, :], 0.0))
    return jnp.concatenate(blocks, axis=0)


def _windows_start(i, slot, p, off_ref, ys_hbm, wbuf, sem):
    for e in range(N_EXPERTS):
        row = pl.multiple_of(off_ref[i * N_EXPERTS + e] + p * WIN, RUN_ALIGN)
        pltpu.make_async_copy(ys_hbm.at[pl.ds(row, WIN)], wbuf.at[slot, pl.ds(e * WIN, WIN)], sem.at[slot]).start()


def _windows_wait(slot, ys_hbm, wbuf, sem):
    pltpu.make_async_copy(ys_hbm.at[pl.ds(0, N_EXPERTS * WIN)], wbuf.at[slot], sem.at[slot]).wait()


def _combined_moe(off_ref, npass_ref, rt_ref, ys_hbm, wbuf, sem):
    i = pl.program_id(0)
    slot = i % 2

    @pl.when(i == 0)
    def _():
        _windows_start(0, 0, 0, off_ref, ys_hbm, wbuf, sem)

    @pl.when(i + 1 < N_TILES)
    def _():
        _windows_start(i + 1, 1 - slot, 0, off_ref, ys_hbm, wbuf, sem)

    gate_t = rt_ref[:N_EXPERTS, :]
    sel_t = rt_ref[N_EXPERTS:, :]
    rank_t = _run_ranks(sel_t)

    def contrib(p):
        g = _window_matrix(rank_t, sel_t, gate_t, p).T.astype(BF16)
        return jnp.dot(g, wbuf[slot], preferred_element_type=F32)

    _windows_wait(slot, ys_hbm, wbuf, sem)
    acc = contrib(0)

    def extra(p, acc):
        _windows_start(i, slot, p, off_ref, ys_hbm, wbuf, sem)
        _windows_wait(slot, ys_hbm, wbuf, sem)
        return acc + contrib(p)

    return lax.fori_loop(1, npass_ref[i], extra, acc)


def _pre_kernel(first, *refs):
    if first:
        (x_ref, mod_ref, g_ref, win_ref, gq_ref, wqb_ref, gkv_ref, tm_ref, ts_ref,
         na_ref, mq_ref, ckv_ref, kpe_ref, sw_ref, wbf, wqbf) = refs
        x = x_ref[...]
    else:
        (off_ref, npass_ref, y_ref, rt_ref, ys_hbm, modp_ref, mod_ref, g_ref, win_ref, gq_ref, wqb_ref, gkv_ref,
         tm_ref, ts_ref, xo_ref, na_ref, mq_ref, ckv_ref, kpe_ref, sw_ref, wbf, wqbf, wbuf, sem) = refs
        x = y_ref[...] + modp_ref[5:6, :] * _combined_moe(off_ref, npass_ref, rt_ref, ys_hbm, wbuf, sem)
        xo_ref[...] = x
    i = pl.program_id(0)

    @pl.when(i == 0)
    def _():
        wbf[...] = win_ref[...].astype(BF16)
        wqbf[...] = wqb_ref[...].astype(BF16)

    h = _rms(x, g_ref[...]) * (1.0 + mod_ref[1:2, :]) + mod_ref[0:1, :]
    z = jnp.dot(h.astype(BF16), wbf[...], preferred_element_type=F32)
    na_ref[...] = z[:, :NA_IN]
    cq = _rms(z[:, NA_IN:NA_IN + MLA_Q_RANK], gq_ref[...])
    ckv_ref[...] = _rms(z[:, NA_IN + MLA_Q_RANK:KPE_OFF], gkv_ref[...])
    mq = jnp.dot(cq.astype(BF16), wqbf[...], preferred_element_type=F32)
    kpe = z[:, KPE_OFF:SW_OFF]
    sw = z[:, SW_OFF:IN_PAD]
    mq_ref[:, :MQ_NOPE] = mq[:, :MQ_NOPE]
    sw_ref[:, SWA_OUT + LANES:] = sw[:, SWA_OUT + LANES:]

    @pl.when(i < CTX_TILES)
    def _():
        mq_ref[:, MQ_NOPE:] = mq[:, MQ_NOPE:]
        kpe_ref[...] = kpe
        sw_ref[:, :SWA_OUT + LANES] = sw[:, :SWA_OUT + LANES]

    @pl.when(i >= CTX_TILES)
    def _():
        for c in range(MQ_NOPE // LANES, MQ_W // LANES):
            mq_ref[:, c * LANES:(c + 1) * LANES] = _rope128(mq[:, c * LANES:(c + 1) * LANES], tm_ref,
                                                            MLA_ROPE_DIM // 2)
        kpe_ref[...] = _rope128(kpe, tm_ref, MLA_ROPE_DIM // 2)
        for c in range((SWA_OUT + LANES) // LANES):
            sw_ref[:, c * LANES:(c + 1) * LANES] = _rope128(sw[:, c * LANES:(c + 1) * LANES], ts_ref,
                                                            HEAD_DIM // 2)


def _combine_scratch():
    return [pltpu.VMEM((2, N_EXPERTS * WIN, D_MODEL), BF16), pltpu.SemaphoreType.DMA((2,))]


def _combine_specs():
    return [pl.BlockSpec((TM, D_MODEL), lambda i, *_: (i, 0)),
            pl.BlockSpec((2 * N_EXPERTS, TM), lambda i, *_: (0, i)),
            pl.BlockSpec(memory_space=pl.ANY)]


def _pre(first, l, xs, mods, g_attn, w_in_p, g_mla_q, w_qb_p, g_mla_kv, t_mla, t_swa):
    tile = lambda w: pl.BlockSpec((TM, w), lambda i, *_: (i, 0))
    mod_spec = lambda ll: pl.BlockSpec((None, None, N_MOD, D_MODEL), lambda i, *_: (ll, _cond_row(i), 0, 0))
    vec = lambda w: pl.BlockSpec((None, 1, w), lambda i, *_: (l, 0, 0))
    in_specs = [tile(D_MODEL)] if first else _combine_specs() + [mod_spec(l - 1)]
    in_specs += [mod_spec(l), vec(D_MODEL),
                 pl.BlockSpec((None, D_MODEL, IN_PAD), lambda i, *_: (l, 0, 0)),
                 vec(MLA_Q_RANK),
                 pl.BlockSpec((None, MLA_Q_RANK, MQ_W), lambda i, *_: (l, 0, 0)),
                 vec(MLA_KV_RANK),
                 pl.BlockSpec((3, TM, LANES), lambda i, *_: (0, _rope_blk(i), 0)),
                 pl.BlockSpec((3, TM, LANES), lambda i, *_: (0, _rope_blk(i), 0))]
    widths = ([] if first else [D_MODEL]) + [NA_IN, MQ_W, MLA_KV_RANK, LANES, SWA_IN]
    args = list(xs) + ([mods] if not first else []) + [
        mods, g_attn.reshape(DEPTH, 1, D_MODEL), w_in_p, g_mla_q.reshape(DEPTH, 1, MLA_Q_RANK), w_qb_p,
        g_mla_kv.reshape(DEPTH, 1, MLA_KV_RANK), t_mla, t_swa]
    scratch = [pltpu.VMEM((D_MODEL, IN_PAD), BF16), pltpu.VMEM((MLA_Q_RANK, MQ_W), BF16)]
    return pl.pallas_call(
        functools.partial(_pre_kernel, first),
        grid_spec=pltpu.PrefetchScalarGridSpec(
            num_scalar_prefetch=0 if first else 2,
            grid=(N_TILES,),
            in_specs=in_specs,
            out_specs=[tile(w) for w in widths],
            scratch_shapes=scratch + ([] if first else _combine_scratch())),
        out_shape=[jax.ShapeDtypeStruct((N_TOK, w), F32) for w in widths],
        compiler_params=_cparams(("arbitrary",)),
        name="pre_attn",
    )(*args)


def _softmax_parts(parts, sink=None):
    m = parts[0].max(axis=-1, keepdims=True)
    for s in parts[1:]:
        m = jnp.maximum(m, s.max(axis=-1, keepdims=True))
    if sink is not None:
        m = jnp.maximum(m, sink)
    ps = [jnp.exp(s - m) for s in parts]
    den = ps[0].sum(axis=-1, keepdims=True)
    for p in ps[1:]:
        den = den + p.sum(axis=-1, keepdims=True)
    if sink is not None:
        den = den + jnp.exp(sink - m)
    return ps, 1.0 / den


def _mla_head(h, mq, wkvb, keys):
    scale = MLA_QK_DIM ** -0.5
    wk = wkvb[:, h * 2 * HEAD_DIM:h * 2 * HEAD_DIM + MLA_NOPE_DIM]
    wv = wkvb[:, h * 2 * HEAD_DIM + MLA_NOPE_DIM:(h + 1) * 2 * HEAD_DIM]
    qa = _dot_nt(mq[:, h * MLA_NOPE_DIM:(h + 1) * MLA_NOPE_DIM], wk)
    qr = mq[:, MQ_NOPE + h * MLA_ROPE_DIM:MQ_NOPE + (h + 1) * MLA_ROPE_DIM]
    parts = [(_dot_nt(qa, ckv) + _dot_nt(qr, kpe)) * scale for ckv, kpe in keys]
    ps, inv = _softmax_parts(parts)
    lat = _dot(ps[0], keys[0][0])
    for p, (ckv, _) in zip(ps[1:], keys[1:]):
        lat = lat + _dot(p, ckv)
    return _dot(lat * inv, wv)


def _ctx_attn_kernel(l, sink_ref, na_ref, mq_ref, ckv_ref, kpe_ref, sw_ref, wkvb_ref, o_ref):
    @pl.when(pl.program_id(0) >= BATCH)
    def _():
        o_ref[...] = jnp.zeros_like(o_ref)

    @pl.when(pl.program_id(0) < BATCH)
    def _():
        _ctx_attn_body(l, sink_ref, na_ref, mq_ref, ckv_ref, kpe_ref, sw_ref, wkvb_ref, o_ref)


def _ctx_attn_body(l, sink_ref, na_ref, mq_ref, ckv_ref, kpe_ref, sw_ref, wkvb_ref, o_ref):
    scale = HEAD_DIM ** -0.5
    mq = mq_ref[...]
    wkvb = wkvb_ref[...]
    keys = [(ckv_ref[...], kpe_ref[:, :MLA_ROPE_DIM])]
    for h in range(MLA_HEADS):
        o_ref[:, h * MLA_V_DIM:(h + 1) * MLA_V_DIM] = _mla_head(h, mq, wkvb, keys)
    for h in range(SWA_HEADS):
        kh = h // (SWA_HEADS // SWA_KV_HEADS)
        q = sw_ref[:, h * HEAD_DIM:(h + 1) * HEAD_DIM]
        k = sw_ref[:, SWA_OUT + kh * HEAD_DIM:SWA_OUT + (kh + 1) * HEAD_DIM]
        v = sw_ref[:, SWA_OUT + LANES + kh * HEAD_DIM:SWA_OUT + LANES + (kh + 1) * HEAD_DIM]
        (p,), inv = _softmax_parts([_dot_nt(q, k) * scale], sink_ref[l, h])
        o_ref[:, MLA_OUT + h * HEAD_DIM:MLA_OUT + (h + 1) * HEAD_DIM] = _dot(p, v) * inv
    for h in range(NA_HEADS):
        q = na_ref[:, h * HEAD_DIM:(h + 1) * HEAD_DIM]
        k = na_ref[:, NA_OUT + h * HEAD_DIM:NA_OUT + (h + 1) * HEAD_DIM]
        v = na_ref[:, 2 * NA_OUT + h * HEAD_DIM:2 * NA_OUT + (h + 1) * HEAD_DIM]
        (p,), inv = _softmax_parts([_dot_nt(q, k) * scale])
        o_ref[:, MLA_OUT + SWA_OUT + h * HEAD_DIM:MLA_OUT + SWA_OUT + (h + 1) * HEAD_DIM] = _dot(p, v) * inv


def _ctx_attn(l, sink, na, mq, ckv, kpe, sw, w_kvb):
    tile = lambda w: pl.BlockSpec((SEQ, w), lambda b: (jnp.minimum(b, BATCH - 1), 0))
    return pl.pallas_call(
        functools.partial(_ctx_attn_kernel, l),
        grid=(N_TOK // SEQ,),
        in_specs=[pl.BlockSpec(memory_space=pltpu.SMEM), tile(NA_IN), tile(MQ_W), tile(MLA_KV_RANK), tile(LANES),
                  tile(SWA_IN), pl.BlockSpec((None, MLA_KV_RANK, MLA_HEADS * 2 * HEAD_DIM), lambda b: (l, 0, 0))],
        out_specs=pl.BlockSpec((SEQ, D_MODEL), lambda b: (b, 0)),
        out_shape=jax.ShapeDtypeStruct((N_TOK, D_MODEL), F32),
        compiler_params=_cparams(("arbitrary",)),
        name="ctx_attn",
    )(sink, na, mq, ckv, kpe, sw, w_kvb)


def _dec_row(b, j):
    return CTX_TILES + b * DEC_TILES_PER_BATCH + j


def _na_dec_kernel(q_ref, k_ref, v_ref, ck_ref, cv_ref, bias_ref, oin_ref, o_ref):
    del oin_ref
    scale = HEAD_DIM ** -0.5
    j = pl.program_id(1)
    w0 = jnp.clip(j * NA_ROWS_PER_STEP - NA_WIN_H // 2, 0, DEC_SEQ // GRID_W - NA_KEY_ROWS)
    start = pl.multiple_of(w0 * GRID_W, GRID_W)
    for h in range(NA_HEADS):
        sl = slice(h * HEAD_DIM, (h + 1) * HEAD_DIM)
        q = q_ref[:, sl]
        k = k_ref[pl.ds(start, NA_KEYS), sl]
        v = v_ref[pl.ds(start, NA_KEYS), sl]
        s_nb = _dot_nt(q, k) * scale + bias_ref[h]
        s_ctx = _dot_nt(q, ck_ref[:, sl]) * scale
        (p_nb, p_ctx), inv = _softmax_parts([s_nb, s_ctx])
        o_ref[:, sl] = (_dot(p_nb, v) + _dot(p_ctx, cv_ref[:, sl])) * inv


def _na_dec(l, na, ck, cv, bias, o):
    pat = lambda j: jnp.where(j == 0, 0, jnp.where(j == DEC_TILES_PER_BATCH - 1, 2, 1))
    return pl.pallas_call(
        _na_dec_kernel,
        grid=(DEC_BATCH, DEC_TILES_PER_BATCH),
        in_specs=[pl.BlockSpec((TM, NA_OUT), lambda b, j: (_dec_row(b, j), 0)),
                  pl.BlockSpec((DEC_SEQ, NA_OUT), lambda b, j: (N_CTX // DEC_SEQ + b, 1)),
                  pl.BlockSpec((DEC_SEQ, NA_OUT), lambda b, j: (N_CTX // DEC_SEQ + b, 2)),
                  pl.BlockSpec((None, None, PAST_LEN, NA_OUT), lambda b, j: (b, l, 0, 0)),
                  pl.BlockSpec((None, None, PAST_LEN, NA_OUT), lambda b, j: (b, l, 0, 0)),
                  pl.BlockSpec((None, None, NA_HEADS, TM, NA_KEYS), lambda b, j: (l, pat(j), 0, 0, 0)),
                  pl.BlockSpec(memory_space=pl.ANY)],
        out_specs=pl.BlockSpec((TM, NA_OUT), lambda b, j: (_dec_row(b, j), (MLA_OUT + SWA_OUT) // NA_OUT)),
        out_shape=jax.ShapeDtypeStruct((N_TOK, D_MODEL), F32),
        input_output_aliases={6: 0},
        compiler_params=_cparams(("arbitrary", "arbitrary")),
        name="na_dec",
    )(na, na, na, ck, cv, bias, o)


def _mla_dec_kernel(mq_ref, ckv_ref, kpe_ref, cckv_ref, ckpe_ref, wkvb_ref, oin_ref, o_ref):
    del oin_ref
    mq = mq_ref[...]
    wkvb = wkvb_ref[...]
    keys = [(ckv_ref[...], kpe_ref[:, :MLA_ROPE_DIM]), (cckv_ref[...], ckpe_ref[...])]
    for h in range(MLA_HEADS):
        o_ref[:, h * MLA_V_DIM:(h + 1) * MLA_V_DIM] = _mla_head(h, mq, wkvb, keys)


def _mla_dec(l, mq, ckv, kpe, cckv, ckpe, w_kvb, o):
    return pl.pallas_call(
        _mla_dec_kernel,
        grid=(DEC_BATCH, DEC_TILES_PER_BATCH),
        in_specs=[pl.BlockSpec((TM, MQ_W), lambda b, j: (_dec_row(b, j), 0)),
                  pl.BlockSpec((DEC_SEQ, MLA_KV_RANK), lambda b, j: (N_CTX // DEC_SEQ + b, 0)),
                  pl.BlockSpec((DEC_SEQ, LANES), lambda b, j: (N_CTX // DEC_SEQ + b, 0)),
                  pl.BlockSpec((None, None, PAST_LEN, MLA_KV_RANK), lambda b, j: (b, l, 0, 0)),
                  pl.BlockSpec((None, None, PAST_LEN, MLA_ROPE_DIM), lambda b, j: (b, l, 0, 0)),
                  pl.BlockSpec((None, MLA_KV_RANK, MLA_HEADS * 2 * HEAD_DIM), lambda b, j: (l, 0, 0)),
                  pl.BlockSpec(memory_space=pl.ANY)],
        out_specs=pl.BlockSpec((TM, MLA_OUT), lambda b, j: (_dec_row(b, j), 0)),
        out_shape=jax.ShapeDtypeStruct((N_TOK, D_MODEL), F32),
        input_output_aliases={6: 0},
        compiler_params=_cparams(("arbitrary", "arbitrary")),
        name="mla_dec",
    )(mq, ckv, kpe, cckv, ckpe, w_kvb, o)


def _swa_dec_kernel(l, sink_ref, q_ref, k_ref, v_ref, ck_ref, cv_ref, oin_ref, o_ref):
    del oin_ref
    scale = HEAD_DIM ** -0.5
    j = pl.program_id(1)
    start = pl.multiple_of(jnp.clip(j * TM - SWA_WINDOW, 0, DEC_SEQ - SWA_KEYS), SWA_WINDOW)
    qpos = j * TM + lax.broadcasted_iota(jnp.int32, (TM, SWA_KEYS), 0)
    kpos = start + lax.broadcasted_iota(jnp.int32, (TM, SWA_KEYS), 1)
    band = jnp.abs(qpos - kpos) <= SWA_WINDOW
    for h in range(SWA_HEADS):
        kh = h // (SWA_HEADS // SWA_KV_HEADS)
        sl = slice(kh * HEAD_DIM, (kh + 1) * HEAD_DIM)
        q = q_ref[:, h * HEAD_DIM:(h + 1) * HEAD_DIM]
        s_loc = jnp.where(band, _dot_nt(q, k_ref[pl.ds(start, SWA_KEYS), sl]) * scale, NEG)
        s_ctx = _dot_nt(q, ck_ref[:, sl]) * scale
        (p_loc, p_ctx), inv = _softmax_parts([s_loc, s_ctx], sink_ref[l, h])
        o_ref[:, h * HEAD_DIM:(h + 1) * HEAD_DIM] = (
            _dot(p_loc, v_ref[pl.ds(start, SWA_KEYS), sl]) + _dot(p_ctx, cv_ref[:, sl])) * inv


def _swa_dec(l, sink, sw, ck, cv, o):
    return pl.pallas_call(
        functools.partial(_swa_dec_kernel, l),
        grid=(DEC_BATCH, DEC_TILES_PER_BATCH),
        in_specs=[pl.BlockSpec(memory_space=pltpu.SMEM),
                  pl.BlockSpec((TM, SWA_OUT), lambda b, j: (_dec_row(b, j), 0)),
                  pl.BlockSpec((DEC_SEQ, LANES), lambda b, j: (N_CTX // DEC_SEQ + b, SWA_OUT // LANES)),
                  pl.BlockSpec((DEC_SEQ, LANES), lambda b, j: (N_CTX // DEC_SEQ + b, SWA_OUT // LANES + 1)),
                  pl.BlockSpec((None, None, PAST_LEN, LANES), lambda b, j: (b, l, 0, 0)),
                  pl.BlockSpec((None, None, PAST_LEN, LANES), lambda b, j: (b, l, 0, 0)),
                  pl.BlockSpec(memory_space=pl.ANY)],
        out_specs=pl.BlockSpec((TM, SWA_OUT), lambda b, j: (_dec_row(b, j), 1)),
        out_shape=jax.ShapeDtypeStruct((N_TOK, D_MODEL), F32),
        input_output_aliases={6: 0},
        compiler_params=_cparams(("arbitrary", "arbitrary")),
        name="swa_dec",
    )(sink, sw, sw, sw, ck, cv, o)


def _route(sc, sel):
    rows = [sel[e:e + 1, :] for e in range(N_EXPERTS)]

    def beats(a, ia, b, ib):
        return (a > b) | ((a == b) & (ia < ib)) if ia < ib else (a > b)

    in_top = []
    gscore = []
    for g in range(N_EXPERT_GROUPS):
        mem = list(range(g * EXPERTS_PER_GROUP, (g + 1) * EXPERTS_PER_GROUP))
        acc = None
        for e in mem:
            rank = sum(beats(rows[o], o, rows[e], e).astype(jnp.int32) for o in mem if o != e)
            top = rank < 2
            in_top.append(top)
            term = jnp.where(top, rows[e], 0.0)
            acc = term if acc is None else acc + term
        gscore.append(acc)
    gates, chosen = [], []
    for g in range(N_EXPERT_GROUPS):
        lost = sum(beats(gscore[o], o, gscore[g], g).astype(jnp.int32) for o in range(N_EXPERT_GROUPS) if o != g)
        best = lost == 0
        for e in range(g * EXPERTS_PER_GROUP, (g + 1) * EXPERTS_PER_GROUP):
            pick = best & in_top[e]
            chosen.append(pick.astype(F32))
            gates.append(jnp.where(pick, sc[e:e + 1, :], 0.0))
    gate = jnp.concatenate(gates, axis=0)
    return gate / gate.sum(axis=0, keepdims=True), jnp.concatenate(chosen, axis=0)


def _post_kernel(x_ref, o_ref, mod_ref, wout_ref, g_ref, wr_ref, br_ref, y_ref, h2_ref, rt_ref, cnt_ref, wobf):
    @pl.when(pl.program_id(0) == 0)
    def _():
        wobf[:MLA_OUT, :] = wout_ref[NA_OUT:NA_OUT + MLA_OUT, :].astype(BF16)
        wobf[MLA_OUT:MLA_OUT + SWA_OUT, :] = wout_ref[NA_OUT + MLA_OUT:, :].astype(BF16)
        wobf[MLA_OUT + SWA_OUT:, :] = wout_ref[:NA_OUT, :].astype(BF16)

    attn = jnp.dot(o_ref[...].astype(BF16), wobf[...], preferred_element_type=F32)
    y = x_ref[...] + mod_ref[2:3, :] * attn
    y_ref[...] = y
    h2 = _rms(y, g_ref[...]) * (1.0 + mod_ref[4:5, :]) + mod_ref[3:4, :]
    logits = lax.dot_general(wr_ref[...], h2, (((1,), (1,)), ((), ())),
                             precision=lax.Precision.HIGHEST, preferred_element_type=F32)
    sc = 1.0 / (1.0 + jnp.exp(-logits))
    gate, chosen = _route(sc, sc + br_ref[...])
    h2_ref[...] = h2.astype(BF16)
    rt_ref[:N_EXPERTS, :] = gate
    rt_ref[N_EXPERTS:, :] = chosen
    cnt_ref[...] = jnp.broadcast_to(jnp.sum(chosen, axis=1, keepdims=True), (N_EXPERTS, LANES))


def _post(l, x, o, mods, w_out, g_ffn, w_router_t, b_router):
    tile = lambda w: pl.BlockSpec((TM, w), lambda i: (i, 0))
    return pl.pallas_call(
        _post_kernel,
        grid=(N_TILES,),
        in_specs=[tile(D_MODEL), tile(D_MODEL),
                  pl.BlockSpec((None, None, N_MOD, D_MODEL), lambda i: (l, _cond_row(i), 0, 0)),
                  pl.BlockSpec((None, D_MODEL, D_MODEL), lambda i: (l, 0, 0)),
                  pl.BlockSpec((None, 1, D_MODEL), lambda i: (l, 0, 0)),
                  pl.BlockSpec((N_EXPERTS, D_MODEL), lambda i: (0, 0)),
                  pl.BlockSpec((N_EXPERTS, 1), lambda i: (0, 0))],
        out_specs=[tile(D_MODEL), tile(D_MODEL), pl.BlockSpec((2 * N_EXPERTS, TM), lambda i: (0, i)),
                   pl.BlockSpec((None, N_EXPERTS, LANES), lambda i: (i, 0, 0))],
        out_shape=[jax.ShapeDtypeStruct((N_TOK, D_MODEL), F32), jax.ShapeDtypeStruct((N_TOK, D_MODEL), BF16),
                   jax.ShapeDtypeStruct((2 * N_EXPERTS, N_TOK), F32),
                   jax.ShapeDtypeStruct((N_TILES, N_EXPERTS, LANES), F32)],
        scratch_shapes=[pltpu.VMEM((D_MODEL, D_MODEL), BF16)],
        compiler_params=_cparams(("arbitrary",)),
        name="post_attn",
    )(x, o, mods, w_out, g_ffn.reshape(DEPTH, 1, D_MODEL), w_router_t, b_router.reshape(N_EXPERTS, 1))


def _plan(cnt_out):
    i32 = jnp.int32
    cnt = cnt_out[:, :, 0].astype(i32)
    plen = (cnt + RUN_ALIGN - 1) // RUN_ALIGN * RUN_ALIGN
    rows = plen.sum(0)
    ntile = (rows + TR - 1) // TR
    tile_end = jnp.cumsum(ntile)
    first_tile = tile_end - ntile
    off = first_tile[None, :] * TR + jnp.cumsum(plen, axis=0) - plen
    j = jnp.arange(NT, dtype=i32)
    last = tile_end[-1] - 1
    xidx = jnp.minimum(j, last)
    texp = jnp.sum(xidx[:, None] >= tile_end[None, :], axis=1)
    as_i32 = lambda a: a.astype(i32)
    return dict(off=as_i32(off.reshape(-1)), cnt=cnt.reshape(-1), npass=as_i32((cnt.max(axis=1) + WIN - 1) // WIN),
                xidx=as_i32(xidx), texp=as_i32(texp), live=as_i32(j <= last),
                tail0=as_i32(first_tile * TR + rows), tail1=as_i32(tile_end * TR), end=as_i32(tile_end[-1:]))


def _dispatch_kernel(off_ref, cnt_ref, npass_ref, tail0_ref, tail1_ref, end_ref, h_ref, rt_ref, xs_hbm,
                     zbuf, zeros, sem):
    b = pl.program_id(0)
    sel_t = rt_ref[N_EXPERTS:, :]
    rank_t = _run_ranks(sel_t)

    def run_copies(p, act):
        def per_expert(e, carry):
            left = cnt_ref[b * N_EXPERTS + e] - p * WIN
            row = off_ref[b * N_EXPERTS + e] + p * WIN
            for k in range(WIN // RUN_ALIGN):
                @pl.when(k * RUN_ALIGN < left)
                def _():
                    src = pl.multiple_of(e * WIN + k * RUN_ALIGN, RUN_ALIGN)
                    dst = pl.multiple_of(row + k * RUN_ALIGN, RUN_ALIGN)
                    act(pltpu.make_async_copy(zbuf.at[pl.ds(src, RUN_ALIGN)], xs_hbm.at[pl.ds(dst, RUN_ALIGN)], sem))
            return carry

        lax.fori_loop(0, N_EXPERTS, per_expert, 0)

    def one_pass(p, carry):
        pick = _window_matrix(rank_t, sel_t, None, p).astype(BF16)
        zbuf[...] = jnp.dot(pick, h_ref[...], preferred_element_type=F32).astype(BF16)
        run_copies(p, lambda cp: cp.start())
        run_copies(p, lambda cp: cp.wait())
        return carry

    lax.fori_loop(0, npass_ref[b], one_pass, 0)

    @pl.when(b == N_TILES - 1)
    def _():
        zeros[...] = jnp.zeros_like(zeros)

        def fill(act):
            def per_expert(e, carry):
                for k in range(TR // RUN_ALIGN):
                    row = pl.multiple_of(tail0_ref[e] + k * RUN_ALIGN, RUN_ALIGN)

                    @pl.when(row < tail1_ref[e])
                    def _():
                        act(pltpu.make_async_copy(zeros.at[pl.ds(0, RUN_ALIGN)], xs_hbm.at[pl.ds(row, RUN_ALIGN)], sem))
                return carry

            lax.fori_loop(0, N_EXPERTS, per_expert, 0)

            def per_tile(t, carry):
                act(pltpu.make_async_copy(zeros, xs_hbm.at[pl.ds(pl.multiple_of(t * TR, TR), TR)], sem))
                return carry

            lax.fori_loop(end_ref[0], NT, per_tile, 0)

        fill(lambda cp: cp.start())
        fill(lambda cp: cp.wait())


def _dispatch(plan, h2, rt):
    names = ("off", "cnt", "npass", "tail0", "tail1", "end")
    return pl.pallas_call(
        _dispatch_kernel,
        grid_spec=pltpu.PrefetchScalarGridSpec(
            num_scalar_prefetch=len(names),
            grid=(N_TILES,),
            in_specs=[pl.BlockSpec((TM, D_MODEL), lambda i, *_: (i, 0)),
                      pl.BlockSpec((2 * N_EXPERTS, TM), lambda i, *_: (0, i))],
            out_specs=pl.BlockSpec(memory_space=pl.ANY),
            scratch_shapes=[pltpu.VMEM((N_EXPERTS * WIN, D_MODEL), BF16), pltpu.VMEM((TR, D_MODEL), BF16),
                            pltpu.SemaphoreType.DMA(())]),
        out_shape=jax.ShapeDtypeStruct((NT * TR, D_MODEL), BF16),
        compiler_params=_cparams(("arbitrary",)),
        name="dispatch",
    )(*[plan[k] for k in names], h2, rt)


def _experts_kernel(xidx, texp, live, x_ref, wg_ref, wu_ref, wd_ref, ys_ref, wgb, wub, wdb):
    t = pl.program_id(0)

    @pl.when(live[t] == 0)
    def _():
        ys_ref[...] = jnp.zeros_like(ys_ref)

    @pl.when(live[t] > 0)
    def _():
        @pl.when((t == 0) | (texp[t] != texp[jnp.maximum(t - 1, 0)]))
        def _():
            wgb[...] = wg_ref[...].astype(BF16)
            wub[...] = wu_ref[...].astype(BF16)
            wdb[...] = wd_ref[...].astype(BF16)

        x = x_ref[...]
        hg = jnp.dot(x, wgb[...], preferred_element_type=F32)
        hu = jnp.dot(x, wub[...], preferred_element_type=F32)
        a = hg * (1.0 / (1.0 + jnp.exp(-hg))) * hu
        ys_ref[...] = jnp.dot(a.astype(BF16), wdb[...], preferred_element_type=F32).astype(BF16)


def _experts(l, plan, xs, w_gate, w_up, w_down):
    wspec = lambda a, b: pl.BlockSpec((None, None, a, b), lambda t, xidx, texp, live: (l, texp[t], 0, 0))
    return pl.pallas_call(
        _experts_kernel,
        grid_spec=pltpu.PrefetchScalarGridSpec(
            num_scalar_prefetch=3,
            grid=(NT,),
            in_specs=[pl.BlockSpec((TR, D_MODEL), lambda t, xidx, texp, live: (xidx[t], 0)),
                      wspec(D_MODEL, D_EXPERT), wspec(D_MODEL, D_EXPERT), wspec(D_EXPERT, D_MODEL)],
            out_specs=pl.BlockSpec((TR, D_MODEL), lambda t, *_: (t, 0)),
            scratch_shapes=[pltpu.VMEM((D_MODEL, D_EXPERT), BF16), pltpu.VMEM((D_MODEL, D_EXPERT), BF16),
                            pltpu.VMEM((D_EXPERT, D_MODEL), BF16)]),
        out_shape=jax.ShapeDtypeStruct((NT * TR, D_MODEL), BF16),
        compiler_params=_cparams(("arbitrary",)),
        name="experts",
    )(plan["xidx"], plan["texp"], plan["live"], xs, w_gate, w_up, w_down)


def _final_kernel(off_ref, npass_ref, y_ref, rt_ref, ys_hbm, mod_ref, g_ref, o_ref, wbuf, sem):
    moe = _combined_moe(off_ref, npass_ref, rt_ref, ys_hbm, wbuf, sem)
    o_ref[...] = _rms(y_ref[...] + mod_ref[5:6, :] * moe, g_ref[...])


def _final(plan, y, rt, ys, mods, g_final):
    return pl.pallas_call(
        _final_kernel,
        grid_spec=pltpu.PrefetchScalarGridSpec(
            num_scalar_prefetch=2,
            grid=(N_TILES,),
            in_specs=_combine_specs() + [
                pl.BlockSpec((None, None, N_MOD, D_MODEL), lambda i, *_: (DEPTH - 1, _cond_row(i), 0, 0)),
                pl.BlockSpec((1, D_MODEL), lambda i, *_: (0, 0))],
            out_specs=pl.BlockSpec((TM, D_MODEL), lambda i, *_: (i, 0)),
            scratch_shapes=_combine_scratch()),
        out_shape=jax.ShapeDtypeStruct((N_TOK, D_MODEL), F32),
        compiler_params=_cparams(("arbitrary",)),
        name="final_norm",
    )(plan["off"], plan["npass"], y, rt, ys, mods, g_final.reshape(1, D_MODEL))


def _rope_tables(rot_dim):
    t = jnp.arange(DEC_SEQ, dtype=jnp.int32)
    row = (t // GRID_W).astype(F32)
    col = (t % GRID_W).astype(F32)
    per_axis = rot_dim // 2
    inv = ROPE_BASE ** (-jnp.arange(0, per_axis, 2, dtype=F32) / per_axis)
    ang = jnp.concatenate([row[:, None] * inv, col[:, None] * inv], axis=-1)
    cos, sin = jnp.cos(ang), jnp.sin(ang)
    zero = jnp.zeros_like(sin)
    rep = LANES // rot_dim
    tabs = [jnp.concatenate([cos, cos], -1), jnp.concatenate([zero, sin], -1), jnp.concatenate([-sin, zero], -1)]
    return jnp.stack([jnp.tile(a, (1, rep)) for a in tabs])


def kernel(x_prompt, x_sample, c, cache_na_k, cache_na_v, cache_mla_ckv, cache_mla_kpe, cache_swa_k, cache_swa_v,
           c_ctx, w_ada, b_ada, g_attn, w_in, g_mla_q, w_mla_qb, g_mla_kv, w_mla_kvb, na_rpb, swa_sink, w_out,
           g_ffn, w_router, b_router, w_gate, w_up, w_down, g_final):
    cond = jnp.concatenate([c_ctx[None], c, jnp.zeros((COND_ROWS - 1 - DEC_BATCH, D_MODEL), F32)], axis=0)
    mods = _ada(cond, w_ada, b_ada).reshape(DEPTH, COND_ROWS, N_MOD, D_MODEL)
    bias = _na_bias(na_rpb)
    t_mla = _rope_tables(MLA_ROPE_DIM)
    t_swa = _rope_tables(HEAD_DIM)

    w_in_p = jnp.concatenate([w_in[..., :KPE_OFF + MLA_ROPE_DIM],
                              jnp.zeros((DEPTH, D_MODEL, LANES - MLA_ROPE_DIM), F32),
                              w_in[..., KPE_OFF + MLA_ROPE_DIM:]], axis=-1)
    wq = w_mla_qb.reshape(DEPTH, MLA_Q_RANK, MLA_HEADS, MLA_QK_DIM)
    w_qb_p = jnp.concatenate([wq[..., :MLA_NOPE_DIM].reshape(DEPTH, MLA_Q_RANK, MQ_NOPE),
                              wq[..., MLA_NOPE_DIM:].reshape(DEPTH, MLA_Q_RANK, MLA_HEADS * MLA_ROPE_DIM),
                              jnp.zeros((DEPTH, MLA_Q_RANK, MQ_W - MLA_HEADS * MLA_QK_DIM), F32)], axis=-1)
    w_router_t = w_router.T

    c_na_k = cache_na_k.reshape(DEC_BATCH, DEPTH, PAST_LEN, NA_OUT)
    c_na_v = cache_na_v.reshape(DEC_BATCH, DEPTH, PAST_LEN, NA_OUT)
    c_sw_k = cache_swa_k.reshape(DEC_BATCH, DEPTH, PAST_LEN, LANES)
    c_sw_v = cache_swa_v.reshape(DEC_BATCH, DEPTH, PAST_LEN, LANES)

    x = jnp.concatenate([x_prompt.reshape(N_CTX, D_MODEL), x_sample.reshape(N_DEC, D_MODEL)], axis=0)
    y = ys = rt = plan = None
    caches = [[] for _ in range(6)]
    for l in range(DEPTH):
        if l == 0:
            na, mq, ckv, kpe, sw = _pre(True, l, [x], mods, g_attn, w_in_p, g_mla_q, w_qb_p, g_mla_kv, t_mla, t_swa)
        else:
            x, na, mq, ckv, kpe, sw = _pre(False, l, [plan["off"], plan["npass"], y, rt, ys], mods, g_attn, w_in_p,
                                           g_mla_q, w_qb_p, g_mla_kv, t_mla, t_swa)
        o = _ctx_attn(l, swa_sink, na, mq, ckv, kpe, sw, w_mla_kvb)
        o = _mla_dec(l, mq, ckv, kpe, cache_mla_ckv, cache_mla_kpe, w_mla_kvb, o)
        o = _swa_dec(l, swa_sink, sw, c_sw_k, c_sw_v, o)
        o = _na_dec(l, na, c_na_k, c_na_v, bias, o)
        y, h2, rt, cnt = _post(l, x, o, mods, w_out, g_ffn, w_router_t, b_router)
        plan = _plan(cnt)
        ys = _experts(l, plan, _dispatch(plan, h2, rt), w_gate, w_up, w_down)
        caches[0].append(na[:N_CTX, NA_OUT:2 * NA_OUT])
        caches[1].append(na[:N_CTX, 2 * NA_OUT:])
        caches[2].append(ckv[:N_CTX])
        caches[3].append(kpe[:N_CTX, :MLA_ROPE_DIM])
        caches[4].append(sw[:N_CTX, SWA_OUT:SWA_OUT + LANES])
        caches[5].append(sw[:N_CTX, SWA_OUT + LANES:])
    out = _final(plan, y, rt, ys, mods, g_final)

    def stack(parts, tail):
        return jnp.stack([p.reshape((BATCH, SEQ) + tail) for p in parts], axis=1)

    return (out[:N_CTX].reshape(BATCH, SEQ, D_MODEL), out[N_CTX:].reshape(DEC_BATCH, DEC_SEQ, D_MODEL),
            stack(caches[0], (NA_HEADS, HEAD_DIM)), stack(caches[1], (NA_HEADS, HEAD_DIM)),
            stack(caches[2], (MLA_KV_RANK,)), stack(caches[3], (MLA_ROPE_DIM,)),
            stack(caches[4], (SWA_KV_HEADS, HEAD_DIM)), stack(caches[5], (SWA_KV_HEADS, HEAD_DIM)))
```

```python
import functools

import jax
import jax.numpy as jnp
from jax import lax
from jax.experimental import pallas as pl
from jax.experimental.pallas import tpu as pltpu

D_MODEL = 1024
BATCH = 16
SEQ = 256
DEPTH = 4
DEC_BATCH = 2
DEC_SEQ = 2048
PAST_LEN = 256
GRID_W = 64
HEAD_DIM = 64
NA_HEADS = 4
NA_WIN_H = 8
NA_WIN_W = 16
MLA_HEADS = 6
MLA_Q_RANK = 256
MLA_KV_RANK = 128
MLA_NOPE_DIM = 64
MLA_ROPE_DIM = 32
MLA_V_DIM = 64
MLA_QK_DIM = MLA_NOPE_DIM + MLA_ROPE_DIM
SWA_HEADS = 6
SWA_KV_HEADS = 2
SWA_WINDOW = 128
ROPE_BASE = 10000.0
N_EXPERTS = 16
N_EXPERT_GROUPS = 4
EXPERTS_PER_GROUP = 4
D_EXPERT = 512
RMS_EPS = 1e-6
N_MOD = 6

NA_IN = 3 * NA_HEADS * HEAD_DIM
MLA_IN = MLA_Q_RANK + MLA_KV_RANK + MLA_ROPE_DIM
SWA_IN = (SWA_HEADS + 2 * SWA_KV_HEADS) * HEAD_DIM
IN_WIDTH = NA_IN + MLA_IN + SWA_IN
NA_OUT = NA_HEADS * HEAD_DIM
MLA_OUT = MLA_HEADS * MLA_V_DIM
SWA_OUT = SWA_HEADS * HEAD_DIM

LANES = 128
N_CTX = BATCH * SEQ
N_DEC = DEC_BATCH * DEC_SEQ
N_TOK = N_CTX + N_DEC
TM = 256
N_TILES = N_TOK // TM
CTX_TILES = N_CTX // TM
DEC_TILES_PER_BATCH = DEC_SEQ // TM
COND_ROWS = 8
KPE_OFF = NA_IN + MLA_Q_RANK + MLA_KV_RANK
SW_OFF = KPE_OFF + LANES
IN_PAD = SW_OFF + SWA_IN
MQ_NOPE = MLA_HEADS * MLA_NOPE_DIM
MQ_W = 640
NA_ROWS_PER_STEP = TM // GRID_W
NA_KEY_ROWS = 12
NA_KEYS = NA_KEY_ROWS * GRID_W
SWA_KEYS = 512
NEG = -1e30
TR = 512
N_ASSIGN = 2 * N_TOK
RUN_ALIGN = 16
WIN = 64
N_RUNS = N_TILES * N_EXPERTS
NT = -(-(N_ASSIGN + N_RUNS * (RUN_ALIGN - 1) + N_EXPERTS * (TR - 1) + WIN) // TR)

F32 = jnp.float32
BF16 = jnp.bfloat16
VMEM_LIMIT = 56 * 1024 * 1024


def _cparams(sem):
    return pltpu.CompilerParams(dimension_semantics=sem, vmem_limit_bytes=VMEM_LIMIT)


def _cond_row(i):
    return jnp.where(i < CTX_TILES, 0, 1 + (i - CTX_TILES) // DEC_TILES_PER_BATCH)


def _rope_blk(i):
    return jnp.where(i < CTX_TILES, 0, (i - CTX_TILES) % DEC_TILES_PER_BATCH)


def _rms(x, g):
    ms = jnp.mean(x * x, axis=-1, keepdims=True)
    return x * lax.rsqrt(ms + RMS_EPS) * g


def _dot(a, b):
    return jnp.dot(a.astype(BF16), b.astype(BF16), preferred_element_type=F32)


def _dot_nt(a, b):
    return lax.dot_general(a.astype(BF16), b.astype(BF16), (((1,), (1,)), ((), ())),
                           preferred_element_type=F32)


def _ada_kernel(c_ref, w_ref, b_ref, o_ref):
    c = c_ref[...]
    s = c * (1.0 / (1.0 + jnp.exp(-c)))
    o_ref[...] = _dot(s, w_ref[...]) + b_ref[...]


def _ada(cond, w_ada, b_ada):
    tn = 1536
    n = N_MOD * D_MODEL
    return pl.pallas_call(
        _ada_kernel,
        grid=(DEPTH, n // tn),
        in_specs=[pl.BlockSpec((COND_ROWS, D_MODEL), lambda l, j: (0, 0)),
                  pl.BlockSpec((None, D_MODEL, tn), lambda l, j: (l, 0, j)),
                  pl.BlockSpec((None, 1, tn), lambda l, j: (l, 0, j))],
        out_specs=pl.BlockSpec((None, COND_ROWS, tn), lambda l, j: (l, 0, j)),
        out_shape=jax.ShapeDtypeStruct((DEPTH, COND_ROWS, n), F32),
        compiler_params=_cparams(("arbitrary", "arbitrary")),
        name="ada_mod",
    )(cond, w_ada, b_ada.reshape(DEPTH, 1, n))


def _bias_kernel(rpb_ref, o_ref):
    g = pl.program_id(0)
    base = g * ((2 * NA_WIN_H - 1) * (2 * NA_WIN_W - 1))
    qc = lax.broadcasted_iota(jnp.int32, (GRID_W, GRID_W), 0)
    kc = lax.broadcasted_iota(jnp.int32, (GRID_W, GRID_W), 1)
    dc = jnp.clip(kc - qc + (NA_WIN_W - 1), 0, 2 * NA_WIN_W - 2)
    cs = jnp.clip(qc - NA_WIN_W // 2, 0, GRID_W - NA_WIN_W)
    col_ok = (kc >= cs) & (kc < cs + NA_WIN_W)
    neg = jnp.full((GRID_W, GRID_W), NEG, F32)
    tabs = []
    for a in range(2 * NA_WIN_H - 1):
        t = jnp.zeros((GRID_W, GRID_W), F32)
        for b in range(2 * NA_WIN_W - 1):
            t = jnp.where(dc == b, rpb_ref[base + a * (2 * NA_WIN_W - 1) + b], t)
        tabs.append(jnp.where(col_ok, t, NEG))
    for p in range(3):
        for qi in range(NA_ROWS_PER_STEP):
            for kj in range(NA_KEY_ROWS):
                if p == 0:
                    ok, dr = kj < NA_WIN_H, kj - qi + 7
                elif p == 1:
                    ok, dr = qi <= kj < qi + NA_WIN_H, kj - qi + 3
                else:
                    ok, dr = kj >= NA_KEY_ROWS - NA_WIN_H, kj - qi - 1
                blk = tabs[dr] if ok else neg
                o_ref[p, qi * GRID_W:(qi + 1) * GRID_W, kj * GRID_W:(kj + 1) * GRID_W] = blk


def _na_bias(na_rpb):
    return pl.pallas_call(
        _bias_kernel,
        grid=(DEPTH * NA_HEADS,),
        in_specs=[pl.BlockSpec(memory_space=pltpu.SMEM)],
        out_specs=pl.BlockSpec((None, 3, None, TM, NA_KEYS),
                               lambda g: (g // NA_HEADS, 0, g % NA_HEADS, 0, 0)),
        out_shape=jax.ShapeDtypeStruct((DEPTH, 3, NA_HEADS, TM, NA_KEYS), F32),
        compiler_params=_cparams(("arbitrary",)),
        name="na_bias",
    )(na_rpb.reshape(-1))


def _rope128(x, t_ref, half):
    return (x * t_ref[0] + pltpu.roll(x, half, 1) * t_ref[1]
            + pltpu.roll(x, LANES - half, 1) * t_ref[2])


def _spread_consts():
    slot_e = jnp.arange(N_EXPERTS * WIN, dtype=jnp.int32) // WIN
    idx32 = jnp.arange(2 * N_EXPERTS, dtype=jnp.int32)
    idx128 = jnp.arange(LANES, dtype=jnp.int32)
    et = (idx32[None, :] == slot_e[:, None] + N_EXPERTS).astype(BF16)
    ek = (idx128[:, None] == slot_e[None, :] + N_EXPERTS).astype(BF16)
    eg = (idx128[:, None] == slot_e[None, :]).astype(BF16)
    return et, ek, eg


def _run_keys(chosen, token_axis):
    a = lax.broadcasted_iota(jnp.int32, (TM, TM), 0)
    b = lax.broadcasted_iota(jnp.int32, (TM, TM), 1)
    if token_axis == 1:
        rank = jnp.dot(chosen.astype(BF16), (a < b).astype(BF16), preferred_element_type=F32)
    else:
        rank = jnp.dot((b < a).astype(BF16), chosen.astype(BF16), preferred_element_type=F32)
    return jnp.where(chosen > 0.5, rank, -1.0)


def _windows_start(i, slot, p, off_ref, ys_hbm, wbuf, sem):
    for e in range(N_EXPERTS):
        row = pl.multiple_of(off_ref[i * N_EXPERTS + e] + p * WIN, RUN_ALIGN)
        pltpu.make_async_copy(ys_hbm.at[pl.ds(row, WIN)], wbuf.at[slot, pl.ds(e * WIN, WIN)], sem.at[slot]).start()


def _windows_wait(slot, ys_hbm, wbuf, sem):
    pltpu.make_async_copy(ys_hbm.at[pl.ds(0, N_EXPERTS * WIN)], wbuf.at[slot], sem.at[slot]).wait()


def _combined_moe(off_ref, npass_ref, rtm_ref, ek_ref, eg_ref, ys_hbm, wbuf, sem):
    i = pl.program_id(0)
    slot = i % 2

    @pl.when(i == 0)
    def _():
        _windows_start(0, 0, 0, off_ref, ys_hbm, wbuf, sem)

    @pl.when(i + 1 < N_TILES)
    def _():
        _windows_start(i + 1, 1 - slot, 0, off_ref, ys_hbm, wbuf, sem)

    r = rtm_ref[...]
    key = jnp.dot(_run_keys(r, 0).astype(BF16), ek_ref[...], preferred_element_type=F32)
    gate = jnp.dot(r.astype(BF16), eg_ref[...], preferred_element_type=F32)
    slot_j = lax.broadcasted_iota(jnp.int32, (TM, N_EXPERTS * WIN), 1) % WIN

    def contrib(p):
        g = jnp.where(key == (slot_j + p * WIN).astype(F32), gate, 0.0).astype(BF16)
        return jnp.dot(g, wbuf[slot], preferred_element_type=F32)

    _windows_wait(slot, ys_hbm, wbuf, sem)
    acc = contrib(0)

    def extra(p, acc):
        _windows_start(i, slot, p, off_ref, ys_hbm, wbuf, sem)
        _windows_wait(slot, ys_hbm, wbuf, sem)
        return acc + contrib(p)

    return lax.fori_loop(1, npass_ref[i], extra, acc)


def _pre_kernel(first, *refs):
    i = pl.program_id(0)
    if first:
        (xp_ref, xs_ref, mod_ref, g_ref, win_ref, gq_ref, wqb_ref, gkv_ref, tm_ref, ts_ref,
         xo_ref, na_ref, mq_ref, ckv_ref, kpe_ref, sw_ref, *cache_refs) = refs
        x = jnp.where(i < CTX_TILES, xp_ref[...], xs_ref[...])
    else:
        (off_ref, npass_ref, y_ref, rtm_ref, ek_ref, eg_ref, ys_hbm, modp_ref, mod_ref, g_ref, win_ref, gq_ref,
         wqb_ref, gkv_ref, tm_ref, ts_ref, xo_ref, na_ref, mq_ref, ckv_ref, kpe_ref, sw_ref, *rest) = refs
        *cache_refs, wbuf, sem = rest
        moe = _combined_moe(off_ref, npass_ref, rtm_ref, ek_ref, eg_ref, ys_hbm, wbuf, sem)
        x = y_ref[...] + modp_ref[5:6, :] * moe
    xo_ref[...] = x
    cnk_ref, cnv_ref, cckv_ref, ckpe_ref, csk_ref, csv_ref = cache_refs

    h = _rms(x, g_ref[...]) * (1.0 + mod_ref[1:2, :]) + mod_ref[0:1, :]
    z = jnp.dot(h.astype(BF16), win_ref[...], preferred_element_type=F32)
    na_ref[...] = z[:, :NA_IN]
    cq = _rms(z[:, NA_IN:NA_IN + MLA_Q_RANK], gq_ref[...])
    ckv_ref[...] = _rms(z[:, NA_IN + MLA_Q_RANK:KPE_OFF], gkv_ref[...])
    mq = jnp.dot(cq.astype(BF16), wqb_ref[...], preferred_element_type=F32)
    kpe = z[:, KPE_OFF:SW_OFF]
    sw = z[:, SW_OFF:IN_PAD]
    mq_ref[:, :MQ_NOPE] = mq[:, :MQ_NOPE]
    sw_ref[:, SWA_OUT + LANES:] = sw[:, SWA_OUT + LANES:]

    @pl.when(i < CTX_TILES)
    def _():
        mq_ref[:, MQ_NOPE:] = mq[:, MQ_NOPE:]
        kpe_ref[...] = kpe
        sw_ref[:, :SWA_OUT + LANES] = sw[:, :SWA_OUT + LANES]
        cnk_ref[...] = z[:, NA_OUT:2 * NA_OUT]
        cnv_ref[...] = z[:, 2 * NA_OUT:NA_IN]
        cckv_ref[...] = ckv_ref[...]
        ckpe_ref[...] = kpe[:, :MLA_ROPE_DIM]
        csk_ref[...] = sw[:, SWA_OUT:SWA_OUT + LANES]
        csv_ref[...] = sw[:, SWA_OUT + LANES:]

    @pl.when(i >= CTX_TILES)
    def _():
        for c in range(MQ_NOPE // LANES, MQ_W // LANES):
            mq_ref[:, c * LANES:(c + 1) * LANES] = _rope128(mq[:, c * LANES:(c + 1) * LANES], tm_ref,
                                                            MLA_ROPE_DIM // 2)
        kpe_ref[...] = _rope128(kpe, tm_ref, MLA_ROPE_DIM // 2)
        for c in range((SWA_OUT + LANES) // LANES):
            sw_ref[:, c * LANES:(c + 1) * LANES] = _rope128(sw[:, c * LANES:(c + 1) * LANES], ts_ref,
                                                            HEAD_DIM // 2)


def _combine_scratch():
    return [pltpu.VMEM((2, N_EXPERTS * WIN, D_MODEL), BF16), pltpu.SemaphoreType.DMA((2,))]


def _combine_specs():
    spread = pl.BlockSpec((LANES, N_EXPERTS * WIN), lambda i, *_: (0, 0))
    return [pl.BlockSpec((TM, D_MODEL), lambda i, *_: (i, 0)),
            pl.BlockSpec((TM, LANES), lambda i, *_: (i, 0)),
            spread, spread, pl.BlockSpec(memory_space=pl.ANY)]


def _pre(first, l, xs, mods, g_attn, w_in_p, g_mla_q, w_qb_p, g_mla_kv, t_mla, t_swa):
    tile = lambda w: pl.BlockSpec((TM, w), lambda i, *_: (i, 0))
    ctx_tile = lambda w: pl.BlockSpec((TM, w), lambda i, *_: (jnp.minimum(i, CTX_TILES - 1), 0))
    mod_spec = lambda ll: pl.BlockSpec((None, None, N_MOD, D_MODEL), lambda i, *_: (ll, _cond_row(i), 0, 0))
    vec = lambda w: pl.BlockSpec((None, 1, w), lambda i, *_: (l, 0, 0))
    if first:
        in_specs = [ctx_tile(D_MODEL), pl.BlockSpec((TM, D_MODEL), lambda i: (jnp.maximum(i - CTX_TILES, 0), 0))]
    else:
        in_specs = _combine_specs() + [mod_spec(l - 1)]
    in_specs += [mod_spec(l), vec(D_MODEL),
                 pl.BlockSpec((None, D_MODEL, IN_PAD), lambda i, *_: (l, 0, 0)),
                 vec(MLA_Q_RANK),
                 pl.BlockSpec((None, MLA_Q_RANK, MQ_W), lambda i, *_: (l, 0, 0)),
                 vec(MLA_KV_RANK),
                 pl.BlockSpec((3, TM, LANES), lambda i, *_: (0, _rope_blk(i), 0)),
                 pl.BlockSpec((3, TM, LANES), lambda i, *_: (0, _rope_blk(i), 0))]
    widths = [D_MODEL, NA_IN, MQ_W, MLA_KV_RANK, LANES, SWA_IN]
    cache_widths = [NA_OUT, NA_OUT, MLA_KV_RANK, MLA_ROPE_DIM, LANES, LANES]
    args = list(xs) + ([mods] if not first else []) + [
        mods, g_attn.reshape(DEPTH, 1, D_MODEL), w_in_p, g_mla_q.reshape(DEPTH, 1, MLA_Q_RANK), w_qb_p,
        g_mla_kv.reshape(DEPTH, 1, MLA_KV_RANK), t_mla, t_swa]
    return pl.pallas_call(
        functools.partial(_pre_kernel, first),
        grid_spec=pltpu.PrefetchScalarGridSpec(
            num_scalar_prefetch=0 if first else 2,
            grid=(N_TILES,),
            in_specs=in_specs,
            out_specs=[tile(w) for w in widths] + [ctx_tile(w) for w in cache_widths],
            scratch_shapes=[] if first else _combine_scratch()),
        out_shape=([jax.ShapeDtypeStruct((N_TOK, w), F32) for w in widths]
                   + [jax.ShapeDtypeStruct((N_CTX, w), F32) for w in cache_widths]),
        compiler_params=_cparams(("arbitrary",)),
        name="pre_attn",
    )(*args)


def _softmax_parts(parts, sink=None):
    m = parts[0].max(axis=-1, keepdims=True)
    for s in parts[1:]:
        m = jnp.maximum(m, s.max(axis=-1, keepdims=True))
    if sink is not None:
        m = jnp.maximum(m, sink)
    ps = [jnp.exp(s - m) for s in parts]
    den = ps[0].sum(axis=-1, keepdims=True)
    for p in ps[1:]:
        den = den + p.sum(axis=-1, keepdims=True)
    if sink is not None:
        den = den + jnp.exp(sink - m)
    return ps, 1.0 / den


def _mla_head(h, mq, wkvb, keys):
    scale = MLA_QK_DIM ** -0.5
    wk = wkvb[:, h * 2 * HEAD_DIM:h * 2 * HEAD_DIM + MLA_NOPE_DIM]
    wv = wkvb[:, h * 2 * HEAD_DIM + MLA_NOPE_DIM:(h + 1) * 2 * HEAD_DIM]
    qa = _dot_nt(mq[:, h * MLA_NOPE_DIM:(h + 1) * MLA_NOPE_DIM], wk)
    qr = mq[:, MQ_NOPE + h * MLA_ROPE_DIM:MQ_NOPE + (h + 1) * MLA_ROPE_DIM]
    parts = [(_dot_nt(qa, ckv) + _dot_nt(qr, kpe)) * scale for ckv, kpe in keys]
    ps, inv = _softmax_parts(parts)
    lat = _dot(ps[0], keys[0][0])
    for p, (ckv, _) in zip(ps[1:], keys[1:]):
        lat = lat + _dot(p, ckv)
    return _dot(lat * inv, wv)


def _ctx_attn_kernel(l, sink_ref, na_ref, mq_ref, ckv_ref, kpe_ref, sw_ref, wkvb_ref, o_ref):
    @pl.when(pl.program_id(0) >= BATCH)
    def _():
        o_ref[...] = jnp.zeros_like(o_ref)

    @pl.when(pl.program_id(0) < BATCH)
    def _():
        _ctx_attn_body(l, sink_ref, na_ref, mq_ref, ckv_ref, kpe_ref, sw_ref, wkvb_ref, o_ref)


def _ctx_attn_body(l, sink_ref, na_ref, mq_ref, ckv_ref, kpe_ref, sw_ref, wkvb_ref, o_ref):
    scale = HEAD_DIM ** -0.5
    mq = mq_ref[...]
    wkvb = wkvb_ref[...]
    keys = [(ckv_ref[...], kpe_ref[:, :MLA_ROPE_DIM])]
    for h in range(MLA_HEADS):
        o_ref[:, h * MLA_V_DIM:(h + 1) * MLA_V_DIM] = _mla_head(h, mq, wkvb, keys)
    for h in range(SWA_HEADS):
        kh = h // (SWA_HEADS // SWA_KV_HEADS)
        q = sw_ref[:, h * HEAD_DIM:(h + 1) * HEAD_DIM]
        k = sw_ref[:, SWA_OUT + kh * HEAD_DIM:SWA_OUT + (kh + 1) * HEAD_DIM]
        v = sw_ref[:, SWA_OUT + LANES + kh * HEAD_DIM:SWA_OUT + LANES + (kh + 1) * HEAD_DIM]
        (p,), inv = _softmax_parts([_dot_nt(q, k) * scale], sink_ref[l, h])
        o_ref[:, MLA_OUT + h * HEAD_DIM:MLA_OUT + (h + 1) * HEAD_DIM] = _dot(p, v) * inv
    for h in range(NA_HEADS):
        q = na_ref[:, h * HEAD_DIM:(h + 1) * HEAD_DIM]
        k = na_ref[:, NA_OUT + h * HEAD_DIM:NA_OUT + (h + 1) * HEAD_DIM]
        v = na_ref[:, 2 * NA_OUT + h * HEAD_DIM:2 * NA_OUT + (h + 1) * HEAD_DIM]
        (p,), inv = _softmax_parts([_dot_nt(q, k) * scale])
        o_ref[:, MLA_OUT + SWA_OUT + h * HEAD_DIM:MLA_OUT + SWA_OUT + (h + 1) * HEAD_DIM] = _dot(p, v) * inv


def _ctx_attn(l, sink, na, mq, ckv, kpe, sw, w_kvb):
    tile = lambda w: pl.BlockSpec((SEQ, w), lambda b: (jnp.minimum(b, BATCH - 1), 0))
    return pl.pallas_call(
        functools.partial(_ctx_attn_kernel, l),
        grid=(N_TOK // SEQ,),
        in_specs=[pl.BlockSpec(memory_space=pltpu.SMEM), tile(NA_IN), tile(MQ_W), tile(MLA_KV_RANK), tile(LANES),
                  tile(SWA_IN), pl.BlockSpec((None, MLA_KV_RANK, MLA_HEADS * 2 * HEAD_DIM), lambda b: (l, 0, 0))],
        out_specs=pl.BlockSpec((SEQ, D_MODEL), lambda b: (b, 0)),
        out_shape=jax.ShapeDtypeStruct((N_TOK, D_MODEL), F32),
        compiler_params=_cparams(("arbitrary",)),
        name="ctx_attn",
    )(sink, na, mq, ckv, kpe, sw, w_kvb)


def _dec_row(b, j):
    return CTX_TILES + b * DEC_TILES_PER_BATCH + j


def _na_dec_kernel(q_ref, k_ref, v_ref, ck_ref, cv_ref, bias_ref, oin_ref, o_ref):
    del oin_ref
    scale = HEAD_DIM ** -0.5
    j = pl.program_id(1)
    w0 = jnp.clip(j * NA_ROWS_PER_STEP - NA_WIN_H // 2, 0, DEC_SEQ // GRID_W - NA_KEY_ROWS)
    start = pl.multiple_of(w0 * GRID_W, GRID_W)
    for h in range(NA_HEADS):
        sl = slice(h * HEAD_DIM, (h + 1) * HEAD_DIM)
        q = q_ref[:, sl]
        k = k_ref[pl.ds(start, NA_KEYS), sl]
        v = v_ref[pl.ds(start, NA_KEYS), sl]
        s_nb = _dot_nt(q, k) * scale + bias_ref[h]
        s_ctx = _dot_nt(q, ck_ref[:, sl]) * scale
        (p_nb, p_ctx), inv = _softmax_parts([s_nb, s_ctx])
        o_ref[:, sl] = (_dot(p_nb, v) + _dot(p_ctx, cv_ref[:, sl])) * inv


def _na_dec(l, na, ck, cv, bias, o):
    pat = lambda j: jnp.where(j == 0, 0, jnp.where(j == DEC_TILES_PER_BATCH - 1, 2, 1))
    return pl.pallas_call(
        _na_dec_kernel,
        grid=(DEC_BATCH, DEC_TILES_PER_BATCH),
        in_specs=[pl.BlockSpec((TM, NA_OUT), lambda b, j: (_dec_row(b, j), 0)),
                  pl.BlockSpec((DEC_SEQ, NA_OUT), lambda b, j: (N_CTX // DEC_SEQ + b, 1)),
                  pl.BlockSpec((DEC_SEQ, NA_OUT), lambda b, j: (N_CTX // DEC_SEQ + b, 2)),
                  pl.BlockSpec((None, None, PAST_LEN, NA_OUT), lambda b, j: (b, l, 0, 0)),
                  pl.BlockSpec((None, None, PAST_LEN, NA_OUT), lambda b, j: (b, l, 0, 0)),
                  pl.BlockSpec((None, None, NA_HEADS, TM, NA_KEYS), lambda b, j: (l, pat(j), 0, 0, 0)),
                  pl.BlockSpec(memory_space=pl.ANY)],
        out_specs=pl.BlockSpec((TM, NA_OUT), lambda b, j: (_dec_row(b, j), (MLA_OUT + SWA_OUT) // NA_OUT)),
        out_shape=jax.ShapeDtypeStruct((N_TOK, D_MODEL), F32),
        input_output_aliases={6: 0},
        compiler_params=_cparams(("arbitrary", "arbitrary")),
        name="na_dec",
    )(na, na, na, ck, cv, bias, o)


def _mla_dec_kernel(mq_ref, ckv_ref, kpe_ref, cckv_ref, ckpe_ref, wkvb_ref, oin_ref, o_ref):
    del oin_ref
    mq = mq_ref[...]
    wkvb = wkvb_ref[...]
    keys = [(ckv_ref[...], kpe_ref[:, :MLA_ROPE_DIM]), (cckv_ref[...], ckpe_ref[...])]
    for h in range(MLA_HEADS):
        o_ref[:, h * MLA_V_DIM:(h + 1) * MLA_V_DIM] = _mla_head(h, mq, wkvb, keys)


def _mla_dec(l, mq, ckv, kpe, cckv, ckpe, w_kvb, o):
    return pl.pallas_call(
        _mla_dec_kernel,
        grid=(DEC_BATCH, DEC_TILES_PER_BATCH),
        in_specs=[pl.BlockSpec((TM, MQ_W), lambda b, j: (_dec_row(b, j), 0)),
                  pl.BlockSpec((DEC_SEQ, MLA_KV_RANK), lambda b, j: (N_CTX // DEC_SEQ + b, 0)),
                  pl.BlockSpec((DEC_SEQ, LANES), lambda b, j: (N_CTX // DEC_SEQ + b, 0)),
                  pl.BlockSpec((None, None, PAST_LEN, MLA_KV_RANK), lambda b, j: (b, l, 0, 0)),
                  pl.BlockSpec((None, None, PAST_LEN, MLA_ROPE_DIM), lambda b, j: (b, l, 0, 0)),
                  pl.BlockSpec((None, MLA_KV_RANK, MLA_HEADS * 2 * HEAD_DIM), lambda b, j: (l, 0, 0)),
                  pl.BlockSpec(memory_space=pl.ANY)],
        out_specs=pl.BlockSpec((TM, MLA_OUT), lambda b, j: (_dec_row(b, j), 0)),
        out_shape=jax.ShapeDtypeStruct((N_TOK, D_MODEL), F32),
        input_output_aliases={6: 0},
        compiler_params=_cparams(("arbitrary", "arbitrary")),
        name="mla_dec",
    )(mq, ckv, kpe, cckv, ckpe, w_kvb, o)


def _swa_dec_kernel(l, sink_ref, q_ref, k_ref, v_ref, ck_ref, cv_ref, oin_ref, o_ref):
    del oin_ref
    scale = HEAD_DIM ** -0.5
    j = pl.program_id(1)
    start = pl.multiple_of(jnp.clip(j * TM - SWA_WINDOW, 0, DEC_SEQ - SWA_KEYS), SWA_WINDOW)
    qpos = j * TM + lax.broadcasted_iota(jnp.int32, (TM, SWA_KEYS), 0)
    kpos = start + lax.broadcasted_iota(jnp.int32, (TM, SWA_KEYS), 1)
    band = jnp.abs(qpos - kpos) <= SWA_WINDOW
    for h in range(SWA_HEADS):
        kh = h // (SWA_HEADS // SWA_KV_HEADS)
        sl = slice(kh * HEAD_DIM, (kh + 1) * HEAD_DIM)
        q = q_ref[:, h * HEAD_DIM:(h + 1) * HEAD_DIM]
        s_loc = jnp.where(band, _dot_nt(q, k_ref[pl.ds(start, SWA_KEYS), sl]) * scale, NEG)
        s_ctx = _dot_nt(q, ck_ref[:, sl]) * scale
        (p_loc, p_ctx), inv = _softmax_parts([s_loc, s_ctx], sink_ref[l, h])
        o_ref[:, h * HEAD_DIM:(h + 1) * HEAD_DIM] = (
            _dot(p_loc, v_ref[pl.ds(start, SWA_KEYS), sl]) + _dot(p_ctx, cv_ref[:, sl])) * inv


def _swa_dec(l, sink, sw, ck, cv, o):
    return pl.pallas_call(
        functools.partial(_swa_dec_kernel, l),
        grid=(DEC_BATCH, DEC_TILES_PER_BATCH),
        in_specs=[pl.BlockSpec(memory_space=pltpu.SMEM),
                  pl.BlockSpec((TM, SWA_OUT), lambda b, j: (_dec_row(b, j), 0)),
                  pl.BlockSpec((DEC_SEQ, LANES), lambda b, j: (N_CTX // DEC_SEQ + b, SWA_OUT // LANES)),
                  pl.BlockSpec((DEC_SEQ, LANES), lambda b, j: (N_CTX // DEC_SEQ + b, SWA_OUT // LANES + 1)),
                  pl.BlockSpec((None, None, PAST_LEN, LANES), lambda b, j: (b, l, 0, 0)),
                  pl.BlockSpec((None, None, PAST_LEN, LANES), lambda b, j: (b, l, 0, 0)),
                  pl.BlockSpec(memory_space=pl.ANY)],
        out_specs=pl.BlockSpec((TM, SWA_OUT), lambda b, j: (_dec_row(b, j), 1)),
        out_shape=jax.ShapeDtypeStruct((N_TOK, D_MODEL), F32),
        input_output_aliases={6: 0},
        compiler_params=_cparams(("arbitrary", "arbitrary")),
        name="swa_dec",
    )(sink, sw, sw, sw, ck, cv, o)


def _route(sc, sel):
    rows = [sel[e:e + 1, :] for e in range(N_EXPERTS)]

    def beats(a, ia, b, ib):
        return (a > b) | ((a == b) & (ia < ib)) if ia < ib else (a > b)

    in_top = []
    gscore = []
    for g in range(N_EXPERT_GROUPS):
        mem = list(range(g * EXPERTS_PER_GROUP, (g + 1) * EXPERTS_PER_GROUP))
        acc = None
        for e in mem:
            rank = sum(beats(rows[o], o, rows[e], e).astype(jnp.int32) for o in mem if o != e)
            top = rank < 2
            in_top.append(top)
            term = jnp.where(top, rows[e], 0.0)
            acc = term if acc is None else acc + term
        gscore.append(acc)
    gates, chosen = [], []
    for g in range(N_EXPERT_GROUPS):
        lost = sum(beats(gscore[o], o, gscore[g], g).astype(jnp.int32) for o in range(N_EXPERT_GROUPS) if o != g)
        best = lost == 0
        for e in range(g * EXPERTS_PER_GROUP, (g + 1) * EXPERTS_PER_GROUP):
            pick = best & in_top[e]
            chosen.append(pick.astype(F32))
            gates.append(jnp.where(pick, sc[e:e + 1, :], 0.0))
    gate = jnp.concatenate(gates, axis=0)
    return gate / gate.sum(axis=0, keepdims=True), jnp.concatenate(chosen, axis=0)


def _post_kernel(x_ref, o_ref, mod_ref, wout_ref, g_ref, wr_ref, br_ref,
                 y_ref, h2_ref, rt_ref, rtm_ref, cnt_ref):
    attn = jnp.dot(o_ref[...].astype(BF16), wout_ref[...], preferred_element_type=F32)
    y = x_ref[...] + mod_ref[2:3, :] * attn
    y_ref[...] = y
    h2 = _rms(y, g_ref[...]) * (1.0 + mod_ref[4:5, :]) + mod_ref[3:4, :]
    h_hi = h2.astype(BF16)
    h_lo = (h2 - h_hi.astype(F32)).astype(BF16)
    w = wr_ref[...]
    w_hi = w.astype(BF16)
    w_lo = (w - w_hi.astype(F32)).astype(BF16)
    logits = (jnp.dot(h_hi, w_hi, preferred_element_type=F32) + jnp.dot(h_lo, w_hi, preferred_element_type=F32)
              + jnp.dot(h_hi, w_lo, preferred_element_type=F32))
    logits = logits.T[:N_EXPERTS, :]
    sc = 1.0 / (1.0 + jnp.exp(-logits))
    gate, chosen = _route(sc, sc + br_ref[...])
    h2_ref[...] = h_hi
    rt = jnp.concatenate([gate, chosen], axis=0)
    rt_ref[...] = rt
    rtm_ref[...] = jnp.concatenate([rt, jnp.zeros((LANES - 2 * N_EXPERTS, TM), F32)], axis=0).T
    cnt_ref[...] = jnp.broadcast_to(jnp.sum(chosen, axis=1, keepdims=True), (N_EXPERTS, LANES))


def _post(l, x, o, mods, w_out_bf, g_ffn, w_router_p, b_router):
    tile = lambda w: pl.BlockSpec((TM, w), lambda i: (i, 0))
    return pl.pallas_call(
        _post_kernel,
        grid=(N_TILES,),
        in_specs=[tile(D_MODEL), tile(D_MODEL),
                  pl.BlockSpec((None, None, N_MOD, D_MODEL), lambda i: (l, _cond_row(i), 0, 0)),
                  pl.BlockSpec((None, D_MODEL, D_MODEL), lambda i: (l, 0, 0)),
                  pl.BlockSpec((None, 1, D_MODEL), lambda i: (l, 0, 0)),
                  pl.BlockSpec((D_MODEL, LANES), lambda i: (0, 0)),
                  pl.BlockSpec((N_EXPERTS, 1), lambda i: (0, 0))],
        out_specs=[tile(D_MODEL), tile(D_MODEL), pl.BlockSpec((2 * N_EXPERTS, TM), lambda i: (0, i)),
                   tile(LANES), pl.BlockSpec((None, N_EXPERTS, LANES), lambda i: (i, 0, 0))],
        out_shape=[jax.ShapeDtypeStruct((N_TOK, D_MODEL), F32), jax.ShapeDtypeStruct((N_TOK, D_MODEL), BF16),
                   jax.ShapeDtypeStruct((2 * N_EXPERTS, N_TOK), F32),
                   jax.ShapeDtypeStruct((N_TOK, LANES), F32),
                   jax.ShapeDtypeStruct((N_TILES, N_EXPERTS, LANES), F32)],
        compiler_params=_cparams(("arbitrary",)),
        name="post_attn",
    )(x, o, mods, w_out_bf, g_ffn.reshape(DEPTH, 1, D_MODEL), w_router_p, b_router.reshape(N_EXPERTS, 1))


def _shr(x, bits):
    return lax.shift_right_logical(x, jnp.int32(bits))


TR_BITS = TR.bit_length() - 1
ALIGN_BITS = RUN_ALIGN.bit_length() - 1
WIN_BITS = WIN.bit_length() - 1


def _plan_rows(cnt_ref, off_ref, npass_ref, seg_ref):
    def per_expert(e, row0):
        def per_tile(bb, r):
            off_ref[bb * N_EXPERTS + e] = r
            return r + (_shr(cnt_ref[bb * N_EXPERTS + e] + (RUN_ALIGN - 1), ALIGN_BITS) << ALIGN_BITS)

        rows_end = lax.fori_loop(0, N_TILES, per_tile, row0)
        n = _shr(rows_end - row0 + (TR - 1), TR_BITS)
        seg_ref[e] = _shr(row0, TR_BITS)
        seg_ref[N_EXPERTS + e] = n
        seg_ref[2 * N_EXPERTS + e] = rows_end
        seg_ref[3 * N_EXPERTS + e] = row0 + (n << TR_BITS)
        return row0 + (n << TR_BITS)

    end_row = lax.fori_loop(0, N_EXPERTS, per_expert, jnp.int32(0))
    seg_ref[4 * N_EXPERTS] = _shr(end_row, TR_BITS)

    def longest(bb, carry):
        m = lax.fori_loop(0, N_EXPERTS, lambda e, m: jnp.maximum(m, cnt_ref[bb * N_EXPERTS + e]), jnp.int32(0))
        npass_ref[bb] = _shr(m + (WIN - 1), WIN_BITS)
        return carry

    lax.fori_loop(0, N_TILES, longest, 0)


def _dispatch_kernel(cnt_ref, h_ref, rt_ref, et_ref, xs_hbm, off_ref, npass_ref, seg_ref, zbuf, zeros, sem):
    b = pl.program_id(0)
    slot = b % 2

    @pl.when(b == 0)
    def _():
        _plan_rows(cnt_ref, off_ref, npass_ref, seg_ref)

    key = jnp.dot(et_ref[...], _run_keys(rt_ref[...], 1).astype(BF16), preferred_element_type=F32)
    slot_j = lax.broadcasted_iota(jnp.int32, (N_EXPERTS * WIN, TM), 0) % WIN

    def run_copies(bb, sl, p, act):
        def per_expert(e, carry):
            left = cnt_ref[bb * N_EXPERTS + e] - p * WIN
            row = off_ref[bb * N_EXPERTS + e] + p * WIN
            whole = row + WIN <= seg_ref[2 * N_EXPERTS + e]

            @pl.when(whole & (left > 0))
            def _():
                src = pl.multiple_of(e * WIN, WIN)
                act(pltpu.make_async_copy(zbuf.at[sl, pl.ds(src, WIN)],
                                          xs_hbm.at[pl.ds(pl.multiple_of(row, RUN_ALIGN), WIN)], sem))

            for k in range(WIN // RUN_ALIGN):
                @pl.when(jnp.logical_not(whole) & (k * RUN_ALIGN < left))
                def _():
                    src = pl.multiple_of(e * WIN + k * RUN_ALIGN, RUN_ALIGN)
                    dst = pl.multiple_of(row + k * RUN_ALIGN, RUN_ALIGN)
                    act(pltpu.make_async_copy(zbuf.at[sl, pl.ds(src, RUN_ALIGN)],
                                              xs_hbm.at[pl.ds(dst, RUN_ALIGN)], sem))
            return carry

        lax.fori_loop(0, N_EXPERTS, per_expert, 0)

    def fill_and_send(p):
        pick = jnp.where(key == (slot_j + p * WIN).astype(F32), 1.0, 0.0).astype(BF16)
        zbuf[slot] = jnp.dot(pick, h_ref[...], preferred_element_type=F32).astype(BF16)
        run_copies(b, slot, p, lambda cp: cp.start())

    @pl.when(b > 0)
    def _():
        run_copies(b - 1, 1 - slot, npass_ref[jnp.maximum(b - 1, 0)] - 1, lambda cp: cp.wait())

    fill_and_send(0)

    def more(p, carry):
        run_copies(b, slot, p - 1, lambda cp: cp.wait())
        fill_and_send(p)
        return carry

    lax.fori_loop(1, npass_ref[b], more, 0)

    @pl.when(b == N_TILES - 1)
    def _():
        run_copies(b, slot, npass_ref[b] - 1, lambda cp: cp.wait())
        zeros[...] = jnp.zeros_like(zeros)

        def fill(act):
            def per_expert(e, carry):
                for k in range(TR // RUN_ALIGN):
                    row = pl.multiple_of(seg_ref[2 * N_EXPERTS + e] + k * RUN_ALIGN, RUN_ALIGN)

                    @pl.when(row < seg_ref[3 * N_EXPERTS + e])
                    def _():
                        act(pltpu.make_async_copy(zeros.at[pl.ds(0, RUN_ALIGN)], xs_hbm.at[pl.ds(row, RUN_ALIGN)], sem))
                return carry

            lax.fori_loop(0, N_EXPERTS, per_expert, 0)

            def per_tile(t, carry):
                act(pltpu.make_async_copy(zeros, xs_hbm.at[pl.ds(pl.multiple_of(t * TR, TR), TR)], sem))
                return carry

            lax.fori_loop(seg_ref[4 * N_EXPERTS], NT, per_tile, 0)

        fill(lambda cp: cp.start())
        fill(lambda cp: cp.wait())


def _dispatch(cnt, h2, rt, et):
    smem = pl.BlockSpec(memory_space=pltpu.SMEM)
    return pl.pallas_call(
        _dispatch_kernel,
        grid_spec=pltpu.PrefetchScalarGridSpec(
            num_scalar_prefetch=1,
            grid=(N_TILES,),
            in_specs=[pl.BlockSpec((TM, D_MODEL), lambda i, *_: (i, 0)),
                      pl.BlockSpec((2 * N_EXPERTS, TM), lambda i, *_: (0, i)),
                      pl.BlockSpec((N_EXPERTS * WIN, 2 * N_EXPERTS), lambda i, *_: (0, 0))],
            out_specs=[pl.BlockSpec(memory_space=pl.ANY), smem, smem, smem],
            scratch_shapes=[pltpu.VMEM((2, N_EXPERTS * WIN, D_MODEL), BF16), pltpu.VMEM((TR, D_MODEL), BF16),
                            pltpu.SemaphoreType.DMA(())]),
        out_shape=[jax.ShapeDtypeStruct((NT * TR, D_MODEL), BF16), jax.ShapeDtypeStruct((N_RUNS,), jnp.int32),
                   jax.ShapeDtypeStruct((N_TILES,), jnp.int32),
                   jax.ShapeDtypeStruct((4 * N_EXPERTS + 1,), jnp.int32)],
        compiler_params=_cparams(("arbitrary",)),
        name="dispatch",
    )(cnt, h2, rt, et)


CAST_ROWS = 128


def _cast_rows(src_ref, dst_ref, n):
    def body(c, carry):
        rows = pl.ds(pl.multiple_of(c * CAST_ROWS, CAST_ROWS), CAST_ROWS)
        dst_ref[rows, :] = src_ref[rows, :].astype(BF16)
        return carry

    lax.fori_loop(0, n, body, 0)


def _experts_kernel(seg_ref, xs_hbm, wg_ref, wu_ref, wd_ref, ys_hbm, wgb, wub, wdb, xbuf, ybuf, semx, semy):
    e = pl.program_id(0)
    t0 = seg_ref[e]
    n = seg_ref[N_EXPERTS + e]
    _cast_rows(wg_ref, wgb, D_MODEL // CAST_ROWS)
    _cast_rows(wu_ref, wub, D_MODEL // CAST_ROWS)
    _cast_rows(wd_ref, wdb, D_EXPERT // CAST_ROWS)

    def rows(k):
        return pl.ds(pl.multiple_of((t0 + k) * TR, TR), TR)

    def fetch(k, s):
        return pltpu.make_async_copy(xs_hbm.at[rows(k)], xbuf.at[s], semx.at[s])

    def put(k, s):
        return pltpu.make_async_copy(ybuf.at[s], ys_hbm.at[rows(k)], semy.at[s])

    @pl.when(n > 0)
    def _():
        fetch(0, 0).start()

    def tile(k, carry):
        s = k % 2

        @pl.when(k + 1 < n)
        def _():
            fetch(k + 1, 1 - s).start()

        fetch(k, s).wait()

        @pl.when(k >= 2)
        def _():
            put(k - 2, s).wait()

        x = xbuf[s]
        hg = jnp.dot(x, wgb[...], preferred_element_type=F32)
        hu = jnp.dot(x, wub[...], preferred_element_type=F32)
        a = hg * (1.0 / (1.0 + jnp.exp(-hg))) * hu
        ybuf[s] = jnp.dot(a.astype(BF16), wdb[...], preferred_element_type=F32).astype(BF16)
        put(k, s).start()
        return carry

    lax.fori_loop(0, n, tile, 0)

    @pl.when(n >= 2)
    def _():
        put(n - 2, n % 2).wait()

    @pl.when(n >= 1)
    def _():
        put(n - 1, (n - 1) % 2).wait()

    @pl.when(e == N_EXPERTS - 1)
    def _():
        ybuf[0] = jnp.zeros((TR, D_MODEL), BF16)

        def fill(act):
            def per_tile(t, carry):
                act(pltpu.make_async_copy(ybuf.at[0], ys_hbm.at[pl.ds(pl.multiple_of(t * TR, TR), TR)], semy.at[0]))
                return carry

            lax.fori_loop(seg_ref[4 * N_EXPERTS], NT, per_tile, 0)

        fill(lambda cp: cp.start())
        fill(lambda cp: cp.wait())


def _experts(l, seg, xs, w_gate, w_up, w_down):
    wspec = lambda a, b: pl.BlockSpec((None, None, a, b), lambda e, seg: (l, e, 0, 0))
    tile_buf = pltpu.VMEM((2, TR, D_MODEL), BF16)
    return pl.pallas_call(
        _experts_kernel,
        grid_spec=pltpu.PrefetchScalarGridSpec(
            num_scalar_prefetch=1,
            grid=(N_EXPERTS,),
            in_specs=[pl.BlockSpec(memory_space=pl.ANY),
                      wspec(D_MODEL, D_EXPERT), wspec(D_MODEL, D_EXPERT), wspec(D_EXPERT, D_MODEL)],
            out_specs=pl.BlockSpec(memory_space=pl.ANY),
            scratch_shapes=[pltpu.VMEM((D_MODEL, D_EXPERT), BF16), pltpu.VMEM((D_MODEL, D_EXPERT), BF16),
                            pltpu.VMEM((D_EXPERT, D_MODEL), BF16), tile_buf, tile_buf,
                            pltpu.SemaphoreType.DMA((2,)), pltpu.SemaphoreType.DMA((2,))]),
        out_shape=jax.ShapeDtypeStruct((NT * TR, D_MODEL), BF16),
        compiler_params=_cparams(("arbitrary",)),
        name="experts",
    )(seg, xs, w_gate, w_up, w_down)


def _final_kernel(off_ref, npass_ref, y_ref, rtm_ref, ek_ref, eg_ref, ys_hbm, mod_ref, g_ref, op_ref, os_ref,
                  wbuf, sem):
    i = pl.program_id(0)
    moe = _combined_moe(off_ref, npass_ref, rtm_ref, ek_ref, eg_ref, ys_hbm, wbuf, sem)
    out = _rms(y_ref[...] + mod_ref[5:6, :] * moe, g_ref[...])

    @pl.when(i < CTX_TILES)
    def _():
        op_ref[...] = out

    @pl.when(i >= CTX_TILES)
    def _():
        os_ref[...] = out


def _final(off, npass, y, rtm, ek, eg, ys, mods, g_final):
    return pl.pallas_call(
        _final_kernel,
        grid_spec=pltpu.PrefetchScalarGridSpec(
            num_scalar_prefetch=2,
            grid=(N_TILES,),
            in_specs=_combine_specs() + [
                pl.BlockSpec((None, None, N_MOD, D_MODEL), lambda i, *_: (DEPTH - 1, _cond_row(i), 0, 0)),
                pl.BlockSpec((1, D_MODEL), lambda i, *_: (0, 0))],
            out_specs=[pl.BlockSpec((TM, D_MODEL), lambda i, *_: (jnp.minimum(i, CTX_TILES - 1), 0)),
                       pl.BlockSpec((TM, D_MODEL), lambda i, *_: (jnp.maximum(i - CTX_TILES, 0), 0))],
            scratch_shapes=_combine_scratch()),
        out_shape=[jax.ShapeDtypeStruct((N_CTX, D_MODEL), F32), jax.ShapeDtypeStruct((N_DEC, D_MODEL), F32)],
        compiler_params=_cparams(("arbitrary",)),
        name="final_norm",
    )(off, npass, y, rtm, ek, eg, ys, mods, g_final.reshape(1, D_MODEL))


def _prep_kernel(win_ref, wout_ref, wi_ref, wo_ref):
    w = win_ref[...]
    split = KPE_OFF + MLA_ROPE_DIM
    wi_ref[...] = jnp.concatenate([w[:, :split], jnp.zeros((CAST_ROWS, SW_OFF - split), F32), w[:, split:]],
                                  axis=1).astype(BF16)
    wo_ref[...] = wout_ref[...].astype(BF16)


def _prep_weights(w_in, w_out):
    n = D_MODEL // CAST_ROWS
    shift = NA_OUT // CAST_ROWS
    return pl.pallas_call(
        _prep_kernel,
        grid=(DEPTH, n),
        in_specs=[pl.BlockSpec((None, CAST_ROWS, IN_WIDTH), lambda l, j: (l, j, 0)),
                  pl.BlockSpec((None, CAST_ROWS, D_MODEL), lambda l, j: (l, (j + shift) % n, 0))],
        out_specs=[pl.BlockSpec((None, CAST_ROWS, IN_PAD), lambda l, j: (l, j, 0)),
                   pl.BlockSpec((None, CAST_ROWS, D_MODEL), lambda l, j: (l, j, 0))],
        out_shape=[jax.ShapeDtypeStruct((DEPTH, D_MODEL, IN_PAD), BF16),
                   jax.ShapeDtypeStruct((DEPTH, D_MODEL, D_MODEL), BF16)],
        compiler_params=_cparams(("arbitrary", "arbitrary")),
        name="prep_weights",
    )(w_in, w_out)


def _rope_tables(rot_dim):
    t = jnp.arange(DEC_SEQ, dtype=jnp.int32)
    row = (t // GRID_W).astype(F32)
    col = (t % GRID_W).astype(F32)
    per_axis = rot_dim // 2
    inv = ROPE_BASE ** (-jnp.arange(0, per_axis, 2, dtype=F32) / per_axis)
    ang = jnp.concatenate([row[:, None] * inv, col[:, None] * inv], axis=-1)
    cos, sin = jnp.cos(ang), jnp.sin(ang)
    zero = jnp.zeros_like(sin)
    rep = LANES // rot_dim
    tabs = [jnp.concatenate([cos, cos], -1), jnp.concatenate([zero, sin], -1), jnp.concatenate([-sin, zero], -1)]
    return jnp.stack([jnp.tile(a, (1, rep)) for a in tabs])


def kernel(x_prompt, x_sample, c, cache_na_k, cache_na_v, cache_mla_ckv, cache_mla_kpe, cache_swa_k, cache_swa_v,
           c_ctx, w_ada, b_ada, g_attn, w_in, g_mla_q, w_mla_qb, g_mla_kv, w_mla_kvb, na_rpb, swa_sink, w_out,
           g_ffn, w_router, b_router, w_gate, w_up, w_down, g_final):
    cond = jnp.concatenate([c_ctx[None], c, jnp.zeros((COND_ROWS - 1 - DEC_BATCH, D_MODEL), F32)], axis=0)
    mods = _ada(cond, w_ada, b_ada).reshape(DEPTH, COND_ROWS, N_MOD, D_MODEL)
    bias = _na_bias(na_rpb)
    t_mla = _rope_tables(MLA_ROPE_DIM)
    t_swa = _rope_tables(HEAD_DIM)

    w_in_p, w_out_p = _prep_weights(w_in, w_out)
    wq = w_mla_qb.reshape(DEPTH, MLA_Q_RANK, MLA_HEADS, MLA_QK_DIM)
    w_qb_p = jnp.concatenate([wq[..., :MLA_NOPE_DIM].reshape(DEPTH, MLA_Q_RANK, MQ_NOPE),
                              wq[..., MLA_NOPE_DIM:].reshape(DEPTH, MLA_Q_RANK, MLA_HEADS * MLA_ROPE_DIM),
                              jnp.zeros((DEPTH, MLA_Q_RANK, MQ_W - MLA_HEADS * MLA_QK_DIM), F32)],
                             axis=-1).astype(BF16)
    w_router_p = jnp.pad(w_router, ((0, 0), (0, LANES - N_EXPERTS)))
    et, ek, eg = _spread_consts()

    c_na_k = cache_na_k.reshape(DEC_BATCH, DEPTH, PAST_LEN, NA_OUT)
    c_na_v = cache_na_v.reshape(DEC_BATCH, DEPTH, PAST_LEN, NA_OUT)
    c_sw_k = cache_swa_k.reshape(DEC_BATCH, DEPTH, PAST_LEN, LANES)
    c_sw_v = cache_swa_v.reshape(DEC_BATCH, DEPTH, PAST_LEN, LANES)

    y = ys = rtm = off = npass = None
    caches = [[] for _ in range(6)]
    for l in range(DEPTH):
        if l == 0:
            srcs = [x_prompt.reshape(N_CTX, D_MODEL), x_sample.reshape(N_DEC, D_MODEL)]
        else:
            srcs = [off, npass, y, rtm, ek, eg, ys]
        x, na, mq, ckv, kpe, sw, *slabs = _pre(l == 0, l, srcs, mods, g_attn, w_in_p, g_mla_q, w_qb_p, g_mla_kv,
                                               t_mla, t_swa)
        for dst, slab in zip(caches, slabs):
            dst.append(slab)
        o = _ctx_attn(l, swa_sink, na, mq, ckv, kpe, sw, w_mla_kvb)
        o = _mla_dec(l, mq, ckv, kpe, cache_mla_ckv, cache_mla_kpe, w_mla_kvb, o)
        o = _swa_dec(l, swa_sink, sw, c_sw_k, c_sw_v, o)
        o = _na_dec(l, na, c_na_k, c_na_v, bias, o)
        y, h2, rt, rtm, cnt = _post(l, x, o, mods, w_out_p, g_ffn, w_router_p, b_router)
        xs, off, npass, seg = _dispatch(cnt[:, :, 0].astype(jnp.int32).reshape(-1), h2, rt, et)
        ys = _experts(l, seg, xs, w_gate, w_up, w_down)
    y_prompt, y_sample = _final(off, npass, y, rtm, ek, eg, ys, mods, g_final)

    def stack(parts, tail):
        return jnp.stack([p.reshape((BATCH, SEQ) + tail) for p in parts], axis=1)

    return (y_prompt.reshape(BATCH, SEQ, D_MODEL), y_sample.reshape(DEC_BATCH, DEC_SEQ, D_MODEL),
            stack(caches[0], (NA_HEADS, HEAD_DIM)), stack(caches[1], (NA_HEADS, HEAD_DIM)),
            stack(caches[2], (MLA_KV_RANK,)), stack(caches[3], (MLA_ROPE_DIM,)),
            stack(caches[4], (SWA_KV_HEADS, HEAD_DIM)), stack(caches[5], (SWA_KV_HEADS, HEAD_DIM)))
```

```python
import functools

import jax
import jax.numpy as jnp
from jax import lax
from jax.experimental import pallas as pl
from jax.experimental.pallas import tpu as pltpu

D_MODEL = 1024
BATCH = 16
SEQ = 256
DEPTH = 4
DEC_BATCH = 2
DEC_SEQ = 2048
PAST_LEN = 256
GRID_W = 64
HEAD_DIM = 64
NA_HEADS = 4
NA_WIN_H = 8
NA_WIN_W = 16
MLA_HEADS = 6
MLA_Q_RANK = 256
MLA_KV_RANK = 128
MLA_NOPE_DIM = 64
MLA_ROPE_DIM = 32
MLA_V_DIM = 64
MLA_QK_DIM = MLA_NOPE_DIM + MLA_ROPE_DIM
SWA_HEADS = 6
SWA_KV_HEADS = 2
SWA_WINDOW = 128
ROPE_BASE = 10000.0
N_EXPERTS = 16
N_EXPERT_GROUPS = 4
EXPERTS_PER_GROUP = 4
D_EXPERT = 512
RMS_EPS = 1e-6
N_MOD = 6

NA_IN = 3 * NA_HEADS * HEAD_DIM
MLA_IN = MLA_Q_RANK + MLA_KV_RANK + MLA_ROPE_DIM
SWA_IN = (SWA_HEADS + 2 * SWA_KV_HEADS) * HEAD_DIM
IN_WIDTH = NA_IN + MLA_IN + SWA_IN
NA_OUT = NA_HEADS * HEAD_DIM
MLA_OUT = MLA_HEADS * MLA_V_DIM
SWA_OUT = SWA_HEADS * HEAD_DIM

LANES = 128
N_CTX = BATCH * SEQ
N_DEC = DEC_BATCH * DEC_SEQ
N_TOK = N_CTX + N_DEC
TM = 256
N_TILES = N_TOK // TM
CTX_TILES = N_CTX // TM
DEC_TILES_PER_BATCH = DEC_SEQ // TM
COND_ROWS = 8
KPE_OFF = NA_IN + MLA_Q_RANK + MLA_KV_RANK
SW_OFF = KPE_OFF + LANES
IN_PAD = SW_OFF + SWA_IN
MQ_NOPE = MLA_HEADS * MLA_NOPE_DIM
MQ_W = 640
NA_ROWS_PER_STEP = TM // GRID_W
NA_KEY_ROWS = 12
NA_KEYS = NA_KEY_ROWS * GRID_W
SWA_KEYS = 512
NEG = -1e30
TR = 512
N_ASSIGN = 2 * N_TOK
RUN_ALIGN = 16
WIN = 64
N_RUNS = N_TILES * N_EXPERTS
NT = -(-(N_ASSIGN + N_RUNS * (RUN_ALIGN - 1) + N_EXPERTS * (TR - 1) + WIN) // TR)

F32 = jnp.float32
BF16 = jnp.bfloat16
VMEM_LIMIT = 56 * 1024 * 1024


def _cparams(sem):
    return pltpu.CompilerParams(dimension_semantics=sem, vmem_limit_bytes=VMEM_LIMIT)


def _cond_row(i):
    return jnp.where(i < CTX_TILES, 0, 1 + (i - CTX_TILES) // DEC_TILES_PER_BATCH)


def _rope_blk(i):
    return jnp.where(i < CTX_TILES, 0, (i - CTX_TILES) % DEC_TILES_PER_BATCH)


def _rms(x, g):
    ms = jnp.mean(x * x, axis=-1, keepdims=True)
    return x * lax.rsqrt(ms + RMS_EPS) * g


def _dot(a, b):
    return jnp.dot(a.astype(BF16), b.astype(BF16), preferred_element_type=F32)


def _dot_nt(a, b):
    return lax.dot_general(a.astype(BF16), b.astype(BF16), (((1,), (1,)), ((), ())),
                           preferred_element_type=F32)


def _ada_kernel(c_ref, w_ref, b_ref, o_ref):
    c = c_ref[...]
    s = c * (1.0 / (1.0 + jnp.exp(-c)))
    o_ref[...] = _dot(s, w_ref[...]) + b_ref[...]


def _ada(cond, w_ada, b_ada):
    tn = 1536
    n = N_MOD * D_MODEL
    return pl.pallas_call(
        _ada_kernel,
        grid=(DEPTH, n // tn),
        in_specs=[pl.BlockSpec((COND_ROWS, D_MODEL), lambda l, j: (0, 0)),
                  pl.BlockSpec((None, D_MODEL, tn), lambda l, j: (l, 0, j)),
                  pl.BlockSpec((None, 1, tn), lambda l, j: (l, 0, j))],
        out_specs=pl.BlockSpec((None, COND_ROWS, tn), lambda l, j: (l, 0, j)),
        out_shape=jax.ShapeDtypeStruct((DEPTH, COND_ROWS, n), F32),
        compiler_params=_cparams(("arbitrary", "arbitrary")),
        name="ada_mod",
    )(cond, w_ada, b_ada.reshape(DEPTH, 1, n))


def _bias_kernel(rpb_ref, o_ref):
    g = pl.program_id(0)
    base = g * ((2 * NA_WIN_H - 1) * (2 * NA_WIN_W - 1))
    qc = lax.broadcasted_iota(jnp.int32, (GRID_W, GRID_W), 0)
    kc = lax.broadcasted_iota(jnp.int32, (GRID_W, GRID_W), 1)
    dc = jnp.clip(kc - qc + (NA_WIN_W - 1), 0, 2 * NA_WIN_W - 2)
    cs = jnp.clip(qc - NA_WIN_W // 2, 0, GRID_W - NA_WIN_W)
    col_ok = (kc >= cs) & (kc < cs + NA_WIN_W)
    neg = jnp.full((GRID_W, GRID_W), NEG, F32)
    tabs = []
    for a in range(2 * NA_WIN_H - 1):
        t = jnp.zeros((GRID_W, GRID_W), F32)
        for b in range(2 * NA_WIN_W - 1):
            t = jnp.where(dc == b, rpb_ref[base + a * (2 * NA_WIN_W - 1) + b], t)
        tabs.append(jnp.where(col_ok, t, NEG))
    for p in range(3):
        for qi in range(NA_ROWS_PER_STEP):
            for kj in range(NA_KEY_ROWS):
                if p == 0:
                    ok, dr = kj < NA_WIN_H, kj - qi + 7
                elif p == 1:
                    ok, dr = qi <= kj < qi + NA_WIN_H, kj - qi + 3
                else:
                    ok, dr = kj >= NA_KEY_ROWS - NA_WIN_H, kj - qi - 1
                blk = tabs[dr] if ok else neg
                o_ref[p, qi * GRID_W:(qi + 1) * GRID_W, kj * GRID_W:(kj + 1) * GRID_W] = blk


def _na_bias(na_rpb):
    return pl.pallas_call(
        _bias_kernel,
        grid=(DEPTH * NA_HEADS,),
        in_specs=[pl.BlockSpec(memory_space=pltpu.SMEM)],
        out_specs=pl.BlockSpec((None, 3, None, TM, NA_KEYS),
                               lambda g: (g // NA_HEADS, 0, g % NA_HEADS, 0, 0)),
        out_shape=jax.ShapeDtypeStruct((DEPTH, 3, NA_HEADS, TM, NA_KEYS), F32),
        compiler_params=_cparams(("arbitrary",)),
        name="na_bias",
    )(na_rpb.reshape(-1))


def _rope128(x, t_ref, half):
    return (x * t_ref[0] + pltpu.roll(x, half, 1) * t_ref[1]
            + pltpu.roll(x, LANES - half, 1) * t_ref[2])


def _spread_consts():
    slot_e = jnp.arange(N_EXPERTS * WIN, dtype=jnp.int32) // WIN
    idx32 = jnp.arange(2 * N_EXPERTS, dtype=jnp.int32)
    idx128 = jnp.arange(LANES, dtype=jnp.int32)
    et = (idx32[None, :] == slot_e[:, None] + N_EXPERTS).astype(BF16)
    ek = (idx128[:, None] == slot_e[None, :] + N_EXPERTS).astype(BF16)
    eg = (idx128[:, None] == slot_e[None, :]).astype(BF16)
    return et, ek, eg


def _run_keys(chosen, token_axis):
    a = lax.broadcasted_iota(jnp.int32, (TM, TM), 0)
    b = lax.broadcasted_iota(jnp.int32, (TM, TM), 1)
    if token_axis == 1:
        rank = jnp.dot(chosen.astype(BF16), (a < b).astype(BF16), preferred_element_type=F32)
    else:
        rank = jnp.dot((b < a).astype(BF16), chosen.astype(BF16), preferred_element_type=F32)
    return jnp.where(chosen > 0.5, rank, -1.0)


def _windows_start(i, slot, p, off_ref, ys_hbm, wbuf, sem):
    for e in range(N_EXPERTS):
        row = pl.multiple_of(off_ref[i * N_EXPERTS + e] + p * WIN, RUN_ALIGN)
        pltpu.make_async_copy(ys_hbm.at[pl.ds(row, WIN)], wbuf.at[slot, pl.ds(e * WIN, WIN)],
                              sem.at[slot]).start(priority=1)


def _windows_wait(slot, ys_hbm, wbuf, sem):
    pltpu.make_async_copy(ys_hbm.at[pl.ds(0, N_EXPERTS * WIN)], wbuf.at[slot], sem.at[slot]).wait()


def _combined_moe(off_ref, npass_ref, rtm_ref, ek_ref, eg_ref, ys_hbm, wbuf, sem):
    i = pl.program_id(0)
    slot = i % 2

    @pl.when(i == 0)
    def _():
        _windows_start(0, 0, 0, off_ref, ys_hbm, wbuf, sem)

    @pl.when(i + 1 < N_TILES)
    def _():
        _windows_start(i + 1, 1 - slot, 0, off_ref, ys_hbm, wbuf, sem)

    r = rtm_ref[...]
    key = jnp.dot(_run_keys(r, 0).astype(BF16), ek_ref[...], preferred_element_type=F32)
    gate = jnp.dot(r.astype(BF16), eg_ref[...], preferred_element_type=F32)
    slot_j = lax.broadcasted_iota(jnp.int32, (TM, N_EXPERTS * WIN), 1) % WIN

    def contrib(p):
        g = jnp.where(key == (slot_j + p * WIN).astype(F32), gate, 0.0).astype(BF16)
        return jnp.dot(g, wbuf[slot], preferred_element_type=F32)

    _windows_wait(slot, ys_hbm, wbuf, sem)
    acc = contrib(0)

    def extra(p, acc):
        _windows_start(i, slot, p, off_ref, ys_hbm, wbuf, sem)
        _windows_wait(slot, ys_hbm, wbuf, sem)
        return acc + contrib(p)

    return lax.fori_loop(1, npass_ref[i], extra, acc)


def _pre_kernel(first, *refs):
    i = pl.program_id(0)
    if first:
        (xp_ref, xs_ref, mod_ref, g_ref, win_ref, gq_ref, wqb_ref, gkv_ref, tm_ref, ts_ref,
         xo_ref, na_ref, mq_ref, ckv_ref, kpe_ref, sw_ref, *cache_refs) = refs
        x = jnp.where(i < CTX_TILES, xp_ref[...], xs_ref[...])
    else:
        (off_ref, npass_ref, y_ref, rtm_ref, ek_ref, eg_ref, ys_hbm, modp_ref, mod_ref, g_ref, win_ref, gq_ref,
         wqb_ref, gkv_ref, tm_ref, ts_ref, xo_ref, na_ref, mq_ref, ckv_ref, kpe_ref, sw_ref, *rest) = refs
        *cache_refs, wbuf, sem = rest
        moe = _combined_moe(off_ref, npass_ref, rtm_ref, ek_ref, eg_ref, ys_hbm, wbuf, sem)
        x = y_ref[...] + modp_ref[5:6, :] * moe
    xo_ref[...] = x
    cnk_ref, cnv_ref, cckv_ref, ckpe_ref, csk_ref, csv_ref = cache_refs

    h = _rms(x, g_ref[...]) * (1.0 + mod_ref[1:2, :]) + mod_ref[0:1, :]
    z = jnp.dot(h.astype(BF16), win_ref[...], preferred_element_type=F32)
    na_ref[...] = z[:, :NA_IN]
    cq = _rms(z[:, NA_IN:NA_IN + MLA_Q_RANK], gq_ref[...])
    ckv_ref[...] = _rms(z[:, NA_IN + MLA_Q_RANK:KPE_OFF], gkv_ref[...])
    mq = jnp.dot(cq.astype(BF16), wqb_ref[...], preferred_element_type=F32)
    kpe = z[:, KPE_OFF:SW_OFF]
    sw = z[:, SW_OFF:IN_PAD]
    mq_ref[:, :MQ_NOPE] = mq[:, :MQ_NOPE]
    sw_ref[:, SWA_OUT + LANES:] = sw[:, SWA_OUT + LANES:]

    @pl.when(i < CTX_TILES)
    def _():
        mq_ref[:, MQ_NOPE:] = mq[:, MQ_NOPE:]
        kpe_ref[...] = kpe
        sw_ref[:, :SWA_OUT + LANES] = sw[:, :SWA_OUT + LANES]
        cnk_ref[...] = z[:, NA_OUT:2 * NA_OUT]
        cnv_ref[...] = z[:, 2 * NA_OUT:NA_IN]
        cckv_ref[...] = ckv_ref[...]
        ckpe_ref[...] = kpe[:, :MLA_ROPE_DIM]
        csk_ref[...] = sw[:, SWA_OUT:SWA_OUT + LANES]
        csv_ref[...] = sw[:, SWA_OUT + LANES:]

    @pl.when(i >= CTX_TILES)
    def _():
        for c in range(MQ_NOPE // LANES, MQ_W // LANES):
            mq_ref[:, c * LANES:(c + 1) * LANES] = _rope128(mq[:, c * LANES:(c + 1) * LANES], tm_ref,
                                                            MLA_ROPE_DIM // 2)
        kpe_ref[...] = _rope128(kpe, tm_ref, MLA_ROPE_DIM // 2)
        for c in range((SWA_OUT + LANES) // LANES):
            sw_ref[:, c * LANES:(c + 1) * LANES] = _rope128(sw[:, c * LANES:(c + 1) * LANES], ts_ref,
                                                            HEAD_DIM // 2)


def _combine_scratch():
    return [pltpu.VMEM((2, N_EXPERTS * WIN, D_MODEL), BF16), pltpu.SemaphoreType.DMA((2,))]


def _combine_specs():
    spread = pl.BlockSpec((LANES, N_EXPERTS * WIN), lambda i, *_: (0, 0))
    return [pl.BlockSpec((TM, D_MODEL), lambda i, *_: (i, 0)),
            pl.BlockSpec((TM, LANES), lambda i, *_: (i, 0)),
            spread, spread, pl.BlockSpec(memory_space=pl.ANY)]


def _pre(first, l, xs, mods, g_attn, w_in_p, g_mla_q, w_qb_p, g_mla_kv, t_mla, t_swa):
    tile = lambda w: pl.BlockSpec((TM, w), lambda i, *_: (i, 0))
    ctx_tile = lambda w: pl.BlockSpec((TM, w), lambda i, *_: (jnp.minimum(i, CTX_TILES - 1), 0))
    mod_spec = lambda ll: pl.BlockSpec((None, None, N_MOD, D_MODEL), lambda i, *_: (ll, _cond_row(i), 0, 0))
    vec = lambda w: pl.BlockSpec((None, 1, w), lambda i, *_: (l, 0, 0))
    if first:
        in_specs = [ctx_tile(D_MODEL), pl.BlockSpec((TM, D_MODEL), lambda i: (jnp.maximum(i - CTX_TILES, 0), 0))]
    else:
        in_specs = _combine_specs() + [mod_spec(l - 1)]
    in_specs += [mod_spec(l), vec(D_MODEL),
                 pl.BlockSpec((None, D_MODEL, IN_PAD), lambda i, *_: (l, 0, 0)),
                 vec(MLA_Q_RANK),
                 pl.BlockSpec((None, MLA_Q_RANK, MQ_W), lambda i, *_: (l, 0, 0)),
                 vec(MLA_KV_RANK),
                 pl.BlockSpec((3, TM, LANES), lambda i, *_: (0, _rope_blk(i), 0)),
                 pl.BlockSpec((3, TM, LANES), lambda i, *_: (0, _rope_blk(i), 0))]
    widths = [D_MODEL, NA_IN, MQ_W, MLA_KV_RANK, LANES, SWA_IN]
    cache_widths = [NA_OUT, NA_OUT, MLA_KV_RANK, MLA_ROPE_DIM, LANES, LANES]
    args = list(xs) + ([mods] if not first else []) + [
        mods, g_attn.reshape(DEPTH, 1, D_MODEL), w_in_p, g_mla_q.reshape(DEPTH, 1, MLA_Q_RANK), w_qb_p,
        g_mla_kv.reshape(DEPTH, 1, MLA_KV_RANK), t_mla, t_swa]
    return pl.pallas_call(
        functools.partial(_pre_kernel, first),
        grid_spec=pltpu.PrefetchScalarGridSpec(
            num_scalar_prefetch=0 if first else 2,
            grid=(N_TILES,),
            in_specs=in_specs,
            out_specs=[tile(w) for w in widths] + [ctx_tile(w) for w in cache_widths],
            scratch_shapes=[] if first else _combine_scratch()),
        out_shape=([jax.ShapeDtypeStruct((N_TOK, w), F32) for w in widths]
                   + [jax.ShapeDtypeStruct((N_CTX, w), F32) for w in cache_widths]),
        compiler_params=_cparams(("arbitrary",)),
        name="pre_attn",
    )(*args)


def _softmax_parts(parts, sink=None):
    m = parts[0].max(axis=-1, keepdims=True)
    for s in parts[1:]:
        m = jnp.maximum(m, s.max(axis=-1, keepdims=True))
    if sink is not None:
        m = jnp.maximum(m, sink)
    ps = [jnp.exp(s - m) for s in parts]
    den = ps[0].sum(axis=-1, keepdims=True)
    for p in ps[1:]:
        den = den + p.sum(axis=-1, keepdims=True)
    if sink is not None:
        den = den + jnp.exp(sink - m)
    return ps, 1.0 / den


def _mla_head(h, mq, wkvb, keys):
    scale = MLA_QK_DIM ** -0.5
    wk = wkvb[:, h * 2 * HEAD_DIM:h * 2 * HEAD_DIM + MLA_NOPE_DIM]
    wv = wkvb[:, h * 2 * HEAD_DIM + MLA_NOPE_DIM:(h + 1) * 2 * HEAD_DIM]
    qa = _dot_nt(mq[:, h * MLA_NOPE_DIM:(h + 1) * MLA_NOPE_DIM], wk)
    qr = mq[:, MQ_NOPE + h * MLA_ROPE_DIM:MQ_NOPE + (h + 1) * MLA_ROPE_DIM]
    parts = [(_dot_nt(qa, ckv) + _dot_nt(qr, kpe)) * scale for ckv, kpe in keys]
    ps, inv = _softmax_parts(parts)
    lat = _dot(ps[0], keys[0][0])
    for p, (ckv, _) in zip(ps[1:], keys[1:]):
        lat = lat + _dot(p, ckv)
    return _dot(lat * inv, wv)


def _ctx_attn_kernel(l, sink_ref, na_ref, mq_ref, ckv_ref, kpe_ref, sw_ref, wkvb_ref, o_ref):
    @pl.when(pl.program_id(0) >= BATCH)
    def _():
        o_ref[...] = jnp.zeros_like(o_ref)

    @pl.when(pl.program_id(0) < BATCH)
    def _():
        _ctx_attn_body(l, sink_ref, na_ref, mq_ref, ckv_ref, kpe_ref, sw_ref, wkvb_ref, o_ref)


def _ctx_attn_body(l, sink_ref, na_ref, mq_ref, ckv_ref, kpe_ref, sw_ref, wkvb_ref, o_ref):
    scale = HEAD_DIM ** -0.5
    mq = mq_ref[...]
    wkvb = wkvb_ref[...]
    keys = [(ckv_ref[...], kpe_ref[:, :MLA_ROPE_DIM])]
    for h in range(MLA_HEADS):
        o_ref[:, h * MLA_V_DIM:(h + 1) * MLA_V_DIM] = _mla_head(h, mq, wkvb, keys)
    for h in range(SWA_HEADS):
        kh = h // (SWA_HEADS // SWA_KV_HEADS)
        q = sw_ref[:, h * HEAD_DIM:(h + 1) * HEAD_DIM]
        k = sw_ref[:, SWA_OUT + kh * HEAD_DIM:SWA_OUT + (kh + 1) * HEAD_DIM]
        v = sw_ref[:, SWA_OUT + LANES + kh * HEAD_DIM:SWA_OUT + LANES + (kh + 1) * HEAD_DIM]
        (p,), inv = _softmax_parts([_dot_nt(q, k) * scale], sink_ref[l, h])
        o_ref[:, MLA_OUT + h * HEAD_DIM:MLA_OUT + (h + 1) * HEAD_DIM] = _dot(p, v) * inv
    for h in range(NA_HEADS):
        q = na_ref[:, h * HEAD_DIM:(h + 1) * HEAD_DIM]
        k = na_ref[:, NA_OUT + h * HEAD_DIM:NA_OUT + (h + 1) * HEAD_DIM]
        v = na_ref[:, 2 * NA_OUT + h * HEAD_DIM:2 * NA_OUT + (h + 1) * HEAD_DIM]
        (p,), inv = _softmax_parts([_dot_nt(q, k) * scale])
        o_ref[:, MLA_OUT + SWA_OUT + h * HEAD_DIM:MLA_OUT + SWA_OUT + (h + 1) * HEAD_DIM] = _dot(p, v) * inv


def _ctx_attn(l, sink, na, mq, ckv, kpe, sw, w_kvb):
    tile = lambda w: pl.BlockSpec((SEQ, w), lambda b: (jnp.minimum(b, BATCH - 1), 0))
    return pl.pallas_call(
        functools.partial(_ctx_attn_kernel, l),
        grid=(N_TOK // SEQ,),
        in_specs=[pl.BlockSpec(memory_space=pltpu.SMEM), tile(NA_IN), tile(MQ_W), tile(MLA_KV_RANK), tile(LANES),
                  tile(SWA_IN), pl.BlockSpec((None, MLA_KV_RANK, MLA_HEADS * 2 * HEAD_DIM), lambda b: (l, 0, 0))],
        out_specs=pl.BlockSpec((SEQ, D_MODEL), lambda b: (b, 0)),
        out_shape=jax.ShapeDtypeStruct((N_TOK, D_MODEL), F32),
        compiler_params=_cparams(("arbitrary",)),
        name="ctx_attn",
    )(sink, na, mq, ckv, kpe, sw, w_kvb)


def _dec_row(b, j):
    return CTX_TILES + b * DEC_TILES_PER_BATCH + j


def _na_dec_kernel(q_ref, k_ref, v_ref, ck_ref, cv_ref, bias_ref, oin_ref, o_ref):
    del oin_ref
    scale = HEAD_DIM ** -0.5
    j = pl.program_id(1)
    w0 = jnp.clip(j * NA_ROWS_PER_STEP - NA_WIN_H // 2, 0, DEC_SEQ // GRID_W - NA_KEY_ROWS)
    start = pl.multiple_of(w0 * GRID_W, GRID_W)
    for h in range(NA_HEADS):
        sl = slice(h * HEAD_DIM, (h + 1) * HEAD_DIM)
        q = q_ref[:, sl]
        k = k_ref[pl.ds(start, NA_KEYS), sl]
        v = v_ref[pl.ds(start, NA_KEYS), sl]
        s_nb = _dot_nt(q, k) * scale + bias_ref[h]
        s_ctx = _dot_nt(q, ck_ref[:, sl]) * scale
        (p_nb, p_ctx), inv = _softmax_parts([s_nb, s_ctx])
        o_ref[:, sl] = (_dot(p_nb, v) + _dot(p_ctx, cv_ref[:, sl])) * inv


def _na_dec(l, na, ck, cv, bias, o):
    pat = lambda j: jnp.where(j == 0, 0, jnp.where(j == DEC_TILES_PER_BATCH - 1, 2, 1))
    return pl.pallas_call(
        _na_dec_kernel,
        grid=(DEC_BATCH, DEC_TILES_PER_BATCH),
        in_specs=[pl.BlockSpec((TM, NA_OUT), lambda b, j: (_dec_row(b, j), 0)),
                  pl.BlockSpec((DEC_SEQ, NA_OUT), lambda b, j: (N_CTX // DEC_SEQ + b, 1)),
                  pl.BlockSpec((DEC_SEQ, NA_OUT), lambda b, j: (N_CTX // DEC_SEQ + b, 2)),
                  pl.BlockSpec((None, None, PAST_LEN, NA_OUT), lambda b, j: (b, l, 0, 0)),
                  pl.BlockSpec((None, None, PAST_LEN, NA_OUT), lambda b, j: (b, l, 0, 0)),
                  pl.BlockSpec((None, None, NA_HEADS, TM, NA_KEYS), lambda b, j: (l, pat(j), 0, 0, 0)),
                  pl.BlockSpec(memory_space=pl.ANY)],
        out_specs=pl.BlockSpec((TM, NA_OUT), lambda b, j: (_dec_row(b, j), (MLA_OUT + SWA_OUT) // NA_OUT)),
        out_shape=jax.ShapeDtypeStruct((N_TOK, D_MODEL), F32),
        input_output_aliases={6: 0},
        compiler_params=_cparams(("arbitrary", "arbitrary")),
        name="na_dec",
    )(na, na, na, ck, cv, bias, o)


def _mla_dec_kernel(mq_ref, ckv_ref, kpe_ref, cckv_ref, ckpe_ref, wkvb_ref, oin_ref, o_ref):
    del oin_ref
    mq = mq_ref[...]
    wkvb = wkvb_ref[...]
    keys = [(ckv_ref[...], kpe_ref[:, :MLA_ROPE_DIM]), (cckv_ref[...], ckpe_ref[...])]
    for h in range(MLA_HEADS):
        o_ref[:, h * MLA_V_DIM:(h + 1) * MLA_V_DIM] = _mla_head(h, mq, wkvb, keys)


def _mla_dec(l, mq, ckv, kpe, cckv, ckpe, w_kvb, o):
    return pl.pallas_call(
        _mla_dec_kernel,
        grid=(DEC_BATCH, DEC_TILES_PER_BATCH),
        in_specs=[pl.BlockSpec((TM, MQ_W), lambda b, j: (_dec_row(b, j), 0)),
                  pl.BlockSpec((DEC_SEQ, MLA_KV_RANK), lambda b, j: (N_CTX // DEC_SEQ + b, 0)),
                  pl.BlockSpec((DEC_SEQ, LANES), lambda b, j: (N_CTX // DEC_SEQ + b, 0)),
                  pl.BlockSpec((None, None, PAST_LEN, MLA_KV_RANK), lambda b, j: (b, l, 0, 0)),
                  pl.BlockSpec((None, None, PAST_LEN, MLA_ROPE_DIM), lambda b, j: (b, l, 0, 0)),
                  pl.BlockSpec((None, MLA_KV_RANK, MLA_HEADS * 2 * HEAD_DIM), lambda b, j: (l, 0, 0)),
                  pl.BlockSpec(memory_space=pl.ANY)],
        out_specs=pl.BlockSpec((TM, MLA_OUT), lambda b, j: (_dec_row(b, j), 0)),
        out_shape=jax.ShapeDtypeStruct((N_TOK, D_MODEL), F32),
        input_output_aliases={6: 0},
        compiler_params=_cparams(("arbitrary", "arbitrary")),
        name="mla_dec",
    )(mq, ckv, kpe, cckv, ckpe, w_kvb, o)


def _swa_dec_kernel(l, sink_ref, q_ref, k_ref, v_ref, ck_ref, cv_ref, oin_ref, o_ref):
    del oin_ref
    scale = HEAD_DIM ** -0.5
    j = pl.program_id(1)
    start = pl.multiple_of(jnp.clip(j * TM - SWA_WINDOW, 0, DEC_SEQ - SWA_KEYS), SWA_WINDOW)
    qpos = j * TM + lax.broadcasted_iota(jnp.int32, (TM, SWA_KEYS), 0)
    kpos = start + lax.broadcasted_iota(jnp.int32, (TM, SWA_KEYS), 1)
    band = jnp.abs(qpos - kpos) <= SWA_WINDOW
    for h in range(SWA_HEADS):
        kh = h // (SWA_HEADS // SWA_KV_HEADS)
        sl = slice(kh * HEAD_DIM, (kh + 1) * HEAD_DIM)
        q = q_ref[:, h * HEAD_DIM:(h + 1) * HEAD_DIM]
        s_loc = jnp.where(band, _dot_nt(q, k_ref[pl.ds(start, SWA_KEYS), sl]) * scale, NEG)
        s_ctx = _dot_nt(q, ck_ref[:, sl]) * scale
        (p_loc, p_ctx), inv = _softmax_parts([s_loc, s_ctx], sink_ref[l, h])
        o_ref[:, h * HEAD_DIM:(h + 1) * HEAD_DIM] = (
            _dot(p_loc, v_ref[pl.ds(start, SWA_KEYS), sl]) + _dot(p_ctx, cv_ref[:, sl])) * inv


def _swa_dec(l, sink, sw, ck, cv, o):
    return pl.pallas_call(
        functools.partial(_swa_dec_kernel, l),
        grid=(DEC_BATCH, DEC_TILES_PER_BATCH),
        in_specs=[pl.BlockSpec(memory_space=pltpu.SMEM),
                  pl.BlockSpec((TM, SWA_OUT), lambda b, j: (_dec_row(b, j), 0)),
                  pl.BlockSpec((DEC_SEQ, LANES), lambda b, j: (N_CTX // DEC_SEQ + b, SWA_OUT // LANES)),
                  pl.BlockSpec((DEC_SEQ, LANES), lambda b, j: (N_CTX // DEC_SEQ + b, SWA_OUT // LANES + 1)),
                  pl.BlockSpec((None, None, PAST_LEN, LANES), lambda b, j: (b, l, 0, 0)),
                  pl.BlockSpec((None, None, PAST_LEN, LANES), lambda b, j: (b, l, 0, 0)),
                  pl.BlockSpec(memory_space=pl.ANY)],
        out_specs=pl.BlockSpec((TM, SWA_OUT), lambda b, j: (_dec_row(b, j), 1)),
        out_shape=jax.ShapeDtypeStruct((N_TOK, D_MODEL), F32),
        input_output_aliases={6: 0},
        compiler_params=_cparams(("arbitrary", "arbitrary")),
        name="swa_dec",
    )(sink, sw, sw, sw, ck, cv, o)


def _route(sc, sel):
    rows = [sel[e:e + 1, :] for e in range(N_EXPERTS)]

    def beats(a, ia, b, ib):
        return (a > b) | ((a == b) & (ia < ib)) if ia < ib else (a > b)

    in_top = []
    gscore = []
    for g in range(N_EXPERT_GROUPS):
        mem = list(range(g * EXPERTS_PER_GROUP, (g + 1) * EXPERTS_PER_GROUP))
        acc = None
        for e in mem:
            rank = sum(beats(rows[o], o, rows[e], e).astype(jnp.int32) for o in mem if o != e)
            top = rank < 2
            in_top.append(top)
            term = jnp.where(top, rows[e], 0.0)
            acc = term if acc is None else acc + term
        gscore.append(acc)
    gates, chosen = [], []
    for g in range(N_EXPERT_GROUPS):
        lost = sum(beats(gscore[o], o, gscore[g], g).astype(jnp.int32) for o in range(N_EXPERT_GROUPS) if o != g)
        best = lost == 0
        for e in range(g * EXPERTS_PER_GROUP, (g + 1) * EXPERTS_PER_GROUP):
            pick = best & in_top[e]
            chosen.append(pick.astype(F32))
            gates.append(jnp.where(pick, sc[e:e + 1, :], 0.0))
    gate = jnp.concatenate(gates, axis=0)
    return gate / gate.sum(axis=0, keepdims=True), jnp.concatenate(chosen, axis=0)


def _post_kernel(x_ref, o_ref, mod_ref, wout_ref, g_ref, wr_ref, br_ref,
                 y_ref, h2_ref, rt_ref, rtm_ref, cnt_ref):
    attn = jnp.dot(o_ref[...].astype(BF16), wout_ref[...], preferred_element_type=F32)
    y = x_ref[...] + mod_ref[2:3, :] * attn
    y_ref[...] = y
    h2 = _rms(y, g_ref[...]) * (1.0 + mod_ref[4:5, :]) + mod_ref[3:4, :]
    h_hi = h2.astype(BF16)
    h_lo = (h2 - h_hi.astype(F32)).astype(BF16)
    w = wr_ref[...]
    w_hi = w.astype(BF16)
    w_lo = (w - w_hi.astype(F32)).astype(BF16)
    logits = (jnp.dot(h_hi, w_hi, preferred_element_type=F32) + jnp.dot(h_lo, w_hi, preferred_element_type=F32)
              + jnp.dot(h_hi, w_lo, preferred_element_type=F32))
    logits = logits.T[:N_EXPERTS, :]
    sc = 1.0 / (1.0 + jnp.exp(-logits))
    gate, chosen = _route(sc, sc + br_ref[...])
    h2_ref[...] = h_hi
    rt = jnp.concatenate([gate, chosen], axis=0)
    rt_ref[...] = rt
    rtm_ref[...] = jnp.concatenate([rt, jnp.zeros((LANES - 2 * N_EXPERTS, TM), F32)], axis=0).T
    cnt_ref[...] = jnp.broadcast_to(jnp.sum(chosen, axis=1, keepdims=True), (N_EXPERTS, LANES))


def _post(l, x, o, mods, w_out_bf, g_ffn, w_router_p, b_router):
    tile = lambda w: pl.BlockSpec((TM, w), lambda i: (i, 0))
    return pl.pallas_call(
        _post_kernel,
        grid=(N_TILES,),
        in_specs=[tile(D_MODEL), tile(D_MODEL),
                  pl.BlockSpec((None, None, N_MOD, D_MODEL), lambda i: (l, _cond_row(i), 0, 0)),
                  pl.BlockSpec((None, D_MODEL, D_MODEL), lambda i: (l, 0, 0)),
                  pl.BlockSpec((None, 1, D_MODEL), lambda i: (l, 0, 0)),
                  pl.BlockSpec((D_MODEL, LANES), lambda i: (0, 0)),
                  pl.BlockSpec((N_EXPERTS, 1), lambda i: (0, 0))],
        out_specs=[tile(D_MODEL), tile(D_MODEL), pl.BlockSpec((2 * N_EXPERTS, TM), lambda i: (0, i)),
                   tile(LANES), pl.BlockSpec((None, N_EXPERTS, LANES), lambda i: (i, 0, 0))],
        out_shape=[jax.ShapeDtypeStruct((N_TOK, D_MODEL), F32), jax.ShapeDtypeStruct((N_TOK, D_MODEL), BF16),
                   jax.ShapeDtypeStruct((2 * N_EXPERTS, N_TOK), F32),
                   jax.ShapeDtypeStruct((N_TOK, LANES), F32),
                   jax.ShapeDtypeStruct((N_TILES, N_EXPERTS, LANES), F32)],
        compiler_params=_cparams(("arbitrary",)),
        name="post_attn",
    )(x, o, mods, w_out_bf, g_ffn.reshape(DEPTH, 1, D_MODEL), w_router_p, b_router.reshape(N_EXPERTS, 1))


def _shr(x, bits):
    return lax.shift_right_logical(x, jnp.int32(bits))


TR_BITS = TR.bit_length() - 1
ALIGN_BITS = RUN_ALIGN.bit_length() - 1
WIN_BITS = WIN.bit_length() - 1


def _plan_rows(cnt_ref, off_ref, npass_ref, seg_ref):
    def per_expert(e, row0):
        def per_tile(bb, r):
            off_ref[bb * N_EXPERTS + e] = r
            return r + (_shr(cnt_ref[bb * N_EXPERTS + e] + (RUN_ALIGN - 1), ALIGN_BITS) << ALIGN_BITS)

        rows_end = lax.fori_loop(0, N_TILES, per_tile, row0)
        n = _shr(rows_end - row0 + (TR - 1), TR_BITS)
        seg_ref[e] = _shr(row0, TR_BITS)
        seg_ref[N_EXPERTS + e] = n
        seg_ref[2 * N_EXPERTS + e] = rows_end
        seg_ref[3 * N_EXPERTS + e] = row0 + (n << TR_BITS)
        return row0 + (n << TR_BITS)

    end_row = lax.fori_loop(0, N_EXPERTS, per_expert, jnp.int32(0))
    seg_ref[4 * N_EXPERTS] = _shr(end_row, TR_BITS)

    def longest(bb, carry):
        m = lax.fori_loop(0, N_EXPERTS, lambda e, m: jnp.maximum(m, cnt_ref[bb * N_EXPERTS + e]), jnp.int32(0))
        npass_ref[bb] = _shr(m + (WIN - 1), WIN_BITS)
        return carry

    lax.fori_loop(0, N_TILES, longest, 0)


def _dispatch_kernel(cnt_ref, h_ref, rt_ref, et_ref, xs_hbm, off_ref, npass_ref, seg_ref, zbuf, zeros, sem):
    b = pl.program_id(0)
    slot = b % 2

    @pl.when(b == 0)
    def _():
        _plan_rows(cnt_ref, off_ref, npass_ref, seg_ref)

    key = jnp.dot(et_ref[...], _run_keys(rt_ref[...], 1).astype(BF16), preferred_element_type=F32)
    slot_j = lax.broadcasted_iota(jnp.int32, (N_EXPERTS * WIN, TM), 0) % WIN

    def run_copies(bb, sl, p, act):
        def per_expert(e, carry):
            left = cnt_ref[bb * N_EXPERTS + e] - p * WIN
            row = off_ref[bb * N_EXPERTS + e] + p * WIN
            whole = row + WIN <= seg_ref[2 * N_EXPERTS + e]

            @pl.when(whole & (left > 0))
            def _():
                src = pl.multiple_of(e * WIN, WIN)
                act(pltpu.make_async_copy(zbuf.at[sl, pl.ds(src, WIN)],
                                          xs_hbm.at[pl.ds(pl.multiple_of(row, RUN_ALIGN), WIN)], sem))

            for k in range(WIN // RUN_ALIGN):
                @pl.when(jnp.logical_not(whole) & (k * RUN_ALIGN < left))
                def _():
                    src = pl.multiple_of(e * WIN + k * RUN_ALIGN, RUN_ALIGN)
                    dst = pl.multiple_of(row + k * RUN_ALIGN, RUN_ALIGN)
                    act(pltpu.make_async_copy(zbuf.at[sl, pl.ds(src, RUN_ALIGN)],
                                              xs_hbm.at[pl.ds(dst, RUN_ALIGN)], sem))
            return carry

        lax.fori_loop(0, N_EXPERTS, per_expert, 0)

    def fill_and_send(p):
        pick = jnp.where(key == (slot_j + p * WIN).astype(F32), 1.0, 0.0).astype(BF16)
        zbuf[slot] = jnp.dot(pick, h_ref[...], preferred_element_type=F32).astype(BF16)
        run_copies(b, slot, p, lambda cp: cp.start(priority=1))

    @pl.when(b > 0)
    def _():
        run_copies(b - 1, 1 - slot, npass_ref[jnp.maximum(b - 1, 0)] - 1, lambda cp: cp.wait())

    fill_and_send(0)

    def more(p, carry):
        run_copies(b, slot, p - 1, lambda cp: cp.wait())
        fill_and_send(p)
        return carry

    lax.fori_loop(1, npass_ref[b], more, 0)

    @pl.when(b == N_TILES - 1)
    def _():
        run_copies(b, slot, npass_ref[b] - 1, lambda cp: cp.wait())
        zeros[...] = jnp.zeros_like(zeros)

        def fill(act):
            def per_expert(e, carry):
                for k in range(TR // RUN_ALIGN):
                    row = pl.multiple_of(seg_ref[2 * N_EXPERTS + e] + k * RUN_ALIGN, RUN_ALIGN)

                    @pl.when(row < seg_ref[3 * N_EXPERTS + e])
                    def _():
                        act(pltpu.make_async_copy(zeros.at[pl.ds(0, RUN_ALIGN)], xs_hbm.at[pl.ds(row, RUN_ALIGN)], sem))
                return carry

            lax.fori_loop(0, N_EXPERTS, per_expert, 0)

            def per_tile(t, carry):
                act(pltpu.make_async_copy(zeros, xs_hbm.at[pl.ds(pl.multiple_of(t * TR, TR), TR)], sem))
                return carry

            lax.fori_loop(seg_ref[4 * N_EXPERTS], NT, per_tile, 0)

        fill(lambda cp: cp.start())
        fill(lambda cp: cp.wait())


def _dispatch(cnt, h2, rt, et):
    smem = pl.BlockSpec(memory_space=pltpu.SMEM)
    return pl.pallas_call(
        _dispatch_kernel,
        grid_spec=pltpu.PrefetchScalarGridSpec(
            num_scalar_prefetch=1,
            grid=(N_TILES,),
            in_specs=[pl.BlockSpec((TM, D_MODEL), lambda i, *_: (i, 0)),
                      pl.BlockSpec((2 * N_EXPERTS, TM), lambda i, *_: (0, i)),
                      pl.BlockSpec((N_EXPERTS * WIN, 2 * N_EXPERTS), lambda i, *_: (0, 0))],
            out_specs=[pl.BlockSpec(memory_space=pl.ANY), smem, smem, smem],
            scratch_shapes=[pltpu.VMEM((2, N_EXPERTS * WIN, D_MODEL), BF16), pltpu.VMEM((TR, D_MODEL), BF16),
                            pltpu.SemaphoreType.DMA(())]),
        out_shape=[jax.ShapeDtypeStruct((NT * TR, D_MODEL), BF16), jax.ShapeDtypeStruct((N_RUNS,), jnp.int32),
                   jax.ShapeDtypeStruct((N_TILES,), jnp.int32),
                   jax.ShapeDtypeStruct((4 * N_EXPERTS + 1,), jnp.int32)],
        compiler_params=_cparams(("arbitrary",)),
        name="dispatch",
    )(cnt, h2, rt, et)


CAST_ROWS = 128


def _cast_rows(src_ref, dst_ref, n):
    def body(c, carry):
        rows = pl.ds(pl.multiple_of(c * CAST_ROWS, CAST_ROWS), CAST_ROWS)
        dst_ref[rows, :] = src_ref[rows, :].astype(BF16)
        return carry

    lax.fori_loop(0, n, body, 0)


def _experts_kernel(seg_ref, xs_hbm, wg_ref, wu_ref, wd_ref, ys_hbm, wgb, wub, wdb, xbuf, ybuf, semx, semy):
    e = pl.program_id(0)
    t0 = seg_ref[e]
    n = seg_ref[N_EXPERTS + e]
    _cast_rows(wg_ref, wgb, D_MODEL // CAST_ROWS)
    _cast_rows(wu_ref, wub, D_MODEL // CAST_ROWS)
    _cast_rows(wd_ref, wdb, D_EXPERT // CAST_ROWS)

    def rows(k):
        return pl.ds(pl.multiple_of((t0 + k) * TR, TR), TR)

    def fetch(k, s):
        return pltpu.make_async_copy(xs_hbm.at[rows(k)], xbuf.at[s], semx.at[s])

    def put(k, s):
        return pltpu.make_async_copy(ybuf.at[s], ys_hbm.at[rows(k)], semy.at[s])

    @pl.when(n > 0)
    def _():
        fetch(0, 0).start(priority=1)

    def tile(k, carry):
        s = k % 2

        @pl.when(k + 1 < n)
        def _():
            fetch(k + 1, 1 - s).start(priority=1)

        fetch(k, s).wait()

        @pl.when(k >= 2)
        def _():
            put(k - 2, s).wait()

        x = xbuf[s]
        hg = jnp.dot(x, wgb[...], preferred_element_type=F32)
        hu = jnp.dot(x, wub[...], preferred_element_type=F32)
        a = hg * (1.0 / (1.0 + jnp.exp(-hg))) * hu
        ybuf[s] = jnp.dot(a.astype(BF16), wdb[...], preferred_element_type=F32).astype(BF16)
        put(k, s).start(priority=1)
        return carry

    lax.fori_loop(0, n, tile, 0)

    @pl.when(n >= 2)
    def _():
        put(n - 2, n % 2).wait()

    @pl.when(n >= 1)
    def _():
        put(n - 1, (n - 1) % 2).wait()

    @pl.when(e == N_EXPERTS - 1)
    def _():
        ybuf[0] = jnp.zeros((TR, D_MODEL), BF16)

        def fill(act):
            def per_tile(t, carry):
                act(pltpu.make_async_copy(ybuf.at[0], ys_hbm.at[pl.ds(pl.multiple_of(t * TR, TR), TR)], semy.at[0]))
                return carry

            lax.fori_loop(seg_ref[4 * N_EXPERTS], NT, per_tile, 0)

        fill(lambda cp: cp.start())
        fill(lambda cp: cp.wait())


def _experts(l, seg, xs, w_gate, w_up, w_down):
    wspec = lambda a, b: pl.BlockSpec((None, None, a, b), lambda e, seg: (l, e, 0, 0))
    tile_buf = pltpu.VMEM((2, TR, D_MODEL), BF16)
    return pl.pallas_call(
        _experts_kernel,
        grid_spec=pltpu.PrefetchScalarGridSpec(
            num_scalar_prefetch=1,
            grid=(N_EXPERTS,),
            in_specs=[pl.BlockSpec(memory_space=pl.ANY),
                      wspec(D_MODEL, D_EXPERT), wspec(D_MODEL, D_EXPERT), wspec(D_EXPERT, D_MODEL)],
            out_specs=pl.BlockSpec(memory_space=pl.ANY),
            scratch_shapes=[pltpu.VMEM((D_MODEL, D_EXPERT), BF16), pltpu.VMEM((D_MODEL, D_EXPERT), BF16),
                            pltpu.VMEM((D_EXPERT, D_MODEL), BF16), tile_buf, tile_buf,
                            pltpu.SemaphoreType.DMA((2,)), pltpu.SemaphoreType.DMA((2,))]),
        out_shape=jax.ShapeDtypeStruct((NT * TR, D_MODEL), BF16),
        compiler_params=_cparams(("arbitrary",)),
        name="experts",
    )(seg, xs, w_gate, w_up, w_down)


def _final_kernel(off_ref, npass_ref, y_ref, rtm_ref, ek_ref, eg_ref, ys_hbm, mod_ref, g_ref, op_ref, os_ref,
                  wbuf, sem):
    i = pl.program_id(0)
    moe = _combined_moe(off_ref, npass_ref, rtm_ref, ek_ref, eg_ref, ys_hbm, wbuf, sem)
    out = _rms(y_ref[...] + mod_ref[5:6, :] * moe, g_ref[...])

    @pl.when(i < CTX_TILES)
    def _():
        op_ref[...] = out

    @pl.when(i >= CTX_TILES)
    def _():
        os_ref[...] = out


def _final(off, npass, y, rtm, ek, eg, ys, mods, g_final):
    return pl.pallas_call(
        _final_kernel,
        grid_spec=pltpu.PrefetchScalarGridSpec(
            num_scalar_prefetch=2,
            grid=(N_TILES,),
            in_specs=_combine_specs() + [
                pl.BlockSpec((None, None, N_MOD, D_MODEL), lambda i, *_: (DEPTH - 1, _cond_row(i), 0, 0)),
                pl.BlockSpec((1, D_MODEL), lambda i, *_: (0, 0))],
            out_specs=[pl.BlockSpec((TM, D_MODEL), lambda i, *_: (jnp.minimum(i, CTX_TILES - 1), 0)),
                       pl.BlockSpec((TM, D_MODEL), lambda i, *_: (jnp.maximum(i - CTX_TILES, 0), 0))],
            scratch_shapes=_combine_scratch()),
        out_shape=[jax.ShapeDtypeStruct((N_CTX, D_MODEL), F32), jax.ShapeDtypeStruct((N_DEC, D_MODEL), F32)],
        compiler_params=_cparams(("arbitrary",)),
        name="final_norm",
    )(off, npass, y, rtm, ek, eg, ys, mods, g_final.reshape(1, D_MODEL))


def _prep_kernel(win_ref, wout_ref, wi_ref, wo_ref):
    w = win_ref[...]
    split = KPE_OFF + MLA_ROPE_DIM
    wi_ref[...] = jnp.concatenate([w[:, :split], jnp.zeros((CAST_ROWS, SW_OFF - split), F32), w[:, split:]],
                                  axis=1).astype(BF16)
    wo_ref[...] = wout_ref[...].astype(BF16)


def _prep_weights(w_in, w_out):
    n = D_MODEL // CAST_ROWS
    shift = NA_OUT // CAST_ROWS
    return pl.pallas_call(
        _prep_kernel,
        grid=(DEPTH, n),
        in_specs=[pl.BlockSpec((None, CAST_ROWS, IN_WIDTH), lambda l, j: (l, j, 0)),
                  pl.BlockSpec((None, CAST_ROWS, D_MODEL), lambda l, j: (l, (j + shift) % n, 0))],
        out_specs=[pl.BlockSpec((None, CAST_ROWS, IN_PAD), lambda l, j: (l, j, 0)),
                   pl.BlockSpec((None, CAST_ROWS, D_MODEL), lambda l, j: (l, j, 0))],
        out_shape=[jax.ShapeDtypeStruct((DEPTH, D_MODEL, IN_PAD), BF16),
                   jax.ShapeDtypeStruct((DEPTH, D_MODEL, D_MODEL), BF16)],
        compiler_params=_cparams(("arbitrary", "arbitrary")),
        name="prep_weights",
    )(w_in, w_out)


def _rope_tables(rot_dim):
    t = jnp.arange(DEC_SEQ, dtype=jnp.int32)
    row = (t // GRID_W).astype(F32)
    col = (t % GRID_W).astype(F32)
    per_axis = rot_dim // 2
    inv = ROPE_BASE ** (-jnp.arange(0, per_axis, 2, dtype=F32) / per_axis)
    ang = jnp.concatenate([row[:, None] * inv, col[:, None] * inv], axis=-1)
    cos, sin = jnp.cos(ang), jnp.sin(ang)
    zero = jnp.zeros_like(sin)
    rep = LANES // rot_dim
    tabs = [jnp.concatenate([cos, cos], -1), jnp.concatenate([zero, sin], -1), jnp.concatenate([-sin, zero], -1)]
    return jnp.stack([jnp.tile(a, (1, rep)) for a in tabs])


def kernel(x_prompt, x_sample, c, cache_na_k, cache_na_v, cache_mla_ckv, cache_mla_kpe, cache_swa_k, cache_swa_v,
           c_ctx, w_ada, b_ada, g_attn, w_in, g_mla_q, w_mla_qb, g_mla_kv, w_mla_kvb, na_rpb, swa_sink, w_out,
           g_ffn, w_router, b_router, w_gate, w_up, w_down, g_final):
    cond = jnp.concatenate([c_ctx[None], c, jnp.zeros((COND_ROWS - 1 - DEC_BATCH, D_MODEL), F32)], axis=0)
    mods = _ada(cond, w_ada, b_ada).reshape(DEPTH, COND_ROWS, N_MOD, D_MODEL)
    bias = _na_bias(na_rpb)
    t_mla = _rope_tables(MLA_ROPE_DIM)
    t_swa = _rope_tables(HEAD_DIM)

    w_in_p, w_out_p = _prep_weights(w_in, w_out)
    wq = w_mla_qb.reshape(DEPTH, MLA_Q_RANK, MLA_HEADS, MLA_QK_DIM)
    w_qb_p = jnp.concatenate([wq[..., :MLA_NOPE_DIM].reshape(DEPTH, MLA_Q_RANK, MQ_NOPE),
                              wq[..., MLA_NOPE_DIM:].reshape(DEPTH, MLA_Q_RANK, MLA_HEADS * MLA_ROPE_DIM),
                              jnp.zeros((DEPTH, MLA_Q_RANK, MQ_W - MLA_HEADS * MLA_QK_DIM), F32)],
                             axis=-1).astype(BF16)
    w_router_p = jnp.pad(w_router, ((0, 0), (0, LANES - N_EXPERTS)))
    et, ek, eg = _spread_consts()

    c_na_k = cache_na_k.reshape(DEC_BATCH, DEPTH, PAST_LEN, NA_OUT)
    c_na_v = cache_na_v.reshape(DEC_BATCH, DEPTH, PAST_LEN, NA_OUT)
    c_sw_k = cache_swa_k.reshape(DEC_BATCH, DEPTH, PAST_LEN, LANES)
    c_sw_v = cache_swa_v.reshape(DEC_BATCH, DEPTH, PAST_LEN, LANES)

    y = ys = rtm = off = npass = None
    caches = [[] for _ in range(6)]
    for l in range(DEPTH):
        if l == 0:
            srcs = [x_prompt.reshape(N_CTX, D_MODEL), x_sample.reshape(N_DEC, D_MODEL)]
        else:
            srcs = [off, npass, y, rtm, ek, eg, ys]
        x, na, mq, ckv, kpe, sw, *slabs = _pre(l == 0, l, srcs, mods, g_attn, w_in_p, g_mla_q, w_qb_p, g_mla_kv,
                                               t_mla, t_swa)
        for dst, slab in zip(caches, slabs):
            dst.append(slab)
        o = _ctx_attn(l, swa_sink, na, mq, ckv, kpe, sw, w_mla_kvb)
        o = _mla_dec(l, mq, ckv, kpe, cache_mla_ckv, cache_mla_kpe, w_mla_kvb, o)
        o = _swa_dec(l, swa_sink, sw, c_sw_k, c_sw_v, o)
        o = _na_dec(l, na, c_na_k, c_na_v, bias, o)
        y, h2, rt, rtm, cnt = _post(l, x, o, mods, w_out_p, g_ffn, w_router_p, b_router)
        xs, off, npass, seg = _dispatch(cnt[:, :, 0].astype(jnp.int32).reshape(-1), h2, rt, et)
        ys = _experts(l, seg, xs, w_gate, w_up, w_down)
    y_prompt, y_sample = _final(off, npass, y, rtm, ek, eg, ys, mods, g_final)

    def stack(parts, tail):
        return jnp.stack([p.reshape((BATCH, SEQ) + tail) for p in parts], axis=1)

    return (y_prompt.reshape(BATCH, SEQ, D_MODEL), y_sample.reshape(DEC_BATCH, DEC_SEQ, D_MODEL),
            stack(caches[0], (NA_HEADS, HEAD_DIM)), stack(caches[1], (NA_HEADS, HEAD_DIM)),
            stack(caches[2], (MLA_KV_RANK,)), stack(caches[3], (MLA_ROPE_DIM,)),
            stack(caches[4], (SWA_KV_HEADS, HEAD_DIM)), stack(caches[5], (SWA_KV_HEADS, HEAD_DIM)))
```

```python
import functools

import jax
import jax.numpy as jnp
from jax import lax
from jax.experimental import pallas as pl
from jax.experimental.pallas import tpu as pltpu

D_MODEL = 1024
BATCH = 16
SEQ = 256
DEPTH = 4
DEC_BATCH = 2
DEC_SEQ = 2048
PAST_LEN = 256
GRID_W = 64
HEAD_DIM = 64
NA_HEADS = 4
NA_WIN_H = 8
NA_WIN_W = 16
MLA_HEADS = 6
MLA_Q_RANK = 256
MLA_KV_RANK = 128
MLA_NOPE_DIM = 64
MLA_ROPE_DIM = 32
MLA_V_DIM = 64
MLA_QK_DIM = MLA_NOPE_DIM + MLA_ROPE_DIM
SWA_HEADS = 6
SWA_KV_HEADS = 2
SWA_WINDOW = 128
ROPE_BASE = 10000.0
N_EXPERTS = 16
N_EXPERT_GROUPS = 4
EXPERTS_PER_GROUP = 4
D_EXPERT = 512
RMS_EPS = 1e-6
N_MOD = 6

NA_IN = 3 * NA_HEADS * HEAD_DIM
MLA_IN = MLA_Q_RANK + MLA_KV_RANK + MLA_ROPE_DIM
SWA_IN = (SWA_HEADS + 2 * SWA_KV_HEADS) * HEAD_DIM
IN_WIDTH = NA_IN + MLA_IN + SWA_IN
NA_OUT = NA_HEADS * HEAD_DIM
MLA_OUT = MLA_HEADS * MLA_V_DIM
SWA_OUT = SWA_HEADS * HEAD_DIM

LANES = 128
N_CTX = BATCH * SEQ
N_DEC = DEC_BATCH * DEC_SEQ
N_TOK = N_CTX + N_DEC
TM = 256
N_TILES = N_TOK // TM
CTX_TILES = N_CTX // TM
DEC_TILES_PER_BATCH = DEC_SEQ // TM
COND_ROWS = 8
KPE_OFF = NA_IN + MLA_Q_RANK + MLA_KV_RANK
SW_OFF = KPE_OFF + LANES
IN_PAD = SW_OFF + SWA_IN
MQ_NOPE = MLA_HEADS * MLA_NOPE_DIM
MQ_W = MQ_NOPE + MLA_HEADS * LANES
MLA_SCALE = MLA_QK_DIM ** -0.5
MLA_GROUP = 3
NA_ROWS_PER_STEP = TM // GRID_W
NA_KEY_ROWS = 12
NA_KEYS = NA_KEY_ROWS * GRID_W
SWA_KEYS = 512
NEG = -1e30
TR = 512
N_ASSIGN = 2 * N_TOK
RUN_ALIGN = 16
WIN = 64
N_RUNS = N_TILES * N_EXPERTS
NT = -(-(N_ASSIGN + N_RUNS * (RUN_ALIGN - 1) + N_EXPERTS * (TR - 1) + WIN) // TR)

F32 = jnp.float32
BF16 = jnp.bfloat16
VMEM_LIMIT = 56 * 1024 * 1024


def _cparams(sem):
    return pltpu.CompilerParams(dimension_semantics=sem, vmem_limit_bytes=VMEM_LIMIT)


def _cond_row(i):
    return jnp.where(i < CTX_TILES, 0, 1 + (i - CTX_TILES) // DEC_TILES_PER_BATCH)


def _rope_blk(i):
    return jnp.where(i < CTX_TILES, 0, (i - CTX_TILES) % DEC_TILES_PER_BATCH)


def _rms(x, g):
    ms = jnp.mean(x * x, axis=-1, keepdims=True)
    return x * lax.rsqrt(ms + RMS_EPS) * g


def _dot(a, b):
    return jnp.dot(a.astype(BF16), b.astype(BF16), preferred_element_type=F32)


def _dot_nt(a, b):
    return lax.dot_general(a.astype(BF16), b.astype(BF16), (((1,), (1,)), ((), ())),
                           preferred_element_type=F32)


def _ada_kernel(c_ref, w_ref, b_ref, o_ref):
    c = c_ref[...]
    s = c * (1.0 / (1.0 + jnp.exp(-c)))
    o_ref[...] = _dot(s, w_ref[...]) + b_ref[...]


def _ada(cond, w_ada, b_ada):
    tn = 1536
    n = N_MOD * D_MODEL
    return pl.pallas_call(
        _ada_kernel,
        grid=(DEPTH, n // tn),
        in_specs=[pl.BlockSpec((COND_ROWS, D_MODEL), lambda l, j: (0, 0)),
                  pl.BlockSpec((None, D_MODEL, tn), lambda l, j: (l, 0, j)),
                  pl.BlockSpec((None, 1, tn), lambda l, j: (l, 0, j))],
        out_specs=pl.BlockSpec((None, COND_ROWS, tn), lambda l, j: (l, 0, j)),
        out_shape=jax.ShapeDtypeStruct((DEPTH, COND_ROWS, n), F32),
        compiler_params=_cparams(("arbitrary", "arbitrary")),
        name="ada_mod",
    )(cond, w_ada, b_ada.reshape(DEPTH, 1, n))


def _bias_kernel(rpb_ref, o_ref):
    g = pl.program_id(0)
    base = g * ((2 * NA_WIN_H - 1) * (2 * NA_WIN_W - 1))
    qc = lax.broadcasted_iota(jnp.int32, (GRID_W, GRID_W), 0)
    kc = lax.broadcasted_iota(jnp.int32, (GRID_W, GRID_W), 1)
    dc = jnp.clip(kc - qc + (NA_WIN_W - 1), 0, 2 * NA_WIN_W - 2)
    cs = jnp.clip(qc - NA_WIN_W // 2, 0, GRID_W - NA_WIN_W)
    col_ok = (kc >= cs) & (kc < cs + NA_WIN_W)
    neg = jnp.full((GRID_W, GRID_W), NEG, F32)
    tabs = []
    for a in range(2 * NA_WIN_H - 1):
        t = jnp.zeros((GRID_W, GRID_W), F32)
        for b in range(2 * NA_WIN_W - 1):
            t = jnp.where(dc == b, rpb_ref[base + a * (2 * NA_WIN_W - 1) + b], t)
        tabs.append(jnp.where(col_ok, t, NEG))
    for p in range(3):
        for qi in range(NA_ROWS_PER_STEP):
            for kj in range(NA_KEY_ROWS):
                if p == 0:
                    ok, dr = kj < NA_WIN_H, kj - qi + 7
                elif p == 1:
                    ok, dr = qi <= kj < qi + NA_WIN_H, kj - qi + 3
                else:
                    ok, dr = kj >= NA_KEY_ROWS - NA_WIN_H, kj - qi - 1
                blk = tabs[dr] if ok else neg
                o_ref[p, qi * GRID_W:(qi + 1) * GRID_W, kj * GRID_W:(kj + 1) * GRID_W] = blk


def _na_bias(na_rpb):
    return pl.pallas_call(
        _bias_kernel,
        grid=(DEPTH * NA_HEADS,),
        in_specs=[pl.BlockSpec(memory_space=pltpu.SMEM)],
        out_specs=pl.BlockSpec((None, 3, None, TM, NA_KEYS),
                               lambda g: (g // NA_HEADS, 0, g % NA_HEADS, 0, 0)),
        out_shape=jax.ShapeDtypeStruct((DEPTH, 3, NA_HEADS, TM, NA_KEYS), F32),
        compiler_params=_cparams(("arbitrary",)),
        name="na_bias",
    )(na_rpb.reshape(-1))


def _rope128(x, t_ref, half):
    return (x * t_ref[0] + pltpu.roll(x, half, 1) * t_ref[1]
            + pltpu.roll(x, LANES - half, 1) * t_ref[2])


def _spread_consts():
    slot_e = jnp.arange(N_EXPERTS * WIN, dtype=jnp.int32) // WIN
    idx32 = jnp.arange(2 * N_EXPERTS, dtype=jnp.int32)
    idx128 = jnp.arange(LANES, dtype=jnp.int32)
    et = (idx32[None, :] == slot_e[:, None] + N_EXPERTS).astype(BF16)
    ek = (idx128[:, None] == slot_e[None, :] + N_EXPERTS).astype(BF16)
    eg = (idx128[:, None] == slot_e[None, :]).astype(BF16)
    return et, ek, eg


def _run_keys(chosen, token_axis):
    a = lax.broadcasted_iota(jnp.int32, (TM, TM), 0)
    b = lax.broadcasted_iota(jnp.int32, (TM, TM), 1)
    if token_axis == 1:
        rank = jnp.dot(chosen.astype(BF16), (a < b).astype(BF16), preferred_element_type=F32)
    else:
        rank = jnp.dot((b < a).astype(BF16), chosen.astype(BF16), preferred_element_type=F32)
    return jnp.where(chosen > 0.5, rank, -1.0)


def _windows_start(i, slot, p, off_ref, ys_hbm, wbuf, sem):
    for e in range(N_EXPERTS):
        row = pl.multiple_of(off_ref[i * N_EXPERTS + e] + p * WIN, RUN_ALIGN)
        pltpu.make_async_copy(ys_hbm.at[pl.ds(row, WIN)], wbuf.at[slot, pl.ds(e * WIN, WIN)],
                              sem.at[slot]).start(priority=1)


def _windows_wait(slot, ys_hbm, wbuf, sem):
    pltpu.make_async_copy(ys_hbm.at[pl.ds(0, N_EXPERTS * WIN)], wbuf.at[slot], sem.at[slot]).wait()


def _combined_moe(off_ref, npass_ref, rtm_ref, ek_ref, eg_ref, ys_hbm, wbuf, sem):
    i = pl.program_id(0)
    slot = i % 2

    @pl.when(i == 0)
    def _():
        _windows_start(0, 0, 0, off_ref, ys_hbm, wbuf, sem)

    @pl.when(i + 1 < N_TILES)
    def _():
        _windows_start(i + 1, 1 - slot, 0, off_ref, ys_hbm, wbuf, sem)

    r = rtm_ref[...]
    key = jnp.dot(_run_keys(r, 0).astype(BF16), ek_ref[...], preferred_element_type=F32)
    gate = jnp.dot(r.astype(BF16), eg_ref[...], preferred_element_type=F32)
    slot_j = lax.broadcasted_iota(jnp.int32, (TM, N_EXPERTS * WIN), 1) % WIN

    def contrib(p):
        g = jnp.where(key == (slot_j + p * WIN).astype(F32), gate, 0.0).astype(BF16)
        return jnp.dot(g, wbuf[slot], preferred_element_type=F32)

    _windows_wait(slot, ys_hbm, wbuf, sem)
    acc = contrib(0)

    def extra(p, acc):
        _windows_start(i, slot, p, off_ref, ys_hbm, wbuf, sem)
        _windows_wait(slot, ys_hbm, wbuf, sem)
        return acc + contrib(p)

    return lax.fori_loop(1, npass_ref[i], extra, acc)


def _pre_kernel(first, *refs):
    i = pl.program_id(0)
    if first:
        (xp_ref, xs_ref, mod_ref, g_ref, win_ref, gq_ref, wqb_ref, gkv_ref, tm_ref, ts_ref,
         xo_ref, na_ref, mq_ref, ckv_ref, kpe_ref, sw_ref, *cache_refs) = refs
        x = jnp.where(i < CTX_TILES, xp_ref[...], xs_ref[...])
    else:
        (off_ref, npass_ref, y_ref, rtm_ref, ek_ref, eg_ref, ys_hbm, modp_ref, mod_ref, g_ref, win_ref, gq_ref,
         wqb_ref, gkv_ref, tm_ref, ts_ref, xo_ref, na_ref, mq_ref, ckv_ref, kpe_ref, sw_ref, *rest) = refs
        *cache_refs, wbuf, sem = rest
        moe = _combined_moe(off_ref, npass_ref, rtm_ref, ek_ref, eg_ref, ys_hbm, wbuf, sem)
        x = y_ref[...] + modp_ref[5:6, :] * moe
    xo_ref[...] = x
    cnk_ref, cnv_ref, cckv_ref, ckpe_ref, csk_ref, csv_ref = cache_refs

    h = _rms(x, g_ref[...]) * (1.0 + mod_ref[1:2, :]) + mod_ref[0:1, :]
    z = jnp.dot(h.astype(BF16), win_ref[...], preferred_element_type=F32)
    na_ref[...] = z[:, :NA_IN].astype(BF16)
    cq = _rms(z[:, NA_IN:NA_IN + MLA_Q_RANK], gq_ref[...])
    ckv = _rms(z[:, NA_IN + MLA_Q_RANK:KPE_OFF], gkv_ref[...])
    ckv_ref[...] = ckv.astype(BF16)
    mq = jnp.dot(cq.astype(BF16), wqb_ref[...], preferred_element_type=F32)
    kpe = z[:, KPE_OFF:SW_OFF]
    sw = z[:, SW_OFF:IN_PAD]
    mq_ref[:, :MQ_NOPE] = mq[:, :MQ_NOPE].astype(BF16)
    sw_ref[:, SWA_OUT + LANES:] = sw[:, SWA_OUT + LANES:].astype(BF16)

    @pl.when(i < CTX_TILES)
    def _():
        mq_ref[:, MQ_NOPE:] = (mq[:, MQ_NOPE:] * MLA_SCALE).astype(BF16)
        kpe_ref[...] = kpe.astype(BF16)
        sw_ref[:, :SWA_OUT + LANES] = sw[:, :SWA_OUT + LANES].astype(BF16)
        cnk_ref[...] = z[:, NA_OUT:2 * NA_OUT]
        cnv_ref[...] = z[:, 2 * NA_OUT:NA_IN]
        cckv_ref[...] = ckv
        ckpe_ref[...] = kpe[:, :MLA_ROPE_DIM]
        csk_ref[...] = sw[:, SWA_OUT:SWA_OUT + LANES]
        csv_ref[...] = sw[:, SWA_OUT + LANES:]

    @pl.when(i >= CTX_TILES)
    def _():
        for c in range(MQ_NOPE // LANES, MQ_W // LANES):
            roped = _rope128(mq[:, c * LANES:(c + 1) * LANES], tm_ref, MLA_ROPE_DIM // 2)
            mq_ref[:, c * LANES:(c + 1) * LANES] = (roped * MLA_SCALE).astype(BF16)
        kpe_ref[...] = _rope128(kpe, tm_ref, MLA_ROPE_DIM // 2).astype(BF16)
        for c in range((SWA_OUT + LANES) // LANES):
            sw_ref[:, c * LANES:(c + 1) * LANES] = _rope128(sw[:, c * LANES:(c + 1) * LANES], ts_ref,
                                                            HEAD_DIM // 2).astype(BF16)


def _combine_scratch():
    return [pltpu.VMEM((2, N_EXPERTS * WIN, D_MODEL), BF16), pltpu.SemaphoreType.DMA((2,))]


def _combine_specs():
    spread = pl.BlockSpec((LANES, N_EXPERTS * WIN), lambda i, *_: (0, 0))
    return [pl.BlockSpec((TM, D_MODEL), lambda i, *_: (i, 0)),
            pl.BlockSpec((TM, LANES), lambda i, *_: (i, 0)),
            spread, spread, pl.BlockSpec(memory_space=pl.ANY)]


def _pre(first, l, xs, mods, g_attn, w_in_p, g_mla_q, w_qb_p, g_mla_kv, t_mla, t_swa):
    tile = lambda w: pl.BlockSpec((TM, w), lambda i, *_: (i, 0))
    ctx_tile = lambda w: pl.BlockSpec((TM, w), lambda i, *_: (jnp.minimum(i, CTX_TILES - 1), 0))
    mod_spec = lambda ll: pl.BlockSpec((None, None, N_MOD, D_MODEL), lambda i, *_: (ll, _cond_row(i), 0, 0))
    vec = lambda w: pl.BlockSpec((None, 1, w), lambda i, *_: (l, 0, 0))
    if first:
        in_specs = [ctx_tile(D_MODEL), pl.BlockSpec((TM, D_MODEL), lambda i: (jnp.maximum(i - CTX_TILES, 0), 0))]
    else:
        in_specs = _combine_specs() + [mod_spec(l - 1)]
    in_specs += [mod_spec(l), vec(D_MODEL),
                 pl.BlockSpec((None, D_MODEL, IN_PAD), lambda i, *_: (l, 0, 0)),
                 vec(MLA_Q_RANK),
                 pl.BlockSpec((None, MLA_Q_RANK, MQ_W), lambda i, *_: (l, 0, 0)),
                 vec(MLA_KV_RANK),
                 pl.BlockSpec((3, TM, LANES), lambda i, *_: (0, _rope_blk(i), 0)),
                 pl.BlockSpec((3, TM, LANES), lambda i, *_: (0, _rope_blk(i), 0))]
    widths = [D_MODEL, NA_IN, MQ_W, MLA_KV_RANK, LANES, SWA_IN]
    cache_widths = [NA_OUT, NA_OUT, MLA_KV_RANK, MLA_ROPE_DIM, LANES, LANES]
    args = list(xs) + ([mods] if not first else []) + [
        mods, g_attn.reshape(DEPTH, 1, D_MODEL), w_in_p, g_mla_q.reshape(DEPTH, 1, MLA_Q_RANK), w_qb_p,
        g_mla_kv.reshape(DEPTH, 1, MLA_KV_RANK), t_mla, t_swa]
    return pl.pallas_call(
        functools.partial(_pre_kernel, first),
        grid_spec=pltpu.PrefetchScalarGridSpec(
            num_scalar_prefetch=0 if first else 2,
            grid=(N_TILES,),
            in_specs=in_specs,
            out_specs=[tile(w) for w in widths] + [ctx_tile(w) for w in cache_widths],
            scratch_shapes=[] if first else _combine_scratch()),
        out_shape=([jax.ShapeDtypeStruct((N_TOK, w), BF16 if n else F32) for n, w in enumerate(widths)]
                   + [jax.ShapeDtypeStruct((N_CTX, w), F32) for w in cache_widths]),
        compiler_params=_cparams(("arbitrary",)),
        name="pre_attn",
    )(*args)


def _softmax_parts(parts, sink=None):
    m = parts[0].max(axis=-1, keepdims=True)
    for s in parts[1:]:
        m = jnp.maximum(m, s.max(axis=-1, keepdims=True))
    if sink is not None:
        m = jnp.maximum(m, sink)
    ps = [jnp.exp(s - m) for s in parts]
    den = ps[0].sum(axis=-1, keepdims=True)
    for p in ps[1:]:
        den = den + p.sum(axis=-1, keepdims=True)
    if sink is not None:
        den = den + jnp.exp(sink - m)
    return ps, 1.0 / den


def _mla_heads(mq_ref, wkvb_ref, kcat, o_ref):
    rows = mq_ref.shape[0]
    for g in range(MLA_HEADS // MLA_GROUP):
        heads = range(g * MLA_GROUP, (g + 1) * MLA_GROUP)
        qs = []
        for h in heads:
            wk = wkvb_ref[:, h * 2 * HEAD_DIM:h * 2 * HEAD_DIM + MLA_NOPE_DIM]
            qa = _dot_nt(mq_ref[:, h * MLA_NOPE_DIM:(h + 1) * MLA_NOPE_DIM], wk) * MLA_SCALE
            qr = mq_ref[:, MQ_NOPE + h * LANES:MQ_NOPE + (h + 1) * LANES]
            qs.append(jnp.concatenate([qa.astype(BF16), qr], axis=1))
        s = _dot_nt(jnp.concatenate(qs, axis=0), kcat[...])
        p = jnp.exp(s - s.max(axis=-1, keepdims=True))
        inv = 1.0 / p.sum(axis=-1, keepdims=True)
        lat = _dot(p, kcat[:, :LANES]) * inv
        for n, h in enumerate(heads):
            wv = wkvb_ref[:, h * 2 * HEAD_DIM + MLA_NOPE_DIM:(h + 1) * 2 * HEAD_DIM]
            o_ref[:, h * MLA_V_DIM:(h + 1) * MLA_V_DIM] = _dot(lat[n * rows:(n + 1) * rows], wv).astype(o_ref.dtype)


def _ctx_attn_kernel(l, sink_ref, na_ref, mq_ref, ckv_ref, kpe_ref, sw_ref, wkvb_ref, o_ref):
    @pl.when(pl.program_id(0) >= BATCH)
    def _():
        o_ref[...] = jnp.zeros_like(o_ref)

    @pl.when(pl.program_id(0) < BATCH)
    def _():
        _ctx_attn_body(l, sink_ref, na_ref, mq_ref, ckv_ref, kpe_ref, sw_ref, wkvb_ref, o_ref)


def _ctx_attn_body(l, sink_ref, na_ref, mq_ref, ckv_ref, kpe_ref, sw_ref, wkvb_ref, o_ref):
    scale = HEAD_DIM ** -0.5
    _mla_heads(mq_ref, wkvb_ref, jnp.concatenate([ckv_ref[...], kpe_ref[...]], axis=1), o_ref)
    for h in range(SWA_HEADS):
        kh = h // (SWA_HEADS // SWA_KV_HEADS)
        q = sw_ref[:, h * HEAD_DIM:(h + 1) * HEAD_DIM]
        k = sw_ref[:, SWA_OUT + kh * HEAD_DIM:SWA_OUT + (kh + 1) * HEAD_DIM]
        v = sw_ref[:, SWA_OUT + LANES + kh * HEAD_DIM:SWA_OUT + LANES + (kh + 1) * HEAD_DIM]
        (p,), inv = _softmax_parts([_dot_nt(q, k) * scale], sink_ref[l, h])
        o_ref[:, MLA_OUT + h * HEAD_DIM:MLA_OUT + (h + 1) * HEAD_DIM] = (_dot(p, v) * inv).astype(o_ref.dtype)
    for h in range(NA_HEADS):
        q = na_ref[:, h * HEAD_DIM:(h + 1) * HEAD_DIM]
        k = na_ref[:, NA_OUT + h * HEAD_DIM:NA_OUT + (h + 1) * HEAD_DIM]
        v = na_ref[:, 2 * NA_OUT + h * HEAD_DIM:2 * NA_OUT + (h + 1) * HEAD_DIM]
        (p,), inv = _softmax_parts([_dot_nt(q, k) * scale])
        o_ref[:, MLA_OUT + SWA_OUT + h * HEAD_DIM:MLA_OUT + SWA_OUT + (h + 1) * HEAD_DIM] = (
            _dot(p, v) * inv).astype(o_ref.dtype)


def _ctx_attn(l, sink, na, mq, ckv, kpe, sw, w_kvb):
    tile = lambda w: pl.BlockSpec((SEQ, w), lambda b: (jnp.minimum(b, BATCH - 1), 0))
    return pl.pallas_call(
        functools.partial(_ctx_attn_kernel, l),
        grid=(N_TOK // SEQ,),
        in_specs=[pl.BlockSpec(memory_space=pltpu.SMEM), tile(NA_IN), tile(MQ_W), tile(MLA_KV_RANK), tile(LANES),
                  tile(SWA_IN), pl.BlockSpec((None, MLA_KV_RANK, MLA_HEADS * 2 * HEAD_DIM), lambda b: (l, 0, 0))],
        out_specs=pl.BlockSpec((SEQ, D_MODEL), lambda b: (b, 0)),
        out_shape=jax.ShapeDtypeStruct((N_TOK, D_MODEL), BF16),
        compiler_params=_cparams(("arbitrary",)),
        name="ctx_attn",
    )(sink, na, mq, ckv, kpe, sw, w_kvb)


def _dec_row(b, j):
    return CTX_TILES + b * DEC_TILES_PER_BATCH + j


def _na_dec_kernel(q_ref, k_ref, v_ref, ck_ref, cv_ref, bias_ref, oin_ref, o_ref):
    del oin_ref
    scale = HEAD_DIM ** -0.5
    j = pl.program_id(1)
    w0 = jnp.clip(j * NA_ROWS_PER_STEP - NA_WIN_H // 2, 0, DEC_SEQ // GRID_W - NA_KEY_ROWS)
    start = pl.multiple_of(w0 * GRID_W, GRID_W)
    for h in range(NA_HEADS):
        sl = slice(h * HEAD_DIM, (h + 1) * HEAD_DIM)
        q = q_ref[:, sl]
        k = k_ref[pl.ds(start, NA_KEYS), sl]
        v = v_ref[pl.ds(start, NA_KEYS), sl]
        s_nb = _dot_nt(q, k) * scale + bias_ref[h]
        s_ctx = _dot_nt(q, ck_ref[:, sl]) * scale
        (p_nb, p_ctx), inv = _softmax_parts([s_nb, s_ctx])
        o_ref[:, sl] = ((_dot(p_nb, v) + _dot(p_ctx, cv_ref[:, sl])) * inv).astype(o_ref.dtype)


def _na_dec(l, na, ck, cv, bias, o):
    pat = lambda j: jnp.where(j == 0, 0, jnp.where(j == DEC_TILES_PER_BATCH - 1, 2, 1))
    return pl.pallas_call(
        _na_dec_kernel,
        grid=(DEC_BATCH, DEC_TILES_PER_BATCH),
        in_specs=[pl.BlockSpec((TM, NA_OUT), lambda b, j: (_dec_row(b, j), 0)),
                  pl.BlockSpec((DEC_SEQ, NA_OUT), lambda b, j: (N_CTX // DEC_SEQ + b, 1)),
                  pl.BlockSpec((DEC_SEQ, NA_OUT), lambda b, j: (N_CTX // DEC_SEQ + b, 2)),
                  pl.BlockSpec((None, None, PAST_LEN, NA_OUT), lambda b, j: (b, l, 0, 0)),
                  pl.BlockSpec((None, None, PAST_LEN, NA_OUT), lambda b, j: (b, l, 0, 0)),
                  pl.BlockSpec((None, None, NA_HEADS, TM, NA_KEYS), lambda b, j: (l, pat(j), 0, 0, 0)),
                  pl.BlockSpec(memory_space=pl.ANY)],
        out_specs=pl.BlockSpec((TM, NA_OUT), lambda b, j: (_dec_row(b, j), (MLA_OUT + SWA_OUT) // NA_OUT)),
        out_shape=jax.ShapeDtypeStruct((N_TOK, D_MODEL), BF16),
        input_output_aliases={6: 0},
        compiler_params=_cparams(("arbitrary", "arbitrary")),
        name="na_dec",
    )(na, na, na, ck, cv, bias, o)


KCAT_ROWS = 256


def _mla_dec_kernel(mq_ref, ckv_ref, kpe_ref, cckv_ref, ckpe_ref, wkvb_ref, oin_ref, o_ref, kcat):
    del oin_ref

    def assemble(c, carry):
        rows = pl.ds(pl.multiple_of(c * KCAT_ROWS, KCAT_ROWS), KCAT_ROWS)
        kcat[rows, :LANES] = ckv_ref[rows, :]
        kcat[rows, LANES:] = kpe_ref[rows, :]
        return carry

    first = pl.program_id(1) == 0
    lax.fori_loop(0, jnp.where(first, DEC_SEQ // KCAT_ROWS, 0), assemble, 0)

    def context(c, carry):
        kcat[DEC_SEQ:, :LANES] = cckv_ref[...].astype(BF16)
        kcat[DEC_SEQ:, LANES:] = jnp.concatenate(
            [ckpe_ref[...], jnp.zeros((PAST_LEN, LANES - MLA_ROPE_DIM), F32)], axis=1).astype(BF16)
        return carry

    lax.fori_loop(0, jnp.where(first, 1, 0), context, 0)
    _mla_heads(mq_ref, wkvb_ref, kcat, o_ref)


def _mla_dec(l, mq, ckv, kpe, cckv, ckpe, w_kvb, o):
    return pl.pallas_call(
        _mla_dec_kernel,
        grid=(DEC_BATCH, DEC_TILES_PER_BATCH),
        in_specs=[pl.BlockSpec((TM, MQ_W), lambda b, j: (_dec_row(b, j), 0)),
                  pl.BlockSpec((DEC_SEQ, MLA_KV_RANK), lambda b, j: (N_CTX // DEC_SEQ + b, 0)),
                  pl.BlockSpec((DEC_SEQ, LANES), lambda b, j: (N_CTX // DEC_SEQ + b, 0)),
                  pl.BlockSpec((None, None, PAST_LEN, MLA_KV_RANK), lambda b, j: (b, l, 0, 0)),
                  pl.BlockSpec((None, None, PAST_LEN, MLA_ROPE_DIM), lambda b, j: (b, l, 0, 0)),
                  pl.BlockSpec((None, MLA_KV_RANK, MLA_HEADS * 2 * HEAD_DIM), lambda b, j: (l, 0, 0)),
                  pl.BlockSpec(memory_space=pl.ANY)],
        out_specs=pl.BlockSpec((TM, MLA_OUT), lambda b, j: (_dec_row(b, j), 0)),
        out_shape=jax.ShapeDtypeStruct((N_TOK, D_MODEL), BF16),
        scratch_shapes=[pltpu.VMEM((DEC_SEQ + PAST_LEN, 2 * LANES), BF16)],
        input_output_aliases={6: 0},
        compiler_params=_cparams(("arbitrary", "arbitrary")),
        name="mla_dec",
    )(mq, ckv, kpe, cckv, ckpe, w_kvb, o)


def _swa_dec_kernel(l, sink_ref, q_ref, k_ref, v_ref, ck_ref, cv_ref, oin_ref, o_ref):
    del oin_ref
    scale = HEAD_DIM ** -0.5
    j = pl.program_id(1)
    start = pl.multiple_of(jnp.clip(j * TM - SWA_WINDOW, 0, DEC_SEQ - SWA_KEYS), SWA_WINDOW)
    qpos = j * TM + lax.broadcasted_iota(jnp.int32, (TM, SWA_KEYS), 0)
    kpos = start + lax.broadcasted_iota(jnp.int32, (TM, SWA_KEYS), 1)
    band = jnp.abs(qpos - kpos) <= SWA_WINDOW
    for h in range(SWA_HEADS):
        kh = h // (SWA_HEADS // SWA_KV_HEADS)
        sl = slice(kh * HEAD_DIM, (kh + 1) * HEAD_DIM)
        q = q_ref[:, h * HEAD_DIM:(h + 1) * HEAD_DIM]
        s_loc = jnp.where(band, _dot_nt(q, k_ref[pl.ds(start, SWA_KEYS), sl]) * scale, NEG)
        s_ctx = _dot_nt(q, ck_ref[:, sl]) * scale
        (p_loc, p_ctx), inv = _softmax_parts([s_loc, s_ctx], sink_ref[l, h])
        o_ref[:, h * HEAD_DIM:(h + 1) * HEAD_DIM] = (
            (_dot(p_loc, v_ref[pl.ds(start, SWA_KEYS), sl]) + _dot(p_ctx, cv_ref[:, sl])) * inv).astype(o_ref.dtype)


def _swa_dec(l, sink, sw, ck, cv, o):
    return pl.pallas_call(
        functools.partial(_swa_dec_kernel, l),
        grid=(DEC_BATCH, DEC_TILES_PER_BATCH),
        in_specs=[pl.BlockSpec(memory_space=pltpu.SMEM),
                  pl.BlockSpec((TM, SWA_OUT), lambda b, j: (_dec_row(b, j), 0)),
                  pl.BlockSpec((DEC_SEQ, LANES), lambda b, j: (N_CTX // DEC_SEQ + b, SWA_OUT // LANES)),
                  pl.BlockSpec((DEC_SEQ, LANES), lambda b, j: (N_CTX // DEC_SEQ + b, SWA_OUT // LANES + 1)),
                  pl.BlockSpec((None, None, PAST_LEN, LANES), lambda b, j: (b, l, 0, 0)),
                  pl.BlockSpec((None, None, PAST_LEN, LANES), lambda b, j: (b, l, 0, 0)),
                  pl.BlockSpec(memory_space=pl.ANY)],
        out_specs=pl.BlockSpec((TM, SWA_OUT), lambda b, j: (_dec_row(b, j), 1)),
        out_shape=jax.ShapeDtypeStruct((N_TOK, D_MODEL), BF16),
        input_output_aliases={6: 0},
        compiler_params=_cparams(("arbitrary", "arbitrary")),
        name="swa_dec",
    )(sink, sw, sw, sw, ck, cv, o)


def _route(sc, sel):
    rows = [sel[e:e + 1, :] for e in range(N_EXPERTS)]

    def beats(a, ia, b, ib):
        return (a > b) | ((a == b) & (ia < ib)) if ia < ib else (a > b)

    in_top = []
    gscore = []
    for g in range(N_EXPERT_GROUPS):
        mem = list(range(g * EXPERTS_PER_GROUP, (g + 1) * EXPERTS_PER_GROUP))
        acc = None
        for e in mem:
            rank = sum(beats(rows[o], o, rows[e], e).astype(jnp.int32) for o in mem if o != e)
            top = rank < 2
            in_top.append(top)
            term = jnp.where(top, rows[e], 0.0)
            acc = term if acc is None else acc + term
        gscore.append(acc)
    gates, chosen = [], []
    for g in range(N_EXPERT_GROUPS):
        lost = sum(beats(gscore[o], o, gscore[g], g).astype(jnp.int32) for o in range(N_EXPERT_GROUPS) if o != g)
        best = lost == 0
        for e in range(g * EXPERTS_PER_GROUP, (g + 1) * EXPERTS_PER_GROUP):
            pick = best & in_top[e]
            chosen.append(pick.astype(F32))
            gates.append(jnp.where(pick, sc[e:e + 1, :], 0.0))
    gate = jnp.concatenate(gates, axis=0)
    return gate / gate.sum(axis=0, keepdims=True), jnp.concatenate(chosen, axis=0)


def _post_kernel(x_ref, o_ref, mod_ref, wout_ref, g_ref, wr_ref, br_ref,
                 y_ref, h2_ref, rt_ref, rtm_ref, cnt_ref):
    attn = jnp.dot(o_ref[...].astype(BF16), wout_ref[...], preferred_element_type=F32)
    y = x_ref[...] + mod_ref[2:3, :] * attn
    y_ref[...] = y
    h2 = _rms(y, g_ref[...]) * (1.0 + mod_ref[4:5, :]) + mod_ref[3:4, :]
    h_hi = h2.astype(BF16)
    h_lo = (h2 - h_hi.astype(F32)).astype(BF16)
    w = wr_ref[...]
    w_hi = w.astype(BF16)
    w_lo = (w - w_hi.astype(F32)).astype(BF16)
    logits = (jnp.dot(h_hi, w_hi, preferred_element_type=F32) + jnp.dot(h_lo, w_hi, preferred_element_type=F32)
              + jnp.dot(h_hi, w_lo, preferred_element_type=F32))
    logits = logits.T[:N_EXPERTS, :]
    sc = 1.0 / (1.0 + jnp.exp(-logits))
    gate, chosen = _route(sc, sc + br_ref[...])
    h2_ref[...] = h_hi
    rt = jnp.concatenate([gate, chosen], axis=0)
    rt_ref[...] = rt
    rtm_ref[...] = jnp.concatenate([rt, jnp.zeros((LANES - 2 * N_EXPERTS, TM), F32)], axis=0).T
    cnt_ref[...] = jnp.broadcast_to(jnp.sum(chosen, axis=1, keepdims=True), (N_EXPERTS, LANES))


def _post(l, x, o, mods, w_out_bf, g_ffn, w_router_p, b_router):
    tile = lambda w: pl.BlockSpec((TM, w), lambda i: (i, 0))
    return pl.pallas_call(
        _post_kernel,
        grid=(N_TILES,),
        in_specs=[tile(D_MODEL), tile(D_MODEL),
                  pl.BlockSpec((None, None, N_MOD, D_MODEL), lambda i: (l, _cond_row(i), 0, 0)),
                  pl.BlockSpec((None, D_MODEL, D_MODEL), lambda i: (l, 0, 0)),
                  pl.BlockSpec((None, 1, D_MODEL), lambda i: (l, 0, 0)),
                  pl.BlockSpec((D_MODEL, LANES), lambda i: (0, 0)),
                  pl.BlockSpec((N_EXPERTS, 1), lambda i: (0, 0))],
        out_specs=[tile(D_MODEL), tile(D_MODEL), pl.BlockSpec((2 * N_EXPERTS, TM), lambda i: (0, i)),
                   tile(LANES), pl.BlockSpec((None, N_EXPERTS, LANES), lambda i: (i, 0, 0))],
        out_shape=[jax.ShapeDtypeStruct((N_TOK, D_MODEL), F32), jax.ShapeDtypeStruct((N_TOK, D_MODEL), BF16),
                   jax.ShapeDtypeStruct((2 * N_EXPERTS, N_TOK), F32),
                   jax.ShapeDtypeStruct((N_TOK, LANES), F32),
                   jax.ShapeDtypeStruct((N_TILES, N_EXPERTS, LANES), F32)],
        compiler_params=_cparams(("arbitrary",)),
        name="post_attn",
    )(x, o, mods, w_out_bf, g_ffn.reshape(DEPTH, 1, D_MODEL), w_router_p, b_router.reshape(N_EXPERTS, 1))


def _shr(x, bits):
    return lax.shift_right_logical(x, jnp.int32(bits))


TR_BITS = TR.bit_length() - 1
ALIGN_BITS = RUN_ALIGN.bit_length() - 1
WIN_BITS = WIN.bit_length() - 1


def _plan_rows(cnt_ref, off_ref, npass_ref, seg_ref):
    def per_expert(e, row0):
        def per_tile(bb, r):
            off_ref[bb * N_EXPERTS + e] = r
            return r + (_shr(cnt_ref[bb * N_EXPERTS + e] + (RUN_ALIGN - 1), ALIGN_BITS) << ALIGN_BITS)

        rows_end = lax.fori_loop(0, N_TILES, per_tile, row0)
        n = _shr(rows_end - row0 + (TR - 1), TR_BITS)
        seg_ref[e] = _shr(row0, TR_BITS)
        seg_ref[N_EXPERTS + e] = n
        seg_ref[2 * N_EXPERTS + e] = rows_end
        seg_ref[3 * N_EXPERTS + e] = row0 + (n << TR_BITS)
        return row0 + (n << TR_BITS)

    end_row = lax.fori_loop(0, N_EXPERTS, per_expert, jnp.int32(0))
    seg_ref[4 * N_EXPERTS] = _shr(end_row, TR_BITS)

    def longest(bb, carry):
        m = lax.fori_loop(0, N_EXPERTS, lambda e, m: jnp.maximum(m, cnt_ref[bb * N_EXPERTS + e]), jnp.int32(0))
        npass_ref[bb] = _shr(m + (WIN - 1), WIN_BITS)
        return carry

    lax.fori_loop(0, N_TILES, longest, 0)


def _dispatch_kernel(cnt_ref, h_ref, rt_ref, et_ref, xs_hbm, off_ref, npass_ref, seg_ref, zbuf, zeros, sem):
    b = pl.program_id(0)
    slot = b % 2

    @pl.when(b == 0)
    def _():
        _plan_rows(cnt_ref, off_ref, npass_ref, seg_ref)

    key = jnp.dot(et_ref[...], _run_keys(rt_ref[...], 1).astype(BF16), preferred_element_type=F32)
    slot_j = lax.broadcasted_iota(jnp.int32, (N_EXPERTS * WIN, TM), 0) % WIN

    def run_copies(bb, sl, p, act):
        def per_expert(e, carry):
            left = cnt_ref[bb * N_EXPERTS + e] - p * WIN
            row = off_ref[bb * N_EXPERTS + e] + p * WIN
            whole = row + WIN <= seg_ref[2 * N_EXPERTS + e]

            @pl.when(whole & (left > 0))
            def _():
                src = pl.multiple_of(e * WIN, WIN)
                act(pltpu.make_async_copy(zbuf.at[sl, pl.ds(src, WIN)],
                                          xs_hbm.at[pl.ds(pl.multiple_of(row, RUN_ALIGN), WIN)], sem))

            for k in range(WIN // RUN_ALIGN):
                @pl.when(jnp.logical_not(whole) & (k * RUN_ALIGN < left))
                def _():
                    src = pl.multiple_of(e * WIN + k * RUN_ALIGN, RUN_ALIGN)
                    dst = pl.multiple_of(row + k * RUN_ALIGN, RUN_ALIGN)
                    act(pltpu.make_async_copy(zbuf.at[sl, pl.ds(src, RUN_ALIGN)],
                                              xs_hbm.at[pl.ds(dst, RUN_ALIGN)], sem))
            return carry

        lax.fori_loop(0, N_EXPERTS, per_expert, 0)

    def fill_and_send(p):
        pick = jnp.where(key == (slot_j + p * WIN).astype(F32), 1.0, 0.0).astype(BF16)
        zbuf[slot] = jnp.dot(pick, h_ref[...], preferred_element_type=F32).astype(BF16)
        run_copies(b, slot, p, lambda cp: cp.start(priority=1))

    @pl.when(b > 0)
    def _():
        run_copies(b - 1, 1 - slot, npass_ref[jnp.maximum(b - 1, 0)] - 1, lambda cp: cp.wait())

    fill_and_send(0)

    def more(p, carry):
        run_copies(b, slot, p - 1, lambda cp: cp.wait())
        fill_and_send(p)
        return carry

    lax.fori_loop(1, npass_ref[b], more, 0)

    @pl.when(b == N_TILES - 1)
    def _():
        run_copies(b, slot, npass_ref[b] - 1, lambda cp: cp.wait())
        zeros[...] = jnp.zeros_like(zeros)

        def fill(act):
            def per_expert(e, carry):
                for k in range(TR // RUN_ALIGN):
                    row = pl.multiple_of(seg_ref[2 * N_EXPERTS + e] + k * RUN_ALIGN, RUN_ALIGN)

                    @pl.when(row < seg_ref[3 * N_EXPERTS + e])
                    def _():
                        act(pltpu.make_async_copy(zeros.at[pl.ds(0, RUN_ALIGN)], xs_hbm.at[pl.ds(row, RUN_ALIGN)], sem))
                return carry

            lax.fori_loop(0, N_EXPERTS, per_expert, 0)

            def per_tile(t, carry):
                act(pltpu.make_async_copy(zeros, xs_hbm.at[pl.ds(pl.multiple_of(t * TR, TR), TR)], sem))
                return carry

            lax.fori_loop(seg_ref[4 * N_EXPERTS], NT, per_tile, 0)

        fill(lambda cp: cp.start())
        fill(lambda cp: cp.wait())


def _dispatch(cnt, h2, rt, et):
    smem = pl.BlockSpec(memory_space=pltpu.SMEM)
    return pl.pallas_call(
        _dispatch_kernel,
        grid_spec=pltpu.PrefetchScalarGridSpec(
            num_scalar_prefetch=1,
            grid=(N_TILES,),
            in_specs=[pl.BlockSpec((TM, D_MODEL), lambda i, *_: (i, 0)),
                      pl.BlockSpec((2 * N_EXPERTS, TM), lambda i, *_: (0, i)),
                      pl.BlockSpec((N_EXPERTS * WIN, 2 * N_EXPERTS), lambda i, *_: (0, 0))],
            out_specs=[pl.BlockSpec(memory_space=pl.ANY), smem, smem, smem],
            scratch_shapes=[pltpu.VMEM((2, N_EXPERTS * WIN, D_MODEL), BF16), pltpu.VMEM((TR, D_MODEL), BF16),
                            pltpu.SemaphoreType.DMA(())]),
        out_shape=[jax.ShapeDtypeStruct((NT * TR, D_MODEL), BF16), jax.ShapeDtypeStruct((N_RUNS,), jnp.int32),
                   jax.ShapeDtypeStruct((N_TILES,), jnp.int32),
                   jax.ShapeDtypeStruct((4 * N_EXPERTS + 1,), jnp.int32)],
        compiler_params=_cparams(("arbitrary",)),
        name="dispatch",
    )(cnt, h2, rt, et)


CAST_ROWS = 128


def _cast_rows(src_ref, dst_ref, n):
    def body(c, carry):
        rows = pl.ds(pl.multiple_of(c * CAST_ROWS, CAST_ROWS), CAST_ROWS)
        dst_ref[rows, :] = src_ref[rows, :].astype(BF16)
        return carry

    lax.fori_loop(0, n, body, 0)


def _experts_kernel(seg_ref, xs_hbm, wg_ref, wu_ref, wd_ref, ys_hbm, wgb, wub, wdb, xbuf, ybuf, semx, semy):
    e = pl.program_id(0)
    t0 = seg_ref[e]
    n = seg_ref[N_EXPERTS + e]
    _cast_rows(wg_ref, wgb, D_MODEL // CAST_ROWS)
    _cast_rows(wu_ref, wub, D_MODEL // CAST_ROWS)
    _cast_rows(wd_ref, wdb, D_EXPERT // CAST_ROWS)

    def rows(k):
        return pl.ds(pl.multiple_of((t0 + k) * TR, TR), TR)

    def fetch(k, s):
        return pltpu.make_async_copy(xs_hbm.at[rows(k)], xbuf.at[s], semx.at[s])

    def put(k, s):
        return pltpu.make_async_copy(ybuf.at[s], ys_hbm.at[rows(k)], semy.at[s])

    @pl.when(n > 0)
    def _():
        fetch(0, 0).start(priority=1)

    def tile(k, carry):
        s = k % 2

        @pl.when(k + 1 < n)
        def _():
            fetch(k + 1, 1 - s).start(priority=1)

        fetch(k, s).wait()

        @pl.when(k >= 2)
        def _():
            put(k - 2, s).wait()

        x = xbuf[s]
        hg = jnp.dot(x, wgb[...], preferred_element_type=F32)
        hu = jnp.dot(x, wub[...], preferred_element_type=F32)
        a = hg * (1.0 / (1.0 + jnp.exp(-hg))) * hu
        ybuf[s] = jnp.dot(a.astype(BF16), wdb[...], preferred_element_type=F32).astype(BF16)
        put(k, s).start(priority=1)
        return carry

    lax.fori_loop(0, n, tile, 0)

    @pl.when(n >= 2)
    def _():
        put(n - 2, n % 2).wait()

    @pl.when(n >= 1)
    def _():
        put(n - 1, (n - 1) % 2).wait()

    @pl.when(e == N_EXPERTS - 1)
    def _():
        ybuf[0] = jnp.zeros((TR, D_MODEL), BF16)

        def fill(act):
            def per_tile(t, carry):
                act(pltpu.make_async_copy(ybuf.at[0], ys_hbm.at[pl.ds(pl.multiple_of(t * TR, TR), TR)], semy.at[0]))
                return carry

            lax.fori_loop(seg_ref[4 * N_EXPERTS], NT, per_tile, 0)

        fill(lambda cp: cp.start())
        fill(lambda cp: cp.wait())


def _experts(l, seg, xs, w_gate, w_up, w_down):
    wspec = lambda a, b: pl.BlockSpec((None, None, a, b), lambda e, seg: (l, e, 0, 0))
    tile_buf = pltpu.VMEM((2, TR, D_MODEL), BF16)
    return pl.pallas_call(
        _experts_kernel,
        grid_spec=pltpu.PrefetchScalarGridSpec(
            num_scalar_prefetch=1,
            grid=(N_EXPERTS,),
            in_specs=[pl.BlockSpec(memory_space=pl.ANY),
                      wspec(D_MODEL, D_EXPERT), wspec(D_MODEL, D_EXPERT), wspec(D_EXPERT, D_MODEL)],
            out_specs=pl.BlockSpec(memory_space=pl.ANY),
            scratch_shapes=[pltpu.VMEM((D_MODEL, D_EXPERT), BF16), pltpu.VMEM((D_MODEL, D_EXPERT), BF16),
                            pltpu.VMEM((D_EXPERT, D_MODEL), BF16), tile_buf, tile_buf,
                            pltpu.SemaphoreType.DMA((2,)), pltpu.SemaphoreType.DMA((2,))]),
        out_shape=jax.ShapeDtypeStruct((NT * TR, D_MODEL), BF16),
        compiler_params=_cparams(("arbitrary",)),
        name="experts",
    )(seg, xs, w_gate, w_up, w_down)


def _final_kernel(off_ref, npass_ref, y_ref, rtm_ref, ek_ref, eg_ref, ys_hbm, mod_ref, g_ref, op_ref, os_ref,
                  wbuf, sem):
    i = pl.program_id(0)
    moe = _combined_moe(off_ref, npass_ref, rtm_ref, ek_ref, eg_ref, ys_hbm, wbuf, sem)
    out = _rms(y_ref[...] + mod_ref[5:6, :] * moe, g_ref[...])

    @pl.when(i < CTX_TILES)
    def _():
        op_ref[...] = out

    @pl.when(i >= CTX_TILES)
    def _():
        os_ref[...] = out


def _final(off, npass, y, rtm, ek, eg, ys, mods, g_final):
    return pl.pallas_call(
        _final_kernel,
        grid_spec=pltpu.PrefetchScalarGridSpec(
            num_scalar_prefetch=2,
            grid=(N_TILES,),
            in_specs=_combine_specs() + [
                pl.BlockSpec((None, None, N_MOD, D_MODEL), lambda i, *_: (DEPTH - 1, _cond_row(i), 0, 0)),
                pl.BlockSpec((1, D_MODEL), lambda i, *_: (0, 0))],
            out_specs=[pl.BlockSpec((TM, D_MODEL), lambda i, *_: (jnp.minimum(i, CTX_TILES - 1), 0)),
                       pl.BlockSpec((TM, D_MODEL), lambda i, *_: (jnp.maximum(i - CTX_TILES, 0), 0))],
            scratch_shapes=_combine_scratch()),
        out_shape=[jax.ShapeDtypeStruct((N_CTX, D_MODEL), F32), jax.ShapeDtypeStruct((N_DEC, D_MODEL), F32)],
        compiler_params=_cparams(("arbitrary",)),
        name="final_norm",
    )(off, npass, y, rtm, ek, eg, ys, mods, g_final.reshape(1, D_MODEL))


def _prep_kernel(win_ref, wout_ref, wi_ref, wo_ref):
    w = win_ref[...]
    split = KPE_OFF + MLA_ROPE_DIM
    wi_ref[...] = jnp.concatenate([w[:, :split], jnp.zeros((CAST_ROWS, SW_OFF - split), F32), w[:, split:]],
                                  axis=1).astype(BF16)
    wo_ref[...] = wout_ref[...].astype(BF16)


def _prep_weights(w_in, w_out):
    n = D_MODEL // CAST_ROWS
    shift = NA_OUT // CAST_ROWS
    return pl.pallas_call(
        _prep_kernel,
        grid=(DEPTH, n),
        in_specs=[pl.BlockSpec((None, CAST_ROWS, IN_WIDTH), lambda l, j: (l, j, 0)),
                  pl.BlockSpec((None, CAST_ROWS, D_MODEL), lambda l, j: (l, (j + shift) % n, 0))],
        out_specs=[pl.BlockSpec((None, CAST_ROWS, IN_PAD), lambda l, j: (l, j, 0)),
                   pl.BlockSpec((None, CAST_ROWS, D_MODEL), lambda l, j: (l, j, 0))],
        out_shape=[jax.ShapeDtypeStruct((DEPTH, D_MODEL, IN_PAD), BF16),
                   jax.ShapeDtypeStruct((DEPTH, D_MODEL, D_MODEL), BF16)],
        compiler_params=_cparams(("arbitrary", "arbitrary")),
        name="prep_weights",
    )(w_in, w_out)


def _rope_tables(rot_dim):
    t = jnp.arange(DEC_SEQ, dtype=jnp.int32)
    row = (t // GRID_W).astype(F32)
    col = (t % GRID_W).astype(F32)
    per_axis = rot_dim // 2
    inv = ROPE_BASE ** (-jnp.arange(0, per_axis, 2, dtype=F32) / per_axis)
    ang = jnp.concatenate([row[:, None] * inv, col[:, None] * inv], axis=-1)
    cos, sin = jnp.cos(ang), jnp.sin(ang)
    zero = jnp.zeros_like(sin)
    rep = LANES // rot_dim
    tabs = [jnp.concatenate([cos, cos], -1), jnp.concatenate([zero, sin], -1), jnp.concatenate([-sin, zero], -1)]
    return jnp.stack([jnp.tile(a, (1, rep)) for a in tabs])


def kernel(x_prompt, x_sample, c, cache_na_k, cache_na_v, cache_mla_ckv, cache_mla_kpe, cache_swa_k, cache_swa_v,
           c_ctx, w_ada, b_ada, g_attn, w_in, g_mla_q, w_mla_qb, g_mla_kv, w_mla_kvb, na_rpb, swa_sink, w_out,
           g_ffn, w_router, b_router, w_gate, w_up, w_down, g_final):
    cond = jnp.concatenate([c_ctx[None], c, jnp.zeros((COND_ROWS - 1 - DEC_BATCH, D_MODEL), F32)], axis=0)
    mods = _ada(cond, w_ada, b_ada).reshape(DEPTH, COND_ROWS, N_MOD, D_MODEL)
    bias = _na_bias(na_rpb)
    t_mla = _rope_tables(MLA_ROPE_DIM)
    t_swa = _rope_tables(HEAD_DIM)

    w_in_p, w_out_p = _prep_weights(w_in, w_out)
    wq = w_mla_qb.reshape(DEPTH, MLA_Q_RANK, MLA_HEADS, MLA_QK_DIM)
    w_rope = jnp.pad(wq[..., MLA_NOPE_DIM:], ((0, 0), (0, 0), (0, 0), (0, LANES - MLA_ROPE_DIM)))
    w_qb_p = jnp.concatenate([wq[..., :MLA_NOPE_DIM].reshape(DEPTH, MLA_Q_RANK, MQ_NOPE),
                              w_rope.reshape(DEPTH, MLA_Q_RANK, MLA_HEADS * LANES)], axis=-1).astype(BF16)
    w_router_p = jnp.pad(w_router, ((0, 0), (0, LANES - N_EXPERTS)))
    et, ek, eg = _spread_consts()

    c_na_k = cache_na_k.reshape(DEC_BATCH, DEPTH, PAST_LEN, NA_OUT)
    c_na_v = cache_na_v.reshape(DEC_BATCH, DEPTH, PAST_LEN, NA_OUT)
    c_sw_k = cache_swa_k.reshape(DEC_BATCH, DEPTH, PAST_LEN, LANES)
    c_sw_v = cache_swa_v.reshape(DEC_BATCH, DEPTH, PAST_LEN, LANES)

    y = ys = rtm = off = npass = None
    caches = [[] for _ in range(6)]
    for l in range(DEPTH):
        if l == 0:
            srcs = [x_prompt.reshape(N_CTX, D_MODEL), x_sample.reshape(N_DEC, D_MODEL)]
        else:
            srcs = [off, npass, y, rtm, ek, eg, ys]
        x, na, mq, ckv, kpe, sw, *slabs = _pre(l == 0, l, srcs, mods, g_attn, w_in_p, g_mla_q, w_qb_p, g_mla_kv,
                                               t_mla, t_swa)
        for dst, slab in zip(caches, slabs):
            dst.append(slab)
        o = _ctx_attn(l, swa_sink, na, mq, ckv, kpe, sw, w_mla_kvb)
        o = _mla_dec(l, mq, ckv, kpe, cache_mla_ckv, cache_mla_kpe, w_mla_kvb, o)
        o = _swa_dec(l, swa_sink, sw, c_sw_k, c_sw_v, o)
        o = _na_dec(l, na, c_na_k, c_na_v, bias, o)
        y, h2, rt, rtm, cnt = _post(l, x, o, mods, w_out_p, g_ffn, w_router_p, b_router)
        xs, off, npass, seg = _dispatch(cnt[:, :, 0].astype(jnp.int32).reshape(-1), h2, rt, et)
        ys = _experts(l, seg, xs, w_gate, w_up, w_down)
    y_prompt, y_sample = _final(off, npass, y, rtm, ek, eg, ys, mods, g_final)

    def stack(parts, tail):
        return jnp.stack([p.reshape((BATCH, SEQ) + tail) for p in parts], axis=1)

    return (y_prompt.reshape(BATCH, SEQ, D_MODEL), y_sample.reshape(DEC_BATCH, DEC_SEQ, D_MODEL),
            stack(caches[0], (NA_HEADS, HEAD_DIM)), stack(caches[1], (NA_HEADS, HEAD_DIM)),
            stack(caches[2], (MLA_KV_RANK,)), stack(caches[3], (MLA_ROPE_DIM,)),
            stack(caches[4], (SWA_KV_HEADS, HEAD_DIM)), stack(caches[5], (SWA_KV_HEADS, HEAD_DIM)))
```

```python
import functools

import jax
import jax.numpy as jnp
from jax import lax
from jax.experimental import pallas as pl
from jax.experimental.pallas import tpu as pltpu

D_MODEL = 1024
BATCH = 16
SEQ = 256
DEPTH = 4
DEC_BATCH = 2
DEC_SEQ = 2048
PAST_LEN = 256
GRID_W = 64
HEAD_DIM = 64
NA_HEADS = 4
NA_WIN_H = 8
NA_WIN_W = 16
MLA_HEADS = 6
MLA_Q_RANK = 256
MLA_KV_RANK = 128
MLA_NOPE_DIM = 64
MLA_ROPE_DIM = 32
MLA_V_DIM = 64
MLA_QK_DIM = MLA_NOPE_DIM + MLA_ROPE_DIM
SWA_HEADS = 6
SWA_KV_HEADS = 2
SWA_WINDOW = 128
ROPE_BASE = 10000.0
N_EXPERTS = 16
N_EXPERT_GROUPS = 4
EXPERTS_PER_GROUP = 4
D_EXPERT = 512
RMS_EPS = 1e-6
N_MOD = 6

NA_IN = 3 * NA_HEADS * HEAD_DIM
MLA_IN = MLA_Q_RANK + MLA_KV_RANK + MLA_ROPE_DIM
SWA_IN = (SWA_HEADS + 2 * SWA_KV_HEADS) * HEAD_DIM
IN_WIDTH = NA_IN + MLA_IN + SWA_IN
NA_OUT = NA_HEADS * HEAD_DIM
MLA_OUT = MLA_HEADS * MLA_V_DIM
SWA_OUT = SWA_HEADS * HEAD_DIM

LANES = 128
N_CTX = BATCH * SEQ
N_DEC = DEC_BATCH * DEC_SEQ
N_TOK = N_CTX + N_DEC
TM = 256
N_TILES = N_TOK // TM
CTX_TILES = N_CTX // TM
DEC_TILES_PER_BATCH = DEC_SEQ // TM
COND_ROWS = 8
KPE_OFF = NA_IN + MLA_Q_RANK + MLA_KV_RANK
SW_OFF = KPE_OFF + LANES
IN_PAD = SW_OFF + SWA_IN
MQ_NOPE = MLA_HEADS * MLA_NOPE_DIM
MQ_W = MQ_NOPE + MLA_HEADS * LANES
MLA_SCALE = MLA_QK_DIM ** -0.5
MLA_GROUP = 3
NA_ROWS_PER_STEP = TM // GRID_W
NA_KEY_ROWS = 12
NA_KEYS = NA_KEY_ROWS * GRID_W
SWA_KEYS = 512
NEG = -1e30
TR = 512
N_ASSIGN = 2 * N_TOK
RUN_ALIGN = 16
WIN = 64
N_RUNS = N_TILES * N_EXPERTS
NT = -(-(N_ASSIGN + N_RUNS * (RUN_ALIGN - 1) + N_EXPERTS * (TR - 1) + WIN) // TR)

F32 = jnp.float32
BF16 = jnp.bfloat16
VMEM_LIMIT = 56 * 1024 * 1024


def _cparams(sem):
    return pltpu.CompilerParams(dimension_semantics=sem, vmem_limit_bytes=VMEM_LIMIT)


def _cond_row(i):
    return jnp.where(i < CTX_TILES, 0, 1 + (i - CTX_TILES) // DEC_TILES_PER_BATCH)


def _rope_blk(i):
    return jnp.where(i < CTX_TILES, 0, (i - CTX_TILES) % DEC_TILES_PER_BATCH)


def _rms(x, g):
    ms = jnp.mean(x * x, axis=-1, keepdims=True)
    return x * lax.rsqrt(ms + RMS_EPS) * g


def _dot(a, b):
    return jnp.dot(a.astype(BF16), b.astype(BF16), preferred_element_type=F32)


def _dot_nt(a, b):
    return lax.dot_general(a.astype(BF16), b.astype(BF16), (((1,), (1,)), ((), ())),
                           preferred_element_type=F32)


def _ada_kernel(c_ref, w_ref, b_ref, o_ref):
    c = c_ref[...]
    s = c * (1.0 / (1.0 + jnp.exp(-c)))
    o_ref[...] = _dot(s, w_ref[...]) + b_ref[...]


def _ada(cond, w_ada, b_ada):
    tn = 1536
    n = N_MOD * D_MODEL
    return pl.pallas_call(
        _ada_kernel,
        grid=(DEPTH, n // tn),
        in_specs=[pl.BlockSpec((COND_ROWS, D_MODEL), lambda l, j: (0, 0)),
                  pl.BlockSpec((None, D_MODEL, tn), lambda l, j: (l, 0, j)),
                  pl.BlockSpec((None, 1, tn), lambda l, j: (l, 0, j))],
        out_specs=pl.BlockSpec((None, COND_ROWS, tn), lambda l, j: (l, 0, j)),
        out_shape=jax.ShapeDtypeStruct((DEPTH, COND_ROWS, n), F32),
        compiler_params=_cparams(("arbitrary", "arbitrary")),
        name="ada_mod",
    )(cond, w_ada, b_ada.reshape(DEPTH, 1, n))


def _bias_kernel(rpb_ref, o_ref):
    g = pl.program_id(0)
    base = g * ((2 * NA_WIN_H - 1) * (2 * NA_WIN_W - 1))
    qc = lax.broadcasted_iota(jnp.int32, (GRID_W, GRID_W), 0)
    kc = lax.broadcasted_iota(jnp.int32, (GRID_W, GRID_W), 1)
    dc = jnp.clip(kc - qc + (NA_WIN_W - 1), 0, 2 * NA_WIN_W - 2)
    cs = jnp.clip(qc - NA_WIN_W // 2, 0, GRID_W - NA_WIN_W)
    col_ok = (kc >= cs) & (kc < cs + NA_WIN_W)
    neg = jnp.full((GRID_W, GRID_W), NEG, F32)
    tabs = []
    for a in range(2 * NA_WIN_H - 1):
        t = jnp.zeros((GRID_W, GRID_W), F32)
        for b in range(2 * NA_WIN_W - 1):
            t = jnp.where(dc == b, rpb_ref[base + a * (2 * NA_WIN_W - 1) + b], t)
        tabs.append(jnp.where(col_ok, t, NEG))
    for p in range(3):
        for qi in range(NA_ROWS_PER_STEP):
            for kj in range(NA_KEY_ROWS):
                if p == 0:
                    ok, dr = kj < NA_WIN_H, kj - qi + 7
                elif p == 1:
                    ok, dr = qi <= kj < qi + NA_WIN_H, kj - qi + 3
                else:
                    ok, dr = kj >= NA_KEY_ROWS - NA_WIN_H, kj - qi - 1
                blk = tabs[dr] if ok else neg
                o_ref[p, qi * GRID_W:(qi + 1) * GRID_W, kj * GRID_W:(kj + 1) * GRID_W] = blk


def _na_bias(na_rpb):
    return pl.pallas_call(
        _bias_kernel,
        grid=(DEPTH * NA_HEADS,),
        in_specs=[pl.BlockSpec(memory_space=pltpu.SMEM)],
        out_specs=pl.BlockSpec((None, 3, None, TM, NA_KEYS),
                               lambda g: (g // NA_HEADS, 0, g % NA_HEADS, 0, 0)),
        out_shape=jax.ShapeDtypeStruct((DEPTH, 3, NA_HEADS, TM, NA_KEYS), F32),
        compiler_params=_cparams(("arbitrary",)),
        name="na_bias",
    )(na_rpb.reshape(-1))


def _rope128(x, t_ref, half):
    return (x * t_ref[0] + pltpu.roll(x, half, 1) * t_ref[1]
            + pltpu.roll(x, LANES - half, 1) * t_ref[2])


def _spread_consts():
    slot_e = jnp.arange(N_EXPERTS * WIN, dtype=jnp.int32) // WIN
    idx32 = jnp.arange(2 * N_EXPERTS, dtype=jnp.int32)
    idx128 = jnp.arange(LANES, dtype=jnp.int32)
    et = (idx32[None, :] == slot_e[:, None] + N_EXPERTS).astype(BF16)
    ek = (idx128[:, None] == slot_e[None, :] + N_EXPERTS).astype(BF16)
    eg = (idx128[:, None] == slot_e[None, :]).astype(BF16)
    return et, ek, eg


def _run_keys(chosen, token_axis):
    a = lax.broadcasted_iota(jnp.int32, (TM, TM), 0)
    b = lax.broadcasted_iota(jnp.int32, (TM, TM), 1)
    if token_axis == 1:
        rank = jnp.dot(chosen.astype(BF16), (a < b).astype(BF16), preferred_element_type=F32)
    else:
        rank = jnp.dot((b < a).astype(BF16), chosen.astype(BF16), preferred_element_type=F32)
    return jnp.where(chosen > 0.5, rank, -1.0)


def _windows_start(i, slot, p, off_ref, ys_hbm, wbuf, sem):
    for e in range(N_EXPERTS):
        row = pl.multiple_of(off_ref[i * N_EXPERTS + e] + p * WIN, RUN_ALIGN)
        pltpu.make_async_copy(ys_hbm.at[pl.ds(row, WIN)], wbuf.at[slot, pl.ds(e * WIN, WIN)],
                              sem.at[slot]).start(priority=1)


def _windows_wait(slot, ys_hbm, wbuf, sem):
    pltpu.make_async_copy(ys_hbm.at[pl.ds(0, N_EXPERTS * WIN)], wbuf.at[slot], sem.at[slot]).wait()


def _combined_moe(off_ref, npass_ref, rtm_ref, ek_ref, eg_ref, ys_hbm, wbuf, sem):
    i = pl.program_id(0)
    slot = i % 2

    @pl.when(i == 0)
    def _():
        _windows_start(0, 0, 0, off_ref, ys_hbm, wbuf, sem)

    @pl.when(i + 1 < N_TILES)
    def _():
        _windows_start(i + 1, 1 - slot, 0, off_ref, ys_hbm, wbuf, sem)

    r = rtm_ref[...]
    key = jnp.dot(_run_keys(r, 0).astype(BF16), ek_ref[...], preferred_element_type=F32)
    gate = jnp.dot(r.astype(BF16), eg_ref[...], preferred_element_type=F32)
    slot_j = lax.broadcasted_iota(jnp.int32, (TM, N_EXPERTS * WIN), 1) % WIN

    def contrib(p):
        g = jnp.where(key == (slot_j + p * WIN).astype(F32), gate, 0.0).astype(BF16)
        return jnp.dot(g, wbuf[slot], preferred_element_type=F32)

    _windows_wait(slot, ys_hbm, wbuf, sem)
    acc = contrib(0)

    def extra(p, acc):
        _windows_start(i, slot, p, off_ref, ys_hbm, wbuf, sem)
        _windows_wait(slot, ys_hbm, wbuf, sem)
        return acc + contrib(p)

    return lax.fori_loop(1, npass_ref[i], extra, acc)


def _pre_kernel(first, *refs):
    i = pl.program_id(0)
    if first:
        (xp_ref, xs_ref, mod_ref, g_ref, win_ref, gq_ref, wqb_ref, gkv_ref, tm_ref, ts_ref, _, _, _, _, _, _,
         xo_ref, na_ref, mq_ref, ckv_ref, kpe_ref, sw_ref, *cache_refs) = refs
        x = jnp.where(i < CTX_TILES, xp_ref[...], xs_ref[...])
    else:
        (off_ref, npass_ref, y_ref, rtm_ref, ek_ref, eg_ref, ys_hbm, modp_ref, mod_ref, g_ref, win_ref, gq_ref,
         wqb_ref, gkv_ref, tm_ref, ts_ref, _, _, _, _, _, _,
         xo_ref, na_ref, mq_ref, ckv_ref, kpe_ref, sw_ref, *rest) = refs
        *cache_refs, wbuf, sem = rest
        moe = _combined_moe(off_ref, npass_ref, rtm_ref, ek_ref, eg_ref, ys_hbm, wbuf, sem)
        x = y_ref[...] + modp_ref[5:6, :] * moe
    xo_ref[...] = x
    cnk_ref, cnv_ref, cckv_ref, ckpe_ref, csk_ref, csv_ref = cache_refs

    h = _rms(x, g_ref[...]) * (1.0 + mod_ref[1:2, :]) + mod_ref[0:1, :]
    z = jnp.dot(h.astype(BF16), win_ref[...], preferred_element_type=F32)
    na_ref[...] = z[:, :NA_IN].astype(BF16)
    cq = _rms(z[:, NA_IN:NA_IN + MLA_Q_RANK], gq_ref[...])
    ckv = _rms(z[:, NA_IN + MLA_Q_RANK:KPE_OFF], gkv_ref[...])
    ckv_ref[...] = ckv.astype(BF16)
    mq = jnp.dot(cq.astype(BF16), wqb_ref[...], preferred_element_type=F32)
    kpe = z[:, KPE_OFF:SW_OFF]
    sw = z[:, SW_OFF:IN_PAD]
    mq_ref[:, :MQ_NOPE] = mq[:, :MQ_NOPE].astype(BF16)
    sw_ref[:, SWA_OUT + LANES:] = sw[:, SWA_OUT + LANES:].astype(BF16)

    @pl.when(i < CTX_TILES)
    def _():
        mq_ref[:, MQ_NOPE:] = (mq[:, MQ_NOPE:] * MLA_SCALE).astype(BF16)
        kpe_ref[...] = kpe.astype(BF16)
        sw_ref[:, :SWA_OUT + LANES] = sw[:, :SWA_OUT + LANES].astype(BF16)
        cnk_ref[...] = z[:, NA_OUT:2 * NA_OUT]
        cnv_ref[...] = z[:, 2 * NA_OUT:NA_IN]
        cckv_ref[...] = ckv
        ckpe_ref[...] = kpe[:, :MLA_ROPE_DIM]
        csk_ref[...] = sw[:, SWA_OUT:SWA_OUT + LANES]
        csv_ref[...] = sw[:, SWA_OUT + LANES:]

    @pl.when(i >= CTX_TILES)
    def _():
        for c in range(MQ_NOPE // LANES, MQ_W // LANES):
            roped = _rope128(mq[:, c * LANES:(c + 1) * LANES], tm_ref, MLA_ROPE_DIM // 2)
            mq_ref[:, c * LANES:(c + 1) * LANES] = (roped * MLA_SCALE).astype(BF16)
        kpe_ref[...] = _rope128(kpe, tm_ref, MLA_ROPE_DIM // 2).astype(BF16)
        for c in range((SWA_OUT + LANES) // LANES):
            sw_ref[:, c * LANES:(c + 1) * LANES] = _rope128(sw[:, c * LANES:(c + 1) * LANES], ts_ref,
                                                            HEAD_DIM // 2).astype(BF16)


def _combine_scratch():
    return [pltpu.VMEM((2, N_EXPERTS * WIN, D_MODEL), BF16), pltpu.SemaphoreType.DMA((2,))]


def _combine_specs():
    spread = pl.BlockSpec((LANES, N_EXPERTS * WIN), lambda i, *_: (0, 0))
    return [pl.BlockSpec((TM, D_MODEL), lambda i, *_: (i, 0)),
            pl.BlockSpec((TM, LANES), lambda i, *_: (i, 0)),
            spread, spread, pl.BlockSpec(memory_space=pl.ANY)]


CACHE_WIDTHS = (NA_OUT, NA_OUT, MLA_KV_RANK, MLA_ROPE_DIM, LANES, LANES)


def _pre(first, l, xs, caches, mods, g_attn, w_in_p, g_mla_q, w_qb_p, g_mla_kv, t_mla, t_swa):
    tile = lambda w: pl.BlockSpec((TM, w), lambda i, *_: (i, 0))
    ctx_tile = lambda w: pl.BlockSpec((TM, w), lambda i, *_: (jnp.minimum(i, CTX_TILES - 1), 0))
    slab = lambda w: pl.BlockSpec((None, None, SEQ, w), lambda i, *_: (jnp.minimum(i, BATCH - 1), l, 0, 0))
    mod_spec = lambda ll: pl.BlockSpec((None, None, N_MOD, D_MODEL), lambda i, *_: (ll, _cond_row(i), 0, 0))
    vec = lambda w: pl.BlockSpec((None, 1, w), lambda i, *_: (l, 0, 0))
    if first:
        in_specs = [ctx_tile(D_MODEL), pl.BlockSpec((TM, D_MODEL), lambda i: (jnp.maximum(i - CTX_TILES, 0), 0))]
    else:
        in_specs = _combine_specs() + [mod_spec(l - 1)]
    in_specs += [mod_spec(l), vec(D_MODEL),
                 pl.BlockSpec((None, D_MODEL, IN_PAD), lambda i, *_: (l, 0, 0)),
                 vec(MLA_Q_RANK),
                 pl.BlockSpec((None, MLA_Q_RANK, MQ_W), lambda i, *_: (l, 0, 0)),
                 vec(MLA_KV_RANK),
                 pl.BlockSpec((3, TM, LANES), lambda i, *_: (0, _rope_blk(i), 0)),
                 pl.BlockSpec((3, TM, LANES), lambda i, *_: (0, _rope_blk(i), 0))]
    in_specs += [pl.BlockSpec(memory_space=pl.ANY)] * len(CACHE_WIDTHS)
    widths = [D_MODEL, NA_IN, MQ_W, MLA_KV_RANK, LANES, SWA_IN]
    args = list(xs) + ([mods] if not first else []) + [
        mods, g_attn.reshape(DEPTH, 1, D_MODEL), w_in_p, g_mla_q.reshape(DEPTH, 1, MLA_Q_RANK), w_qb_p,
        g_mla_kv.reshape(DEPTH, 1, MLA_KV_RANK), t_mla, t_swa]
    aliases = {len(args) + n: len(widths) + n for n in range(len(CACHE_WIDTHS))}
    return pl.pallas_call(
        functools.partial(_pre_kernel, first),
        grid_spec=pltpu.PrefetchScalarGridSpec(
            num_scalar_prefetch=0 if first else 2,
            grid=(N_TILES,),
            in_specs=in_specs,
            out_specs=[tile(w) for w in widths] + [slab(w) for w in CACHE_WIDTHS],
            scratch_shapes=[] if first else _combine_scratch()),
        out_shape=([jax.ShapeDtypeStruct((N_TOK, w), BF16 if n else F32) for n, w in enumerate(widths)]
                   + [jax.ShapeDtypeStruct((BATCH, DEPTH, SEQ, w), F32) for w in CACHE_WIDTHS]),
        input_output_aliases=aliases,
        compiler_params=_cparams(("arbitrary",)),
        name="pre_attn",
    )(*args, *caches)


def _softmax_parts(parts, sink=None):
    m = parts[0].max(axis=-1, keepdims=True)
    for s in parts[1:]:
        m = jnp.maximum(m, s.max(axis=-1, keepdims=True))
    if sink is not None:
        m = jnp.maximum(m, sink)
    ps = [jnp.exp(s - m) for s in parts]
    den = ps[0].sum(axis=-1, keepdims=True)
    for p in ps[1:]:
        den = den + p.sum(axis=-1, keepdims=True)
    if sink is not None:
        den = den + jnp.exp(sink - m)
    return ps, 1.0 / den


def _mla_heads(mq_ref, wkvb_ref, kcat, o_ref):
    rows = mq_ref.shape[0]
    for g in range(MLA_HEADS // MLA_GROUP):
        heads = range(g * MLA_GROUP, (g + 1) * MLA_GROUP)
        qs = []
        for h in heads:
            wk = wkvb_ref[:, h * 2 * HEAD_DIM:h * 2 * HEAD_DIM + MLA_NOPE_DIM]
            qa = _dot_nt(mq_ref[:, h * MLA_NOPE_DIM:(h + 1) * MLA_NOPE_DIM], wk) * MLA_SCALE
            qr = mq_ref[:, MQ_NOPE + h * LANES:MQ_NOPE + (h + 1) * LANES]
            qs.append(jnp.concatenate([qa.astype(BF16), qr], axis=1))
        s = _dot_nt(jnp.concatenate(qs, axis=0), kcat[...])
        p = jnp.exp(s - s.max(axis=-1, keepdims=True))
        inv = 1.0 / p.sum(axis=-1, keepdims=True)
        lat = _dot(p, kcat[:, :LANES]) * inv
        for n, h in enumerate(heads):
            wv = wkvb_ref[:, h * 2 * HEAD_DIM + MLA_NOPE_DIM:(h + 1) * 2 * HEAD_DIM]
            o_ref[:, h * MLA_V_DIM:(h + 1) * MLA_V_DIM] = _dot(lat[n * rows:(n + 1) * rows], wv).astype(o_ref.dtype)


def _ctx_attn_kernel(l, sink_ref, na_ref, mq_ref, ckv_ref, kpe_ref, sw_ref, wkvb_ref, o_ref):
    @pl.when(pl.program_id(0) >= BATCH)
    def _():
        o_ref[...] = jnp.zeros_like(o_ref)

    @pl.when(pl.program_id(0) < BATCH)
    def _():
        _ctx_attn_body(l, sink_ref, na_ref, mq_ref, ckv_ref, kpe_ref, sw_ref, wkvb_ref, o_ref)


def _ctx_attn_body(l, sink_ref, na_ref, mq_ref, ckv_ref, kpe_ref, sw_ref, wkvb_ref, o_ref):
    scale = HEAD_DIM ** -0.5
    _mla_heads(mq_ref, wkvb_ref, jnp.concatenate([ckv_ref[...], kpe_ref[...]], axis=1), o_ref)
    for h in range(SWA_HEADS):
        kh = h // (SWA_HEADS // SWA_KV_HEADS)
        q = sw_ref[:, h * HEAD_DIM:(h + 1) * HEAD_DIM]
        k = sw_ref[:, SWA_OUT + kh * HEAD_DIM:SWA_OUT + (kh + 1) * HEAD_DIM]
        v = sw_ref[:, SWA_OUT + LANES + kh * HEAD_DIM:SWA_OUT + LANES + (kh + 1) * HEAD_DIM]
        (p,), inv = _softmax_parts([_dot_nt(q, k) * scale], sink_ref[l, h])
        o_ref[:, MLA_OUT + h * HEAD_DIM:MLA_OUT + (h + 1) * HEAD_DIM] = (_dot(p, v) * inv).astype(o_ref.dtype)
    for h in range(NA_HEADS):
        q = na_ref[:, h * HEAD_DIM:(h + 1) * HEAD_DIM]
        k = na_ref[:, NA_OUT + h * HEAD_DIM:NA_OUT + (h + 1) * HEAD_DIM]
        v = na_ref[:, 2 * NA_OUT + h * HEAD_DIM:2 * NA_OUT + (h + 1) * HEAD_DIM]
        (p,), inv = _softmax_parts([_dot_nt(q, k) * scale])
        o_ref[:, MLA_OUT + SWA_OUT + h * HEAD_DIM:MLA_OUT + SWA_OUT + (h + 1) * HEAD_DIM] = (
            _dot(p, v) * inv).astype(o_ref.dtype)


def _ctx_attn(l, sink, na, mq, ckv, kpe, sw, w_kvb):
    tile = lambda w: pl.BlockSpec((SEQ, w), lambda b: (jnp.minimum(b, BATCH - 1), 0))
    return pl.pallas_call(
        functools.partial(_ctx_attn_kernel, l),
        grid=(N_TOK // SEQ,),
        in_specs=[pl.BlockSpec(memory_space=pltpu.SMEM), tile(NA_IN), tile(MQ_W), tile(MLA_KV_RANK), tile(LANES),
                  tile(SWA_IN), pl.BlockSpec((None, MLA_KV_RANK, MLA_HEADS * 2 * HEAD_DIM), lambda b: (l, 0, 0))],
        out_specs=pl.BlockSpec((SEQ, D_MODEL), lambda b: (b, 0)),
        out_shape=jax.ShapeDtypeStruct((N_TOK, D_MODEL), BF16),
        compiler_params=_cparams(("arbitrary",)),
        name="ctx_attn",
    )(sink, na, mq, ckv, kpe, sw, w_kvb)


def _dec_row(b, j):
    return CTX_TILES + b * DEC_TILES_PER_BATCH + j


def _na_dec_kernel(q_ref, k_ref, v_ref, ck_ref, cv_ref, bias_ref, oin_ref, o_ref):
    del oin_ref
    scale = HEAD_DIM ** -0.5
    j = pl.program_id(1)
    w0 = jnp.clip(j * NA_ROWS_PER_STEP - NA_WIN_H // 2, 0, DEC_SEQ // GRID_W - NA_KEY_ROWS)
    start = pl.multiple_of(w0 * GRID_W, GRID_W)
    for h in range(NA_HEADS):
        sl = slice(h * HEAD_DIM, (h + 1) * HEAD_DIM)
        q = q_ref[:, sl]
        k = k_ref[pl.ds(start, NA_KEYS), sl]
        v = v_ref[pl.ds(start, NA_KEYS), sl]
        s_nb = _dot_nt(q, k) * scale + bias_ref[h]
        s_ctx = _dot_nt(q, ck_ref[:, sl]) * scale
        (p_nb, p_ctx), inv = _softmax_parts([s_nb, s_ctx])
        o_ref[:, sl] = ((_dot(p_nb, v) + _dot(p_ctx, cv_ref[:, sl])) * inv).astype(o_ref.dtype)


def _na_dec(l, na, ck, cv, bias, o):
    pat = lambda j: jnp.where(j == 0, 0, jnp.where(j == DEC_TILES_PER_BATCH - 1, 2, 1))
    return pl.pallas_call(
        _na_dec_kernel,
        grid=(DEC_BATCH, DEC_TILES_PER_BATCH),
        in_specs=[pl.BlockSpec((TM, NA_OUT), lambda b, j: (_dec_row(b, j), 0)),
                  pl.BlockSpec((DEC_SEQ, NA_OUT), lambda b, j: (N_CTX // DEC_SEQ + b, 1)),
                  pl.BlockSpec((DEC_SEQ, NA_OUT), lambda b, j: (N_CTX // DEC_SEQ + b, 2)),
                  pl.BlockSpec((None, None, PAST_LEN, NA_OUT), lambda b, j: (b, l, 0, 0)),
                  pl.BlockSpec((None, None, PAST_LEN, NA_OUT), lambda b, j: (b, l, 0, 0)),
                  pl.BlockSpec((None, None, NA_HEADS, TM, NA_KEYS), lambda b, j: (l, pat(j), 0, 0, 0)),
                  pl.BlockSpec(memory_space=pl.ANY)],
        out_specs=pl.BlockSpec((TM, NA_OUT), lambda b, j: (_dec_row(b, j), (MLA_OUT + SWA_OUT) // NA_OUT)),
        out_shape=jax.ShapeDtypeStruct((N_TOK, D_MODEL), BF16),
        input_output_aliases={6: 0},
        compiler_params=_cparams(("arbitrary", "arbitrary")),
        name="na_dec",
    )(na, na, na, ck, cv, bias, o)


KCAT_ROWS = 256


def _mla_dec_kernel(mq_ref, ckv_ref, kpe_ref, cckv_ref, ckpe_ref, wkvb_ref, oin_ref, o_ref, kcat):
    del oin_ref

    def assemble(c, carry):
        rows = pl.ds(pl.multiple_of(c * KCAT_ROWS, KCAT_ROWS), KCAT_ROWS)
        kcat[rows, :LANES] = ckv_ref[rows, :]
        kcat[rows, LANES:] = kpe_ref[rows, :]
        return carry

    first = pl.program_id(1) == 0
    lax.fori_loop(0, jnp.where(first, DEC_SEQ // KCAT_ROWS, 0), assemble, 0)

    def context(c, carry):
        kcat[DEC_SEQ:, :LANES] = cckv_ref[...].astype(BF16)
        kcat[DEC_SEQ:, LANES:] = jnp.concatenate(
            [ckpe_ref[...], jnp.zeros((PAST_LEN, LANES - MLA_ROPE_DIM), F32)], axis=1).astype(BF16)
        return carry

    lax.fori_loop(0, jnp.where(first, 1, 0), context, 0)
    _mla_heads(mq_ref, wkvb_ref, kcat, o_ref)


def _mla_dec(l, mq, ckv, kpe, cckv, ckpe, w_kvb, o):
    return pl.pallas_call(
        _mla_dec_kernel,
        grid=(DEC_BATCH, DEC_TILES_PER_BATCH),
        in_specs=[pl.BlockSpec((TM, MQ_W), lambda b, j: (_dec_row(b, j), 0)),
                  pl.BlockSpec((DEC_SEQ, MLA_KV_RANK), lambda b, j: (N_CTX // DEC_SEQ + b, 0)),
                  pl.BlockSpec((DEC_SEQ, LANES), lambda b, j: (N_CTX // DEC_SEQ + b, 0)),
                  pl.BlockSpec((None, None, PAST_LEN, MLA_KV_RANK), lambda b, j: (b, l, 0, 0)),
                  pl.BlockSpec((None, None, PAST_LEN, MLA_ROPE_DIM), lambda b, j: (b, l, 0, 0)),
                  pl.BlockSpec((None, MLA_KV_RANK, MLA_HEADS * 2 * HEAD_DIM), lambda b, j: (l, 0, 0)),
                  pl.BlockSpec(memory_space=pl.ANY)],
        out_specs=pl.BlockSpec((TM, MLA_OUT), lambda b, j: (_dec_row(b, j), 0)),
        out_shape=jax.ShapeDtypeStruct((N_TOK, D_MODEL), BF16),
        scratch_shapes=[pltpu.VMEM((DEC_SEQ + PAST_LEN, 2 * LANES), BF16)],
        input_output_aliases={6: 0},
        compiler_params=_cparams(("arbitrary", "arbitrary")),
        name="mla_dec",
    )(mq, ckv, kpe, cckv, ckpe, w_kvb, o)


def _swa_dec_kernel(l, sink_ref, q_ref, k_ref, v_ref, ck_ref, cv_ref, oin_ref, o_ref):
    del oin_ref
    scale = HEAD_DIM ** -0.5
    j = pl.program_id(1)
    start = pl.multiple_of(jnp.clip(j * TM - SWA_WINDOW, 0, DEC_SEQ - SWA_KEYS), SWA_WINDOW)
    qpos = j * TM + lax.broadcasted_iota(jnp.int32, (TM, SWA_KEYS), 0)
    kpos = start + lax.broadcasted_iota(jnp.int32, (TM, SWA_KEYS), 1)
    band = jnp.abs(qpos - kpos) <= SWA_WINDOW
    for h in range(SWA_HEADS):
        kh = h // (SWA_HEADS // SWA_KV_HEADS)
        sl = slice(kh * HEAD_DIM, (kh + 1) * HEAD_DIM)
        q = q_ref[:, h * HEAD_DIM:(h + 1) * HEAD_DIM]
        s_loc = jnp.where(band, _dot_nt(q, k_ref[pl.ds(start, SWA_KEYS), sl]) * scale, NEG)
        s_ctx = _dot_nt(q, ck_ref[:, sl]) * scale
        (p_loc, p_ctx), inv = _softmax_parts([s_loc, s_ctx], sink_ref[l, h])
        o_ref[:, h * HEAD_DIM:(h + 1) * HEAD_DIM] = (
            (_dot(p_loc, v_ref[pl.ds(start, SWA_KEYS), sl]) + _dot(p_ctx, cv_ref[:, sl])) * inv).astype(o_ref.dtype)


def _swa_dec(l, sink, sw, ck, cv, o):
    return pl.pallas_call(
        functools.partial(_swa_dec_kernel, l),
        grid=(DEC_BATCH, DEC_TILES_PER_BATCH),
        in_specs=[pl.BlockSpec(memory_space=pltpu.SMEM),
                  pl.BlockSpec((TM, SWA_OUT), lambda b, j: (_dec_row(b, j), 0)),
                  pl.BlockSpec((DEC_SEQ, LANES), lambda b, j: (N_CTX // DEC_SEQ + b, SWA_OUT // LANES)),
                  pl.BlockSpec((DEC_SEQ, LANES), lambda b, j: (N_CTX // DEC_SEQ + b, SWA_OUT // LANES + 1)),
                  pl.BlockSpec((None, None, PAST_LEN, LANES), lambda b, j: (b, l, 0, 0)),
                  pl.BlockSpec((None, None, PAST_LEN, LANES), lambda b, j: (b, l, 0, 0)),
                  pl.BlockSpec(memory_space=pl.ANY)],
        out_specs=pl.BlockSpec((TM, SWA_OUT), lambda b, j: (_dec_row(b, j), 1)),
        out_shape=jax.ShapeDtypeStruct((N_TOK, D_MODEL), BF16),
        input_output_aliases={6: 0},
        compiler_params=_cparams(("arbitrary", "arbitrary")),
        name="swa_dec",
    )(sink, sw, sw, sw, ck, cv, o)


def _route(sc, sel):
    rows = [sel[e:e + 1, :] for e in range(N_EXPERTS)]

    def beats(a, ia, b, ib):
        return (a > b) | ((a == b) & (ia < ib)) if ia < ib else (a > b)

    in_top = []
    gscore = []
    for g in range(N_EXPERT_GROUPS):
        mem = list(range(g * EXPERTS_PER_GROUP, (g + 1) * EXPERTS_PER_GROUP))
        acc = None
        for e in mem:
            rank = sum(beats(rows[o], o, rows[e], e).astype(jnp.int32) for o in mem if o != e)
            top = rank < 2
            in_top.append(top)
            term = jnp.where(top, rows[e], 0.0)
            acc = term if acc is None else acc + term
        gscore.append(acc)
    gates, chosen = [], []
    for g in range(N_EXPERT_GROUPS):
        lost = sum(beats(gscore[o], o, gscore[g], g).astype(jnp.int32) for o in range(N_EXPERT_GROUPS) if o != g)
        best = lost == 0
        for e in range(g * EXPERTS_PER_GROUP, (g + 1) * EXPERTS_PER_GROUP):
            pick = best & in_top[e]
            chosen.append(pick.astype(F32))
            gates.append(jnp.where(pick, sc[e:e + 1, :], 0.0))
    gate = jnp.concatenate(gates, axis=0)
    return gate / gate.sum(axis=0, keepdims=True), jnp.concatenate(chosen, axis=0)


def _post_kernel(x_ref, o_ref, mod_ref, wout_ref, g_ref, wr_ref, br_ref,
                 y_ref, h2_ref, rt_ref, rtm_ref, cnt_ref):
    attn = jnp.dot(o_ref[...].astype(BF16), wout_ref[...], preferred_element_type=F32)
    y = x_ref[...] + mod_ref[2:3, :] * attn
    y_ref[...] = y
    h2 = _rms(y, g_ref[...]) * (1.0 + mod_ref[4:5, :]) + mod_ref[3:4, :]
    h_hi = h2.astype(BF16)
    h_lo = (h2 - h_hi.astype(F32)).astype(BF16)
    w = wr_ref[...]
    w_hi = w.astype(BF16)
    w_lo = (w - w_hi.astype(F32)).astype(BF16)
    logits = (jnp.dot(h_hi, w_hi, preferred_element_type=F32) + jnp.dot(h_lo, w_hi, preferred_element_type=F32)
              + jnp.dot(h_hi, w_lo, preferred_element_type=F32))
    logits = logits.T[:N_EXPERTS, :]
    sc = 1.0 / (1.0 + jnp.exp(-logits))
    gate, chosen = _route(sc, sc + br_ref[...])
    h2_ref[...] = h_hi
    rt = jnp.concatenate([gate, chosen], axis=0)
    rt_ref[...] = rt
    rtm_ref[...] = jnp.concatenate([rt, jnp.zeros((LANES - 2 * N_EXPERTS, TM), F32)], axis=0).T
    cnt_ref[...] = jnp.broadcast_to(jnp.sum(chosen, axis=1, keepdims=True), (N_EXPERTS, LANES))


def _post(l, x, o, mods, w_out_bf, g_ffn, w_router_p, b_router):
    tile = lambda w: pl.BlockSpec((TM, w), lambda i: (i, 0))
    return pl.pallas_call(
        _post_kernel,
        grid=(N_TILES,),
        in_specs=[tile(D_MODEL), tile(D_MODEL),
                  pl.BlockSpec((None, None, N_MOD, D_MODEL), lambda i: (l, _cond_row(i), 0, 0)),
                  pl.BlockSpec((None, D_MODEL, D_MODEL), lambda i: (l, 0, 0)),
                  pl.BlockSpec((None, 1, D_MODEL), lambda i: (l, 0, 0)),
                  pl.BlockSpec((D_MODEL, LANES), lambda i: (0, 0)),
                  pl.BlockSpec((N_EXPERTS, 1), lambda i: (0, 0))],
        out_specs=[tile(D_MODEL), tile(D_MODEL), pl.BlockSpec((2 * N_EXPERTS, TM), lambda i: (0, i)),
                   tile(LANES), pl.BlockSpec((None, N_EXPERTS, LANES), lambda i: (i, 0, 0))],
        out_shape=[jax.ShapeDtypeStruct((N_TOK, D_MODEL), F32), jax.ShapeDtypeStruct((N_TOK, D_MODEL), BF16),
                   jax.ShapeDtypeStruct((2 * N_EXPERTS, N_TOK), F32),
                   jax.ShapeDtypeStruct((N_TOK, LANES), F32),
                   jax.ShapeDtypeStruct((N_TILES, N_EXPERTS, LANES), F32)],
        compiler_params=_cparams(("arbitrary",)),
        name="post_attn",
    )(x, o, mods, w_out_bf, g_ffn.reshape(DEPTH, 1, D_MODEL), w_router_p, b_router.reshape(N_EXPERTS, 1))


def _shr(x, bits):
    return lax.shift_right_logical(x, jnp.int32(bits))


TR_BITS = TR.bit_length() - 1
ALIGN_BITS = RUN_ALIGN.bit_length() - 1
WIN_BITS = WIN.bit_length() - 1


def _plan_rows(cnt_ref, off_ref, npass_ref, seg_ref):
    def per_expert(e, row0):
        def per_tile(bb, r):
            off_ref[bb * N_EXPERTS + e] = r
            return r + (_shr(cnt_ref[bb * N_EXPERTS + e] + (RUN_ALIGN - 1), ALIGN_BITS) << ALIGN_BITS)

        rows_end = lax.fori_loop(0, N_TILES, per_tile, row0)
        n = _shr(rows_end - row0 + (TR - 1), TR_BITS)
        seg_ref[e] = _shr(row0, TR_BITS)
        seg_ref[N_EXPERTS + e] = n
        seg_ref[2 * N_EXPERTS + e] = rows_end
        seg_ref[3 * N_EXPERTS + e] = row0 + (n << TR_BITS)
        return row0 + (n << TR_BITS)

    end_row = lax.fori_loop(0, N_EXPERTS, per_expert, jnp.int32(0))
    seg_ref[4 * N_EXPERTS] = _shr(end_row, TR_BITS)

    def longest(bb, carry):
        m = lax.fori_loop(0, N_EXPERTS, lambda e, m: jnp.maximum(m, cnt_ref[bb * N_EXPERTS + e]), jnp.int32(0))
        npass_ref[bb] = _shr(m + (WIN - 1), WIN_BITS)
        return carry

    lax.fori_loop(0, N_TILES, longest, 0)


def _dispatch_kernel(cnt_ref, h_ref, rt_ref, et_ref, xs_hbm, off_ref, npass_ref, seg_ref, zbuf, zeros, sem):
    b = pl.program_id(0)
    slot = b % 2

    @pl.when(b == 0)
    def _():
        _plan_rows(cnt_ref, off_ref, npass_ref, seg_ref)

    key = jnp.dot(et_ref[...], _run_keys(rt_ref[...], 1).astype(BF16), preferred_element_type=F32)
    slot_j = lax.broadcasted_iota(jnp.int32, (N_EXPERTS * WIN, TM), 0) % WIN

    def run_copies(bb, sl, p, act):
        def per_expert(e, carry):
            left = cnt_ref[bb * N_EXPERTS + e] - p * WIN
            row = off_ref[bb * N_EXPERTS + e] + p * WIN
            whole = row + WIN <= seg_ref[2 * N_EXPERTS + e]

            @pl.when(whole & (left > 0))
            def _():
                src = pl.multiple_of(e * WIN, WIN)
                act(pltpu.make_async_copy(zbuf.at[sl, pl.ds(src, WIN)],
                                          xs_hbm.at[pl.ds(pl.multiple_of(row, RUN_ALIGN), WIN)], sem))

            def piece(k, c):
                src = pl.multiple_of(e * WIN + k * RUN_ALIGN, RUN_ALIGN)
                dst = pl.multiple_of(row + k * RUN_ALIGN, RUN_ALIGN)
                act(pltpu.make_async_copy(zbuf.at[sl, pl.ds(src, RUN_ALIGN)],
                                          xs_hbm.at[pl.ds(dst, RUN_ALIGN)], sem))
                return c

            pieces = jnp.minimum(_shr(jnp.maximum(left, 0) + (RUN_ALIGN - 1), ALIGN_BITS), WIN // RUN_ALIGN)
            lax.fori_loop(0, jnp.where(whole, 0, pieces), piece, 0)
            return carry

        lax.fori_loop(0, N_EXPERTS, per_expert, 0)

    def fill_and_send(p):
        pick = jnp.where(key == (slot_j + p * WIN).astype(F32), 1.0, 0.0).astype(BF16)
        zbuf[slot] = jnp.dot(pick, h_ref[...], preferred_element_type=F32).astype(BF16)
        run_copies(b, slot, p, lambda cp: cp.start(priority=1))

    @pl.when(b > 0)
    def _():
        run_copies(b - 1, 1 - slot, npass_ref[jnp.maximum(b - 1, 0)] - 1, lambda cp: cp.wait())

    fill_and_send(0)

    def more(p, carry):
        run_copies(b, slot, p - 1, lambda cp: cp.wait())
        fill_and_send(p)
        return carry

    lax.fori_loop(1, npass_ref[b], more, 0)

    @pl.when(b == N_TILES - 1)
    def _():
        run_copies(b, slot, npass_ref[b] - 1, lambda cp: cp.wait())
        zeros[...] = jnp.zeros_like(zeros)

        def fill(act):
            def per_expert(e, carry):
                for k in range(TR // RUN_ALIGN):
                    row = pl.multiple_of(seg_ref[2 * N_EXPERTS + e] + k * RUN_ALIGN, RUN_ALIGN)

                    @pl.when(row < seg_ref[3 * N_EXPERTS + e])
                    def _():
                        act(pltpu.make_async_copy(zeros.at[pl.ds(0, RUN_ALIGN)], xs_hbm.at[pl.ds(row, RUN_ALIGN)], sem))
                return carry

            lax.fori_loop(0, N_EXPERTS, per_expert, 0)

            def per_tile(t, carry):
                act(pltpu.make_async_copy(zeros, xs_hbm.at[pl.ds(pl.multiple_of(t * TR, TR), TR)], sem))
                return carry

            lax.fori_loop(seg_ref[4 * N_EXPERTS], NT, per_tile, 0)

        fill(lambda cp: cp.start())
        fill(lambda cp: cp.wait())


def _dispatch(cnt, h2, rt, et):
    smem = pl.BlockSpec(memory_space=pltpu.SMEM)
    return pl.pallas_call(
        _dispatch_kernel,
        grid_spec=pltpu.PrefetchScalarGridSpec(
            num_scalar_prefetch=1,
            grid=(N_TILES,),
            in_specs=[pl.BlockSpec((TM, D_MODEL), lambda i, *_: (i, 0)),
                      pl.BlockSpec((2 * N_EXPERTS, TM), lambda i, *_: (0, i)),
                      pl.BlockSpec((N_EXPERTS * WIN, 2 * N_EXPERTS), lambda i, *_: (0, 0))],
            out_specs=[pl.BlockSpec(memory_space=pl.ANY), smem, smem, smem],
            scratch_shapes=[pltpu.VMEM((2, N_EXPERTS * WIN, D_MODEL), BF16), pltpu.VMEM((TR, D_MODEL), BF16),
                            pltpu.SemaphoreType.DMA(())]),
        out_shape=[jax.ShapeDtypeStruct((NT * TR, D_MODEL), BF16), jax.ShapeDtypeStruct((N_RUNS,), jnp.int32),
                   jax.ShapeDtypeStruct((N_TILES,), jnp.int32),
                   jax.ShapeDtypeStruct((4 * N_EXPERTS + 1,), jnp.int32)],
        compiler_params=_cparams(("arbitrary",)),
        name="dispatch",
    )(cnt, h2, rt, et)


CAST_ROWS = 128


def _cast_rows(src_ref, dst_ref, n):
    def body(c, carry):
        rows = pl.ds(pl.multiple_of(c * CAST_ROWS, CAST_ROWS), CAST_ROWS)
        dst_ref[rows, :] = src_ref[rows, :].astype(BF16)
        return carry

    lax.fori_loop(0, n, body, 0)


def _experts_kernel(seg_ref, xs_hbm, wg_ref, wu_ref, wd_ref, ys_hbm, wgb, wub, wdb, xbuf, ybuf, semx, semy):
    e = pl.program_id(0)
    t0 = seg_ref[e]
    n = seg_ref[N_EXPERTS + e]
    _cast_rows(wg_ref, wgb, D_MODEL // CAST_ROWS)
    _cast_rows(wu_ref, wub, D_MODEL // CAST_ROWS)
    _cast_rows(wd_ref, wdb, D_EXPERT // CAST_ROWS)

    def rows(k):
        return pl.ds(pl.multiple_of((t0 + k) * TR, TR), TR)

    def fetch(k, s):
        return pltpu.make_async_copy(xs_hbm.at[rows(k)], xbuf.at[s], semx.at[s])

    def put(k, s):
        return pltpu.make_async_copy(ybuf.at[s], ys_hbm.at[rows(k)], semy.at[s])

    @pl.when(n > 0)
    def _():
        fetch(0, 0).start(priority=1)

    def tile(k, carry):
        s = k % 2

        @pl.when(k + 1 < n)
        def _():
            fetch(k + 1, 1 - s).start(priority=1)

        fetch(k, s).wait()

        @pl.when(k >= 2)
        def _():
            put(k - 2, s).wait()

        x = xbuf[s]
        hg = jnp.dot(x, wgb[...], preferred_element_type=F32)
        hu = jnp.dot(x, wub[...], preferred_element_type=F32)
        a = hg * (1.0 / (1.0 + jnp.exp(-hg))) * hu
        ybuf[s] = jnp.dot(a.astype(BF16), wdb[...], preferred_element_type=F32).astype(BF16)
        put(k, s).start(priority=1)
        return carry

    lax.fori_loop(0, n, tile, 0)

    @pl.when(n >= 2)
    def _():
        put(n - 2, n % 2).wait()

    @pl.when(n >= 1)
    def _():
        put(n - 1, (n - 1) % 2).wait()

    @pl.when(e == N_EXPERTS - 1)
    def _():
        ybuf[0] = jnp.zeros((TR, D_MODEL), BF16)

        def fill(act):
            def per_tile(t, carry):
                act(pltpu.make_async_copy(ybuf.at[0], ys_hbm.at[pl.ds(pl.multiple_of(t * TR, TR), TR)], semy.at[0]))
                return carry

            lax.fori_loop(seg_ref[4 * N_EXPERTS], NT, per_tile, 0)

        fill(lambda cp: cp.start())
        fill(lambda cp: cp.wait())


def _experts(l, seg, xs, w_gate, w_up, w_down):
    wspec = lambda a, b: pl.BlockSpec((None, None, a, b), lambda e, seg: (l, e, 0, 0))
    tile_buf = pltpu.VMEM((2, TR, D_MODEL), BF16)
    return pl.pallas_call(
        _experts_kernel,
        grid_spec=pltpu.PrefetchScalarGridSpec(
            num_scalar_prefetch=1,
            grid=(N_EXPERTS,),
            in_specs=[pl.BlockSpec(memory_space=pl.ANY),
                      wspec(D_MODEL, D_EXPERT), wspec(D_MODEL, D_EXPERT), wspec(D_EXPERT, D_MODEL)],
            out_specs=pl.BlockSpec(memory_space=pl.ANY),
            scratch_shapes=[pltpu.VMEM((D_MODEL, D_EXPERT), BF16), pltpu.VMEM((D_MODEL, D_EXPERT), BF16),
                            pltpu.VMEM((D_EXPERT, D_MODEL), BF16), tile_buf, tile_buf,
                            pltpu.SemaphoreType.DMA((2,)), pltpu.SemaphoreType.DMA((2,))]),
        out_shape=jax.ShapeDtypeStruct((NT * TR, D_MODEL), BF16),
        compiler_params=_cparams(("arbitrary",)),
        name="experts",
    )(seg, xs, w_gate, w_up, w_down)


def _final_kernel(off_ref, npass_ref, y_ref, rtm_ref, ek_ref, eg_ref, ys_hbm, mod_ref, g_ref, op_ref, os_ref,
                  wbuf, sem):
    i = pl.program_id(0)
    moe = _combined_moe(off_ref, npass_ref, rtm_ref, ek_ref, eg_ref, ys_hbm, wbuf, sem)
    out = _rms(y_ref[...] + mod_ref[5:6, :] * moe, g_ref[...])

    @pl.when(i < CTX_TILES)
    def _():
        op_ref[...] = out

    @pl.when(i >= CTX_TILES)
    def _():
        os_ref[...] = out


def _final(off, npass, y, rtm, ek, eg, ys, mods, g_final):
    return pl.pallas_call(
        _final_kernel,
        grid_spec=pltpu.PrefetchScalarGridSpec(
            num_scalar_prefetch=2,
            grid=(N_TILES,),
            in_specs=_combine_specs() + [
                pl.BlockSpec((None, None, N_MOD, D_MODEL), lambda i, *_: (DEPTH - 1, _cond_row(i), 0, 0)),
                pl.BlockSpec((1, D_MODEL), lambda i, *_: (0, 0))],
            out_specs=[pl.BlockSpec((TM, D_MODEL), lambda i, *_: (jnp.minimum(i, CTX_TILES - 1), 0)),
                       pl.BlockSpec((TM, D_MODEL), lambda i, *_: (jnp.maximum(i - CTX_TILES, 0), 0))],
            scratch_shapes=_combine_scratch()),
        out_shape=[jax.ShapeDtypeStruct((N_CTX, D_MODEL), F32), jax.ShapeDtypeStruct((N_DEC, D_MODEL), F32)],
        compiler_params=_cparams(("arbitrary",)),
        name="final_norm",
    )(off, npass, y, rtm, ek, eg, ys, mods, g_final.reshape(1, D_MODEL))


def _prep_kernel(win_ref, wout_ref, wi_ref, wo_ref):
    w = win_ref[...]
    split = KPE_OFF + MLA_ROPE_DIM
    wi_ref[...] = jnp.concatenate([w[:, :split], jnp.zeros((CAST_ROWS, SW_OFF - split), F32), w[:, split:]],
                                  axis=1).astype(BF16)
    wo_ref[...] = wout_ref[...].astype(BF16)


def _prep_weights(w_in, w_out):
    n = D_MODEL // CAST_ROWS
    shift = NA_OUT // CAST_ROWS
    return pl.pallas_call(
        _prep_kernel,
        grid=(DEPTH, n),
        in_specs=[pl.BlockSpec((None, CAST_ROWS, IN_WIDTH), lambda l, j: (l, j, 0)),
                  pl.BlockSpec((None, CAST_ROWS, D_MODEL), lambda l, j: (l, (j + shift) % n, 0))],
        out_specs=[pl.BlockSpec((None, CAST_ROWS, IN_PAD), lambda l, j: (l, j, 0)),
                   pl.BlockSpec((None, CAST_ROWS, D_MODEL), lambda l, j: (l, j, 0))],
        out_shape=[jax.ShapeDtypeStruct((DEPTH, D_MODEL, IN_PAD), BF16),
                   jax.ShapeDtypeStruct((DEPTH, D_MODEL, D_MODEL), BF16)],
        compiler_params=_cparams(("arbitrary", "arbitrary")),
        name="prep_weights",
    )(w_in, w_out)


def _rope_tables(rot_dim):
    t = jnp.arange(DEC_SEQ, dtype=jnp.int32)
    row = (t // GRID_W).astype(F32)
    col = (t % GRID_W).astype(F32)
    per_axis = rot_dim // 2
    inv = ROPE_BASE ** (-jnp.arange(0, per_axis, 2, dtype=F32) / per_axis)
    ang = jnp.concatenate([row[:, None] * inv, col[:, None] * inv], axis=-1)
    cos, sin = jnp.cos(ang), jnp.sin(ang)
    zero = jnp.zeros_like(sin)
    rep = LANES // rot_dim
    tabs = [jnp.concatenate([cos, cos], -1), jnp.concatenate([zero, sin], -1), jnp.concatenate([-sin, zero], -1)]
    return jnp.stack([jnp.tile(a, (1, rep)) for a in tabs])


def kernel(x_prompt, x_sample, c, cache_na_k, cache_na_v, cache_mla_ckv, cache_mla_kpe, cache_swa_k, cache_swa_v,
           c_ctx, w_ada, b_ada, g_attn, w_in, g_mla_q, w_mla_qb, g_mla_kv, w_mla_kvb, na_rpb, swa_sink, w_out,
           g_ffn, w_router, b_router, w_gate, w_up, w_down, g_final):
    cond = jnp.concatenate([c_ctx[None], c, jnp.zeros((COND_ROWS - 1 - DEC_BATCH, D_MODEL), F32)], axis=0)
    mods = _ada(cond, w_ada, b_ada).reshape(DEPTH, COND_ROWS, N_MOD, D_MODEL)
    bias = _na_bias(na_rpb)
    t_mla = _rope_tables(MLA_ROPE_DIM)
    t_swa = _rope_tables(HEAD_DIM)

    w_in_p, w_out_p = _prep_weights(w_in, w_out)
    wq = w_mla_qb.reshape(DEPTH, MLA_Q_RANK, MLA_HEADS, MLA_QK_DIM)
    w_rope = jnp.pad(wq[..., MLA_NOPE_DIM:], ((0, 0), (0, 0), (0, 0), (0, LANES - MLA_ROPE_DIM)))
    w_qb_p = jnp.concatenate([wq[..., :MLA_NOPE_DIM].reshape(DEPTH, MLA_Q_RANK, MQ_NOPE),
                              w_rope.reshape(DEPTH, MLA_Q_RANK, MLA_HEADS * LANES)], axis=-1).astype(BF16)
    w_router_p = jnp.pad(w_router, ((0, 0), (0, LANES - N_EXPERTS)))
    et, ek, eg = _spread_consts()

    c_na_k = cache_na_k.reshape(DEC_BATCH, DEPTH, PAST_LEN, NA_OUT)
    c_na_v = cache_na_v.reshape(DEC_BATCH, DEPTH, PAST_LEN, NA_OUT)
    c_sw_k = cache_swa_k.reshape(DEC_BATCH, DEPTH, PAST_LEN, LANES)
    c_sw_v = cache_swa_v.reshape(DEC_BATCH, DEPTH, PAST_LEN, LANES)

    y = ys = rtm = off = npass = None
    caches = [jnp.zeros((BATCH, DEPTH, SEQ, w), F32) for w in CACHE_WIDTHS]
    for l in range(DEPTH):
        if l == 0:
            srcs = [x_prompt.reshape(N_CTX, D_MODEL), x_sample.reshape(N_DEC, D_MODEL)]
        else:
            srcs = [off, npass, y, rtm, ek, eg, ys]
        x, na, mq, ckv, kpe, sw, *caches = _pre(l == 0, l, srcs, caches, mods, g_attn, w_in_p, g_mla_q, w_qb_p,
                                                g_mla_kv, t_mla, t_swa)
        o = _ctx_attn(l, swa_sink, na, mq, ckv, kpe, sw, w_mla_kvb)
        o = _mla_dec(l, mq, ckv, kpe, cache_mla_ckv, cache_mla_kpe, w_mla_kvb, o)
        o = _swa_dec(l, swa_sink, sw, c_sw_k, c_sw_v, o)
        o = _na_dec(l, na, c_na_k, c_na_v, bias, o)
        y, h2, rt, rtm, cnt = _post(l, x, o, mods, w_out_p, g_ffn, w_router_p, b_router)
        xs, off, npass, seg = _dispatch(cnt[:, :, 0].astype(jnp.int32).reshape(-1), h2, rt, et)
        ys = _experts(l, seg, xs, w_gate, w_up, w_down)
    y_prompt, y_sample = _final(off, npass, y, rtm, ek, eg, ys, mods, g_final)

    heads = lambda a, n: a.reshape(BATCH, DEPTH, SEQ, n, HEAD_DIM)
    return (y_prompt.reshape(BATCH, SEQ, D_MODEL), y_sample.reshape(DEC_BATCH, DEC_SEQ, D_MODEL),
            heads(caches[0], NA_HEADS), heads(caches[1], NA_HEADS), caches[2], caches[3],
            heads(caches[4], SWA_KV_HEADS), heads(caches[5], SWA_KV_HEADS))
```

```python
import functools

import jax
import jax.numpy as jnp
from jax import lax
from jax.experimental import pallas as pl
from jax.experimental.pallas import tpu as pltpu

D_MODEL = 1024
BATCH = 16
SEQ = 256
DEPTH = 4
DEC_BATCH = 2
DEC_SEQ = 2048
PAST_LEN = 256
GRID_W = 64
HEAD_DIM = 64
NA_HEADS = 4
NA_WIN_H = 8
NA_WIN_W = 16
MLA_HEADS = 6
MLA_Q_RANK = 256
MLA_KV_RANK = 128
MLA_NOPE_DIM = 64
MLA_ROPE_DIM = 32
MLA_V_DIM = 64
MLA_QK_DIM = MLA_NOPE_DIM + MLA_ROPE_DIM
SWA_HEADS = 6
SWA_KV_HEADS = 2
SWA_WINDOW = 128
ROPE_BASE = 10000.0
N_EXPERTS = 16
N_EXPERT_GROUPS = 4
EXPERTS_PER_GROUP = 4
D_EXPERT = 512
RMS_EPS = 1e-6
N_MOD = 6

NA_IN = 3 * NA_HEADS * HEAD_DIM
MLA_IN = MLA_Q_RANK + MLA_KV_RANK + MLA_ROPE_DIM
SWA_IN = (SWA_HEADS + 2 * SWA_KV_HEADS) * HEAD_DIM
IN_WIDTH = NA_IN + MLA_IN + SWA_IN
NA_OUT = NA_HEADS * HEAD_DIM
MLA_OUT = MLA_HEADS * MLA_V_DIM
SWA_OUT = SWA_HEADS * HEAD_DIM

LANES = 128
N_CTX = BATCH * SEQ
N_DEC = DEC_BATCH * DEC_SEQ
N_TOK = N_CTX + N_DEC
TM = 256
N_TILES = N_TOK // TM
CTX_TILES = N_CTX // TM
DEC_TILES_PER_BATCH = DEC_SEQ // TM
COND_ROWS = 8
KPE_OFF = NA_IN + MLA_Q_RANK + MLA_KV_RANK
SW_OFF = KPE_OFF + LANES
IN_PAD = SW_OFF + SWA_IN
MQ_NOPE = MLA_HEADS * MLA_NOPE_DIM
MQ_W = MQ_NOPE + MLA_HEADS * LANES
MLA_SCALE = MLA_QK_DIM ** -0.5
MLA_GROUP = 3
NA_ROWS_PER_STEP = TM // GRID_W
NA_KEY_ROWS = 12
NA_KEYS = NA_KEY_ROWS * GRID_W
SWA_KEYS = 512
NEG = -1e30
TR = 512
N_ASSIGN = 2 * N_TOK
RUN_ALIGN = 16
WIN = 64
N_RUNS = N_TILES * N_EXPERTS
NT = -(-(N_ASSIGN + N_RUNS * (RUN_ALIGN - 1) + N_EXPERTS * (TR - 1) + WIN) // TR)

F32 = jnp.float32
BF16 = jnp.bfloat16
VMEM_LIMIT = 56 * 1024 * 1024


def _cparams(sem):
    return pltpu.CompilerParams(dimension_semantics=sem, vmem_limit_bytes=VMEM_LIMIT)


def _cond_row(i):
    return jnp.where(i < CTX_TILES, 0, 1 + (i - CTX_TILES) // DEC_TILES_PER_BATCH)


def _rope_blk(i):
    return jnp.where(i < CTX_TILES, 0, (i - CTX_TILES) % DEC_TILES_PER_BATCH)


def _rms(x, g):
    ms = jnp.mean(x * x, axis=-1, keepdims=True)
    return x * lax.rsqrt(ms + RMS_EPS) * g


def _dot(a, b):
    return jnp.dot(a.astype(BF16), b.astype(BF16), preferred_element_type=F32)


def _dot_nt(a, b):
    return lax.dot_general(a.astype(BF16), b.astype(BF16), (((1,), (1,)), ((), ())),
                           preferred_element_type=F32)


def _ada_kernel(c_ref, w_ref, b_ref, o_ref):
    c = c_ref[...]
    s = c * (1.0 / (1.0 + jnp.exp(-c)))
    o_ref[...] = _dot(s, w_ref[...]) + b_ref[...]


def _ada(cond, w_ada, b_ada):
    tn = 1536
    n = N_MOD * D_MODEL
    return pl.pallas_call(
        _ada_kernel,
        grid=(DEPTH, n // tn),
        in_specs=[pl.BlockSpec((COND_ROWS, D_MODEL), lambda l, j: (0, 0)),
                  pl.BlockSpec((None, D_MODEL, tn), lambda l, j: (l, 0, j)),
                  pl.BlockSpec((None, 1, tn), lambda l, j: (l, 0, j))],
        out_specs=pl.BlockSpec((None, COND_ROWS, tn), lambda l, j: (l, 0, j)),
        out_shape=jax.ShapeDtypeStruct((DEPTH, COND_ROWS, n), F32),
        compiler_params=_cparams(("arbitrary", "arbitrary")),
        name="ada_mod",
    )(cond, w_ada, b_ada.reshape(DEPTH, 1, n))


def _bias_kernel(rpb_ref, o_ref):
    g = pl.program_id(0)
    base = g * ((2 * NA_WIN_H - 1) * (2 * NA_WIN_W - 1))
    qc = lax.broadcasted_iota(jnp.int32, (GRID_W, GRID_W), 0)
    kc = lax.broadcasted_iota(jnp.int32, (GRID_W, GRID_W), 1)
    dc = jnp.clip(kc - qc + (NA_WIN_W - 1), 0, 2 * NA_WIN_W - 2)
    cs = jnp.clip(qc - NA_WIN_W // 2, 0, GRID_W - NA_WIN_W)
    col_ok = (kc >= cs) & (kc < cs + NA_WIN_W)
    neg = jnp.full((GRID_W, GRID_W), NEG, F32)
    tabs = []
    for a in range(2 * NA_WIN_H - 1):
        t = jnp.zeros((GRID_W, GRID_W), F32)
        for b in range(2 * NA_WIN_W - 1):
            t = jnp.where(dc == b, rpb_ref[base + a * (2 * NA_WIN_W - 1) + b], t)
        tabs.append(jnp.where(col_ok, t, NEG))
    for p in range(3):
        for qi in range(NA_ROWS_PER_STEP):
            for kj in range(NA_KEY_ROWS):
                if p == 0:
                    ok, dr = kj < NA_WIN_H, kj - qi + 7
                elif p == 1:
                    ok, dr = qi <= kj < qi + NA_WIN_H, kj - qi + 3
                else:
                    ok, dr = kj >= NA_KEY_ROWS - NA_WIN_H, kj - qi - 1
                blk = tabs[dr] if ok else neg
                o_ref[p, qi * GRID_W:(qi + 1) * GRID_W, kj * GRID_W:(kj + 1) * GRID_W] = blk


def _na_bias(na_rpb):
    return pl.pallas_call(
        _bias_kernel,
        grid=(DEPTH * NA_HEADS,),
        in_specs=[pl.BlockSpec(memory_space=pltpu.SMEM)],
        out_specs=pl.BlockSpec((None, 3, None, TM, NA_KEYS),
                               lambda g: (g // NA_HEADS, 0, g % NA_HEADS, 0, 0)),
        out_shape=jax.ShapeDtypeStruct((DEPTH, 3, NA_HEADS, TM, NA_KEYS), F32),
        compiler_params=_cparams(("arbitrary",)),
        name="na_bias",
    )(na_rpb.reshape(-1))


def _rope128(x, t_ref, half):
    return (x * t_ref[0] + pltpu.roll(x, half, 1) * t_ref[1]
            + pltpu.roll(x, LANES - half, 1) * t_ref[2])


def _spread_consts():
    slot_e = jnp.arange(N_EXPERTS * WIN, dtype=jnp.int32) // WIN
    idx32 = jnp.arange(2 * N_EXPERTS, dtype=jnp.int32)
    idx128 = jnp.arange(LANES, dtype=jnp.int32)
    et = (idx32[None, :] == slot_e[:, None] + N_EXPERTS).astype(BF16)
    ek = (idx128[:, None] == slot_e[None, :] + N_EXPERTS).astype(BF16)
    eg = (idx128[:, None] == slot_e[None, :]).astype(BF16)
    return et, ek, eg


def _run_keys(chosen, token_axis):
    a = lax.broadcasted_iota(jnp.int32, (TM, TM), 0)
    b = lax.broadcasted_iota(jnp.int32, (TM, TM), 1)
    if token_axis == 1:
        rank = jnp.dot(chosen.astype(BF16), (a < b).astype(BF16), preferred_element_type=F32)
    else:
        rank = jnp.dot((b < a).astype(BF16), chosen.astype(BF16), preferred_element_type=F32)
    return jnp.where(chosen > 0.5, rank, -1.0)


def _windows_start(i, slot, p, off_ref, ys_hbm, wbuf, sem):
    for e in range(N_EXPERTS):
        row = pl.multiple_of(off_ref[i * N_EXPERTS + e] + p * WIN, RUN_ALIGN)
        pltpu.make_async_copy(ys_hbm.at[pl.ds(row, WIN)], wbuf.at[slot, pl.ds(e * WIN, WIN)],
                              sem.at[slot]).start(priority=e % 2)


def _windows_wait(slot, ys_hbm, wbuf, sem):
    pltpu.make_async_copy(ys_hbm.at[pl.ds(0, N_EXPERTS * WIN)], wbuf.at[slot], sem.at[slot]).wait()


def _combined_moe(off_ref, npass_ref, rtm_ref, ek_ref, eg_ref, ys_hbm, wbuf, sem):
    i = pl.program_id(0)
    slot = i % 2

    @pl.when(i == 0)
    def _():
        _windows_start(0, 0, 0, off_ref, ys_hbm, wbuf, sem)

    @pl.when(i + 1 < N_TILES)
    def _():
        _windows_start(i + 1, 1 - slot, 0, off_ref, ys_hbm, wbuf, sem)

    r = rtm_ref[...]
    key = jnp.dot(_run_keys(r, 0).astype(BF16), ek_ref[...], preferred_element_type=F32)
    gate = jnp.dot(r.astype(BF16), eg_ref[...], preferred_element_type=F32)
    slot_j = lax.broadcasted_iota(jnp.int32, (TM, N_EXPERTS * WIN), 1) % WIN

    def contrib(p):
        g = jnp.where(key == (slot_j + p * WIN).astype(F32), gate, 0.0).astype(BF16)
        return jnp.dot(g, wbuf[slot], preferred_element_type=F32)

    _windows_wait(slot, ys_hbm, wbuf, sem)
    acc = contrib(0)

    def extra(p, acc):
        _windows_start(i, slot, p, off_ref, ys_hbm, wbuf, sem)
        _windows_wait(slot, ys_hbm, wbuf, sem)
        return acc + contrib(p)

    return lax.fori_loop(1, npass_ref[i], extra, acc)


def _pre_kernel(first, *refs):
    i = pl.program_id(0)
    if first:
        (xp_ref, xs_ref, mod_ref, g_ref, win_ref, gq_ref, wqb_ref, gkv_ref, tm_ref, ts_ref, _, _, _, _, _, _,
         xo_ref, na_ref, mq_ref, ckv_ref, kpe_ref, sw_ref, *cache_refs) = refs
        x = jnp.where(i < CTX_TILES, xp_ref[...], xs_ref[...])
    else:
        (off_ref, npass_ref, y_ref, rtm_ref, ek_ref, eg_ref, ys_hbm, modp_ref, mod_ref, g_ref, win_ref, gq_ref,
         wqb_ref, gkv_ref, tm_ref, ts_ref, _, _, _, _, _, _,
         xo_ref, na_ref, mq_ref, ckv_ref, kpe_ref, sw_ref, *rest) = refs
        *cache_refs, wbuf, sem = rest
        moe = _combined_moe(off_ref, npass_ref, rtm_ref, ek_ref, eg_ref, ys_hbm, wbuf, sem)
        x = y_ref[...] + modp_ref[5:6, :] * moe
    xo_ref[...] = x
    cnk_ref, cnv_ref, cckv_ref, ckpe_ref, csk_ref, csv_ref = cache_refs

    h = _rms(x, g_ref[...]) * (1.0 + mod_ref[1:2, :]) + mod_ref[0:1, :]
    z = jnp.dot(h.astype(BF16), win_ref[...], preferred_element_type=F32)
    na_ref[...] = z[:, :NA_IN].astype(BF16)
    cq = _rms(z[:, NA_IN:NA_IN + MLA_Q_RANK], gq_ref[...])
    ckv = _rms(z[:, NA_IN + MLA_Q_RANK:KPE_OFF], gkv_ref[...])
    ckv_ref[...] = ckv.astype(BF16)
    mq = jnp.dot(cq.astype(BF16), wqb_ref[...], preferred_element_type=F32)
    kpe = z[:, KPE_OFF:SW_OFF]
    sw = z[:, SW_OFF:IN_PAD]
    mq_ref[:, :MQ_NOPE] = mq[:, :MQ_NOPE].astype(BF16)
    sw_ref[:, SWA_OUT + LANES:] = sw[:, SWA_OUT + LANES:].astype(BF16)

    @pl.when(i < CTX_TILES)
    def _():
        mq_ref[:, MQ_NOPE:] = (mq[:, MQ_NOPE:] * MLA_SCALE).astype(BF16)
        kpe_ref[...] = kpe.astype(BF16)
        sw_ref[:, :SWA_OUT + LANES] = sw[:, :SWA_OUT + LANES].astype(BF16)
        cnk_ref[...] = z[:, NA_OUT:2 * NA_OUT]
        cnv_ref[...] = z[:, 2 * NA_OUT:NA_IN]
        cckv_ref[...] = ckv
        ckpe_ref[...] = kpe[:, :MLA_ROPE_DIM]
        csk_ref[...] = sw[:, SWA_OUT:SWA_OUT + LANES]
        csv_ref[...] = sw[:, SWA_OUT + LANES:]

    @pl.when(i >= CTX_TILES)
    def _():
        for c in range(MQ_NOPE // LANES, MQ_W // LANES):
            roped = _rope128(mq[:, c * LANES:(c + 1) * LANES], tm_ref, MLA_ROPE_DIM // 2)
            mq_ref[:, c * LANES:(c + 1) * LANES] = (roped * MLA_SCALE).astype(BF16)
        kpe_ref[...] = _rope128(kpe, tm_ref, MLA_ROPE_DIM // 2).astype(BF16)
        for c in range((SWA_OUT + LANES) // LANES):
            sw_ref[:, c * LANES:(c + 1) * LANES] = _rope128(sw[:, c * LANES:(c + 1) * LANES], ts_ref,
                                                            HEAD_DIM // 2).astype(BF16)


def _combine_scratch():
    return [pltpu.VMEM((2, N_EXPERTS * WIN, D_MODEL), BF16), pltpu.SemaphoreType.DMA((2,))]


def _combine_specs():
    spread = pl.BlockSpec((LANES, N_EXPERTS * WIN), lambda i, *_: (0, 0))
    return [pl.BlockSpec((TM, D_MODEL), lambda i, *_: (i, 0)),
            pl.BlockSpec((TM, LANES), lambda i, *_: (i, 0)),
            spread, spread, pl.BlockSpec(memory_space=pl.ANY)]


CACHE_WIDTHS = (NA_OUT, NA_OUT, MLA_KV_RANK, MLA_ROPE_DIM, LANES, LANES)


def _pre(first, l, xs, caches, mods, g_attn, w_in_p, g_mla_q, w_qb_p, g_mla_kv, t_mla, t_swa):
    tile = lambda w: pl.BlockSpec((TM, w), lambda i, *_: (i, 0))
    ctx_tile = lambda w: pl.BlockSpec((TM, w), lambda i, *_: (jnp.minimum(i, CTX_TILES - 1), 0))
    slab = lambda w: pl.BlockSpec((None, None, SEQ, w), lambda i, *_: (jnp.minimum(i, BATCH - 1), l, 0, 0))
    mod_spec = lambda ll: pl.BlockSpec((None, None, N_MOD, D_MODEL), lambda i, *_: (ll, _cond_row(i), 0, 0))
    vec = lambda w: pl.BlockSpec((None, 1, w), lambda i, *_: (l, 0, 0))
    if first:
        in_specs = [ctx_tile(D_MODEL), pl.BlockSpec((TM, D_MODEL), lambda i: (jnp.maximum(i - CTX_TILES, 0), 0))]
    else:
        in_specs = _combine_specs() + [mod_spec(l - 1)]
    in_specs += [mod_spec(l), vec(D_MODEL),
                 pl.BlockSpec((None, D_MODEL, IN_PAD), lambda i, *_: (l, 0, 0)),
                 vec(MLA_Q_RANK),
                 pl.BlockSpec((None, MLA_Q_RANK, MQ_W), lambda i, *_: (l, 0, 0)),
                 vec(MLA_KV_RANK),
                 pl.BlockSpec((3, TM, LANES), lambda i, *_: (0, _rope_blk(i), 0)),
                 pl.BlockSpec((3, TM, LANES), lambda i, *_: (0, _rope_blk(i), 0))]
    in_specs += [pl.BlockSpec(memory_space=pl.ANY)] * len(CACHE_WIDTHS)
    widths = [D_MODEL, NA_IN, MQ_W, MLA_KV_RANK, LANES, SWA_IN]
    args = list(xs) + ([mods] if not first else []) + [
        mods, g_attn.reshape(DEPTH, 1, D_MODEL), w_in_p, g_mla_q.reshape(DEPTH, 1, MLA_Q_RANK), w_qb_p,
        g_mla_kv.reshape(DEPTH, 1, MLA_KV_RANK), t_mla, t_swa]
    aliases = {len(args) + n: len(widths) + n for n in range(len(CACHE_WIDTHS))}
    return pl.pallas_call(
        functools.partial(_pre_kernel, first),
        grid_spec=pltpu.PrefetchScalarGridSpec(
            num_scalar_prefetch=0 if first else 2,
            grid=(N_TILES,),
            in_specs=in_specs,
            out_specs=[tile(w) for w in widths] + [slab(w) for w in CACHE_WIDTHS],
            scratch_shapes=[] if first else _combine_scratch()),
        out_shape=([jax.ShapeDtypeStruct((N_TOK, w), BF16 if n else F32) for n, w in enumerate(widths)]
                   + [jax.ShapeDtypeStruct((BATCH, DEPTH, SEQ, w), F32) for w in CACHE_WIDTHS]),
        input_output_aliases=aliases,
        compiler_params=_cparams(("arbitrary",)),
        name="pre_attn",
    )(*args, *caches)


def _softmax_parts(parts, sink=None):
    m = parts[0].max(axis=-1, keepdims=True)
    for s in parts[1:]:
        m = jnp.maximum(m, s.max(axis=-1, keepdims=True))
    if sink is not None:
        m = jnp.maximum(m, sink)
    ps = [jnp.exp(s - m) for s in parts]
    den = ps[0].sum(axis=-1, keepdims=True)
    for p in ps[1:]:
        den = den + p.sum(axis=-1, keepdims=True)
    if sink is not None:
        den = den + jnp.exp(sink - m)
    return ps, 1.0 / den


def _mla_heads(mq_ref, wkvb_ref, kcat, o_ref):
    rows = mq_ref.shape[0]
    for g in range(MLA_HEADS // MLA_GROUP):
        heads = range(g * MLA_GROUP, (g + 1) * MLA_GROUP)
        qs = []
        for h in heads:
            wk = wkvb_ref[:, h * 2 * HEAD_DIM:h * 2 * HEAD_DIM + MLA_NOPE_DIM]
            qa = _dot_nt(mq_ref[:, h * MLA_NOPE_DIM:(h + 1) * MLA_NOPE_DIM], wk) * MLA_SCALE
            qr = mq_ref[:, MQ_NOPE + h * LANES:MQ_NOPE + (h + 1) * LANES]
            qs.append(jnp.concatenate([qa.astype(BF16), qr], axis=1))
        s = _dot_nt(jnp.concatenate(qs, axis=0), kcat[...])
        p = jnp.exp(s - s.max(axis=-1, keepdims=True))
        inv = 1.0 / p.sum(axis=-1, keepdims=True)
        lat = _dot(p, kcat[:, :LANES]) * inv
        for n, h in enumerate(heads):
            wv = wkvb_ref[:, h * 2 * HEAD_DIM + MLA_NOPE_DIM:(h + 1) * 2 * HEAD_DIM]
            o_ref[:, h * MLA_V_DIM:(h + 1) * MLA_V_DIM] = _dot(lat[n * rows:(n + 1) * rows], wv).astype(o_ref.dtype)


def _ctx_attn_kernel(l, sink_ref, na_ref, mq_ref, ckv_ref, kpe_ref, sw_ref, wkvb_ref, o_ref):
    @pl.when(pl.program_id(0) >= BATCH)
    def _():
        o_ref[...] = jnp.zeros_like(o_ref)

    @pl.when(pl.program_id(0) < BATCH)
    def _():
        _ctx_attn_body(l, sink_ref, na_ref, mq_ref, ckv_ref, kpe_ref, sw_ref, wkvb_ref, o_ref)


def _ctx_attn_body(l, sink_ref, na_ref, mq_ref, ckv_ref, kpe_ref, sw_ref, wkvb_ref, o_ref):
    scale = HEAD_DIM ** -0.5
    _mla_heads(mq_ref, wkvb_ref, jnp.concatenate([ckv_ref[...], kpe_ref[...]], axis=1), o_ref)
    for h in range(SWA_HEADS):
        kh = h // (SWA_HEADS // SWA_KV_HEADS)
        q = sw_ref[:, h * HEAD_DIM:(h + 1) * HEAD_DIM]
        k = sw_ref[:, SWA_OUT + kh * HEAD_DIM:SWA_OUT + (kh + 1) * HEAD_DIM]
        v = sw_ref[:, SWA_OUT + LANES + kh * HEAD_DIM:SWA_OUT + LANES + (kh + 1) * HEAD_DIM]
        (p,), inv = _softmax_parts([_dot_nt(q, k) * scale], sink_ref[l, h])
        o_ref[:, MLA_OUT + h * HEAD_DIM:MLA_OUT + (h + 1) * HEAD_DIM] = (_dot(p, v) * inv).astype(o_ref.dtype)
    for h in range(NA_HEADS):
        q = na_ref[:, h * HEAD_DIM:(h + 1) * HEAD_DIM]
        k = na_ref[:, NA_OUT + h * HEAD_DIM:NA_OUT + (h + 1) * HEAD_DIM]
        v = na_ref[:, 2 * NA_OUT + h * HEAD_DIM:2 * NA_OUT + (h + 1) * HEAD_DIM]
        (p,), inv = _softmax_parts([_dot_nt(q, k) * scale])
        o_ref[:, MLA_OUT + SWA_OUT + h * HEAD_DIM:MLA_OUT + SWA_OUT + (h + 1) * HEAD_DIM] = (
            _dot(p, v) * inv).astype(o_ref.dtype)


def _ctx_attn(l, sink, na, mq, ckv, kpe, sw, w_kvb):
    tile = lambda w: pl.BlockSpec((SEQ, w), lambda b: (jnp.minimum(b, BATCH - 1), 0))
    return pl.pallas_call(
        functools.partial(_ctx_attn_kernel, l),
        grid=(N_TOK // SEQ,),
        in_specs=[pl.BlockSpec(memory_space=pltpu.SMEM), tile(NA_IN), tile(MQ_W), tile(MLA_KV_RANK), tile(LANES),
                  tile(SWA_IN), pl.BlockSpec((None, MLA_KV_RANK, MLA_HEADS * 2 * HEAD_DIM), lambda b: (l, 0, 0))],
        out_specs=pl.BlockSpec((SEQ, D_MODEL), lambda b: (b, 0)),
        out_shape=jax.ShapeDtypeStruct((N_TOK, D_MODEL), BF16),
        compiler_params=_cparams(("arbitrary",)),
        name="ctx_attn",
    )(sink, na, mq, ckv, kpe, sw, w_kvb)


def _dec_row(b, j):
    return CTX_TILES + b * DEC_TILES_PER_BATCH + j


def _na_dec_kernel(q_ref, k_ref, v_ref, ck_ref, cv_ref, bias_ref, oin_ref, o_ref):
    del oin_ref
    scale = HEAD_DIM ** -0.5
    j = pl.program_id(1)
    w0 = jnp.clip(j * NA_ROWS_PER_STEP - NA_WIN_H // 2, 0, DEC_SEQ // GRID_W - NA_KEY_ROWS)
    start = pl.multiple_of(w0 * GRID_W, GRID_W)
    for h in range(NA_HEADS):
        sl = slice(h * HEAD_DIM, (h + 1) * HEAD_DIM)
        q = q_ref[:, sl]
        k = k_ref[pl.ds(start, NA_KEYS), sl]
        v = v_ref[pl.ds(start, NA_KEYS), sl]
        s_nb = _dot_nt(q, k) * scale + bias_ref[h]
        s_ctx = _dot_nt(q, ck_ref[:, sl]) * scale
        (p_nb, p_ctx), inv = _softmax_parts([s_nb, s_ctx])
        o_ref[:, sl] = ((_dot(p_nb, v) + _dot(p_ctx, cv_ref[:, sl])) * inv).astype(o_ref.dtype)


def _na_dec(l, na, ck, cv, bias, o):
    pat = lambda j: jnp.where(j == 0, 0, jnp.where(j == DEC_TILES_PER_BATCH - 1, 2, 1))
    return pl.pallas_call(
        _na_dec_kernel,
        grid=(DEC_BATCH, DEC_TILES_PER_BATCH),
        in_specs=[pl.BlockSpec((TM, NA_OUT), lambda b, j: (_dec_row(b, j), 0)),
                  pl.BlockSpec((DEC_SEQ, NA_OUT), lambda b, j: (N_CTX // DEC_SEQ + b, 1)),
                  pl.BlockSpec((DEC_SEQ, NA_OUT), lambda b, j: (N_CTX // DEC_SEQ + b, 2)),
                  pl.BlockSpec((None, None, PAST_LEN, NA_OUT), lambda b, j: (b, l, 0, 0)),
                  pl.BlockSpec((None, None, PAST_LEN, NA_OUT), lambda b, j: (b, l, 0, 0)),
                  pl.BlockSpec((None, None, NA_HEADS, TM, NA_KEYS), lambda b, j: (l, pat(j), 0, 0, 0)),
                  pl.BlockSpec(memory_space=pl.ANY)],
        out_specs=pl.BlockSpec((TM, NA_OUT), lambda b, j: (_dec_row(b, j), (MLA_OUT + SWA_OUT) // NA_OUT)),
        out_shape=jax.ShapeDtypeStruct((N_TOK, D_MODEL), BF16),
        input_output_aliases={6: 0},
        compiler_params=_cparams(("arbitrary", "arbitrary")),
        name="na_dec",
    )(na, na, na, ck, cv, bias, o)


KCAT_ROWS = 256


def _mla_dec_kernel(mq_ref, ckv_ref, kpe_ref, cckv_ref, ckpe_ref, wkvb_ref, oin_ref, o_ref, kcat):
    del oin_ref

    def assemble(c, carry):
        rows = pl.ds(pl.multiple_of(c * KCAT_ROWS, KCAT_ROWS), KCAT_ROWS)
        kcat[rows, :LANES] = ckv_ref[rows, :]
        kcat[rows, LANES:] = kpe_ref[rows, :]
        return carry

    first = pl.program_id(1) == 0
    lax.fori_loop(0, jnp.where(first, DEC_SEQ // KCAT_ROWS, 0), assemble, 0)

    def context(c, carry):
        kcat[DEC_SEQ:, :LANES] = cckv_ref[...].astype(BF16)
        kcat[DEC_SEQ:, LANES:] = jnp.concatenate(
            [ckpe_ref[...], jnp.zeros((PAST_LEN, LANES - MLA_ROPE_DIM), F32)], axis=1).astype(BF16)
        return carry

    lax.fori_loop(0, jnp.where(first, 1, 0), context, 0)
    _mla_heads(mq_ref, wkvb_ref, kcat, o_ref)


def _mla_dec(l, mq, ckv, kpe, cckv, ckpe, w_kvb, o):
    return pl.pallas_call(
        _mla_dec_kernel,
        grid=(DEC_BATCH, DEC_TILES_PER_BATCH),
        in_specs=[pl.BlockSpec((TM, MQ_W), lambda b, j: (_dec_row(b, j), 0)),
                  pl.BlockSpec((DEC_SEQ, MLA_KV_RANK), lambda b, j: (N_CTX // DEC_SEQ + b, 0)),
                  pl.BlockSpec((DEC_SEQ, LANES), lambda b, j: (N_CTX // DEC_SEQ + b, 0)),
                  pl.BlockSpec((None, None, PAST_LEN, MLA_KV_RANK), lambda b, j: (b, l, 0, 0)),
                  pl.BlockSpec((None, None, PAST_LEN, MLA_ROPE_DIM), lambda b, j: (b, l, 0, 0)),
                  pl.BlockSpec((None, MLA_KV_RANK, MLA_HEADS * 2 * HEAD_DIM), lambda b, j: (l, 0, 0)),
                  pl.BlockSpec(memory_space=pl.ANY)],
        out_specs=pl.BlockSpec((TM, MLA_OUT), lambda b, j: (_dec_row(b, j), 0)),
        out_shape=jax.ShapeDtypeStruct((N_TOK, D_MODEL), BF16),
        scratch_shapes=[pltpu.VMEM((DEC_SEQ + PAST_LEN, 2 * LANES), BF16)],
        input_output_aliases={6: 0},
        compiler_params=_cparams(("arbitrary", "arbitrary")),
        name="mla_dec",
    )(mq, ckv, kpe, cckv, ckpe, w_kvb, o)


def _swa_dec_kernel(l, sink_ref, q_ref, k_ref, v_ref, ck_ref, cv_ref, oin_ref, o_ref):
    del oin_ref
    scale = HEAD_DIM ** -0.5
    j = pl.program_id(1)
    start = pl.multiple_of(jnp.clip(j * TM - SWA_WINDOW, 0, DEC_SEQ - SWA_KEYS), SWA_WINDOW)
    qpos = j * TM + lax.broadcasted_iota(jnp.int32, (TM, SWA_KEYS), 0)
    kpos = start + lax.broadcasted_iota(jnp.int32, (TM, SWA_KEYS), 1)
    band = jnp.abs(qpos - kpos) <= SWA_WINDOW
    for h in range(SWA_HEADS):
        kh = h // (SWA_HEADS // SWA_KV_HEADS)
        sl = slice(kh * HEAD_DIM, (kh + 1) * HEAD_DIM)
        q = q_ref[:, h * HEAD_DIM:(h + 1) * HEAD_DIM]
        s_loc = jnp.where(band, _dot_nt(q, k_ref[pl.ds(start, SWA_KEYS), sl]) * scale, NEG)
        s_ctx = _dot_nt(q, ck_ref[:, sl]) * scale
        (p_loc, p_ctx), inv = _softmax_parts([s_loc, s_ctx], sink_ref[l, h])
        o_ref[:, h * HEAD_DIM:(h + 1) * HEAD_DIM] = (
            (_dot(p_loc, v_ref[pl.ds(start, SWA_KEYS), sl]) + _dot(p_ctx, cv_ref[:, sl])) * inv).astype(o_ref.dtype)


def _swa_dec(l, sink, sw, ck, cv, o):
    return pl.pallas_call(
        functools.partial(_swa_dec_kernel, l),
        grid=(DEC_BATCH, DEC_TILES_PER_BATCH),
        in_specs=[pl.BlockSpec(memory_space=pltpu.SMEM),
                  pl.BlockSpec((TM, SWA_OUT), lambda b, j: (_dec_row(b, j), 0)),
                  pl.BlockSpec((DEC_SEQ, LANES), lambda b, j: (N_CTX // DEC_SEQ + b, SWA_OUT // LANES)),
                  pl.BlockSpec((DEC_SEQ, LANES), lambda b, j: (N_CTX // DEC_SEQ + b, SWA_OUT // LANES + 1)),
                  pl.BlockSpec((None, None, PAST_LEN, LANES), lambda b, j: (b, l, 0, 0)),
                  pl.BlockSpec((None, None, PAST_LEN, LANES), lambda b, j: (b, l, 0, 0)),
                  pl.BlockSpec(memory_space=pl.ANY)],
        out_specs=pl.BlockSpec((TM, SWA_OUT), lambda b, j: (_dec_row(b, j), 1)),
        out_shape=jax.ShapeDtypeStruct((N_TOK, D_MODEL), BF16),
        input_output_aliases={6: 0},
        compiler_params=_cparams(("arbitrary", "arbitrary")),
        name="swa_dec",
    )(sink, sw, sw, sw, ck, cv, o)


def _route(sc, sel):
    rows = [sel[e:e + 1, :] for e in range(N_EXPERTS)]

    def beats(a, ia, b, ib):
        return (a > b) | ((a == b) & (ia < ib)) if ia < ib else (a > b)

    in_top = []
    gscore = []
    for g in range(N_EXPERT_GROUPS):
        mem = list(range(g * EXPERTS_PER_GROUP, (g + 1) * EXPERTS_PER_GROUP))
        acc = None
        for e in mem:
            rank = sum(beats(rows[o], o, rows[e], e).astype(jnp.int32) for o in mem if o != e)
            top = rank < 2
            in_top.append(top)
            term = jnp.where(top, rows[e], 0.0)
            acc = term if acc is None else acc + term
        gscore.append(acc)
    gates, chosen = [], []
    for g in range(N_EXPERT_GROUPS):
        lost = sum(beats(gscore[o], o, gscore[g], g).astype(jnp.int32) for o in range(N_EXPERT_GROUPS) if o != g)
        best = lost == 0
        for e in range(g * EXPERTS_PER_GROUP, (g + 1) * EXPERTS_PER_GROUP):
            pick = best & in_top[e]
            chosen.append(pick.astype(F32))
            gates.append(jnp.where(pick, sc[e:e + 1, :], 0.0))
    gate = jnp.concatenate(gates, axis=0)
    return gate / gate.sum(axis=0, keepdims=True), jnp.concatenate(chosen, axis=0)


def _post_kernel(x_ref, o_ref, mod_ref, wout_ref, g_ref, wr_ref, br_ref,
                 y_ref, h2_ref, rt_ref, rtm_ref, cnt_ref):
    attn = jnp.dot(o_ref[...].astype(BF16), wout_ref[...], preferred_element_type=F32)
    y = x_ref[...] + mod_ref[2:3, :] * attn
    y_ref[...] = y
    h2 = _rms(y, g_ref[...]) * (1.0 + mod_ref[4:5, :]) + mod_ref[3:4, :]
    h_hi = h2.astype(BF16)
    h_lo = (h2 - h_hi.astype(F32)).astype(BF16)
    w = wr_ref[...]
    w_hi = w.astype(BF16)
    w_lo = (w - w_hi.astype(F32)).astype(BF16)
    logits = (jnp.dot(h_hi, w_hi, preferred_element_type=F32) + jnp.dot(h_lo, w_hi, preferred_element_type=F32)
              + jnp.dot(h_hi, w_lo, preferred_element_type=F32))
    logits = logits.T[:N_EXPERTS, :]
    sc = 1.0 / (1.0 + jnp.exp(-logits))
    gate, chosen = _route(sc, sc + br_ref[...])
    h2_ref[...] = h_hi
    rt = jnp.concatenate([gate, chosen], axis=0)
    rt_ref[...] = rt
    rtm_ref[...] = jnp.concatenate([rt, jnp.zeros((LANES - 2 * N_EXPERTS, TM), F32)], axis=0).T
    cnt_ref[...] = jnp.broadcast_to(jnp.sum(chosen, axis=1, keepdims=True), (N_EXPERTS, LANES))


def _post(l, x, o, mods, w_out_bf, g_ffn, w_router_p, b_router):
    tile = lambda w: pl.BlockSpec((TM, w), lambda i: (i, 0))
    return pl.pallas_call(
        _post_kernel,
        grid=(N_TILES,),
        in_specs=[tile(D_MODEL), tile(D_MODEL),
                  pl.BlockSpec((None, None, N_MOD, D_MODEL), lambda i: (l, _cond_row(i), 0, 0)),
                  pl.BlockSpec((None, D_MODEL, D_MODEL), lambda i: (l, 0, 0)),
                  pl.BlockSpec((None, 1, D_MODEL), lambda i: (l, 0, 0)),
                  pl.BlockSpec((D_MODEL, LANES), lambda i: (0, 0)),
                  pl.BlockSpec((N_EXPERTS, 1), lambda i: (0, 0))],
        out_specs=[tile(D_MODEL), tile(D_MODEL), pl.BlockSpec((2 * N_EXPERTS, TM), lambda i: (0, i)),
                   tile(LANES), pl.BlockSpec((None, N_EXPERTS, LANES), lambda i: (i, 0, 0))],
        out_shape=[jax.ShapeDtypeStruct((N_TOK, D_MODEL), F32), jax.ShapeDtypeStruct((N_TOK, D_MODEL), BF16),
                   jax.ShapeDtypeStruct((2 * N_EXPERTS, N_TOK), F32),
                   jax.ShapeDtypeStruct((N_TOK, LANES), F32),
                   jax.ShapeDtypeStruct((N_TILES, N_EXPERTS, LANES), F32)],
        compiler_params=_cparams(("arbitrary",)),
        name="post_attn",
    )(x, o, mods, w_out_bf, g_ffn.reshape(DEPTH, 1, D_MODEL), w_router_p, b_router.reshape(N_EXPERTS, 1))


def _shr(x, bits):
    return lax.shift_right_logical(x, jnp.int32(bits))


TR_BITS = TR.bit_length() - 1
ALIGN_BITS = RUN_ALIGN.bit_length() - 1
WIN_BITS = WIN.bit_length() - 1


def _plan_rows(cnt_ref, off_ref, npass_ref, seg_ref):
    def per_expert(e, row0):
        def per_tile(bb, r):
            off_ref[bb * N_EXPERTS + e] = r
            return r + (_shr(cnt_ref[bb * N_EXPERTS + e] + (RUN_ALIGN - 1), ALIGN_BITS) << ALIGN_BITS)

        rows_end = lax.fori_loop(0, N_TILES, per_tile, row0)
        n = _shr(rows_end - row0 + (TR - 1), TR_BITS)
        seg_ref[e] = _shr(row0, TR_BITS)
        seg_ref[N_EXPERTS + e] = n
        seg_ref[2 * N_EXPERTS + e] = rows_end
        seg_ref[3 * N_EXPERTS + e] = row0 + (n << TR_BITS)
        return row0 + (n << TR_BITS)

    end_row = lax.fori_loop(0, N_EXPERTS, per_expert, jnp.int32(0))
    seg_ref[4 * N_EXPERTS] = _shr(end_row, TR_BITS)

    def longest(bb, carry):
        m = lax.fori_loop(0, N_EXPERTS, lambda e, m: jnp.maximum(m, cnt_ref[bb * N_EXPERTS + e]), jnp.int32(0))
        npass_ref[bb] = _shr(m + (WIN - 1), WIN_BITS)
        return carry

    lax.fori_loop(0, N_TILES, longest, 0)


def _dispatch_kernel(cnt_ref, h_ref, rt_ref, et_ref, xs_hbm, off_ref, npass_ref, seg_ref, zbuf, zeros, sem):
    b = pl.program_id(0)
    slot = b % 2

    @pl.when(b == 0)
    def _():
        _plan_rows(cnt_ref, off_ref, npass_ref, seg_ref)

    key = jnp.dot(et_ref[...], _run_keys(rt_ref[...], 1).astype(BF16), preferred_element_type=F32)
    slot_j = lax.broadcasted_iota(jnp.int32, (N_EXPERTS * WIN, TM), 0) % WIN

    def run_copies(bb, sl, p, act):
        def one_expert(e, queue):
            left = cnt_ref[bb * N_EXPERTS + e] - p * WIN
            row = off_ref[bb * N_EXPERTS + e] + p * WIN
            whole = row + WIN <= seg_ref[2 * N_EXPERTS + e]

            @pl.when(whole & (left > 0))
            def _():
                src = pl.multiple_of(e * WIN, WIN)
                act(pltpu.make_async_copy(zbuf.at[sl, pl.ds(src, WIN)],
                                          xs_hbm.at[pl.ds(pl.multiple_of(row, RUN_ALIGN), WIN)], sem), queue)

            def piece(k, c):
                src = pl.multiple_of(e * WIN + k * RUN_ALIGN, RUN_ALIGN)
                dst = pl.multiple_of(row + k * RUN_ALIGN, RUN_ALIGN)
                act(pltpu.make_async_copy(zbuf.at[sl, pl.ds(src, RUN_ALIGN)],
                                          xs_hbm.at[pl.ds(dst, RUN_ALIGN)], sem), queue)
                return c

            pieces = jnp.minimum(_shr(jnp.maximum(left, 0) + (RUN_ALIGN - 1), ALIGN_BITS), WIN // RUN_ALIGN)
            lax.fori_loop(0, jnp.where(whole, 0, pieces), piece, 0)

        def expert_pair(e2, carry):
            one_expert(2 * e2, 0)
            one_expert(2 * e2 + 1, 1)
            return carry

        lax.fori_loop(0, N_EXPERTS // 2, expert_pair, 0)

    def fill_and_send(p):
        pick = jnp.where(key == (slot_j + p * WIN).astype(F32), 1.0, 0.0).astype(BF16)
        zbuf[slot] = jnp.dot(pick, h_ref[...], preferred_element_type=F32).astype(BF16)
        run_copies(b, slot, p, lambda cp, queue: cp.start(priority=queue))

    @pl.when(b > 0)
    def _():
        run_copies(b - 1, 1 - slot, npass_ref[jnp.maximum(b - 1, 0)] - 1, lambda cp, queue: cp.wait())

    fill_and_send(0)

    def more(p, carry):
        run_copies(b, slot, p - 1, lambda cp, queue: cp.wait())
        fill_and_send(p)
        return carry

    lax.fori_loop(1, npass_ref[b], more, 0)

    @pl.when(b == N_TILES - 1)
    def _():
        run_copies(b, slot, npass_ref[b] - 1, lambda cp, queue: cp.wait())
        zeros[...] = jnp.zeros_like(zeros)

        def fill(act):
            def per_expert(e, carry):
                for k in range(TR // RUN_ALIGN):
                    row = pl.multiple_of(seg_ref[2 * N_EXPERTS + e] + k * RUN_ALIGN, RUN_ALIGN)

                    @pl.when(row < seg_ref[3 * N_EXPERTS + e])
                    def _():
                        act(pltpu.make_async_copy(zeros.at[pl.ds(0, RUN_ALIGN)], xs_hbm.at[pl.ds(row, RUN_ALIGN)], sem))
                return carry

            lax.fori_loop(0, N_EXPERTS, per_expert, 0)

            def per_tile(t, carry):
                act(pltpu.make_async_copy(zeros, xs_hbm.at[pl.ds(pl.multiple_of(t * TR, TR), TR)], sem))
                return carry

            lax.fori_loop(seg_ref[4 * N_EXPERTS], NT, per_tile, 0)

        fill(lambda cp: cp.start())
        fill(lambda cp: cp.wait())


def _dispatch(cnt, h2, rt, et):
    smem = pl.BlockSpec(memory_space=pltpu.SMEM)
    return pl.pallas_call(
        _dispatch_kernel,
        grid_spec=pltpu.PrefetchScalarGridSpec(
            num_scalar_prefetch=1,
            grid=(N_TILES,),
            in_specs=[pl.BlockSpec((TM, D_MODEL), lambda i, *_: (i, 0)),
                      pl.BlockSpec((2 * N_EXPERTS, TM), lambda i, *_: (0, i)),
                      pl.BlockSpec((N_EXPERTS * WIN, 2 * N_EXPERTS), lambda i, *_: (0, 0))],
            out_specs=[pl.BlockSpec(memory_space=pl.ANY), smem, smem, smem],
            scratch_shapes=[pltpu.VMEM((2, N_EXPERTS * WIN, D_MODEL), BF16), pltpu.VMEM((TR, D_MODEL), BF16),
                            pltpu.SemaphoreType.DMA(())]),
        out_shape=[jax.ShapeDtypeStruct((NT * TR, D_MODEL), BF16), jax.ShapeDtypeStruct((N_RUNS,), jnp.int32),
                   jax.ShapeDtypeStruct((N_TILES,), jnp.int32),
                   jax.ShapeDtypeStruct((4 * N_EXPERTS + 1,), jnp.int32)],
        compiler_params=_cparams(("arbitrary",)),
        name="dispatch",
    )(cnt, h2, rt, et)


CAST_ROWS = 128


def _cast_rows(src_ref, dst_ref, n):
    def body(c, carry):
        rows = pl.ds(pl.multiple_of(c * CAST_ROWS, CAST_ROWS), CAST_ROWS)
        dst_ref[rows, :] = src_ref[rows, :].astype(BF16)
        return carry

    lax.fori_loop(0, n, body, 0)


def _experts_kernel(seg_ref, xs_hbm, wg_ref, wu_ref, wd_ref, ys_hbm, wgb, wub, wdb, xbuf, ybuf, semx, semy):
    e = pl.program_id(0)
    t0 = seg_ref[e]
    n = seg_ref[N_EXPERTS + e]
    _cast_rows(wg_ref, wgb, D_MODEL // CAST_ROWS)
    _cast_rows(wu_ref, wub, D_MODEL // CAST_ROWS)
    _cast_rows(wd_ref, wdb, D_EXPERT // CAST_ROWS)

    def rows(k):
        return pl.ds(pl.multiple_of((t0 + k) * TR, TR), TR)

    def fetch(k, s):
        return pltpu.make_async_copy(xs_hbm.at[rows(k)], xbuf.at[s], semx.at[s])

    def put(k, s):
        return pltpu.make_async_copy(ybuf.at[s], ys_hbm.at[rows(k)], semy.at[s])

    @pl.when(n > 0)
    def _():
        fetch(0, 0).start()

    def tile(k, carry):
        s = k % 2

        @pl.when(k + 1 < n)
        def _():
            fetch(k + 1, 1 - s).start()

        fetch(k, s).wait()

        @pl.when(k >= 2)
        def _():
            put(k - 2, s).wait()

        x = xbuf[s]
        hg = jnp.dot(x, wgb[...], preferred_element_type=F32)
        hu = jnp.dot(x, wub[...], preferred_element_type=F32)
        a = hg * (1.0 / (1.0 + jnp.exp(-hg))) * hu
        ybuf[s] = jnp.dot(a.astype(BF16), wdb[...], preferred_element_type=F32).astype(BF16)
        put(k, s).start(priority=1)
        return carry

    lax.fori_loop(0, n, tile, 0)

    @pl.when(n >= 2)
    def _():
        put(n - 2, n % 2).wait()

    @pl.when(n >= 1)
    def _():
        put(n - 1, (n - 1) % 2).wait()

    @pl.when(e == N_EXPERTS - 1)
    def _():
        ybuf[0] = jnp.zeros((TR, D_MODEL), BF16)

        def fill(act):
            def per_tile(t, carry):
                act(pltpu.make_async_copy(ybuf.at[0], ys_hbm.at[pl.ds(pl.multiple_of(t * TR, TR), TR)], semy.at[0]))
                return carry

            lax.fori_loop(seg_ref[4 * N_EXPERTS], NT, per_tile, 0)

        fill(lambda cp: cp.start())
        fill(lambda cp: cp.wait())


def _experts(l, seg, xs, w_gate, w_up, w_down):
    wspec = lambda a, b: pl.BlockSpec((None, None, a, b), lambda e, seg: (l, e, 0, 0))
    tile_buf = pltpu.VMEM((2, TR, D_MODEL), BF16)
    return pl.pallas_call(
        _experts_kernel,
        grid_spec=pltpu.PrefetchScalarGridSpec(
            num_scalar_prefetch=1,
            grid=(N_EXPERTS,),
            in_specs=[pl.BlockSpec(memory_space=pl.ANY),
                      wspec(D_MODEL, D_EXPERT), wspec(D_MODEL, D_EXPERT), wspec(D_EXPERT, D_MODEL)],
            out_specs=pl.BlockSpec(memory_space=pl.ANY),
            scratch_shapes=[pltpu.VMEM((D_MODEL, D_EXPERT), BF16), pltpu.VMEM((D_MODEL, D_EXPERT), BF16),
                            pltpu.VMEM((D_EXPERT, D_MODEL), BF16), tile_buf, tile_buf,
                            pltpu.SemaphoreType.DMA((2,)), pltpu.SemaphoreType.DMA((2,))]),
        out_shape=jax.ShapeDtypeStruct((NT * TR, D_MODEL), BF16),
        compiler_params=_cparams(("arbitrary",)),
        name="experts",
    )(seg, xs, w_gate, w_up, w_down)


def _final_kernel(off_ref, npass_ref, y_ref, rtm_ref, ek_ref, eg_ref, ys_hbm, mod_ref, g_ref, op_ref, os_ref,
                  wbuf, sem):
    i = pl.program_id(0)
    moe = _combined_moe(off_ref, npass_ref, rtm_ref, ek_ref, eg_ref, ys_hbm, wbuf, sem)
    out = _rms(y_ref[...] + mod_ref[5:6, :] * moe, g_ref[...])

    @pl.when(i < CTX_TILES)
    def _():
        op_ref[...] = out

    @pl.when(i >= CTX_TILES)
    def _():
        os_ref[...] = out


def _final(off, npass, y, rtm, ek, eg, ys, mods, g_final):
    return pl.pallas_call(
        _final_kernel,
        grid_spec=pltpu.PrefetchScalarGridSpec(
            num_scalar_prefetch=2,
            grid=(N_TILES,),
            in_specs=_combine_specs() + [
                pl.BlockSpec((None, None, N_MOD, D_MODEL), lambda i, *_: (DEPTH - 1, _cond_row(i), 0, 0)),
                pl.BlockSpec((1, D_MODEL), lambda i, *_: (0, 0))],
            out_specs=[pl.BlockSpec((TM, D_MODEL), lambda i, *_: (jnp.minimum(i, CTX_TILES - 1), 0)),
                       pl.BlockSpec((TM, D_MODEL), lambda i, *_: (jnp.maximum(i - CTX_TILES, 0), 0))],
            scratch_shapes=_combine_scratch()),
        out_shape=[jax.ShapeDtypeStruct((N_CTX, D_MODEL), F32), jax.ShapeDtypeStruct((N_DEC, D_MODEL), F32)],
        compiler_params=_cparams(("arbitrary",)),
        name="final_norm",
    )(off, npass, y, rtm, ek, eg, ys, mods, g_final.reshape(1, D_MODEL))


def _prep_kernel(win_ref, wout_ref, wi_ref, wo_ref):
    w = win_ref[...]
    split = KPE_OFF + MLA_ROPE_DIM
    wi_ref[...] = jnp.concatenate([w[:, :split], jnp.zeros((CAST_ROWS, SW_OFF - split), F32), w[:, split:]],
                                  axis=1).astype(BF16)
    wo_ref[...] = wout_ref[...].astype(BF16)


def _prep_weights(w_in, w_out):
    n = D_MODEL // CAST_ROWS
    shift = NA_OUT // CAST_ROWS
    return pl.pallas_call(
        _prep_kernel,
        grid=(DEPTH, n),
        in_specs=[pl.BlockSpec((None, CAST_ROWS, IN_WIDTH), lambda l, j: (l, j, 0)),
                  pl.BlockSpec((None, CAST_ROWS, D_MODEL), lambda l, j: (l, (j + shift) % n, 0))],
        out_specs=[pl.BlockSpec((None, CAST_ROWS, IN_PAD), lambda l, j: (l, j, 0)),
                   pl.BlockSpec((None, CAST_ROWS, D_MODEL), lambda l, j: (l, j, 0))],
        out_shape=[jax.ShapeDtypeStruct((DEPTH, D_MODEL, IN_PAD), BF16),
                   jax.ShapeDtypeStruct((DEPTH, D_MODEL, D_MODEL), BF16)],
        compiler_params=_cparams(("arbitrary", "arbitrary")),
        name="prep_weights",
    )(w_in, w_out)


def _rope_tables(rot_dim):
    t = jnp.arange(DEC_SEQ, dtype=jnp.int32)
    row = (t // GRID_W).astype(F32)
    col = (t % GRID_W).astype(F32)
    per_axis = rot_dim // 2
    inv = ROPE_BASE ** (-jnp.arange(0, per_axis, 2, dtype=F32) / per_axis)
    ang = jnp.concatenate([row[:, None] * inv, col[:, None] * inv], axis=-1)
    cos, sin = jnp.cos(ang), jnp.sin(ang)
    zero = jnp.zeros_like(sin)
    rep = LANES // rot_dim
    tabs = [jnp.concatenate([cos, cos], -1), jnp.concatenate([zero, sin], -1), jnp.concatenate([-sin, zero], -1)]
    return jnp.stack([jnp.tile(a, (1, rep)) for a in tabs])


def kernel(x_prompt, x_sample, c, cache_na_k, cache_na_v, cache_mla_ckv, cache_mla_kpe, cache_swa_k, cache_swa_v,
           c_ctx, w_ada, b_ada, g_attn, w_in, g_mla_q, w_mla_qb, g_mla_kv, w_mla_kvb, na_rpb, swa_sink, w_out,
           g_ffn, w_router, b_router, w_gate, w_up, w_down, g_final):
    cond = jnp.concatenate([c_ctx[None], c, jnp.zeros((COND_ROWS - 1 - DEC_BATCH, D_MODEL), F32)], axis=0)
    mods = _ada(cond, w_ada, b_ada).reshape(DEPTH, COND_ROWS, N_MOD, D_MODEL)
    bias = _na_bias(na_rpb)
    t_mla = _rope_tables(MLA_ROPE_DIM)
    t_swa = _rope_tables(HEAD_DIM)

    w_in_p, w_out_p = _prep_weights(w_in, w_out)
    wq = w_mla_qb.reshape(DEPTH, MLA_Q_RANK, MLA_HEADS, MLA_QK_DIM)
    w_rope = jnp.pad(wq[..., MLA_NOPE_DIM:], ((0, 0), (0, 0), (0, 0), (0, LANES - MLA_ROPE_DIM)))
    w_qb_p = jnp.concatenate([wq[..., :MLA_NOPE_DIM].reshape(DEPTH, MLA_Q_RANK, MQ_NOPE),
                              w_rope.reshape(DEPTH, MLA_Q_RANK, MLA_HEADS * LANES)], axis=-1).astype(BF16)
    w_router_p = jnp.pad(w_router, ((0, 0), (0, LANES - N_EXPERTS)))
    et, ek, eg = _spread_consts()

    c_na_k = cache_na_k.reshape(DEC_BATCH, DEPTH, PAST_LEN, NA_OUT)
    c_na_v = cache_na_v.reshape(DEC_BATCH, DEPTH, PAST_LEN, NA_OUT)
    c_sw_k = cache_swa_k.reshape(DEC_BATCH, DEPTH, PAST_LEN, LANES)
    c_sw_v = cache_swa_v.reshape(DEC_BATCH, DEPTH, PAST_LEN, LANES)

    y = ys = rtm = off = npass = None
    caches = [jnp.zeros((BATCH, DEPTH, SEQ, w), F32) for w in CACHE_WIDTHS]
    for l in range(DEPTH):
        if l == 0:
            srcs = [x_prompt.reshape(N_CTX, D_MODEL), x_sample.reshape(N_DEC, D_MODEL)]
        else:
            srcs = [off, npass, y, rtm, ek, eg, ys]
        x, na, mq, ckv, kpe, sw, *caches = _pre(l == 0, l, srcs, caches, mods, g_attn, w_in_p, g_mla_q, w_qb_p,
                                                g_mla_kv, t_mla, t_swa)
        o = _ctx_attn(l, swa_sink, na, mq, ckv, kpe, sw, w_mla_kvb)
        o = _mla_dec(l, mq, ckv, kpe, cache_mla_ckv, cache_mla_kpe, w_mla_kvb, o)
        o = _swa_dec(l, swa_sink, sw, c_sw_k, c_sw_v, o)
        o = _na_dec(l, na, c_na_k, c_na_v, bias, o)
        y, h2, rt, rtm, cnt = _post(l, x, o, mods, w_out_p, g_ffn, w_router_p, b_router)
        xs, off, npass, seg = _dispatch(cnt[:, :, 0].astype(jnp.int32).reshape(-1), h2, rt, et)
        ys = _experts(l, seg, xs, w_gate, w_up, w_down)
    y_prompt, y_sample = _final(off, npass, y, rtm, ek, eg, ys, mods, g_final)

    heads = lambda a, n: a.reshape(BATCH, DEPTH, SEQ, n, HEAD_DIM)
    return (y_prompt.reshape(BATCH, SEQ, D_MODEL), y_sample.reshape(DEC_BATCH, DEC_SEQ, D_MODEL),
            heads(caches[0], NA_HEADS), heads(caches[1], NA_HEADS), caches[2], caches[3],
            heads(caches[4], SWA_KV_HEADS), heads(caches[5], SWA_KV_HEADS))
```

```python
import functools

import jax
import jax.numpy as jnp
from jax import lax
from jax.experimental import pallas as pl
from jax.experimental.pallas import tpu as pltpu

D_MODEL = 1024
BATCH = 16
SEQ = 256
DEPTH = 4
DEC_BATCH = 2
DEC_SEQ = 2048
PAST_LEN = 256
GRID_W = 64
HEAD_DIM = 64
NA_HEADS = 4
NA_WIN_H = 8
NA_WIN_W = 16
MLA_HEADS = 6
MLA_Q_RANK = 256
MLA_KV_RANK = 128
MLA_NOPE_DIM = 64
MLA_ROPE_DIM = 32
MLA_V_DIM = 64
MLA_QK_DIM = MLA_NOPE_DIM + MLA_ROPE_DIM
SWA_HEADS = 6
SWA_KV_HEADS = 2
SWA_WINDOW = 128
ROPE_BASE = 10000.0
N_EXPERTS = 16
N_EXPERT_GROUPS = 4
EXPERTS_PER_GROUP = 4
D_EXPERT = 512
RMS_EPS = 1e-6
N_MOD = 6

NA_IN = 3 * NA_HEADS * HEAD_DIM
MLA_IN = MLA_Q_RANK + MLA_KV_RANK + MLA_ROPE_DIM
SWA_IN = (SWA_HEADS + 2 * SWA_KV_HEADS) * HEAD_DIM
IN_WIDTH = NA_IN + MLA_IN + SWA_IN
NA_OUT = NA_HEADS * HEAD_DIM
MLA_OUT = MLA_HEADS * MLA_V_DIM
SWA_OUT = SWA_HEADS * HEAD_DIM

LANES = 128
N_CTX = BATCH * SEQ
N_DEC = DEC_BATCH * DEC_SEQ
N_TOK = N_CTX + N_DEC
TM = 256
N_TILES = N_TOK // TM
CTX_TILES = N_CTX // TM
DEC_TILES_PER_BATCH = DEC_SEQ // TM
COND_ROWS = 8
KPE_OFF = NA_IN + MLA_Q_RANK + MLA_KV_RANK
SW_OFF = KPE_OFF + LANES
IN_PAD = SW_OFF + SWA_IN
MQ_NOPE = MLA_HEADS * MLA_NOPE_DIM
MQ_W = MQ_NOPE + MLA_HEADS * LANES
MLA_SCALE = MLA_QK_DIM ** -0.5
MLA_GROUP = 3
NA_ROWS_PER_STEP = TM // GRID_W
NA_KEY_ROWS = 12
NA_KEYS = NA_KEY_ROWS * GRID_W
SWA_KEYS = 512
NEG = -1e30
TR = 512
N_ASSIGN = 2 * N_TOK
RUN_ALIGN = 16
WIN = 48
N_RUNS = N_TILES * N_EXPERTS
NT = -(-(N_ASSIGN + N_RUNS * (RUN_ALIGN - 1) + N_EXPERTS * (TR - 1) + WIN) // TR)

F32 = jnp.float32
BF16 = jnp.bfloat16
VMEM_LIMIT = 56 * 1024 * 1024


def _cparams(sem):
    return pltpu.CompilerParams(dimension_semantics=sem, vmem_limit_bytes=VMEM_LIMIT)


def _cond_row(i):
    return jnp.where(i < CTX_TILES, 0, 1 + (i - CTX_TILES) // DEC_TILES_PER_BATCH)


def _rope_blk(i):
    return jnp.where(i < CTX_TILES, 0, (i - CTX_TILES) % DEC_TILES_PER_BATCH)


def _rms(x, g):
    ms = jnp.mean(x * x, axis=-1, keepdims=True)
    return x * lax.rsqrt(ms + RMS_EPS) * g


def _dot(a, b):
    return jnp.dot(a.astype(BF16), b.astype(BF16), preferred_element_type=F32)


def _dot_nt(a, b):
    return lax.dot_general(a.astype(BF16), b.astype(BF16), (((1,), (1,)), ((), ())),
                           preferred_element_type=F32)


def _ada_kernel(c_ref, w_ref, b_ref, o_ref):
    c = c_ref[...]
    s = c * (1.0 / (1.0 + jnp.exp(-c)))
    o_ref[...] = _dot(s, w_ref[...]) + b_ref[...]


def _ada(cond, w_ada, b_ada):
    tn = 1536
    n = N_MOD * D_MODEL
    return pl.pallas_call(
        _ada_kernel,
        grid=(DEPTH, n // tn),
        in_specs=[pl.BlockSpec((COND_ROWS, D_MODEL), lambda l, j: (0, 0)),
                  pl.BlockSpec((None, D_MODEL, tn), lambda l, j: (l, 0, j)),
                  pl.BlockSpec((None, 1, tn), lambda l, j: (l, 0, j))],
        out_specs=pl.BlockSpec((None, COND_ROWS, tn), lambda l, j: (l, 0, j)),
        out_shape=jax.ShapeDtypeStruct((DEPTH, COND_ROWS, n), F32),
        compiler_params=_cparams(("arbitrary", "arbitrary")),
        name="ada_mod",
    )(cond, w_ada, b_ada.reshape(DEPTH, 1, n))


def _bias_kernel(rpb_ref, o_ref):
    g = pl.program_id(0)
    base = g * ((2 * NA_WIN_H - 1) * (2 * NA_WIN_W - 1))
    qc = lax.broadcasted_iota(jnp.int32, (GRID_W, GRID_W), 0)
    kc = lax.broadcasted_iota(jnp.int32, (GRID_W, GRID_W), 1)
    dc = jnp.clip(kc - qc + (NA_WIN_W - 1), 0, 2 * NA_WIN_W - 2)
    cs = jnp.clip(qc - NA_WIN_W // 2, 0, GRID_W - NA_WIN_W)
    col_ok = (kc >= cs) & (kc < cs + NA_WIN_W)
    neg = jnp.full((GRID_W, GRID_W), NEG, F32)
    tabs = []
    for a in range(2 * NA_WIN_H - 1):
        t = jnp.zeros((GRID_W, GRID_W), F32)
        for b in range(2 * NA_WIN_W - 1):
            t = jnp.where(dc == b, rpb_ref[base + a * (2 * NA_WIN_W - 1) + b], t)
        tabs.append(jnp.where(col_ok, t, NEG))
    for p in range(3):
        for qi in range(NA_ROWS_PER_STEP):
            for kj in range(NA_KEY_ROWS):
                if p == 0:
                    ok, dr = kj < NA_WIN_H, kj - qi + 7
                elif p == 1:
                    ok, dr = qi <= kj < qi + NA_WIN_H, kj - qi + 3
                else:
                    ok, dr = kj >= NA_KEY_ROWS - NA_WIN_H, kj - qi - 1
                blk = tabs[dr] if ok else neg
                o_ref[p, qi * GRID_W:(qi + 1) * GRID_W, kj * GRID_W:(kj + 1) * GRID_W] = blk


def _na_bias(na_rpb):
    return pl.pallas_call(
        _bias_kernel,
        grid=(DEPTH * NA_HEADS,),
        in_specs=[pl.BlockSpec(memory_space=pltpu.SMEM)],
        out_specs=pl.BlockSpec((None, 3, None, TM, NA_KEYS),
                               lambda g: (g // NA_HEADS, 0, g % NA_HEADS, 0, 0)),
        out_shape=jax.ShapeDtypeStruct((DEPTH, 3, NA_HEADS, TM, NA_KEYS), F32),
        compiler_params=_cparams(("arbitrary",)),
        name="na_bias",
    )(na_rpb.reshape(-1))


def _rope128(x, t_ref, half):
    return (x * t_ref[0] + pltpu.roll(x, half, 1) * t_ref[1]
            + pltpu.roll(x, LANES - half, 1) * t_ref[2])


def _spread_consts():
    slot_e = jnp.arange(N_EXPERTS * WIN, dtype=jnp.int32) // WIN
    idx32 = jnp.arange(2 * N_EXPERTS, dtype=jnp.int32)
    idx128 = jnp.arange(LANES, dtype=jnp.int32)
    et = (idx32[None, :] == slot_e[:, None] + N_EXPERTS).astype(BF16)
    ek = (idx128[:, None] == slot_e[None, :] + N_EXPERTS).astype(BF16)
    eg = (idx128[:, None] == slot_e[None, :]).astype(BF16)
    return et, ek, eg


def _window_slot(shape, axis):
    r = lax.broadcasted_iota(jnp.int32, shape, axis).astype(F32)
    return r - WIN * jnp.floor((r + 0.5) * (1.0 / WIN))


def _run_keys(chosen, token_axis):
    a = lax.broadcasted_iota(jnp.int32, (TM, TM), 0)
    b = lax.broadcasted_iota(jnp.int32, (TM, TM), 1)
    if token_axis == 1:
        rank = jnp.dot(chosen.astype(BF16), (a < b).astype(BF16), preferred_element_type=F32)
    else:
        rank = jnp.dot((b < a).astype(BF16), chosen.astype(BF16), preferred_element_type=F32)
    return jnp.where(chosen > 0.5, rank, -1.0)


def _windows_start(i, slot, p, off_ref, ys_hbm, wbuf, sem):
    for e in range(N_EXPERTS):
        row = pl.multiple_of(off_ref[i * N_EXPERTS + e] + p * WIN, RUN_ALIGN)
        pltpu.make_async_copy(ys_hbm.at[pl.ds(row, WIN)], wbuf.at[slot, pl.ds(e * WIN, WIN)],
                              sem.at[slot]).start(priority=e % 2)


def _windows_wait(slot, ys_hbm, wbuf, sem):
    pltpu.make_async_copy(ys_hbm.at[pl.ds(0, N_EXPERTS * WIN)], wbuf.at[slot], sem.at[slot]).wait()


def _combined_moe(off_ref, npass_ref, rtm_ref, ek_ref, eg_ref, ys_hbm, wbuf, sem):
    i = pl.program_id(0)
    slot = i % 2

    @pl.when(i == 0)
    def _():
        _windows_start(0, 0, 0, off_ref, ys_hbm, wbuf, sem)

    @pl.when(i + 1 < N_TILES)
    def _():
        _windows_start(i + 1, 1 - slot, 0, off_ref, ys_hbm, wbuf, sem)

    r = rtm_ref[...]
    key = jnp.dot(_run_keys(r, 0).astype(BF16), ek_ref[...], preferred_element_type=F32)
    gate = jnp.dot(r.astype(BF16), eg_ref[...], preferred_element_type=F32)
    slot_j = _window_slot((TM, N_EXPERTS * WIN), 1)

    def contrib(p):
        g = jnp.where(key == slot_j + jnp.asarray(p * WIN, F32), gate, 0.0).astype(BF16)
        return jnp.dot(g, wbuf[slot], preferred_element_type=F32)

    _windows_wait(slot, ys_hbm, wbuf, sem)
    acc = contrib(0)

    def extra(p, acc):
        _windows_start(i, slot, p, off_ref, ys_hbm, wbuf, sem)
        _windows_wait(slot, ys_hbm, wbuf, sem)
        return acc + contrib(p)

    return lax.fori_loop(1, npass_ref[i], extra, acc)


def _pre_kernel(first, *refs):
    i = pl.program_id(0)
    if first:
        (xp_ref, xs_ref, mod_ref, g_ref, win_ref, gq_ref, wqb_ref, gkv_ref, tm_ref, ts_ref, _, _, _, _, _, _,
         xo_ref, na_ref, mq_ref, ckv_ref, kpe_ref, sw_ref, *cache_refs) = refs
        x = jnp.where(i < CTX_TILES, xp_ref[...], xs_ref[...])
    else:
        (off_ref, npass_ref, y_ref, rtm_ref, ek_ref, eg_ref, ys_hbm, modp_ref, mod_ref, g_ref, win_ref, gq_ref,
         wqb_ref, gkv_ref, tm_ref, ts_ref, _, _, _, _, _, _,
         xo_ref, na_ref, mq_ref, ckv_ref, kpe_ref, sw_ref, *rest) = refs
        *cache_refs, wbuf, sem = rest
        moe = _combined_moe(off_ref, npass_ref, rtm_ref, ek_ref, eg_ref, ys_hbm, wbuf, sem)
        x = y_ref[...] + modp_ref[5:6, :] * moe
    xo_ref[...] = x
    cnk_ref, cnv_ref, cckv_ref, ckpe_ref, csk_ref, csv_ref = cache_refs

    h = _rms(x, g_ref[...]) * (1.0 + mod_ref[1:2, :]) + mod_ref[0:1, :]
    z = jnp.dot(h.astype(BF16), win_ref[...], preferred_element_type=F32)
    na_ref[...] = z[:, :NA_IN].astype(BF16)
    cq = _rms(z[:, NA_IN:NA_IN + MLA_Q_RANK], gq_ref[...])
    ckv = _rms(z[:, NA_IN + MLA_Q_RANK:KPE_OFF], gkv_ref[...])
    ckv_ref[...] = ckv.astype(BF16)
    mq = jnp.dot(cq.astype(BF16), wqb_ref[...], preferred_element_type=F32)
    kpe = z[:, KPE_OFF:SW_OFF]
    sw = z[:, SW_OFF:IN_PAD]
    mq_ref[:, :MQ_NOPE] = mq[:, :MQ_NOPE].astype(BF16)
    sw_ref[:, SWA_OUT + LANES:] = sw[:, SWA_OUT + LANES:].astype(BF16)

    @pl.when(i < CTX_TILES)
    def _():
        mq_ref[:, MQ_NOPE:] = (mq[:, MQ_NOPE:] * MLA_SCALE).astype(BF16)
        kpe_ref[...] = kpe.astype(BF16)
        sw_ref[:, :SWA_OUT + LANES] = sw[:, :SWA_OUT + LANES].astype(BF16)
        cnk_ref[...] = z[:, NA_OUT:2 * NA_OUT]
        cnv_ref[...] = z[:, 2 * NA_OUT:NA_IN]
        cckv_ref[...] = ckv
        ckpe_ref[...] = kpe[:, :MLA_ROPE_DIM]
        csk_ref[...] = sw[:, SWA_OUT:SWA_OUT + LANES]
        csv_ref[...] = sw[:, SWA_OUT + LANES:]

    @pl.when(i >= CTX_TILES)
    def _():
        for c in range(MQ_NOPE // LANES, MQ_W // LANES):
            roped = _rope128(mq[:, c * LANES:(c + 1) * LANES], tm_ref, MLA_ROPE_DIM // 2)
            mq_ref[:, c * LANES:(c + 1) * LANES] = (roped * MLA_SCALE).astype(BF16)
        kpe_ref[...] = _rope128(kpe, tm_ref, MLA_ROPE_DIM // 2).astype(BF16)
        for c in range((SWA_OUT + LANES) // LANES):
            sw_ref[:, c * LANES:(c + 1) * LANES] = _rope128(sw[:, c * LANES:(c + 1) * LANES], ts_ref,
                                                            HEAD_DIM // 2).astype(BF16)


def _combine_scratch():
    return [pltpu.VMEM((2, N_EXPERTS * WIN, D_MODEL), BF16), pltpu.SemaphoreType.DMA((2,))]


def _combine_specs():
    spread = pl.BlockSpec((LANES, N_EXPERTS * WIN), lambda i, *_: (0, 0))
    return [pl.BlockSpec((TM, D_MODEL), lambda i, *_: (i, 0)),
            pl.BlockSpec((TM, LANES), lambda i, *_: (i, 0)),
            spread, spread, pl.BlockSpec(memory_space=pl.ANY)]


CACHE_WIDTHS = (NA_OUT, NA_OUT, MLA_KV_RANK, MLA_ROPE_DIM, LANES, LANES)


def _pre(first, l, xs, caches, mods, g_attn, w_in_p, g_mla_q, w_qb_p, g_mla_kv, t_mla, t_swa):
    tile = lambda w: pl.BlockSpec((TM, w), lambda i, *_: (i, 0))
    ctx_tile = lambda w: pl.BlockSpec((TM, w), lambda i, *_: (jnp.minimum(i, CTX_TILES - 1), 0))
    slab = lambda w: pl.BlockSpec((None, None, SEQ, w), lambda i, *_: (jnp.minimum(i, BATCH - 1), l, 0, 0))
    mod_spec = lambda ll: pl.BlockSpec((None, None, N_MOD, D_MODEL), lambda i, *_: (ll, _cond_row(i), 0, 0))
    vec = lambda w: pl.BlockSpec((None, 1, w), lambda i, *_: (l, 0, 0))
    if first:
        in_specs = [ctx_tile(D_MODEL), pl.BlockSpec((TM, D_MODEL), lambda i: (jnp.maximum(i - CTX_TILES, 0), 0))]
    else:
        in_specs = _combine_specs() + [mod_spec(l - 1)]
    in_specs += [mod_spec(l), vec(D_MODEL),
                 pl.BlockSpec((None, D_MODEL, IN_PAD), lambda i, *_: (l, 0, 0)),
                 vec(MLA_Q_RANK),
                 pl.BlockSpec((None, MLA_Q_RANK, MQ_W), lambda i, *_: (l, 0, 0)),
                 vec(MLA_KV_RANK),
                 pl.BlockSpec((3, TM, LANES), lambda i, *_: (0, _rope_blk(i), 0)),
                 pl.BlockSpec((3, TM, LANES), lambda i, *_: (0, _rope_blk(i), 0))]
    in_specs += [pl.BlockSpec(memory_space=pl.ANY)] * len(CACHE_WIDTHS)
    widths = [D_MODEL, NA_IN, MQ_W, MLA_KV_RANK, LANES, SWA_IN]
    args = list(xs) + ([mods] if not first else []) + [
        mods, g_attn.reshape(DEPTH, 1, D_MODEL), w_in_p, g_mla_q.reshape(DEPTH, 1, MLA_Q_RANK), w_qb_p,
        g_mla_kv.reshape(DEPTH, 1, MLA_KV_RANK), t_mla, t_swa]
    aliases = {len(args) + n: len(widths) + n for n in range(len(CACHE_WIDTHS))}
    return pl.pallas_call(
        functools.partial(_pre_kernel, first),
        grid_spec=pltpu.PrefetchScalarGridSpec(
            num_scalar_prefetch=0 if first else 2,
            grid=(N_TILES,),
            in_specs=in_specs,
            out_specs=[tile(w) for w in widths] + [slab(w) for w in CACHE_WIDTHS],
            scratch_shapes=[] if first else _combine_scratch()),
        out_shape=([jax.ShapeDtypeStruct((N_TOK, w), BF16 if n else F32) for n, w in enumerate(widths)]
                   + [jax.ShapeDtypeStruct((BATCH, DEPTH, SEQ, w), F32) for w in CACHE_WIDTHS]),
        input_output_aliases=aliases,
        compiler_params=_cparams(("arbitrary",)),
        name="pre_attn",
    )(*args, *caches)


def _softmax_parts(parts, sink=None):
    m = parts[0].max(axis=-1, keepdims=True)
    for s in parts[1:]:
        m = jnp.maximum(m, s.max(axis=-1, keepdims=True))
    if sink is not None:
        m = jnp.maximum(m, sink)
    ps = [jnp.exp(s - m) for s in parts]
    den = ps[0].sum(axis=-1, keepdims=True)
    for p in ps[1:]:
        den = den + p.sum(axis=-1, keepdims=True)
    if sink is not None:
        den = den + jnp.exp(sink - m)
    return ps, 1.0 / den


def _mla_heads(mq_ref, wkvb_ref, kcat, o_ref):
    rows = mq_ref.shape[0]
    for g in range(MLA_HEADS // MLA_GROUP):
        heads = range(g * MLA_GROUP, (g + 1) * MLA_GROUP)
        qs = []
        for h in heads:
            wk = wkvb_ref[:, h * 2 * HEAD_DIM:h * 2 * HEAD_DIM + MLA_NOPE_DIM]
            qa = _dot_nt(mq_ref[:, h * MLA_NOPE_DIM:(h + 1) * MLA_NOPE_DIM], wk) * MLA_SCALE
            qr = mq_ref[:, MQ_NOPE + h * LANES:MQ_NOPE + (h + 1) * LANES]
            qs.append(jnp.concatenate([qa.astype(BF16), qr], axis=1))
        s = _dot_nt(jnp.concatenate(qs, axis=0), kcat[...])
        p = jnp.exp(s - s.max(axis=-1, keepdims=True))
        inv = 1.0 / p.sum(axis=-1, keepdims=True)
        lat = _dot(p, kcat[:, :LANES]) * inv
        for n, h in enumerate(heads):
            wv = wkvb_ref[:, h * 2 * HEAD_DIM + MLA_NOPE_DIM:(h + 1) * 2 * HEAD_DIM]
            o_ref[:, h * MLA_V_DIM:(h + 1) * MLA_V_DIM] = _dot(lat[n * rows:(n + 1) * rows], wv).astype(o_ref.dtype)


def _ctx_attn_kernel(l, sink_ref, na_ref, mq_ref, ckv_ref, kpe_ref, sw_ref, wkvb_ref, o_ref):
    @pl.when(pl.program_id(0) >= BATCH)
    def _():
        o_ref[...] = jnp.zeros_like(o_ref)

    @pl.when(pl.program_id(0) < BATCH)
    def _():
        _ctx_attn_body(l, sink_ref, na_ref, mq_ref, ckv_ref, kpe_ref, sw_ref, wkvb_ref, o_ref)


def _ctx_attn_body(l, sink_ref, na_ref, mq_ref, ckv_ref, kpe_ref, sw_ref, wkvb_ref, o_ref):
    scale = HEAD_DIM ** -0.5
    _mla_heads(mq_ref, wkvb_ref, jnp.concatenate([ckv_ref[...], kpe_ref[...]], axis=1), o_ref)
    for h in range(SWA_HEADS):
        kh = h // (SWA_HEADS // SWA_KV_HEADS)
        q = sw_ref[:, h * HEAD_DIM:(h + 1) * HEAD_DIM]
        k = sw_ref[:, SWA_OUT + kh * HEAD_DIM:SWA_OUT + (kh + 1) * HEAD_DIM]
        v = sw_ref[:, SWA_OUT + LANES + kh * HEAD_DIM:SWA_OUT + LANES + (kh + 1) * HEAD_DIM]
        (p,), inv = _softmax_parts([_dot_nt(q, k) * scale], sink_ref[l, h])
        o_ref[:, MLA_OUT + h * HEAD_DIM:MLA_OUT + (h + 1) * HEAD_DIM] = (_dot(p, v) * inv).astype(o_ref.dtype)
    for h in range(NA_HEADS):
        q = na_ref[:, h * HEAD_DIM:(h + 1) * HEAD_DIM]
        k = na_ref[:, NA_OUT + h * HEAD_DIM:NA_OUT + (h + 1) * HEAD_DIM]
        v = na_ref[:, 2 * NA_OUT + h * HEAD_DIM:2 * NA_OUT + (h + 1) * HEAD_DIM]
        (p,), inv = _softmax_parts([_dot_nt(q, k) * scale])
        o_ref[:, MLA_OUT + SWA_OUT + h * HEAD_DIM:MLA_OUT + SWA_OUT + (h + 1) * HEAD_DIM] = (
            _dot(p, v) * inv).astype(o_ref.dtype)


def _ctx_attn(l, sink, na, mq, ckv, kpe, sw, w_kvb):
    tile = lambda w: pl.BlockSpec((SEQ, w), lambda b: (jnp.minimum(b, BATCH - 1), 0))
    return pl.pallas_call(
        functools.partial(_ctx_attn_kernel, l),
        grid=(N_TOK // SEQ,),
        in_specs=[pl.BlockSpec(memory_space=pltpu.SMEM), tile(NA_IN), tile(MQ_W), tile(MLA_KV_RANK), tile(LANES),
                  tile(SWA_IN), pl.BlockSpec((None, MLA_KV_RANK, MLA_HEADS * 2 * HEAD_DIM), lambda b: (l, 0, 0))],
        out_specs=pl.BlockSpec((SEQ, D_MODEL), lambda b: (b, 0)),
        out_shape=jax.ShapeDtypeStruct((N_TOK, D_MODEL), BF16),
        compiler_params=_cparams(("arbitrary",)),
        name="ctx_attn",
    )(sink, na, mq, ckv, kpe, sw, w_kvb)


def _dec_row(b, j):
    return CTX_TILES + b * DEC_TILES_PER_BATCH + j


def _na_dec_kernel(q_ref, k_ref, v_ref, ck_ref, cv_ref, bias_ref, oin_ref, o_ref):
    del oin_ref
    scale = HEAD_DIM ** -0.5
    j = pl.program_id(1)
    w0 = jnp.clip(j * NA_ROWS_PER_STEP - NA_WIN_H // 2, 0, DEC_SEQ // GRID_W - NA_KEY_ROWS)
    start = pl.multiple_of(w0 * GRID_W, GRID_W)
    for h in range(NA_HEADS):
        sl = slice(h * HEAD_DIM, (h + 1) * HEAD_DIM)
        q = q_ref[:, sl]
        k = k_ref[pl.ds(start, NA_KEYS), sl]
        v = v_ref[pl.ds(start, NA_KEYS), sl]
        s_nb = _dot_nt(q, k) * scale + bias_ref[h]
        s_ctx = _dot_nt(q, ck_ref[:, sl]) * scale
        (p_nb, p_ctx), inv = _softmax_parts([s_nb, s_ctx])
        o_ref[:, sl] = ((_dot(p_nb, v) + _dot(p_ctx, cv_ref[:, sl])) * inv).astype(o_ref.dtype)


def _na_dec(l, na, ck, cv, bias, o):
    pat = lambda j: jnp.where(j == 0, 0, jnp.where(j == DEC_TILES_PER_BATCH - 1, 2, 1))
    return pl.pallas_call(
        _na_dec_kernel,
        grid=(DEC_BATCH, DEC_TILES_PER_BATCH),
        in_specs=[pl.BlockSpec((TM, NA_OUT), lambda b, j: (_dec_row(b, j), 0)),
                  pl.BlockSpec((DEC_SEQ, NA_OUT), lambda b, j: (N_CTX // DEC_SEQ + b, 1)),
                  pl.BlockSpec((DEC_SEQ, NA_OUT), lambda b, j: (N_CTX // DEC_SEQ + b, 2)),
                  pl.BlockSpec((None, None, PAST_LEN, NA_OUT), lambda b, j: (b, l, 0, 0)),
                  pl.BlockSpec((None, None, PAST_LEN, NA_OUT), lambda b, j: (b, l, 0, 0)),
                  pl.BlockSpec((None, None, NA_HEADS, TM, NA_KEYS), lambda b, j: (l, pat(j), 0, 0, 0)),
                  pl.BlockSpec(memory_space=pl.ANY)],
        out_specs=pl.BlockSpec((TM, NA_OUT), lambda b, j: (_dec_row(b, j), (MLA_OUT + SWA_OUT) // NA_OUT)),
        out_shape=jax.ShapeDtypeStruct((N_TOK, D_MODEL), BF16),
        input_output_aliases={6: 0},
        compiler_params=_cparams(("arbitrary", "arbitrary")),
        name="na_dec",
    )(na, na, na, ck, cv, bias, o)


KCAT_ROWS = 256


def _mla_dec_kernel(mq_ref, ckv_ref, kpe_ref, cckv_ref, ckpe_ref, wkvb_ref, oin_ref, o_ref, kcat):
    del oin_ref

    def assemble(c, carry):
        rows = pl.ds(pl.multiple_of(c * KCAT_ROWS, KCAT_ROWS), KCAT_ROWS)
        kcat[rows, :LANES] = ckv_ref[rows, :]
        kcat[rows, LANES:] = kpe_ref[rows, :]
        return carry

    first = pl.program_id(1) == 0
    lax.fori_loop(0, jnp.where(first, DEC_SEQ // KCAT_ROWS, 0), assemble, 0)

    def context(c, carry):
        kcat[DEC_SEQ:, :LANES] = cckv_ref[...].astype(BF16)
        kcat[DEC_SEQ:, LANES:] = jnp.concatenate(
            [ckpe_ref[...], jnp.zeros((PAST_LEN, LANES - MLA_ROPE_DIM), F32)], axis=1).astype(BF16)
        return carry

    lax.fori_loop(0, jnp.where(first, 1, 0), context, 0)
    _mla_heads(mq_ref, wkvb_ref, kcat, o_ref)


def _mla_dec(l, mq, ckv, kpe, cckv, ckpe, w_kvb, o):
    return pl.pallas_call(
        _mla_dec_kernel,
        grid=(DEC_BATCH, DEC_TILES_PER_BATCH),
        in_specs=[pl.BlockSpec((TM, MQ_W), lambda b, j: (_dec_row(b, j), 0)),
                  pl.BlockSpec((DEC_SEQ, MLA_KV_RANK), lambda b, j: (N_CTX // DEC_SEQ + b, 0)),
                  pl.BlockSpec((DEC_SEQ, LANES), lambda b, j: (N_CTX // DEC_SEQ + b, 0)),
                  pl.BlockSpec((None, None, PAST_LEN, MLA_KV_RANK), lambda b, j: (b, l, 0, 0)),
                  pl.BlockSpec((None, None, PAST_LEN, MLA_ROPE_DIM), lambda b, j: (b, l, 0, 0)),
                  pl.BlockSpec((None, MLA_KV_RANK, MLA_HEADS * 2 * HEAD_DIM), lambda b, j: (l, 0, 0)),
                  pl.BlockSpec(memory_space=pl.ANY)],
        out_specs=pl.BlockSpec((TM, MLA_OUT), lambda b, j: (_dec_row(b, j), 0)),
        out_shape=jax.ShapeDtypeStruct((N_TOK, D_MODEL), BF16),
        scratch_shapes=[pltpu.VMEM((DEC_SEQ + PAST_LEN, 2 * LANES), BF16)],
        input_output_aliases={6: 0},
        compiler_params=_cparams(("arbitrary", "arbitrary")),
        name="mla_dec",
    )(mq, ckv, kpe, cckv, ckpe, w_kvb, o)


def _swa_dec_kernel(l, sink_ref, q_ref, k_ref, v_ref, ck_ref, cv_ref, oin_ref, o_ref):
    del oin_ref
    scale = HEAD_DIM ** -0.5
    j = pl.program_id(1)
    start = pl.multiple_of(jnp.clip(j * TM - SWA_WINDOW, 0, DEC_SEQ - SWA_KEYS), SWA_WINDOW)
    qpos = j * TM + lax.broadcasted_iota(jnp.int32, (TM, SWA_KEYS), 0)
    kpos = start + lax.broadcasted_iota(jnp.int32, (TM, SWA_KEYS), 1)
    band = jnp.abs(qpos - kpos) <= SWA_WINDOW
    for h in range(SWA_HEADS):
        kh = h // (SWA_HEADS // SWA_KV_HEADS)
        sl = slice(kh * HEAD_DIM, (kh + 1) * HEAD_DIM)
        q = q_ref[:, h * HEAD_DIM:(h + 1) * HEAD_DIM]
        s_loc = jnp.where(band, _dot_nt(q, k_ref[pl.ds(start, SWA_KEYS), sl]) * scale, NEG)
        s_ctx = _dot_nt(q, ck_ref[:, sl]) * scale
        (p_loc, p_ctx), inv = _softmax_parts([s_loc, s_ctx], sink_ref[l, h])
        o_ref[:, h * HEAD_DIM:(h + 1) * HEAD_DIM] = (
            (_dot(p_loc, v_ref[pl.ds(start, SWA_KEYS), sl]) + _dot(p_ctx, cv_ref[:, sl])) * inv).astype(o_ref.dtype)


def _swa_dec(l, sink, sw, ck, cv, o):
    return pl.pallas_call(
        functools.partial(_swa_dec_kernel, l),
        grid=(DEC_BATCH, DEC_TILES_PER_BATCH),
        in_specs=[pl.BlockSpec(memory_space=pltpu.SMEM),
                  pl.BlockSpec((TM, SWA_OUT), lambda b, j: (_dec_row(b, j), 0)),
                  pl.BlockSpec((DEC_SEQ, LANES), lambda b, j: (N_CTX // DEC_SEQ + b, SWA_OUT // LANES)),
                  pl.BlockSpec((DEC_SEQ, LANES), lambda b, j: (N_CTX // DEC_SEQ + b, SWA_OUT // LANES + 1)),
                  pl.BlockSpec((None, None, PAST_LEN, LANES), lambda b, j: (b, l, 0, 0)),
                  pl.BlockSpec((None, None, PAST_LEN, LANES), lambda b, j: (b, l, 0, 0)),
                  pl.BlockSpec(memory_space=pl.ANY)],
        out_specs=pl.BlockSpec((TM, SWA_OUT), lambda b, j: (_dec_row(b, j), 1)),
        out_shape=jax.ShapeDtypeStruct((N_TOK, D_MODEL), BF16),
        input_output_aliases={6: 0},
        compiler_params=_cparams(("arbitrary", "arbitrary")),
        name="swa_dec",
    )(sink, sw, sw, sw, ck, cv, o)


def _route(sc, sel):
    rows = [sel[e:e + 1, :] for e in range(N_EXPERTS)]

    def beats(a, ia, b, ib):
        return (a > b) | ((a == b) & (ia < ib)) if ia < ib else (a > b)

    in_top = []
    gscore = []
    for g in range(N_EXPERT_GROUPS):
        mem = list(range(g * EXPERTS_PER_GROUP, (g + 1) * EXPERTS_PER_GROUP))
        acc = None
        for e in mem:
            rank = sum(beats(rows[o], o, rows[e], e).astype(jnp.int32) for o in mem if o != e)
            top = rank < 2
            in_top.append(top)
            term = jnp.where(top, rows[e], 0.0)
            acc = term if acc is None else acc + term
        gscore.append(acc)
    gates, chosen = [], []
    for g in range(N_EXPERT_GROUPS):
        lost = sum(beats(gscore[o], o, gscore[g], g).astype(jnp.int32) for o in range(N_EXPERT_GROUPS) if o != g)
        best = lost == 0
        for e in range(g * EXPERTS_PER_GROUP, (g + 1) * EXPERTS_PER_GROUP):
            pick = best & in_top[e]
            chosen.append(pick.astype(F32))
            gates.append(jnp.where(pick, sc[e:e + 1, :], 0.0))
    gate = jnp.concatenate(gates, axis=0)
    return gate / gate.sum(axis=0, keepdims=True), jnp.concatenate(chosen, axis=0)


def _post_kernel(x_ref, o_ref, mod_ref, wout_ref, g_ref, wr_ref, br_ref,
                 y_ref, h2_ref, rt_ref, rtm_ref, cnt_ref):
    attn = jnp.dot(o_ref[...].astype(BF16), wout_ref[...], preferred_element_type=F32)
    y = x_ref[...] + mod_ref[2:3, :] * attn
    y_ref[...] = y
    h2 = _rms(y, g_ref[...]) * (1.0 + mod_ref[4:5, :]) + mod_ref[3:4, :]
    h_hi = h2.astype(BF16)
    h_lo = (h2 - h_hi.astype(F32)).astype(BF16)
    w = wr_ref[...]
    w_hi = w.astype(BF16)
    w_lo = (w - w_hi.astype(F32)).astype(BF16)
    logits = (jnp.dot(h_hi, w_hi, preferred_element_type=F32) + jnp.dot(h_lo, w_hi, preferred_element_type=F32)
              + jnp.dot(h_hi, w_lo, preferred_element_type=F32))
    logits = logits.T[:N_EXPERTS, :]
    sc = 1.0 / (1.0 + jnp.exp(-logits))
    gate, chosen = _route(sc, sc + br_ref[...])
    h2_ref[...] = h_hi
    rt = jnp.concatenate([gate, chosen], axis=0)
    rt_ref[...] = rt
    rtm_ref[...] = jnp.concatenate([rt, jnp.zeros((LANES - 2 * N_EXPERTS, TM), F32)], axis=0).T
    cnt_ref[...] = jnp.broadcast_to(jnp.sum(chosen, axis=1, keepdims=True), (N_EXPERTS, LANES))


def _post(l, x, o, mods, w_out_bf, g_ffn, w_router_p, b_router):
    tile = lambda w: pl.BlockSpec((TM, w), lambda i: (i, 0))
    return pl.pallas_call(
        _post_kernel,
        grid=(N_TILES,),
        in_specs=[tile(D_MODEL), tile(D_MODEL),
                  pl.BlockSpec((None, None, N_MOD, D_MODEL), lambda i: (l, _cond_row(i), 0, 0)),
                  pl.BlockSpec((None, D_MODEL, D_MODEL), lambda i: (l, 0, 0)),
                  pl.BlockSpec((None, 1, D_MODEL), lambda i: (l, 0, 0)),
                  pl.BlockSpec((D_MODEL, LANES), lambda i: (0, 0)),
                  pl.BlockSpec((N_EXPERTS, 1), lambda i: (0, 0))],
        out_specs=[tile(D_MODEL), tile(D_MODEL), pl.BlockSpec((2 * N_EXPERTS, TM), lambda i: (0, i)),
                   tile(LANES), pl.BlockSpec((None, N_EXPERTS, LANES), lambda i: (i, 0, 0))],
        out_shape=[jax.ShapeDtypeStruct((N_TOK, D_MODEL), F32), jax.ShapeDtypeStruct((N_TOK, D_MODEL), BF16),
                   jax.ShapeDtypeStruct((2 * N_EXPERTS, N_TOK), F32),
                   jax.ShapeDtypeStruct((N_TOK, LANES), F32),
                   jax.ShapeDtypeStruct((N_TILES, N_EXPERTS, LANES), F32)],
        compiler_params=_cparams(("arbitrary",)),
        name="post_attn",
    )(x, o, mods, w_out_bf, g_ffn.reshape(DEPTH, 1, D_MODEL), w_router_p, b_router.reshape(N_EXPERTS, 1))


def _shr(x, bits):
    return lax.shift_right_logical(x, jnp.int32(bits))


TR_BITS = TR.bit_length() - 1
ALIGN_BITS = RUN_ALIGN.bit_length() - 1


def _plan_rows(cnt_ref, off_ref, npass_ref, seg_ref):
    def per_expert(e, row0):
        def per_tile(bb, r):
            off_ref[bb * N_EXPERTS + e] = r
            return r + (_shr(cnt_ref[bb * N_EXPERTS + e] + (RUN_ALIGN - 1), ALIGN_BITS) << ALIGN_BITS)

        rows_end = lax.fori_loop(0, N_TILES, per_tile, row0)
        n = _shr(rows_end - row0 + (TR - 1), TR_BITS)
        seg_ref[e] = _shr(row0, TR_BITS)
        seg_ref[N_EXPERTS + e] = n
        seg_ref[2 * N_EXPERTS + e] = rows_end
        seg_ref[3 * N_EXPERTS + e] = row0 + (n << TR_BITS)
        return row0 + (n << TR_BITS)

    end_row = lax.fori_loop(0, N_EXPERTS, per_expert, jnp.int32(0))
    seg_ref[4 * N_EXPERTS] = _shr(end_row, TR_BITS)

    def longest(bb, carry):
        m = lax.fori_loop(0, N_EXPERTS, lambda e, m: jnp.maximum(m, cnt_ref[bb * N_EXPERTS + e]), jnp.int32(0))
        npass_ref[bb] = sum((m > k * WIN).astype(jnp.int32) for k in range(-(-TM // WIN)))
        return carry

    lax.fori_loop(0, N_TILES, longest, 0)


def _dispatch_kernel(cnt_ref, h_ref, rt_ref, et_ref, xs_hbm, off_ref, npass_ref, seg_ref, zbuf, zeros, sem):
    b = pl.program_id(0)
    slot = b % 2

    @pl.when(b == 0)
    def _():
        _plan_rows(cnt_ref, off_ref, npass_ref, seg_ref)

    key = jnp.dot(et_ref[...], _run_keys(rt_ref[...], 1).astype(BF16), preferred_element_type=F32)
    slot_j = _window_slot((N_EXPERTS * WIN, TM), 0)

    def run_copies(bb, sl, p, act):
        def one_expert(e, queue):
            left = cnt_ref[bb * N_EXPERTS + e] - p * WIN
            row = off_ref[bb * N_EXPERTS + e] + p * WIN
            whole = row + WIN <= seg_ref[2 * N_EXPERTS + e]

            @pl.when(whole & (left > 0))
            def _():
                src = pl.multiple_of(e * WIN, RUN_ALIGN)
                act(pltpu.make_async_copy(zbuf.at[sl, pl.ds(src, WIN)],
                                          xs_hbm.at[pl.ds(pl.multiple_of(row, RUN_ALIGN), WIN)], sem), queue)

            def piece(k, c):
                src = pl.multiple_of(e * WIN + k * RUN_ALIGN, RUN_ALIGN)
                dst = pl.multiple_of(row + k * RUN_ALIGN, RUN_ALIGN)
                act(pltpu.make_async_copy(zbuf.at[sl, pl.ds(src, RUN_ALIGN)],
                                          xs_hbm.at[pl.ds(dst, RUN_ALIGN)], sem), queue)
                return c

            pieces = jnp.minimum(_shr(jnp.maximum(left, 0) + (RUN_ALIGN - 1), ALIGN_BITS), WIN // RUN_ALIGN)
            lax.fori_loop(0, jnp.where(whole, 0, pieces), piece, 0)

        def expert_pair(e2, carry):
            one_expert(2 * e2, 0)
            one_expert(2 * e2 + 1, 1)
            return carry

        lax.fori_loop(0, N_EXPERTS // 2, expert_pair, 0)

    def fill_and_send(p):
        pick = jnp.where(key == slot_j + jnp.asarray(p * WIN, F32), 1.0, 0.0).astype(BF16)
        zbuf[slot] = jnp.dot(pick, h_ref[...], preferred_element_type=F32).astype(BF16)
        run_copies(b, slot, p, lambda cp, queue: cp.start(priority=queue))

    @pl.when(b > 0)
    def _():
        run_copies(b - 1, 1 - slot, npass_ref[jnp.maximum(b - 1, 0)] - 1, lambda cp, queue: cp.wait())

    fill_and_send(0)

    def more(p, carry):
        run_copies(b, slot, p - 1, lambda cp, queue: cp.wait())
        fill_and_send(p)
        return carry

    lax.fori_loop(1, npass_ref[b], more, 0)

    @pl.when(b == N_TILES - 1)
    def _():
        run_copies(b, slot, npass_ref[b] - 1, lambda cp, queue: cp.wait())
        zeros[...] = jnp.zeros_like(zeros)

        def fill(act):
            def per_expert(e, carry):
                for k in range(TR // RUN_ALIGN):
                    row = pl.multiple_of(seg_ref[2 * N_EXPERTS + e] + k * RUN_ALIGN, RUN_ALIGN)

                    @pl.when(row < seg_ref[3 * N_EXPERTS + e])
                    def _():
                        act(pltpu.make_async_copy(zeros.at[pl.ds(0, RUN_ALIGN)], xs_hbm.at[pl.ds(row, RUN_ALIGN)], sem))
                return carry

            lax.fori_loop(0, N_EXPERTS, per_expert, 0)

            def per_tile(t, carry):
                act(pltpu.make_async_copy(zeros, xs_hbm.at[pl.ds(pl.multiple_of(t * TR, TR), TR)], sem))
                return carry

            lax.fori_loop(seg_ref[4 * N_EXPERTS], NT, per_tile, 0)

        fill(lambda cp: cp.start())
        fill(lambda cp: cp.wait())


def _dispatch(cnt, h2, rt, et):
    smem = pl.BlockSpec(memory_space=pltpu.SMEM)
    return pl.pallas_call(
        _dispatch_kernel,
        grid_spec=pltpu.PrefetchScalarGridSpec(
            num_scalar_prefetch=1,
            grid=(N_TILES,),
            in_specs=[pl.BlockSpec((TM, D_MODEL), lambda i, *_: (i, 0)),
                      pl.BlockSpec((2 * N_EXPERTS, TM), lambda i, *_: (0, i)),
                      pl.BlockSpec((N_EXPERTS * WIN, 2 * N_EXPERTS), lambda i, *_: (0, 0))],
            out_specs=[pl.BlockSpec(memory_space=pl.ANY), smem, smem, smem],
            scratch_shapes=[pltpu.VMEM((2, N_EXPERTS * WIN, D_MODEL), BF16), pltpu.VMEM((TR, D_MODEL), BF16),
                            pltpu.SemaphoreType.DMA(())]),
        out_shape=[jax.ShapeDtypeStruct((NT * TR, D_MODEL), BF16), jax.ShapeDtypeStruct((N_RUNS,), jnp.int32),
                   jax.ShapeDtypeStruct((N_TILES,), jnp.int32),
                   jax.ShapeDtypeStruct((4 * N_EXPERTS + 1,), jnp.int32)],
        compiler_params=_cparams(("arbitrary",)),
        name="dispatch",
    )(cnt, h2, rt, et)


CAST_ROWS = 128


def _cast_rows(src_ref, dst_ref, n):
    def body(c, carry):
        rows = pl.ds(pl.multiple_of(c * CAST_ROWS, CAST_ROWS), CAST_ROWS)
        dst_ref[rows, :] = src_ref[rows, :].astype(BF16)
        return carry

    lax.fori_loop(0, n, body, 0)


def _experts_kernel(seg_ref, xs_hbm, wg_ref, wu_ref, wd_ref, ys_hbm, wgb, wub, wdb, xbuf, ybuf, semx, semy):
    e = pl.program_id(0)
    t0 = seg_ref[e]
    n = seg_ref[N_EXPERTS + e]
    _cast_rows(wg_ref, wgb, D_MODEL // CAST_ROWS)
    _cast_rows(wu_ref, wub, D_MODEL // CAST_ROWS)
    _cast_rows(wd_ref, wdb, D_EXPERT // CAST_ROWS)

    def rows(k):
        return pl.ds(pl.multiple_of((t0 + k) * TR, TR), TR)

    def fetch(k, s):
        return pltpu.make_async_copy(xs_hbm.at[rows(k)], xbuf.at[s], semx.at[s])

    def put(k, s):
        return pltpu.make_async_copy(ybuf.at[s], ys_hbm.at[rows(k)], semy.at[s])

    @pl.when(n > 0)
    def _():
        fetch(0, 0).start()

    def tile(k, carry):
        s = k % 2

        @pl.when(k + 1 < n)
        def _():
            fetch(k + 1, 1 - s).start()

        fetch(k, s).wait()

        @pl.when(k >= 2)
        def _():
            put(k - 2, s).wait()

        x = xbuf[s]
        hg = jnp.dot(x, wgb[...], preferred_element_type=F32)
        hu = jnp.dot(x, wub[...], preferred_element_type=F32)
        a = hg * (1.0 / (1.0 + jnp.exp(-hg))) * hu
        ybuf[s] = jnp.dot(a.astype(BF16), wdb[...], preferred_element_type=F32).astype(BF16)
        put(k, s).start(priority=1)
        return carry

    lax.fori_loop(0, n, tile, 0)

    @pl.when(n >= 2)
    def _():
        put(n - 2, n % 2).wait()

    @pl.when(n >= 1)
    def _():
        put(n - 1, (n - 1) % 2).wait()

    @pl.when(e == N_EXPERTS - 1)
    def _():
        ybuf[0] = jnp.zeros((TR, D_MODEL), BF16)

        def fill(act):
            def per_tile(t, carry):
                act(pltpu.make_async_copy(ybuf.at[0], ys_hbm.at[pl.ds(pl.multiple_of(t * TR, TR), TR)], semy.at[0]))
                return carry

            lax.fori_loop(seg_ref[4 * N_EXPERTS], NT, per_tile, 0)

        fill(lambda cp: cp.start())
        fill(lambda cp: cp.wait())


def _experts(l, seg, xs, w_gate, w_up, w_down):
    wspec = lambda a, b: pl.BlockSpec((None, None, a, b), lambda e, seg: (l, e, 0, 0))
    tile_buf = pltpu.VMEM((2, TR, D_MODEL), BF16)
    return pl.pallas_call(
        _experts_kernel,
        grid_spec=pltpu.PrefetchScalarGridSpec(
            num_scalar_prefetch=1,
            grid=(N_EXPERTS,),
            in_specs=[pl.BlockSpec(memory_space=pl.ANY),
                      wspec(D_MODEL, D_EXPERT), wspec(D_MODEL, D_EXPERT), wspec(D_EXPERT, D_MODEL)],
            out_specs=pl.BlockSpec(memory_space=pl.ANY),
            scratch_shapes=[pltpu.VMEM((D_MODEL, D_EXPERT), BF16), pltpu.VMEM((D_MODEL, D_EXPERT), BF16),
                            pltpu.VMEM((D_EXPERT, D_MODEL), BF16), tile_buf, tile_buf,
                            pltpu.SemaphoreType.DMA((2,)), pltpu.SemaphoreType.DMA((2,))]),
        out_shape=jax.ShapeDtypeStruct((NT * TR, D_MODEL), BF16),
        compiler_params=_cparams(("arbitrary",)),
        name="experts",
    )(seg, xs, w_gate, w_up, w_down)


def _final_kernel(off_ref, npass_ref, y_ref, rtm_ref, ek_ref, eg_ref, ys_hbm, mod_ref, g_ref, op_ref, os_ref,
                  wbuf, sem):
    i = pl.program_id(0)
    moe = _combined_moe(off_ref, npass_ref, rtm_ref, ek_ref, eg_ref, ys_hbm, wbuf, sem)
    out = _rms(y_ref[...] + mod_ref[5:6, :] * moe, g_ref[...])

    @pl.when(i < CTX_TILES)
    def _():
        op_ref[...] = out

    @pl.when(i >= CTX_TILES)
    def _():
        os_ref[...] = out


def _final(off, npass, y, rtm, ek, eg, ys, mods, g_final):
    return pl.pallas_call(
        _final_kernel,
        grid_spec=pltpu.PrefetchScalarGridSpec(
            num_scalar_prefetch=2,
            grid=(N_TILES,),
            in_specs=_combine_specs() + [
                pl.BlockSpec((None, None, N_MOD, D_MODEL), lambda i, *_: (DEPTH - 1, _cond_row(i), 0, 0)),
                pl.BlockSpec((1, D_MODEL), lambda i, *_: (0, 0))],
            out_specs=[pl.BlockSpec((TM, D_MODEL), lambda i, *_: (jnp.minimum(i, CTX_TILES - 1), 0)),
                       pl.BlockSpec((TM, D_MODEL), lambda i, *_: (jnp.maximum(i - CTX_TILES, 0), 0))],
            scratch_shapes=_combine_scratch()),
        out_shape=[jax.ShapeDtypeStruct((N_CTX, D_MODEL), F32), jax.ShapeDtypeStruct((N_DEC, D_MODEL), F32)],
        compiler_params=_cparams(("arbitrary",)),
        name="final_norm",
    )(off, npass, y, rtm, ek, eg, ys, mods, g_final.reshape(1, D_MODEL))


def _prep_kernel(win_ref, wout_ref, wi_ref, wo_ref):
    w = win_ref[...]
    split = KPE_OFF + MLA_ROPE_DIM
    wi_ref[...] = jnp.concatenate([w[:, :split], jnp.zeros((CAST_ROWS, SW_OFF - split), F32), w[:, split:]],
                                  axis=1).astype(BF16)
    wo_ref[...] = wout_ref[...].astype(BF16)


def _prep_weights(w_in, w_out):
    n = D_MODEL // CAST_ROWS
    shift = NA_OUT // CAST_ROWS
    return pl.pallas_call(
        _prep_kernel,
        grid=(DEPTH, n),
        in_specs=[pl.BlockSpec((None, CAST_ROWS, IN_WIDTH), lambda l, j: (l, j, 0)),
                  pl.BlockSpec((None, CAST_ROWS, D_MODEL), lambda l, j: (l, (j + shift) % n, 0))],
        out_specs=[pl.BlockSpec((None, CAST_ROWS, IN_PAD), lambda l, j: (l, j, 0)),
                   pl.BlockSpec((None, CAST_ROWS, D_MODEL), lambda l, j: (l, j, 0))],
        out_shape=[jax.ShapeDtypeStruct((DEPTH, D_MODEL, IN_PAD), BF16),
                   jax.ShapeDtypeStruct((DEPTH, D_MODEL, D_MODEL), BF16)],
        compiler_params=_cparams(("arbitrary", "arbitrary")),
        name="prep_weights",
    )(w_in, w_out)


def _rope_tables(rot_dim):
    t = jnp.arange(DEC_SEQ, dtype=jnp.int32)
    row = (t // GRID_W).astype(F32)
    col = (t % GRID_W).astype(F32)
    per_axis = rot_dim // 2
    inv = ROPE_BASE ** (-jnp.arange(0, per_axis, 2, dtype=F32) / per_axis)
    ang = jnp.concatenate([row[:, None] * inv, col[:, None] * inv], axis=-1)
    cos, sin = jnp.cos(ang), jnp.sin(ang)
    zero = jnp.zeros_like(sin)
    rep = LANES // rot_dim
    tabs = [jnp.concatenate([cos, cos], -1), jnp.concatenate([zero, sin], -1), jnp.concatenate([-sin, zero], -1)]
    return jnp.stack([jnp.tile(a, (1, rep)) for a in tabs])


def kernel(x_prompt, x_sample, c, cache_na_k, cache_na_v, cache_mla_ckv, cache_mla_kpe, cache_swa_k, cache_swa_v,
           c_ctx, w_ada, b_ada, g_attn, w_in, g_mla_q, w_mla_qb, g_mla_kv, w_mla_kvb, na_rpb, swa_sink, w_out,
           g_ffn, w_router, b_router, w_gate, w_up, w_down, g_final):
    cond = jnp.concatenate([c_ctx[None], c, jnp.zeros((COND_ROWS - 1 - DEC_BATCH, D_MODEL), F32)], axis=0)
    mods = _ada(cond, w_ada, b_ada).reshape(DEPTH, COND_ROWS, N_MOD, D_MODEL)
    bias = _na_bias(na_rpb)
    t_mla = _rope_tables(MLA_ROPE_DIM)
    t_swa = _rope_tables(HEAD_DIM)

    w_in_p, w_out_p = _prep_weights(w_in, w_out)
    wq = w_mla_qb.reshape(DEPTH, MLA_Q_RANK, MLA_HEADS, MLA_QK_DIM)
    w_rope = jnp.pad(wq[..., MLA_NOPE_DIM:], ((0, 0), (0, 0), (0, 0), (0, LANES - MLA_ROPE_DIM)))
    w_qb_p = jnp.concatenate([wq[..., :MLA_NOPE_DIM].reshape(DEPTH, MLA_Q_RANK, MQ_NOPE),
                              w_rope.reshape(DEPTH, MLA_Q_RANK, MLA_HEADS * LANES)], axis=-1).astype(BF16)
    w_router_p = jnp.pad(w_router, ((0, 0), (0, LANES - N_EXPERTS)))
    et, ek, eg = _spread_consts()

    c_na_k = cache_na_k.reshape(DEC_BATCH, DEPTH, PAST_LEN, NA_OUT)
    c_na_v = cache_na_v.reshape(DEC_BATCH, DEPTH, PAST_LEN, NA_OUT)
    c_sw_k = cache_swa_k.reshape(DEC_BATCH, DEPTH, PAST_LEN, LANES)
    c_sw_v = cache_swa_v.reshape(DEC_BATCH, DEPTH, PAST_LEN, LANES)

    y = ys = rtm = off = npass = None
    caches = [jnp.zeros((BATCH, DEPTH, SEQ, w), F32) for w in CACHE_WIDTHS]
    for l in range(DEPTH):
        if l == 0:
            srcs = [x_prompt.reshape(N_CTX, D_MODEL), x_sample.reshape(N_DEC, D_MODEL)]
        else:
            srcs = [off, npass, y, rtm, ek, eg, ys]
        x, na, mq, ckv, kpe, sw, *caches = _pre(l == 0, l, srcs, caches, mods, g_attn, w_in_p, g_mla_q, w_qb_p,
                                                g_mla_kv, t_mla, t_swa)
        o = _ctx_attn(l, swa_sink, na, mq, ckv, kpe, sw, w_mla_kvb)
        o = _mla_dec(l, mq, ckv, kpe, cache_mla_ckv, cache_mla_kpe, w_mla_kvb, o)
        o = _swa_dec(l, swa_sink, sw, c_sw_k, c_sw_v, o)
        o = _na_dec(l, na, c_na_k, c_na_v, bias, o)
        y, h2, rt, rtm, cnt = _post(l, x, o, mods, w_out_p, g_ffn, w_router_p, b_router)
        xs, off, npass, seg = _dispatch(cnt[:, :, 0].astype(jnp.int32).reshape(-1), h2, rt, et)
        ys = _experts(l, seg, xs, w_gate, w_up, w_down)
    y_prompt, y_sample = _final(off, npass, y, rtm, ek, eg, ys, mods, g_final)

    heads = lambda a, n: a.reshape(BATCH, DEPTH, SEQ, n, HEAD_DIM)
    return (y_prompt.reshape(BATCH, SEQ, D_MODEL), y_sample.reshape(DEC_BATCH, DEC_SEQ, D_MODEL),
            heads(caches[0], NA_HEADS), heads(caches[1], NA_HEADS), caches[2], caches[3],
            heads(caches[4], SWA_KV_HEADS), heads(caches[5], SWA_KV_HEADS))
```

```python
import functools

import jax
import jax.numpy as jnp
from jax import lax
from jax.experimental import pallas as pl
from jax.experimental.pallas import tpu as pltpu

D_MODEL = 1024
BATCH = 16
SEQ = 256
DEPTH = 4
DEC_BATCH = 2
DEC_SEQ = 2048
PAST_LEN = 256
GRID_W = 64
HEAD_DIM = 64
NA_HEADS = 4
NA_WIN_H = 8
NA_WIN_W = 16
MLA_HEADS = 6
MLA_Q_RANK = 256
MLA_KV_RANK = 128
MLA_NOPE_DIM = 64
MLA_ROPE_DIM = 32
MLA_V_DIM = 64
MLA_QK_DIM = MLA_NOPE_DIM + MLA_ROPE_DIM
SWA_HEADS = 6
SWA_KV_HEADS = 2
SWA_WINDOW = 128
ROPE_BASE = 10000.0
N_EXPERTS = 16
N_EXPERT_GROUPS = 4
EXPERTS_PER_GROUP = 4
D_EXPERT = 512
RMS_EPS = 1e-6
N_MOD = 6

NA_IN = 3 * NA_HEADS * HEAD_DIM
MLA_IN = MLA_Q_RANK + MLA_KV_RANK + MLA_ROPE_DIM
SWA_IN = (SWA_HEADS + 2 * SWA_KV_HEADS) * HEAD_DIM
IN_WIDTH = NA_IN + MLA_IN + SWA_IN
NA_OUT = NA_HEADS * HEAD_DIM
MLA_OUT = MLA_HEADS * MLA_V_DIM
SWA_OUT = SWA_HEADS * HEAD_DIM

LANES = 128
N_CTX = BATCH * SEQ
N_DEC = DEC_BATCH * DEC_SEQ
N_TOK = N_CTX + N_DEC
TM = 256
N_TILES = N_TOK // TM
CTX_TILES = N_CTX // TM
DEC_TILES_PER_BATCH = DEC_SEQ // TM
COND_ROWS = 8
KPE_OFF = NA_IN + MLA_Q_RANK + MLA_KV_RANK
SW_OFF = KPE_OFF + LANES
IN_PAD = SW_OFF + SWA_IN
MQ_NOPE = MLA_HEADS * MLA_NOPE_DIM
MQ_W = MQ_NOPE + MLA_HEADS * LANES
MLA_SCALE = MLA_QK_DIM ** -0.5
MLA_GROUP = 3
NA_ROWS_PER_STEP = TM // GRID_W
NA_KEY_ROWS = 12
NA_KEYS = NA_KEY_ROWS * GRID_W
SWA_KEYS = 512
NEG = -1e30
TR = 512
N_ASSIGN = 2 * N_TOK
RUN_ALIGN = 16
WIN = 80
N_RUNS = N_TILES * N_EXPERTS
NT = -(-(N_ASSIGN + N_RUNS * (RUN_ALIGN - 1) + N_EXPERTS * (TR - 1) + WIN) // TR)

F32 = jnp.float32
BF16 = jnp.bfloat16
VMEM_LIMIT = 56 * 1024 * 1024


def _cparams(sem):
    return pltpu.CompilerParams(dimension_semantics=sem, vmem_limit_bytes=VMEM_LIMIT)


def _cond_row(i):
    return jnp.where(i < CTX_TILES, 0, 1 + (i - CTX_TILES) // DEC_TILES_PER_BATCH)


def _rope_blk(i):
    return jnp.where(i < CTX_TILES, 0, (i - CTX_TILES) % DEC_TILES_PER_BATCH)


def _rms(x, g):
    ms = jnp.mean(x * x, axis=-1, keepdims=True)
    return x * lax.rsqrt(ms + RMS_EPS) * g


def _dot(a, b):
    return jnp.dot(a.astype(BF16), b.astype(BF16), preferred_element_type=F32)


def _dot_nt(a, b):
    return lax.dot_general(a.astype(BF16), b.astype(BF16), (((1,), (1,)), ((), ())),
                           preferred_element_type=F32)


def _ada_kernel(c_ref, w_ref, b_ref, o_ref):
    c = c_ref[...]
    s = c * (1.0 / (1.0 + jnp.exp(-c)))
    o_ref[...] = _dot(s, w_ref[...]) + b_ref[...]


def _ada(cond, w_ada, b_ada):
    tn = 1536
    n = N_MOD * D_MODEL
    return pl.pallas_call(
        _ada_kernel,
        grid=(DEPTH, n // tn),
        in_specs=[pl.BlockSpec((COND_ROWS, D_MODEL), lambda l, j: (0, 0)),
                  pl.BlockSpec((None, D_MODEL, tn), lambda l, j: (l, 0, j)),
                  pl.BlockSpec((None, 1, tn), lambda l, j: (l, 0, j))],
        out_specs=pl.BlockSpec((None, COND_ROWS, tn), lambda l, j: (l, 0, j)),
        out_shape=jax.ShapeDtypeStruct((DEPTH, COND_ROWS, n), F32),
        compiler_params=_cparams(("arbitrary", "arbitrary")),
        name="ada_mod",
    )(cond, w_ada, b_ada.reshape(DEPTH, 1, n))


def _bias_kernel(rpb_ref, o_ref):
    g = pl.program_id(0)
    base = g * ((2 * NA_WIN_H - 1) * (2 * NA_WIN_W - 1))
    qc = lax.broadcasted_iota(jnp.int32, (GRID_W, GRID_W), 0)
    kc = lax.broadcasted_iota(jnp.int32, (GRID_W, GRID_W), 1)
    dc = jnp.clip(kc - qc + (NA_WIN_W - 1), 0, 2 * NA_WIN_W - 2)
    cs = jnp.clip(qc - NA_WIN_W // 2, 0, GRID_W - NA_WIN_W)
    col_ok = (kc >= cs) & (kc < cs + NA_WIN_W)
    neg = jnp.full((GRID_W, GRID_W), NEG, F32)
    tabs = []
    for a in range(2 * NA_WIN_H - 1):
        t = jnp.zeros((GRID_W, GRID_W), F32)
        for b in range(2 * NA_WIN_W - 1):
            t = jnp.where(dc == b, rpb_ref[base + a * (2 * NA_WIN_W - 1) + b], t)
        tabs.append(jnp.where(col_ok, t, NEG))
    for p in range(3):
        for qi in range(NA_ROWS_PER_STEP):
            for kj in range(NA_KEY_ROWS):
                if p == 0:
                    ok, dr = kj < NA_WIN_H, kj - qi + 7
                elif p == 1:
                    ok, dr = qi <= kj < qi + NA_WIN_H, kj - qi + 3
                else:
                    ok, dr = kj >= NA_KEY_ROWS - NA_WIN_H, kj - qi - 1
                blk = tabs[dr] if ok else neg
                o_ref[p, qi * GRID_W:(qi + 1) * GRID_W, kj * GRID_W:(kj + 1) * GRID_W] = blk


def _na_bias(na_rpb):
    return pl.pallas_call(
        _bias_kernel,
        grid=(DEPTH * NA_HEADS,),
        in_specs=[pl.BlockSpec(memory_space=pltpu.SMEM)],
        out_specs=pl.BlockSpec((None, 3, None, TM, NA_KEYS),
                               lambda g: (g // NA_HEADS, 0, g % NA_HEADS, 0, 0)),
        out_shape=jax.ShapeDtypeStruct((DEPTH, 3, NA_HEADS, TM, NA_KEYS), F32),
        compiler_params=_cparams(("arbitrary",)),
        name="na_bias",
    )(na_rpb.reshape(-1))


def _rope128(x, t_ref, half):
    return (x * t_ref[0] + pltpu.roll(x, half, 1) * t_ref[1]
            + pltpu.roll(x, LANES - half, 1) * t_ref[2])


def _spread_consts():
    slot_e = jnp.arange(N_EXPERTS * WIN, dtype=jnp.int32) // WIN
    idx32 = jnp.arange(2 * N_EXPERTS, dtype=jnp.int32)
    idx128 = jnp.arange(LANES, dtype=jnp.int32)
    et = (idx32[None, :] == slot_e[:, None] + N_EXPERTS).astype(BF16)
    ek = (idx128[:, None] == slot_e[None, :] + N_EXPERTS).astype(BF16)
    eg = (idx128[:, None] == slot_e[None, :]).astype(BF16)
    return et, ek, eg


def _window_slot(shape, axis):
    r = lax.broadcasted_iota(jnp.int32, shape, axis).astype(F32)
    return r - WIN * jnp.floor((r + 0.5) * (1.0 / WIN))


def _run_keys(chosen, token_axis):
    a = lax.broadcasted_iota(jnp.int32, (TM, TM), 0)
    b = lax.broadcasted_iota(jnp.int32, (TM, TM), 1)
    if token_axis == 1:
        rank = jnp.dot(chosen.astype(BF16), (a < b).astype(BF16), preferred_element_type=F32)
    else:
        rank = jnp.dot((b < a).astype(BF16), chosen.astype(BF16), preferred_element_type=F32)
    return jnp.where(chosen > 0.5, rank, -1.0)


def _windows_start(i, slot, p, off_ref, ys_hbm, wbuf, sem):
    for e in range(N_EXPERTS):
        row = pl.multiple_of(off_ref[i * N_EXPERTS + e] + p * WIN, RUN_ALIGN)
        pltpu.make_async_copy(ys_hbm.at[pl.ds(row, WIN)], wbuf.at[slot, pl.ds(e * WIN, WIN)],
                              sem.at[slot]).start(priority=e % 2)


def _windows_wait(slot, ys_hbm, wbuf, sem):
    pltpu.make_async_copy(ys_hbm.at[pl.ds(0, N_EXPERTS * WIN)], wbuf.at[slot], sem.at[slot]).wait()


def _combined_moe(off_ref, npass_ref, rtm_ref, ek_ref, eg_ref, ys_hbm, wbuf, sem):
    i = pl.program_id(0)
    slot = i % 2

    @pl.when(i == 0)
    def _():
        _windows_start(0, 0, 0, off_ref, ys_hbm, wbuf, sem)

    @pl.when(i + 1 < N_TILES)
    def _():
        _windows_start(i + 1, 1 - slot, 0, off_ref, ys_hbm, wbuf, sem)

    r = rtm_ref[...]
    key = jnp.dot(_run_keys(r, 0).astype(BF16), ek_ref[...], preferred_element_type=F32)
    gate = jnp.dot(r.astype(BF16), eg_ref[...], preferred_element_type=F32)
    slot_j = _window_slot((TM, N_EXPERTS * WIN), 1)

    def contrib(p):
        g = jnp.where(key == slot_j + jnp.asarray(p * WIN, F32), gate, 0.0).astype(BF16)
        return jnp.dot(g, wbuf[slot], preferred_element_type=F32)

    _windows_wait(slot, ys_hbm, wbuf, sem)
    acc = contrib(0)

    def extra(p, acc):
        _windows_start(i, slot, p, off_ref, ys_hbm, wbuf, sem)
        _windows_wait(slot, ys_hbm, wbuf, sem)
        return acc + contrib(p)

    return lax.fori_loop(1, npass_ref[i], extra, acc)


def _pre_kernel(first, *refs):
    i = pl.program_id(0)
    if first:
        (xp_ref, xs_ref, mod_ref, g_ref, win_ref, gq_ref, wqb_ref, gkv_ref, tm_ref, ts_ref, _, _, _, _, _, _,
         xo_ref, na_ref, mq_ref, ckv_ref, kpe_ref, sw_ref, *cache_refs) = refs
        x = jnp.where(i < CTX_TILES, xp_ref[...], xs_ref[...])
    else:
        (off_ref, npass_ref, y_ref, rtm_ref, ek_ref, eg_ref, ys_hbm, modp_ref, mod_ref, g_ref, win_ref, gq_ref,
         wqb_ref, gkv_ref, tm_ref, ts_ref, _, _, _, _, _, _,
         xo_ref, na_ref, mq_ref, ckv_ref, kpe_ref, sw_ref, *rest) = refs
        *cache_refs, wbuf, sem = rest
        moe = _combined_moe(off_ref, npass_ref, rtm_ref, ek_ref, eg_ref, ys_hbm, wbuf, sem)
        x = y_ref[...] + modp_ref[5:6, :] * moe
    xo_ref[...] = x
    cnk_ref, cnv_ref, cckv_ref, ckpe_ref, csk_ref, csv_ref = cache_refs

    h = _rms(x, g_ref[...]) * (1.0 + mod_ref[1:2, :]) + mod_ref[0:1, :]
    z = jnp.dot(h.astype(BF16), win_ref[...], preferred_element_type=F32)
    na_ref[...] = z[:, :NA_IN].astype(BF16)
    cq = _rms(z[:, NA_IN:NA_IN + MLA_Q_RANK], gq_ref[...])
    ckv = _rms(z[:, NA_IN + MLA_Q_RANK:KPE_OFF], gkv_ref[...])
    ckv_ref[...] = ckv.astype(BF16)
    mq = jnp.dot(cq.astype(BF16), wqb_ref[...], preferred_element_type=F32)
    kpe = z[:, KPE_OFF:SW_OFF]
    sw = z[:, SW_OFF:IN_PAD]
    mq_ref[:, :MQ_NOPE] = mq[:, :MQ_NOPE].astype(BF16)
    sw_ref[:, SWA_OUT + LANES:] = sw[:, SWA_OUT + LANES:].astype(BF16)

    @pl.when(i < CTX_TILES)
    def _():
        mq_ref[:, MQ_NOPE:] = (mq[:, MQ_NOPE:] * MLA_SCALE).astype(BF16)
        kpe_ref[...] = kpe.astype(BF16)
        sw_ref[:, :SWA_OUT + LANES] = sw[:, :SWA_OUT + LANES].astype(BF16)
        cnk_ref[...] = z[:, NA_OUT:2 * NA_OUT]
        cnv_ref[...] = z[:, 2 * NA_OUT:NA_IN]
        cckv_ref[...] = ckv
        ckpe_ref[...] = kpe[:, :MLA_ROPE_DIM]
        csk_ref[...] = sw[:, SWA_OUT:SWA_OUT + LANES]
        csv_ref[...] = sw[:, SWA_OUT + LANES:]

    @pl.when(i >= CTX_TILES)
    def _():
        for c in range(MQ_NOPE // LANES, MQ_W // LANES):
            roped = _rope128(mq[:, c * LANES:(c + 1) * LANES], tm_ref, MLA_ROPE_DIM // 2)
            mq_ref[:, c * LANES:(c + 1) * LANES] = (roped * MLA_SCALE).astype(BF16)
        kpe_ref[...] = _rope128(kpe, tm_ref, MLA_ROPE_DIM // 2).astype(BF16)
        for c in range((SWA_OUT + LANES) // LANES):
            sw_ref[:, c * LANES:(c + 1) * LANES] = _rope128(sw[:, c * LANES:(c + 1) * LANES], ts_ref,
                                                            HEAD_DIM // 2).astype(BF16)


def _combine_scratch():
    return [pltpu.VMEM((2, N_EXPERTS * WIN, D_MODEL), BF16), pltpu.SemaphoreType.DMA((2,))]


def _combine_specs():
    spread = pl.BlockSpec((LANES, N_EXPERTS * WIN), lambda i, *_: (0, 0))
    return [pl.BlockSpec((TM, D_MODEL), lambda i, *_: (i, 0)),
            pl.BlockSpec((TM, LANES), lambda i, *_: (i, 0)),
            spread, spread, pl.BlockSpec(memory_space=pl.ANY)]


CACHE_WIDTHS = (NA_OUT, NA_OUT, MLA_KV_RANK, MLA_ROPE_DIM, LANES, LANES)


def _pre(first, l, xs, caches, mods, g_attn, w_in_p, g_mla_q, w_qb_p, g_mla_kv, t_mla, t_swa):
    tile = lambda w: pl.BlockSpec((TM, w), lambda i, *_: (i, 0))
    ctx_tile = lambda w: pl.BlockSpec((TM, w), lambda i, *_: (jnp.minimum(i, CTX_TILES - 1), 0))
    slab = lambda w: pl.BlockSpec((None, None, SEQ, w), lambda i, *_: (jnp.minimum(i, BATCH - 1), l, 0, 0))
    mod_spec = lambda ll: pl.BlockSpec((None, None, N_MOD, D_MODEL), lambda i, *_: (ll, _cond_row(i), 0, 0))
    vec = lambda w: pl.BlockSpec((None, 1, w), lambda i, *_: (l, 0, 0))
    if first:
        in_specs = [ctx_tile(D_MODEL), pl.BlockSpec((TM, D_MODEL), lambda i: (jnp.maximum(i - CTX_TILES, 0), 0))]
    else:
        in_specs = _combine_specs() + [mod_spec(l - 1)]
    in_specs += [mod_spec(l), vec(D_MODEL),
                 pl.BlockSpec((None, D_MODEL, IN_PAD), lambda i, *_: (l, 0, 0)),
                 vec(MLA_Q_RANK),
                 pl.BlockSpec((None, MLA_Q_RANK, MQ_W), lambda i, *_: (l, 0, 0)),
                 vec(MLA_KV_RANK),
                 pl.BlockSpec((3, TM, LANES), lambda i, *_: (0, _rope_blk(i), 0)),
                 pl.BlockSpec((3, TM, LANES), lambda i, *_: (0, _rope_blk(i), 0))]
    in_specs += [pl.BlockSpec(memory_space=pl.ANY)] * len(CACHE_WIDTHS)
    widths = [D_MODEL, NA_IN, MQ_W, MLA_KV_RANK, LANES, SWA_IN]
    args = list(xs) + ([mods] if not first else []) + [
        mods, g_attn.reshape(DEPTH, 1, D_MODEL), w_in_p, g_mla_q.reshape(DEPTH, 1, MLA_Q_RANK), w_qb_p,
        g_mla_kv.reshape(DEPTH, 1, MLA_KV_RANK), t_mla, t_swa]
    aliases = {len(args) + n: len(widths) + n for n in range(len(CACHE_WIDTHS))}
    return pl.pallas_call(
        functools.partial(_pre_kernel, first),
        grid_spec=pltpu.PrefetchScalarGridSpec(
            num_scalar_prefetch=0 if first else 2,
            grid=(N_TILES,),
            in_specs=in_specs,
            out_specs=[tile(w) for w in widths] + [slab(w) for w in CACHE_WIDTHS],
            scratch_shapes=[] if first else _combine_scratch()),
        out_shape=([jax.ShapeDtypeStruct((N_TOK, w), BF16 if n else F32) for n, w in enumerate(widths)]
                   + [jax.ShapeDtypeStruct((BATCH, DEPTH, SEQ, w), F32) for w in CACHE_WIDTHS]),
        input_output_aliases=aliases,
        compiler_params=_cparams(("arbitrary",)),
        name="pre_attn",
    )(*args, *caches)


def _softmax_parts(parts, sink=None):
    m = parts[0].max(axis=-1, keepdims=True)
    for s in parts[1:]:
        m = jnp.maximum(m, s.max(axis=-1, keepdims=True))
    if sink is not None:
        m = jnp.maximum(m, sink)
    ps = [jnp.exp(s - m) for s in parts]
    den = ps[0].sum(axis=-1, keepdims=True)
    for p in ps[1:]:
        den = den + p.sum(axis=-1, keepdims=True)
    if sink is not None:
        den = den + jnp.exp(sink - m)
    return ps, 1.0 / den


def _mla_heads(mq_ref, wkvb_ref, kcat, o_ref):
    rows = mq_ref.shape[0]
    for g in range(MLA_HEADS // MLA_GROUP):
        heads = range(g * MLA_GROUP, (g + 1) * MLA_GROUP)
        qs = []
        for h in heads:
            wk = wkvb_ref[:, h * 2 * HEAD_DIM:h * 2 * HEAD_DIM + MLA_NOPE_DIM]
            qa = _dot_nt(mq_ref[:, h * MLA_NOPE_DIM:(h + 1) * MLA_NOPE_DIM], wk) * MLA_SCALE
            qr = mq_ref[:, MQ_NOPE + h * LANES:MQ_NOPE + (h + 1) * LANES]
            qs.append(jnp.concatenate([qa.astype(BF16), qr], axis=1))
        s = _dot_nt(jnp.concatenate(qs, axis=0), kcat[...])
        p = jnp.exp(s - s.max(axis=-1, keepdims=True))
        inv = 1.0 / p.sum(axis=-1, keepdims=True)
        lat = _dot(p, kcat[:, :LANES]) * inv
        for n, h in enumerate(heads):
            wv = wkvb_ref[:, h * 2 * HEAD_DIM + MLA_NOPE_DIM:(h + 1) * 2 * HEAD_DIM]
            o_ref[:, h * MLA_V_DIM:(h + 1) * MLA_V_DIM] = _dot(lat[n * rows:(n + 1) * rows], wv).astype(o_ref.dtype)


def _ctx_attn_kernel(l, sink_ref, na_ref, mq_ref, ckv_ref, kpe_ref, sw_ref, wkvb_ref, o_ref):
    @pl.when(pl.program_id(0) >= BATCH)
    def _():
        o_ref[...] = jnp.zeros_like(o_ref)

    @pl.when(pl.program_id(0) < BATCH)
    def _():
        _ctx_attn_body(l, sink_ref, na_ref, mq_ref, ckv_ref, kpe_ref, sw_ref, wkvb_ref, o_ref)


def _ctx_attn_body(l, sink_ref, na_ref, mq_ref, ckv_ref, kpe_ref, sw_ref, wkvb_ref, o_ref):
    scale = HEAD_DIM ** -0.5
    _mla_heads(mq_ref, wkvb_ref, jnp.concatenate([ckv_ref[...], kpe_ref[...]], axis=1), o_ref)
    for h in range(SWA_HEADS):
        kh = h // (SWA_HEADS // SWA_KV_HEADS)
        q = sw_ref[:, h * HEAD_DIM:(h + 1) * HEAD_DIM]
        k = sw_ref[:, SWA_OUT + kh * HEAD_DIM:SWA_OUT + (kh + 1) * HEAD_DIM]
        v = sw_ref[:, SWA_OUT + LANES + kh * HEAD_DIM:SWA_OUT + LANES + (kh + 1) * HEAD_DIM]
        (p,), inv = _softmax_parts([_dot_nt(q, k) * scale], sink_ref[l, h])
        o_ref[:, MLA_OUT + h * HEAD_DIM:MLA_OUT + (h + 1) * HEAD_DIM] = (_dot(p, v) * inv).astype(o_ref.dtype)
    for h in range(NA_HEADS):
        q = na_ref[:, h * HEAD_DIM:(h + 1) * HEAD_DIM]
        k = na_ref[:, NA_OUT + h * HEAD_DIM:NA_OUT + (h + 1) * HEAD_DIM]
        v = na_ref[:, 2 * NA_OUT + h * HEAD_DIM:2 * NA_OUT + (h + 1) * HEAD_DIM]
        (p,), inv = _softmax_parts([_dot_nt(q, k) * scale])
        o_ref[:, MLA_OUT + SWA_OUT + h * HEAD_DIM:MLA_OUT + SWA_OUT + (h + 1) * HEAD_DIM] = (
            _dot(p, v) * inv).astype(o_ref.dtype)


def _ctx_attn(l, sink, na, mq, ckv, kpe, sw, w_kvb):
    tile = lambda w: pl.BlockSpec((SEQ, w), lambda b: (jnp.minimum(b, BATCH - 1), 0))
    return pl.pallas_call(
        functools.partial(_ctx_attn_kernel, l),
        grid=(N_TOK // SEQ,),
        in_specs=[pl.BlockSpec(memory_space=pltpu.SMEM), tile(NA_IN), tile(MQ_W), tile(MLA_KV_RANK), tile(LANES),
                  tile(SWA_IN), pl.BlockSpec((None, MLA_KV_RANK, MLA_HEADS * 2 * HEAD_DIM), lambda b: (l, 0, 0))],
        out_specs=pl.BlockSpec((SEQ, D_MODEL), lambda b: (b, 0)),
        out_shape=jax.ShapeDtypeStruct((N_TOK, D_MODEL), BF16),
        compiler_params=_cparams(("arbitrary",)),
        name="ctx_attn",
    )(sink, na, mq, ckv, kpe, sw, w_kvb)


def _dec_row(b, j):
    return CTX_TILES + b * DEC_TILES_PER_BATCH + j


def _na_dec_kernel(q_ref, k_ref, v_ref, ck_ref, cv_ref, bias_ref, oin_ref, o_ref):
    del oin_ref
    scale = HEAD_DIM ** -0.5
    j = pl.program_id(1)
    w0 = jnp.clip(j * NA_ROWS_PER_STEP - NA_WIN_H // 2, 0, DEC_SEQ // GRID_W - NA_KEY_ROWS)
    start = pl.multiple_of(w0 * GRID_W, GRID_W)
    for h in range(NA_HEADS):
        sl = slice(h * HEAD_DIM, (h + 1) * HEAD_DIM)
        q = q_ref[:, sl]
        k = k_ref[pl.ds(start, NA_KEYS), sl]
        v = v_ref[pl.ds(start, NA_KEYS), sl]
        s_nb = _dot_nt(q, k) * scale + bias_ref[h]
        s_ctx = _dot_nt(q, ck_ref[:, sl]) * scale
        (p_nb, p_ctx), inv = _softmax_parts([s_nb, s_ctx])
        o_ref[:, sl] = ((_dot(p_nb, v) + _dot(p_ctx, cv_ref[:, sl])) * inv).astype(o_ref.dtype)


def _na_dec(l, na, ck, cv, bias, o):
    pat = lambda j: jnp.where(j == 0, 0, jnp.where(j == DEC_TILES_PER_BATCH - 1, 2, 1))
    return pl.pallas_call(
        _na_dec_kernel,
        grid=(DEC_BATCH, DEC_TILES_PER_BATCH),
        in_specs=[pl.BlockSpec((TM, NA_OUT), lambda b, j: (_dec_row(b, j), 0)),
                  pl.BlockSpec((DEC_SEQ, NA_OUT), lambda b, j: (N_CTX // DEC_SEQ + b, 1)),
                  pl.BlockSpec((DEC_SEQ, NA_OUT), lambda b, j: (N_CTX // DEC_SEQ + b, 2)),
                  pl.BlockSpec((None, None, PAST_LEN, NA_OUT), lambda b, j: (b, l, 0, 0)),
                  pl.BlockSpec((None, None, PAST_LEN, NA_OUT), lambda b, j: (b, l, 0, 0)),
                  pl.BlockSpec((None, None, NA_HEADS, TM, NA_KEYS), lambda b, j: (l, pat(j), 0, 0, 0)),
                  pl.BlockSpec(memory_space=pl.ANY)],
        out_specs=pl.BlockSpec((TM, NA_OUT), lambda b, j: (_dec_row(b, j), (MLA_OUT + SWA_OUT) // NA_OUT)),
        out_shape=jax.ShapeDtypeStruct((N_TOK, D_MODEL), BF16),
        input_output_aliases={6: 0},
        compiler_params=_cparams(("arbitrary", "arbitrary")),
        name="na_dec",
    )(na, na, na, ck, cv, bias, o)


KCAT_ROWS = 256


def _mla_dec_kernel(mq_ref, ckv_ref, kpe_ref, cckv_ref, ckpe_ref, wkvb_ref, oin_ref, o_ref, kcat):
    del oin_ref

    def assemble(c, carry):
        rows = pl.ds(pl.multiple_of(c * KCAT_ROWS, KCAT_ROWS), KCAT_ROWS)
        kcat[rows, :LANES] = ckv_ref[rows, :]
        kcat[rows, LANES:] = kpe_ref[rows, :]
        return carry

    first = pl.program_id(1) == 0
    lax.fori_loop(0, jnp.where(first, DEC_SEQ // KCAT_ROWS, 0), assemble, 0)

    def context(c, carry):
        kcat[DEC_SEQ:, :LANES] = cckv_ref[...].astype(BF16)
        kcat[DEC_SEQ:, LANES:] = jnp.concatenate(
            [ckpe_ref[...], jnp.zeros((PAST_LEN, LANES - MLA_ROPE_DIM), F32)], axis=1).astype(BF16)
        return carry

    lax.fori_loop(0, jnp.where(first, 1, 0), context, 0)
    _mla_heads(mq_ref, wkvb_ref, kcat, o_ref)


def _mla_dec(l, mq, ckv, kpe, cckv, ckpe, w_kvb, o):
    return pl.pallas_call(
        _mla_dec_kernel,
        grid=(DEC_BATCH, DEC_TILES_PER_BATCH),
        in_specs=[pl.BlockSpec((TM, MQ_W), lambda b, j: (_dec_row(b, j), 0)),
                  pl.BlockSpec((DEC_SEQ, MLA_KV_RANK), lambda b, j: (N_CTX // DEC_SEQ + b, 0)),
                  pl.BlockSpec((DEC_SEQ, LANES), lambda b, j: (N_CTX // DEC_SEQ + b, 0)),
                  pl.BlockSpec((None, None, PAST_LEN, MLA_KV_RANK), lambda b, j: (b, l, 0, 0)),
                  pl.BlockSpec((None, None, PAST_LEN, MLA_ROPE_DIM), lambda b, j: (b, l, 0, 0)),
                  pl.BlockSpec((None, MLA_KV_RANK, MLA_HEADS * 2 * HEAD_DIM), lambda b, j: (l, 0, 0)),
                  pl.BlockSpec(memory_space=pl.ANY)],
        out_specs=pl.BlockSpec((TM, MLA_OUT), lambda b, j: (_dec_row(b, j), 0)),
        out_shape=jax.ShapeDtypeStruct((N_TOK, D_MODEL), BF16),
        scratch_shapes=[pltpu.VMEM((DEC_SEQ + PAST_LEN, 2 * LANES), BF16)],
        input_output_aliases={6: 0},
        compiler_params=_cparams(("arbitrary", "arbitrary")),
        name="mla_dec",
    )(mq, ckv, kpe, cckv, ckpe, w_kvb, o)


def _swa_dec_kernel(l, sink_ref, q_ref, k_ref, v_ref, ck_ref, cv_ref, oin_ref, o_ref):
    del oin_ref
    scale = HEAD_DIM ** -0.5
    j = pl.program_id(1)
    start = pl.multiple_of(jnp.clip(j * TM - SWA_WINDOW, 0, DEC_SEQ - SWA_KEYS), SWA_WINDOW)
    qpos = j * TM + lax.broadcasted_iota(jnp.int32, (TM, SWA_KEYS), 0)
    kpos = start + lax.broadcasted_iota(jnp.int32, (TM, SWA_KEYS), 1)
    band = jnp.abs(qpos - kpos) <= SWA_WINDOW
    for h in range(SWA_HEADS):
        kh = h // (SWA_HEADS // SWA_KV_HEADS)
        sl = slice(kh * HEAD_DIM, (kh + 1) * HEAD_DIM)
        q = q_ref[:, h * HEAD_DIM:(h + 1) * HEAD_DIM]
        s_loc = jnp.where(band, _dot_nt(q, k_ref[pl.ds(start, SWA_KEYS), sl]) * scale, NEG)
        s_ctx = _dot_nt(q, ck_ref[:, sl]) * scale
        (p_loc, p_ctx), inv = _softmax_parts([s_loc, s_ctx], sink_ref[l, h])
        o_ref[:, h * HEAD_DIM:(h + 1) * HEAD_DIM] = (
            (_dot(p_loc, v_ref[pl.ds(start, SWA_KEYS), sl]) + _dot(p_ctx, cv_ref[:, sl])) * inv).astype(o_ref.dtype)


def _swa_dec(l, sink, sw, ck, cv, o):
    return pl.pallas_call(
        functools.partial(_swa_dec_kernel, l),
        grid=(DEC_BATCH, DEC_TILES_PER_BATCH),
        in_specs=[pl.BlockSpec(memory_space=pltpu.SMEM),
                  pl.BlockSpec((TM, SWA_OUT), lambda b, j: (_dec_row(b, j), 0)),
                  pl.BlockSpec((DEC_SEQ, LANES), lambda b, j: (N_CTX // DEC_SEQ + b, SWA_OUT // LANES)),
                  pl.BlockSpec((DEC_SEQ, LANES), lambda b, j: (N_CTX // DEC_SEQ + b, SWA_OUT // LANES + 1)),
                  pl.BlockSpec((None, None, PAST_LEN, LANES), lambda b, j: (b, l, 0, 0)),
                  pl.BlockSpec((None, None, PAST_LEN, LANES), lambda b, j: (b, l, 0, 0)),
                  pl.BlockSpec(memory_space=pl.ANY)],
        out_specs=pl.BlockSpec((TM, SWA_OUT), lambda b, j: (_dec_row(b, j), 1)),
        out_shape=jax.ShapeDtypeStruct((N_TOK, D_MODEL), BF16),
        input_output_aliases={6: 0},
        compiler_params=_cparams(("arbitrary", "arbitrary")),
        name="swa_dec",
    )(sink, sw, sw, sw, ck, cv, o)


def _route(sc, sel):
    rows = [sel[e:e + 1, :] for e in range(N_EXPERTS)]

    def beats(a, ia, b, ib):
        return (a > b) | ((a == b) & (ia < ib)) if ia < ib else (a > b)

    in_top = []
    gscore = []
    for g in range(N_EXPERT_GROUPS):
        mem = list(range(g * EXPERTS_PER_GROUP, (g + 1) * EXPERTS_PER_GROUP))
        acc = None
        for e in mem:
            rank = sum(beats(rows[o], o, rows[e], e).astype(jnp.int32) for o in mem if o != e)
            top = rank < 2
            in_top.append(top)
            term = jnp.where(top, rows[e], 0.0)
            acc = term if acc is None else acc + term
        gscore.append(acc)
    gates, chosen = [], []
    for g in range(N_EXPERT_GROUPS):
        lost = sum(beats(gscore[o], o, gscore[g], g).astype(jnp.int32) for o in range(N_EXPERT_GROUPS) if o != g)
        best = lost == 0
        for e in range(g * EXPERTS_PER_GROUP, (g + 1) * EXPERTS_PER_GROUP):
            pick = best & in_top[e]
            chosen.append(pick.astype(F32))
            gates.append(jnp.where(pick, sc[e:e + 1, :], 0.0))
    gate = jnp.concatenate(gates, axis=0)
    return gate / gate.sum(axis=0, keepdims=True), jnp.concatenate(chosen, axis=0)


def _post_kernel(x_ref, o_ref, mod_ref, wout_ref, g_ref, wr_ref, br_ref,
                 y_ref, h2_ref, rt_ref, rtm_ref, cnt_ref):
    attn = jnp.dot(o_ref[...].astype(BF16), wout_ref[...], preferred_element_type=F32)
    y = x_ref[...] + mod_ref[2:3, :] * attn
    y_ref[...] = y
    h2 = _rms(y, g_ref[...]) * (1.0 + mod_ref[4:5, :]) + mod_ref[3:4, :]
    h_hi = h2.astype(BF16)
    h_lo = (h2 - h_hi.astype(F32)).astype(BF16)
    w = wr_ref[...]
    w_hi = w.astype(BF16)
    w_lo = (w - w_hi.astype(F32)).astype(BF16)
    logits = (jnp.dot(h_hi, w_hi, preferred_element_type=F32) + jnp.dot(h_lo, w_hi, preferred_element_type=F32)
              + jnp.dot(h_hi, w_lo, preferred_element_type=F32))
    logits = logits.T[:N_EXPERTS, :]
    sc = 1.0 / (1.0 + jnp.exp(-logits))
    gate, chosen = _route(sc, sc + br_ref[...])
    h2_ref[...] = h_hi
    rt = jnp.concatenate([gate, chosen], axis=0)
    rt_ref[...] = rt
    rtm_ref[...] = jnp.concatenate([rt, jnp.zeros((LANES - 2 * N_EXPERTS, TM), F32)], axis=0).T
    cnt_ref[...] = jnp.broadcast_to(jnp.sum(chosen, axis=1, keepdims=True), (N_EXPERTS, LANES))


def _post(l, x, o, mods, w_out_bf, g_ffn, w_router_p, b_router):
    tile = lambda w: pl.BlockSpec((TM, w), lambda i: (i, 0))
    return pl.pallas_call(
        _post_kernel,
        grid=(N_TILES,),
        in_specs=[tile(D_MODEL), tile(D_MODEL),
                  pl.BlockSpec((None, None, N_MOD, D_MODEL), lambda i: (l, _cond_row(i), 0, 0)),
                  pl.BlockSpec((None, D_MODEL, D_MODEL), lambda i: (l, 0, 0)),
                  pl.BlockSpec((None, 1, D_MODEL), lambda i: (l, 0, 0)),
                  pl.BlockSpec((D_MODEL, LANES), lambda i: (0, 0)),
                  pl.BlockSpec((N_EXPERTS, 1), lambda i: (0, 0))],
        out_specs=[tile(D_MODEL), tile(D_MODEL), pl.BlockSpec((2 * N_EXPERTS, TM), lambda i: (0, i)),
                   tile(LANES), pl.BlockSpec((None, N_EXPERTS, LANES), lambda i: (i, 0, 0))],
        out_shape=[jax.ShapeDtypeStruct((N_TOK, D_MODEL), F32), jax.ShapeDtypeStruct((N_TOK, D_MODEL), BF16),
                   jax.ShapeDtypeStruct((2 * N_EXPERTS, N_TOK), F32),
                   jax.ShapeDtypeStruct((N_TOK, LANES), F32),
                   jax.ShapeDtypeStruct((N_TILES, N_EXPERTS, LANES), F32)],
        compiler_params=_cparams(("arbitrary",)),
        name="post_attn",
    )(x, o, mods, w_out_bf, g_ffn.reshape(DEPTH, 1, D_MODEL), w_router_p, b_router.reshape(N_EXPERTS, 1))


def _shr(x, bits):
    return lax.shift_right_logical(x, jnp.int32(bits))


TR_BITS = TR.bit_length() - 1
ALIGN_BITS = RUN_ALIGN.bit_length() - 1


def _plan_rows(cnt_ref, off_ref, npass_ref, seg_ref):
    def per_expert(e, row0):
        def per_tile(bb, r):
            off_ref[bb * N_EXPERTS + e] = r
            return r + (_shr(cnt_ref[bb * N_EXPERTS + e] + (RUN_ALIGN - 1), ALIGN_BITS) << ALIGN_BITS)

        rows_end = lax.fori_loop(0, N_TILES, per_tile, row0)
        n = _shr(rows_end - row0 + (TR - 1), TR_BITS)
        seg_ref[e] = _shr(row0, TR_BITS)
        seg_ref[N_EXPERTS + e] = n
        seg_ref[2 * N_EXPERTS + e] = rows_end
        seg_ref[3 * N_EXPERTS + e] = row0 + (n << TR_BITS)
        return row0 + (n << TR_BITS)

    end_row = lax.fori_loop(0, N_EXPERTS, per_expert, jnp.int32(0))
    seg_ref[4 * N_EXPERTS] = _shr(end_row, TR_BITS)

    def longest(bb, carry):
        m = lax.fori_loop(0, N_EXPERTS, lambda e, m: jnp.maximum(m, cnt_ref[bb * N_EXPERTS + e]), jnp.int32(0))
        npass_ref[bb] = sum((m > k * WIN).astype(jnp.int32) for k in range(-(-TM // WIN)))
        return carry

    lax.fori_loop(0, N_TILES, longest, 0)


def _dispatch_kernel(cnt_ref, h_ref, rt_ref, et_ref, xs_hbm, off_ref, npass_ref, seg_ref, zbuf, zeros, sem):
    b = pl.program_id(0)
    slot = b % 2

    @pl.when(b == 0)
    def _():
        _plan_rows(cnt_ref, off_ref, npass_ref, seg_ref)

    key = jnp.dot(et_ref[...], _run_keys(rt_ref[...], 1).astype(BF16), preferred_element_type=F32)
    slot_j = _window_slot((N_EXPERTS * WIN, TM), 0)

    def run_copies(bb, sl, p, act):
        def one_expert(e, queue):
            left = cnt_ref[bb * N_EXPERTS + e] - p * WIN
            row = off_ref[bb * N_EXPERTS + e] + p * WIN
            whole = row + WIN <= seg_ref[2 * N_EXPERTS + e]

            @pl.when(whole & (left > 0))
            def _():
                src = pl.multiple_of(e * WIN, RUN_ALIGN)
                act(pltpu.make_async_copy(zbuf.at[sl, pl.ds(src, WIN)],
                                          xs_hbm.at[pl.ds(pl.multiple_of(row, RUN_ALIGN), WIN)], sem), queue)

            def piece(k, c):
                src = pl.multiple_of(e * WIN + k * RUN_ALIGN, RUN_ALIGN)
                dst = pl.multiple_of(row + k * RUN_ALIGN, RUN_ALIGN)
                act(pltpu.make_async_copy(zbuf.at[sl, pl.ds(src, RUN_ALIGN)],
                                          xs_hbm.at[pl.ds(dst, RUN_ALIGN)], sem), queue)
                return c

            pieces = jnp.minimum(_shr(jnp.maximum(left, 0) + (RUN_ALIGN - 1), ALIGN_BITS), WIN // RUN_ALIGN)
            lax.fori_loop(0, jnp.where(whole, 0, pieces), piece, 0)

        def expert_pair(e2, carry):
            one_expert(2 * e2, 0)
            one_expert(2 * e2 + 1, 1)
            return carry

        lax.fori_loop(0, N_EXPERTS // 2, expert_pair, 0)

    def fill_and_send(p):
        pick = jnp.where(key == slot_j + jnp.asarray(p * WIN, F32), 1.0, 0.0).astype(BF16)
        zbuf[slot] = jnp.dot(pick, h_ref[...], preferred_element_type=F32).astype(BF16)
        run_copies(b, slot, p, lambda cp, queue: cp.start(priority=queue))

    @pl.when(b > 0)
    def _():
        run_copies(b - 1, 1 - slot, npass_ref[jnp.maximum(b - 1, 0)] - 1, lambda cp, queue: cp.wait())

    fill_and_send(0)

    def more(p, carry):
        run_copies(b, slot, p - 1, lambda cp, queue: cp.wait())
        fill_and_send(p)
        return carry

    lax.fori_loop(1, npass_ref[b], more, 0)

    @pl.when(b == N_TILES - 1)
    def _():
        run_copies(b, slot, npass_ref[b] - 1, lambda cp, queue: cp.wait())
        zeros[...] = jnp.zeros_like(zeros)

        def fill(act):
            def per_expert(e, carry):
                for k in range(TR // RUN_ALIGN):
                    row = pl.multiple_of(seg_ref[2 * N_EXPERTS + e] + k * RUN_ALIGN, RUN_ALIGN)

                    @pl.when(row < seg_ref[3 * N_EXPERTS + e])
                    def _():
                        act(pltpu.make_async_copy(zeros.at[pl.ds(0, RUN_ALIGN)], xs_hbm.at[pl.ds(row, RUN_ALIGN)], sem))
                return carry

            lax.fori_loop(0, N_EXPERTS, per_expert, 0)

            def per_tile(t, carry):
                act(pltpu.make_async_copy(zeros, xs_hbm.at[pl.ds(pl.multiple_of(t * TR, TR), TR)], sem))
                return carry

            lax.fori_loop(seg_ref[4 * N_EXPERTS], NT, per_tile, 0)

        fill(lambda cp: cp.start())
        fill(lambda cp: cp.wait())


def _dispatch(cnt, h2, rt, et):
    smem = pl.BlockSpec(memory_space=pltpu.SMEM)
    return pl.pallas_call(
        _dispatch_kernel,
        grid_spec=pltpu.PrefetchScalarGridSpec(
            num_scalar_prefetch=1,
            grid=(N_TILES,),
            in_specs=[pl.BlockSpec((TM, D_MODEL), lambda i, *_: (i, 0)),
                      pl.BlockSpec((2 * N_EXPERTS, TM), lambda i, *_: (0, i)),
                      pl.BlockSpec((N_EXPERTS * WIN, 2 * N_EXPERTS), lambda i, *_: (0, 0))],
            out_specs=[pl.BlockSpec(memory_space=pl.ANY), smem, smem, smem],
            scratch_shapes=[pltpu.VMEM((2, N_EXPERTS * WIN, D_MODEL), BF16), pltpu.VMEM((TR, D_MODEL), BF16),
                            pltpu.SemaphoreType.DMA(())]),
        out_shape=[jax.ShapeDtypeStruct((NT * TR, D_MODEL), BF16), jax.ShapeDtypeStruct((N_RUNS,), jnp.int32),
                   jax.ShapeDtypeStruct((N_TILES,), jnp.int32),
                   jax.ShapeDtypeStruct((4 * N_EXPERTS + 1,), jnp.int32)],
        compiler_params=_cparams(("arbitrary",)),
        name="dispatch",
    )(cnt, h2, rt, et)


CAST_ROWS = 128


def _cast_rows(src_ref, dst_ref, n):
    def body(c, carry):
        rows = pl.ds(pl.multiple_of(c * CAST_ROWS, CAST_ROWS), CAST_ROWS)
        dst_ref[rows, :] = src_ref[rows, :].astype(BF16)
        return carry

    lax.fori_loop(0, n, body, 0)


def _experts_kernel(seg_ref, xs_hbm, wg_ref, wu_ref, wd_ref, ys_hbm, wgb, wub, wdb, xbuf, ybuf, semx, semy):
    e = pl.program_id(0)
    t0 = seg_ref[e]
    n = seg_ref[N_EXPERTS + e]
    _cast_rows(wg_ref, wgb, D_MODEL // CAST_ROWS)
    _cast_rows(wu_ref, wub, D_MODEL // CAST_ROWS)
    _cast_rows(wd_ref, wdb, D_EXPERT // CAST_ROWS)

    def rows(k):
        return pl.ds(pl.multiple_of((t0 + k) * TR, TR), TR)

    def fetch(k, s):
        return pltpu.make_async_copy(xs_hbm.at[rows(k)], xbuf.at[s], semx.at[s])

    def put(k, s):
        return pltpu.make_async_copy(ybuf.at[s], ys_hbm.at[rows(k)], semy.at[s])

    @pl.when(n > 0)
    def _():
        fetch(0, 0).start()

    def tile(k, carry):
        s = k % 2

        @pl.when(k + 1 < n)
        def _():
            fetch(k + 1, 1 - s).start()

        fetch(k, s).wait()

        @pl.when(k >= 2)
        def _():
            put(k - 2, s).wait()

        x = xbuf[s]
        hg = jnp.dot(x, wgb[...], preferred_element_type=F32)
        hu = jnp.dot(x, wub[...], preferred_element_type=F32)
        a = hg * (1.0 / (1.0 + jnp.exp(-hg))) * hu
        ybuf[s] = jnp.dot(a.astype(BF16), wdb[...], preferred_element_type=F32).astype(BF16)
        put(k, s).start(priority=1)
        return carry

    lax.fori_loop(0, n, tile, 0)

    @pl.when(n >= 2)
    def _():
        put(n - 2, n % 2).wait()

    @pl.when(n >= 1)
    def _():
        put(n - 1, (n - 1) % 2).wait()

    @pl.when(e == N_EXPERTS - 1)
    def _():
        ybuf[0] = jnp.zeros((TR, D_MODEL), BF16)

        def fill(act):
            def per_tile(t, carry):
                act(pltpu.make_async_copy(ybuf.at[0], ys_hbm.at[pl.ds(pl.multiple_of(t * TR, TR), TR)], semy.at[0]))
                return carry

            lax.fori_loop(seg_ref[4 * N_EXPERTS], NT, per_tile, 0)

        fill(lambda cp: cp.start())
        fill(lambda cp: cp.wait())


def _experts(l, seg, xs, w_gate, w_up, w_down):
    wspec = lambda a, b: pl.BlockSpec((None, None, a, b), lambda e, seg: (l, e, 0, 0))
    tile_buf = pltpu.VMEM((2, TR, D_MODEL), BF16)
    return pl.pallas_call(
        _experts_kernel,
        grid_spec=pltpu.PrefetchScalarGridSpec(
            num_scalar_prefetch=1,
            grid=(N_EXPERTS,),
            in_specs=[pl.BlockSpec(memory_space=pl.ANY),
                      wspec(D_MODEL, D_EXPERT), wspec(D_MODEL, D_EXPERT), wspec(D_EXPERT, D_MODEL)],
            out_specs=pl.BlockSpec(memory_space=pl.ANY),
            scratch_shapes=[pltpu.VMEM((D_MODEL, D_EXPERT), BF16), pltpu.VMEM((D_MODEL, D_EXPERT), BF16),
                            pltpu.VMEM((D_EXPERT, D_MODEL), BF16), tile_buf, tile_buf,
                            pltpu.SemaphoreType.DMA((2,)), pltpu.SemaphoreType.DMA((2,))]),
        out_shape=jax.ShapeDtypeStruct((NT * TR, D_MODEL), BF16),
        compiler_params=_cparams(("arbitrary",)),
        name="experts",
    )(seg, xs, w_gate, w_up, w_down)


def _final_kernel(off_ref, npass_ref, y_ref, rtm_ref, ek_ref, eg_ref, ys_hbm, mod_ref, g_ref, op_ref, os_ref,
                  wbuf, sem):
    i = pl.program_id(0)
    moe = _combined_moe(off_ref, npass_ref, rtm_ref, ek_ref, eg_ref, ys_hbm, wbuf, sem)
    out = _rms(y_ref[...] + mod_ref[5:6, :] * moe, g_ref[...])

    @pl.when(i < CTX_TILES)
    def _():
        op_ref[...] = out

    @pl.when(i >= CTX_TILES)
    def _():
        os_ref[...] = out


def _final(off, npass, y, rtm, ek, eg, ys, mods, g_final):
    return pl.pallas_call(
        _final_kernel,
        grid_spec=pltpu.PrefetchScalarGridSpec(
            num_scalar_prefetch=2,
            grid=(N_TILES,),
            in_specs=_combine_specs() + [
                pl.BlockSpec((None, None, N_MOD, D_MODEL), lambda i, *_: (DEPTH - 1, _cond_row(i), 0, 0)),
                pl.BlockSpec((1, D_MODEL), lambda i, *_: (0, 0))],
            out_specs=[pl.BlockSpec((TM, D_MODEL), lambda i, *_: (jnp.minimum(i, CTX_TILES - 1), 0)),
                       pl.BlockSpec((TM, D_MODEL), lambda i, *_: (jnp.maximum(i - CTX_TILES, 0), 0))],
            scratch_shapes=_combine_scratch()),
        out_shape=[jax.ShapeDtypeStruct((N_CTX, D_MODEL), F32), jax.ShapeDtypeStruct((N_DEC, D_MODEL), F32)],
        compiler_params=_cparams(("arbitrary",)),
        name="final_norm",
    )(off, npass, y, rtm, ek, eg, ys, mods, g_final.reshape(1, D_MODEL))


def _prep_kernel(win_ref, wout_ref, wi_ref, wo_ref):
    w = win_ref[...]
    split = KPE_OFF + MLA_ROPE_DIM
    wi_ref[...] = jnp.concatenate([w[:, :split], jnp.zeros((CAST_ROWS, SW_OFF - split), F32), w[:, split:]],
                                  axis=1).astype(BF16)
    wo_ref[...] = wout_ref[...].astype(BF16)


def _prep_weights(w_in, w_out):
    n = D_MODEL // CAST_ROWS
    shift = NA_OUT // CAST_ROWS
    return pl.pallas_call(
        _prep_kernel,
        grid=(DEPTH, n),
        in_specs=[pl.BlockSpec((None, CAST_ROWS, IN_WIDTH), lambda l, j: (l, j, 0)),
                  pl.BlockSpec((None, CAST_ROWS, D_MODEL), lambda l, j: (l, (j + shift) % n, 0))],
        out_specs=[pl.BlockSpec((None, CAST_ROWS, IN_PAD), lambda l, j: (l, j, 0)),
                   pl.BlockSpec((None, CAST_ROWS, D_MODEL), lambda l, j: (l, j, 0))],
        out_shape=[jax.ShapeDtypeStruct((DEPTH, D_MODEL, IN_PAD), BF16),
                   jax.ShapeDtypeStruct((DEPTH, D_MODEL, D_MODEL), BF16)],
        compiler_params=_cparams(("arbitrary", "arbitrary")),
        name="prep_weights",
    )(w_in, w_out)


def _rope_tables(rot_dim):
    t = jnp.arange(DEC_SEQ, dtype=jnp.int32)
    row = (t // GRID_W).astype(F32)
    col = (t % GRID_W).astype(F32)
    per_axis = rot_dim // 2
    inv = ROPE_BASE ** (-jnp.arange(0, per_axis, 2, dtype=F32) / per_axis)
    ang = jnp.concatenate([row[:, None] * inv, col[:, None] * inv], axis=-1)
    cos, sin = jnp.cos(ang), jnp.sin(ang)
    zero = jnp.zeros_like(sin)
    rep = LANES // rot_dim
    tabs = [jnp.concatenate([cos, cos], -1), jnp.concatenate([zero, sin], -1), jnp.concatenate([-sin, zero], -1)]
    return jnp.stack([jnp.tile(a, (1, rep)) for a in tabs])


def kernel(x_prompt, x_sample, c, cache_na_k, cache_na_v, cache_mla_ckv, cache_mla_kpe, cache_swa_k, cache_swa_v,
           c_ctx, w_ada, b_ada, g_attn, w_in, g_mla_q, w_mla_qb, g_mla_kv, w_mla_kvb, na_rpb, swa_sink, w_out,
           g_ffn, w_router, b_router, w_gate, w_up, w_down, g_final):
    cond = jnp.concatenate([c_ctx[None], c, jnp.zeros((COND_ROWS - 1 - DEC_BATCH, D_MODEL), F32)], axis=0)
    mods = _ada(cond, w_ada, b_ada).reshape(DEPTH, COND_ROWS, N_MOD, D_MODEL)
    bias = _na_bias(na_rpb)
    t_mla = _rope_tables(MLA_ROPE_DIM)
    t_swa = _rope_tables(HEAD_DIM)

    w_in_p, w_out_p = _prep_weights(w_in, w_out)
    wq = w_mla_qb.reshape(DEPTH, MLA_Q_RANK, MLA_HEADS, MLA_QK_DIM)
    w_rope = jnp.pad(wq[..., MLA_NOPE_DIM:], ((0, 0), (0, 0), (0, 0), (0, LANES - MLA_ROPE_DIM)))
    w_qb_p = jnp.concatenate([wq[..., :MLA_NOPE_DIM].reshape(DEPTH, MLA_Q_RANK, MQ_NOPE),
                              w_rope.reshape(DEPTH, MLA_Q_RANK, MLA_HEADS * LANES)], axis=-1).astype(BF16)
    w_router_p = jnp.pad(w_router, ((0, 0), (0, LANES - N_EXPERTS)))
    et, ek, eg = _spread_consts()

    c_na_k = cache_na_k.reshape(DEC_BATCH, DEPTH, PAST_LEN, NA_OUT)
    c_na_v = cache_na_v.reshape(DEC_BATCH, DEPTH, PAST_LEN, NA_OUT)
    c_sw_k = cache_swa_k.reshape(DEC_BATCH, DEPTH, PAST_LEN, LANES)
    c_sw_v = cache_swa_v.reshape(DEC_BATCH, DEPTH, PAST_LEN, LANES)

    y = ys = rtm = off = npass = None
    caches = [jnp.zeros((BATCH, DEPTH, SEQ, w), F32) for w in CACHE_WIDTHS]
    for l in range(DEPTH):
        if l == 0:
            srcs = [x_prompt.reshape(N_CTX, D_MODEL), x_sample.reshape(N_DEC, D_MODEL)]
        else:
            srcs = [off, npass, y, rtm, ek, eg, ys]
        x, na, mq, ckv, kpe, sw, *caches = _pre(l == 0, l, srcs, caches, mods, g_attn, w_in_p, g_mla_q, w_qb_p,
                                                g_mla_kv, t_mla, t_swa)
        o = _ctx_attn(l, swa_sink, na, mq, ckv, kpe, sw, w_mla_kvb)
        o = _mla_dec(l, mq, ckv, kpe, cache_mla_ckv, cache_mla_kpe, w_mla_kvb, o)
        o = _swa_dec(l, swa_sink, sw, c_sw_k, c_sw_v, o)
        o = _na_dec(l, na, c_na_k, c_na_v, bias, o)
        y, h2, rt, rtm, cnt = _post(l, x, o, mods, w_out_p, g_ffn, w_router_p, b_router)
        xs, off, npass, seg = _dispatch(cnt[:, :, 0].astype(jnp.int32).reshape(-1), h2, rt, et)
        ys = _experts(l, seg, xs, w_gate, w_up, w_down)
    y_prompt, y_sample = _final(off, npass, y, rtm, ek, eg, ys, mods, g_final)

    heads = lambda a, n: a.reshape(BATCH, DEPTH, SEQ, n, HEAD_DIM)
    return (y_prompt.reshape(BATCH, SEQ, D_MODEL), y_sample.reshape(DEC_BATCH, DEC_SEQ, D_MODEL),
            heads(caches[0], NA_HEADS), heads(caches[1], NA_HEADS), caches[2], caches[3],
            heads(caches[4], SWA_KV_HEADS), heads(caches[5], SWA_KV_HEADS))
```

```python
import functools

import jax
import jax.numpy as jnp
from jax import lax
from jax.experimental import pallas as pl
from jax.experimental.pallas import tpu as pltpu

D_MODEL = 1024
BATCH = 16
SEQ = 256
DEPTH = 4
DEC_BATCH = 2
DEC_SEQ = 2048
PAST_LEN = 256
GRID_W = 64
HEAD_DIM = 64
NA_HEADS = 4
NA_WIN_H = 8
NA_WIN_W = 16
MLA_HEADS = 6
MLA_Q_RANK = 256
MLA_KV_RANK = 128
MLA_NOPE_DIM = 64
MLA_ROPE_DIM = 32
MLA_V_DIM = 64
MLA_QK_DIM = MLA_NOPE_DIM + MLA_ROPE_DIM
SWA_HEADS = 6
SWA_KV_HEADS = 2
SWA_WINDOW = 128
ROPE_BASE = 10000.0
N_EXPERTS = 16
N_EXPERT_GROUPS = 4
EXPERTS_PER_GROUP = 4
D_EXPERT = 512
RMS_EPS = 1e-6
N_MOD = 6

NA_IN = 3 * NA_HEADS * HEAD_DIM
MLA_IN = MLA_Q_RANK + MLA_KV_RANK + MLA_ROPE_DIM
SWA_IN = (SWA_HEADS + 2 * SWA_KV_HEADS) * HEAD_DIM
IN_WIDTH = NA_IN + MLA_IN + SWA_IN
NA_OUT = NA_HEADS * HEAD_DIM
MLA_OUT = MLA_HEADS * MLA_V_DIM
SWA_OUT = SWA_HEADS * HEAD_DIM

LANES = 128
N_CTX = BATCH * SEQ
N_DEC = DEC_BATCH * DEC_SEQ
N_TOK = N_CTX + N_DEC
TM = 256
N_TILES = N_TOK // TM
CTX_TILES = N_CTX // TM
DEC_TILES_PER_BATCH = DEC_SEQ // TM
COND_ROWS = 8
KPE_OFF = NA_IN + MLA_Q_RANK + MLA_KV_RANK
SW_OFF = KPE_OFF + LANES
IN_PAD = SW_OFF + SWA_IN
MQ_NOPE = MLA_HEADS * MLA_NOPE_DIM
MQ_W = MQ_NOPE + MLA_HEADS * LANES
MLA_SCALE = MLA_QK_DIM ** -0.5
MLA_GROUP = 3
NA_ROWS_PER_STEP = TM // GRID_W
NA_KEY_ROWS = 12
NA_KEYS = NA_KEY_ROWS * GRID_W
SWA_KEYS = 512
NEG = -1e30
TR = 256
N_ASSIGN = 2 * N_TOK
RUN_ALIGN = 16
WIN = 80
N_RUNS = N_TILES * N_EXPERTS
NT = -(-(N_ASSIGN + N_RUNS * (RUN_ALIGN - 1) + N_EXPERTS * (TR - 1) + WIN) // TR)

F32 = jnp.float32
BF16 = jnp.bfloat16
VMEM_LIMIT = 56 * 1024 * 1024


def _cparams(sem):
    return pltpu.CompilerParams(dimension_semantics=sem, vmem_limit_bytes=VMEM_LIMIT)


def _cond_row(i):
    return jnp.where(i < CTX_TILES, 0, 1 + (i - CTX_TILES) // DEC_TILES_PER_BATCH)


def _rope_blk(i):
    return jnp.where(i < CTX_TILES, 0, (i - CTX_TILES) % DEC_TILES_PER_BATCH)


def _rms(x, g):
    ms = jnp.mean(x * x, axis=-1, keepdims=True)
    return x * lax.rsqrt(ms + RMS_EPS) * g


def _dot(a, b):
    return jnp.dot(a.astype(BF16), b.astype(BF16), preferred_element_type=F32)


def _dot_nt(a, b):
    return lax.dot_general(a.astype(BF16), b.astype(BF16), (((1,), (1,)), ((), ())),
                           preferred_element_type=F32)


def _ada_kernel(c_ref, w_ref, b_ref, o_ref):
    c = c_ref[...]
    s = c * (1.0 / (1.0 + jnp.exp(-c)))
    o_ref[...] = _dot(s, w_ref[...]) + b_ref[...]


def _ada(cond, w_ada, b_ada):
    tn = 1536
    n = N_MOD * D_MODEL
    return pl.pallas_call(
        _ada_kernel,
        grid=(DEPTH, n // tn),
        in_specs=[pl.BlockSpec((COND_ROWS, D_MODEL), lambda l, j: (0, 0)),
                  pl.BlockSpec((None, D_MODEL, tn), lambda l, j: (l, 0, j)),
                  pl.BlockSpec((None, 1, tn), lambda l, j: (l, 0, j))],
        out_specs=pl.BlockSpec((None, COND_ROWS, tn), lambda l, j: (l, 0, j)),
        out_shape=jax.ShapeDtypeStruct((DEPTH, COND_ROWS, n), F32),
        compiler_params=_cparams(("arbitrary", "arbitrary")),
        name="ada_mod",
    )(cond, w_ada, b_ada.reshape(DEPTH, 1, n))


def _bias_kernel(rpb_ref, o_ref):
    g = pl.program_id(0)
    base = g * ((2 * NA_WIN_H - 1) * (2 * NA_WIN_W - 1))
    qc = lax.broadcasted_iota(jnp.int32, (GRID_W, GRID_W), 0)
    kc = lax.broadcasted_iota(jnp.int32, (GRID_W, GRID_W), 1)
    dc = jnp.clip(kc - qc + (NA_WIN_W - 1), 0, 2 * NA_WIN_W - 2)
    cs = jnp.clip(qc - NA_WIN_W // 2, 0, GRID_W - NA_WIN_W)
    col_ok = (kc >= cs) & (kc < cs + NA_WIN_W)
    neg = jnp.full((GRID_W, GRID_W), NEG, F32)
    tabs = []
    for a in range(2 * NA_WIN_H - 1):
        t = jnp.zeros((GRID_W, GRID_W), F32)
        for b in range(2 * NA_WIN_W - 1):
            t = jnp.where(dc == b, rpb_ref[base + a * (2 * NA_WIN_W - 1) + b], t)
        tabs.append(jnp.where(col_ok, t, NEG))
    for p in range(3):
        for qi in range(NA_ROWS_PER_STEP):
            for kj in range(NA_KEY_ROWS):
                if p == 0:
                    ok, dr = kj < NA_WIN_H, kj - qi + 7
                elif p == 1:
                    ok, dr = qi <= kj < qi + NA_WIN_H, kj - qi + 3
                else:
                    ok, dr = kj >= NA_KEY_ROWS - NA_WIN_H, kj - qi - 1
                blk = tabs[dr] if ok else neg
                o_ref[p, qi * GRID_W:(qi + 1) * GRID_W, kj * GRID_W:(kj + 1) * GRID_W] = blk


def _na_bias(na_rpb):
    return pl.pallas_call(
        _bias_kernel,
        grid=(DEPTH * NA_HEADS,),
        in_specs=[pl.BlockSpec(memory_space=pltpu.SMEM)],
        out_specs=pl.BlockSpec((None, 3, None, TM, NA_KEYS),
                               lambda g: (g // NA_HEADS, 0, g % NA_HEADS, 0, 0)),
        out_shape=jax.ShapeDtypeStruct((DEPTH, 3, NA_HEADS, TM, NA_KEYS), F32),
        compiler_params=_cparams(("arbitrary",)),
        name="na_bias",
    )(na_rpb.reshape(-1))


def _rope128(x, t_ref, half):
    return (x * t_ref[0] + pltpu.roll(x, half, 1) * t_ref[1]
            + pltpu.roll(x, LANES - half, 1) * t_ref[2])


def _spread_consts():
    slot_e = jnp.arange(N_EXPERTS * WIN, dtype=jnp.int32) // WIN
    idx32 = jnp.arange(2 * N_EXPERTS, dtype=jnp.int32)
    idx128 = jnp.arange(LANES, dtype=jnp.int32)
    et = (idx32[None, :] == slot_e[:, None] + N_EXPERTS).astype(BF16)
    ek = (idx128[:, None] == slot_e[None, :] + N_EXPERTS).astype(BF16)
    eg = (idx128[:, None] == slot_e[None, :]).astype(BF16)
    return et, ek, eg


def _window_slot(shape, axis):
    r = lax.broadcasted_iota(jnp.int32, shape, axis).astype(F32)
    return r - WIN * jnp.floor((r + 0.5) * (1.0 / WIN))


def _run_keys(chosen, token_axis):
    a = lax.broadcasted_iota(jnp.int32, (TM, TM), 0)
    b = lax.broadcasted_iota(jnp.int32, (TM, TM), 1)
    if token_axis == 1:
        rank = jnp.dot(chosen.astype(BF16), (a < b).astype(BF16), preferred_element_type=F32)
    else:
        rank = jnp.dot((b < a).astype(BF16), chosen.astype(BF16), preferred_element_type=F32)
    return jnp.where(chosen > 0.5, rank, -1.0)


def _windows_start(i, slot, p, off_ref, ys_hbm, wbuf, sem):
    for e in range(N_EXPERTS):
        row = pl.multiple_of(off_ref[i * N_EXPERTS + e] + p * WIN, RUN_ALIGN)
        pltpu.make_async_copy(ys_hbm.at[pl.ds(row, WIN)], wbuf.at[slot, pl.ds(e * WIN, WIN)],
                              sem.at[slot]).start(priority=e % 2)


def _windows_wait(slot, ys_hbm, wbuf, sem):
    pltpu.make_async_copy(ys_hbm.at[pl.ds(0, N_EXPERTS * WIN)], wbuf.at[slot], sem.at[slot]).wait()


def _combined_moe(off_ref, npass_ref, rtm_ref, ek_ref, eg_ref, ys_hbm, wbuf, sem):
    i = pl.program_id(0)
    slot = i % 2

    @pl.when(i == 0)
    def _():
        _windows_start(0, 0, 0, off_ref, ys_hbm, wbuf, sem)

    @pl.when(i + 1 < N_TILES)
    def _():
        _windows_start(i + 1, 1 - slot, 0, off_ref, ys_hbm, wbuf, sem)

    r = rtm_ref[...]
    key = jnp.dot(_run_keys(r, 0).astype(BF16), ek_ref[...], preferred_element_type=F32)
    gate = jnp.dot(r.astype(BF16), eg_ref[...], preferred_element_type=F32)
    slot_j = _window_slot((TM, N_EXPERTS * WIN), 1)

    def contrib(p):
        g = jnp.where(key == slot_j + jnp.asarray(p * WIN, F32), gate, 0.0).astype(BF16)
        return jnp.dot(g, wbuf[slot], preferred_element_type=F32)

    _windows_wait(slot, ys_hbm, wbuf, sem)
    acc = contrib(0)

    def extra(p, acc):
        _windows_start(i, slot, p, off_ref, ys_hbm, wbuf, sem)
        _windows_wait(slot, ys_hbm, wbuf, sem)
        return acc + contrib(p)

    return lax.fori_loop(1, npass_ref[i], extra, acc)


def _pre_kernel(first, *refs):
    i = pl.program_id(0)
    if first:
        (xp_ref, xs_ref, mod_ref, g_ref, win_ref, gq_ref, wqb_ref, gkv_ref, tm_ref, ts_ref, _, _, _, _, _, _,
         xo_ref, na_ref, mq_ref, ckv_ref, kpe_ref, sw_ref, *cache_refs) = refs
        x = jnp.where(i < CTX_TILES, xp_ref[...], xs_ref[...])
    else:
        (off_ref, npass_ref, y_ref, rtm_ref, ek_ref, eg_ref, ys_hbm, modp_ref, mod_ref, g_ref, win_ref, gq_ref,
         wqb_ref, gkv_ref, tm_ref, ts_ref, _, _, _, _, _, _,
         xo_ref, na_ref, mq_ref, ckv_ref, kpe_ref, sw_ref, *rest) = refs
        *cache_refs, wbuf, sem = rest
        moe = _combined_moe(off_ref, npass_ref, rtm_ref, ek_ref, eg_ref, ys_hbm, wbuf, sem)
        x = y_ref[...] + modp_ref[5:6, :] * moe
    xo_ref[...] = x
    cnk_ref, cnv_ref, cckv_ref, ckpe_ref, csk_ref, csv_ref = cache_refs

    h = _rms(x, g_ref[...]) * (1.0 + mod_ref[1:2, :]) + mod_ref[0:1, :]
    z = jnp.dot(h.astype(BF16), win_ref[...], preferred_element_type=F32)
    na_ref[...] = z[:, :NA_IN].astype(BF16)
    cq = _rms(z[:, NA_IN:NA_IN + MLA_Q_RANK], gq_ref[...])
    ckv = _rms(z[:, NA_IN + MLA_Q_RANK:KPE_OFF], gkv_ref[...])
    ckv_ref[...] = ckv.astype(BF16)
    mq = jnp.dot(cq.astype(BF16), wqb_ref[...], preferred_element_type=F32)
    kpe = z[:, KPE_OFF:SW_OFF]
    sw = z[:, SW_OFF:IN_PAD]
    mq_ref[:, :MQ_NOPE] = mq[:, :MQ_NOPE].astype(BF16)
    sw_ref[:, SWA_OUT + LANES:] = sw[:, SWA_OUT + LANES:].astype(BF16)

    @pl.when(i < CTX_TILES)
    def _():
        mq_ref[:, MQ_NOPE:] = (mq[:, MQ_NOPE:] * MLA_SCALE).astype(BF16)
        kpe_ref[...] = kpe.astype(BF16)
        sw_ref[:, :SWA_OUT + LANES] = sw[:, :SWA_OUT + LANES].astype(BF16)
        cnk_ref[...] = z[:, NA_OUT:2 * NA_OUT]
        cnv_ref[...] = z[:, 2 * NA_OUT:NA_IN]
        cckv_ref[...] = ckv
        ckpe_ref[...] = kpe[:, :MLA_ROPE_DIM]
        csk_ref[...] = sw[:, SWA_OUT:SWA_OUT + LANES]
        csv_ref[...] = sw[:, SWA_OUT + LANES:]

    @pl.when(i >= CTX_TILES)
    def _():
        for c in range(MQ_NOPE // LANES, MQ_W // LANES):
            roped = _rope128(mq[:, c * LANES:(c + 1) * LANES], tm_ref, MLA_ROPE_DIM // 2)
            mq_ref[:, c * LANES:(c + 1) * LANES] = (roped * MLA_SCALE).astype(BF16)
        kpe_ref[...] = _rope128(kpe, tm_ref, MLA_ROPE_DIM // 2).astype(BF16)
        for c in range((SWA_OUT + LANES) // LANES):
            sw_ref[:, c * LANES:(c + 1) * LANES] = _rope128(sw[:, c * LANES:(c + 1) * LANES], ts_ref,
                                                            HEAD_DIM // 2).astype(BF16)


def _combine_scratch():
    return [pltpu.VMEM((2, N_EXPERTS * WIN, D_MODEL), BF16), pltpu.SemaphoreType.DMA((2,))]


def _combine_specs():
    spread = pl.BlockSpec((LANES, N_EXPERTS * WIN), lambda i, *_: (0, 0))
    return [pl.BlockSpec((TM, D_MODEL), lambda i, *_: (i, 0)),
            pl.BlockSpec((TM, LANES), lambda i, *_: (i, 0)),
            spread, spread, pl.BlockSpec(memory_space=pl.ANY)]


CACHE_WIDTHS = (NA_OUT, NA_OUT, MLA_KV_RANK, MLA_ROPE_DIM, LANES, LANES)


def _pre(first, l, xs, caches, mods, g_attn, w_in_p, g_mla_q, w_qb_p, g_mla_kv, t_mla, t_swa):
    tile = lambda w: pl.BlockSpec((TM, w), lambda i, *_: (i, 0))
    ctx_tile = lambda w: pl.BlockSpec((TM, w), lambda i, *_: (jnp.minimum(i, CTX_TILES - 1), 0))
    slab = lambda w: pl.BlockSpec((None, None, SEQ, w), lambda i, *_: (jnp.minimum(i, BATCH - 1), l, 0, 0))
    mod_spec = lambda ll: pl.BlockSpec((None, None, N_MOD, D_MODEL), lambda i, *_: (ll, _cond_row(i), 0, 0))
    vec = lambda w: pl.BlockSpec((None, 1, w), lambda i, *_: (l, 0, 0))
    if first:
        in_specs = [ctx_tile(D_MODEL), pl.BlockSpec((TM, D_MODEL), lambda i: (jnp.maximum(i - CTX_TILES, 0), 0))]
    else:
        in_specs = _combine_specs() + [mod_spec(l - 1)]
    in_specs += [mod_spec(l), vec(D_MODEL),
                 pl.BlockSpec((None, D_MODEL, IN_PAD), lambda i, *_: (l, 0, 0)),
                 vec(MLA_Q_RANK),
                 pl.BlockSpec((None, MLA_Q_RANK, MQ_W), lambda i, *_: (l, 0, 0)),
                 vec(MLA_KV_RANK),
                 pl.BlockSpec((3, TM, LANES), lambda i, *_: (0, _rope_blk(i), 0)),
                 pl.BlockSpec((3, TM, LANES), lambda i, *_: (0, _rope_blk(i), 0))]
    in_specs += [pl.BlockSpec(memory_space=pl.ANY)] * len(CACHE_WIDTHS)
    widths = [D_MODEL, NA_IN, MQ_W, MLA_KV_RANK, LANES, SWA_IN]
    args = list(xs) + ([mods] if not first else []) + [
        mods, g_attn.reshape(DEPTH, 1, D_MODEL), w_in_p, g_mla_q.reshape(DEPTH, 1, MLA_Q_RANK), w_qb_p,
        g_mla_kv.reshape(DEPTH, 1, MLA_KV_RANK), t_mla, t_swa]
    aliases = {len(args) + n: len(widths) + n for n in range(len(CACHE_WIDTHS))}
    return pl.pallas_call(
        functools.partial(_pre_kernel, first),
        grid_spec=pltpu.PrefetchScalarGridSpec(
            num_scalar_prefetch=0 if first else 2,
            grid=(N_TILES,),
            in_specs=in_specs,
            out_specs=[tile(w) for w in widths] + [slab(w) for w in CACHE_WIDTHS],
            scratch_shapes=[] if first else _combine_scratch()),
        out_shape=([jax.ShapeDtypeStruct((N_TOK, w), BF16 if n else F32) for n, w in enumerate(widths)]
                   + [jax.ShapeDtypeStruct((BATCH, DEPTH, SEQ, w), F32) for w in CACHE_WIDTHS]),
        input_output_aliases=aliases,
        compiler_params=_cparams(("arbitrary",)),
        name="pre_attn",
    )(*args, *caches)


def _softmax_parts(parts, sink=None):
    m = parts[0].max(axis=-1, keepdims=True)
    for s in parts[1:]:
        m = jnp.maximum(m, s.max(axis=-1, keepdims=True))
    if sink is not None:
        m = jnp.maximum(m, sink)
    ps = [jnp.exp(s - m) for s in parts]
    den = ps[0].sum(axis=-1, keepdims=True)
    for p in ps[1:]:
        den = den + p.sum(axis=-1, keepdims=True)
    if sink is not None:
        den = den + jnp.exp(sink - m)
    return ps, 1.0 / den


def _mla_heads(mq_ref, wkvb_ref, kcat, o_ref):
    rows = mq_ref.shape[0]
    for g in range(MLA_HEADS // MLA_GROUP):
        heads = range(g * MLA_GROUP, (g + 1) * MLA_GROUP)
        qs = []
        for h in heads:
            wk = wkvb_ref[:, h * 2 * HEAD_DIM:h * 2 * HEAD_DIM + MLA_NOPE_DIM]
            qa = _dot_nt(mq_ref[:, h * MLA_NOPE_DIM:(h + 1) * MLA_NOPE_DIM], wk) * MLA_SCALE
            qr = mq_ref[:, MQ_NOPE + h * LANES:MQ_NOPE + (h + 1) * LANES]
            qs.append(jnp.concatenate([qa.astype(BF16), qr], axis=1))
        s = _dot_nt(jnp.concatenate(qs, axis=0), kcat[...])
        p = jnp.exp(s - s.max(axis=-1, keepdims=True))
        inv = 1.0 / p.sum(axis=-1, keepdims=True)
        lat = _dot(p, kcat[:, :LANES]) * inv
        for n, h in enumerate(heads):
            wv = wkvb_ref[:, h * 2 * HEAD_DIM + MLA_NOPE_DIM:(h + 1) * 2 * HEAD_DIM]
            o_ref[:, h * MLA_V_DIM:(h + 1) * MLA_V_DIM] = _dot(lat[n * rows:(n + 1) * rows], wv).astype(o_ref.dtype)


def _ctx_attn_kernel(l, sink_ref, na_ref, mq_ref, ckv_ref, kpe_ref, sw_ref, wkvb_ref, oin_ref, o_ref):
    del oin_ref
    scale = HEAD_DIM ** -0.5
    _mla_heads(mq_ref, wkvb_ref, jnp.concatenate([ckv_ref[...], kpe_ref[...]], axis=1), o_ref)
    for h in range(SWA_HEADS):
        kh = h // (SWA_HEADS // SWA_KV_HEADS)
        q = sw_ref[:, h * HEAD_DIM:(h + 1) * HEAD_DIM]
        k = sw_ref[:, SWA_OUT + kh * HEAD_DIM:SWA_OUT + (kh + 1) * HEAD_DIM]
        v = sw_ref[:, SWA_OUT + LANES + kh * HEAD_DIM:SWA_OUT + LANES + (kh + 1) * HEAD_DIM]
        (p,), inv = _softmax_parts([_dot_nt(q, k) * scale], sink_ref[l, h])
        o_ref[:, MLA_OUT + h * HEAD_DIM:MLA_OUT + (h + 1) * HEAD_DIM] = (_dot(p, v) * inv).astype(o_ref.dtype)
    for h in range(NA_HEADS):
        q = na_ref[:, h * HEAD_DIM:(h + 1) * HEAD_DIM]
        k = na_ref[:, NA_OUT + h * HEAD_DIM:NA_OUT + (h + 1) * HEAD_DIM]
        v = na_ref[:, 2 * NA_OUT + h * HEAD_DIM:2 * NA_OUT + (h + 1) * HEAD_DIM]
        (p,), inv = _softmax_parts([_dot_nt(q, k) * scale])
        o_ref[:, MLA_OUT + SWA_OUT + h * HEAD_DIM:MLA_OUT + SWA_OUT + (h + 1) * HEAD_DIM] = (
            _dot(p, v) * inv).astype(o_ref.dtype)


def _ctx_attn(l, sink, na, mq, ckv, kpe, sw, w_kvb, o):
    tile = lambda w: pl.BlockSpec((SEQ, w), lambda b: (b, 0))
    return pl.pallas_call(
        functools.partial(_ctx_attn_kernel, l),
        grid=(BATCH,),
        in_specs=[pl.BlockSpec(memory_space=pltpu.SMEM), tile(NA_IN), tile(MQ_W), tile(MLA_KV_RANK), tile(LANES),
                  tile(SWA_IN), pl.BlockSpec((None, MLA_KV_RANK, MLA_HEADS * 2 * HEAD_DIM), lambda b: (l, 0, 0)),
                  pl.BlockSpec(memory_space=pl.ANY)],
        out_specs=tile(D_MODEL),
        out_shape=jax.ShapeDtypeStruct((N_TOK, D_MODEL), BF16),
        input_output_aliases={7: 0},
        compiler_params=_cparams(("arbitrary",)),
        name="ctx_attn",
    )(sink, na, mq, ckv, kpe, sw, w_kvb, o)


def _dec_row(b, j):
    return CTX_TILES + b * DEC_TILES_PER_BATCH + j


def _na_dec_kernel(q_ref, k_ref, v_ref, ck_ref, cv_ref, bias_ref, oin_ref, o_ref):
    del oin_ref
    scale = HEAD_DIM ** -0.5
    j = pl.program_id(1)
    w0 = jnp.clip(j * NA_ROWS_PER_STEP - NA_WIN_H // 2, 0, DEC_SEQ // GRID_W - NA_KEY_ROWS)
    start = pl.multiple_of(w0 * GRID_W, GRID_W)
    for h in range(NA_HEADS):
        sl = slice(h * HEAD_DIM, (h + 1) * HEAD_DIM)
        q = q_ref[:, sl]
        k = k_ref[pl.ds(start, NA_KEYS), sl]
        v = v_ref[pl.ds(start, NA_KEYS), sl]
        s_nb = _dot_nt(q, k) * scale + bias_ref[h]
        s_ctx = _dot_nt(q, ck_ref[:, sl]) * scale
        (p_nb, p_ctx), inv = _softmax_parts([s_nb, s_ctx])
        o_ref[:, sl] = ((_dot(p_nb, v) + _dot(p_ctx, cv_ref[:, sl])) * inv).astype(o_ref.dtype)


def _na_dec(l, na, ck, cv, bias, o):
    pat = lambda j: jnp.where(j == 0, 0, jnp.where(j == DEC_TILES_PER_BATCH - 1, 2, 1))
    return pl.pallas_call(
        _na_dec_kernel,
        grid=(DEC_BATCH, DEC_TILES_PER_BATCH),
        in_specs=[pl.BlockSpec((TM, NA_OUT), lambda b, j: (_dec_row(b, j), 0)),
                  pl.BlockSpec((DEC_SEQ, NA_OUT), lambda b, j: (N_CTX // DEC_SEQ + b, 1)),
                  pl.BlockSpec((DEC_SEQ, NA_OUT), lambda b, j: (N_CTX // DEC_SEQ + b, 2)),
                  pl.BlockSpec((None, None, PAST_LEN, NA_OUT), lambda b, j: (b, l, 0, 0)),
                  pl.BlockSpec((None, None, PAST_LEN, NA_OUT), lambda b, j: (b, l, 0, 0)),
                  pl.BlockSpec((None, None, NA_HEADS, TM, NA_KEYS), lambda b, j: (l, pat(j), 0, 0, 0)),
                  pl.BlockSpec(memory_space=pl.ANY)],
        out_specs=pl.BlockSpec((TM, NA_OUT), lambda b, j: (_dec_row(b, j), (MLA_OUT + SWA_OUT) // NA_OUT)),
        out_shape=jax.ShapeDtypeStruct((N_TOK, D_MODEL), BF16),
        input_output_aliases={6: 0},
        compiler_params=_cparams(("arbitrary", "arbitrary")),
        name="na_dec",
    )(na, na, na, ck, cv, bias, o)


KCAT_ROWS = 256


def _mla_dec_kernel(mq_ref, ckv_ref, kpe_ref, cckv_ref, ckpe_ref, wkvb_ref, oin_ref, o_ref, kcat):
    del oin_ref

    def assemble(c, carry):
        rows = pl.ds(pl.multiple_of(c * KCAT_ROWS, KCAT_ROWS), KCAT_ROWS)
        kcat[rows, :LANES] = ckv_ref[rows, :]
        kcat[rows, LANES:] = kpe_ref[rows, :]
        return carry

    first = pl.program_id(1) == 0
    lax.fori_loop(0, jnp.where(first, DEC_SEQ // KCAT_ROWS, 0), assemble, 0)

    def context(c, carry):
        kcat[DEC_SEQ:, :LANES] = cckv_ref[...].astype(BF16)
        kcat[DEC_SEQ:, LANES:] = jnp.concatenate(
            [ckpe_ref[...], jnp.zeros((PAST_LEN, LANES - MLA_ROPE_DIM), F32)], axis=1).astype(BF16)
        return carry

    lax.fori_loop(0, jnp.where(first, 1, 0), context, 0)
    _mla_heads(mq_ref, wkvb_ref, kcat, o_ref)


def _mla_dec(l, mq, ckv, kpe, cckv, ckpe, w_kvb, o):
    return pl.pallas_call(
        _mla_dec_kernel,
        grid=(DEC_BATCH, DEC_TILES_PER_BATCH),
        in_specs=[pl.BlockSpec((TM, MQ_W), lambda b, j: (_dec_row(b, j), 0)),
                  pl.BlockSpec((DEC_SEQ, MLA_KV_RANK), lambda b, j: (N_CTX // DEC_SEQ + b, 0)),
                  pl.BlockSpec((DEC_SEQ, LANES), lambda b, j: (N_CTX // DEC_SEQ + b, 0)),
                  pl.BlockSpec((None, None, PAST_LEN, MLA_KV_RANK), lambda b, j: (b, l, 0, 0)),
                  pl.BlockSpec((None, None, PAST_LEN, MLA_ROPE_DIM), lambda b, j: (b, l, 0, 0)),
                  pl.BlockSpec((None, MLA_KV_RANK, MLA_HEADS * 2 * HEAD_DIM), lambda b, j: (l, 0, 0)),
                  pl.BlockSpec(memory_space=pl.ANY)],
        out_specs=pl.BlockSpec((TM, MLA_OUT), lambda b, j: (_dec_row(b, j), 0)),
        out_shape=jax.ShapeDtypeStruct((N_TOK, D_MODEL), BF16),
        scratch_shapes=[pltpu.VMEM((DEC_SEQ + PAST_LEN, 2 * LANES), BF16)],
        input_output_aliases={6: 0},
        compiler_params=_cparams(("arbitrary", "arbitrary")),
        name="mla_dec",
    )(mq, ckv, kpe, cckv, ckpe, w_kvb, o)


def _swa_dec_kernel(l, sink_ref, q_ref, k_ref, v_ref, ck_ref, cv_ref, oin_ref, o_ref):
    del oin_ref
    scale = HEAD_DIM ** -0.5
    j = pl.program_id(1)
    start = pl.multiple_of(jnp.clip(j * TM - SWA_WINDOW, 0, DEC_SEQ - SWA_KEYS), SWA_WINDOW)
    qpos = j * TM + lax.broadcasted_iota(jnp.int32, (TM, SWA_KEYS), 0)
    kpos = start + lax.broadcasted_iota(jnp.int32, (TM, SWA_KEYS), 1)
    band = jnp.abs(qpos - kpos) <= SWA_WINDOW
    for h in range(SWA_HEADS):
        kh = h // (SWA_HEADS // SWA_KV_HEADS)
        sl = slice(kh * HEAD_DIM, (kh + 1) * HEAD_DIM)
        q = q_ref[:, h * HEAD_DIM:(h + 1) * HEAD_DIM]
        s_loc = jnp.where(band, _dot_nt(q, k_ref[pl.ds(start, SWA_KEYS), sl]) * scale, NEG)
        s_ctx = _dot_nt(q, ck_ref[:, sl]) * scale
        (p_loc, p_ctx), inv = _softmax_parts([s_loc, s_ctx], sink_ref[l, h])
        o_ref[:, h * HEAD_DIM:(h + 1) * HEAD_DIM] = (
            (_dot(p_loc, v_ref[pl.ds(start, SWA_KEYS), sl]) + _dot(p_ctx, cv_ref[:, sl])) * inv).astype(o_ref.dtype)


def _swa_dec(l, sink, sw, ck, cv, o):
    return pl.pallas_call(
        functools.partial(_swa_dec_kernel, l),
        grid=(DEC_BATCH, DEC_TILES_PER_BATCH),
        in_specs=[pl.BlockSpec(memory_space=pltpu.SMEM),
                  pl.BlockSpec((TM, SWA_OUT), lambda b, j: (_dec_row(b, j), 0)),
                  pl.BlockSpec((DEC_SEQ, LANES), lambda b, j: (N_CTX // DEC_SEQ + b, SWA_OUT // LANES)),
                  pl.BlockSpec((DEC_SEQ, LANES), lambda b, j: (N_CTX // DEC_SEQ + b, SWA_OUT // LANES + 1)),
                  pl.BlockSpec((None, None, PAST_LEN, LANES), lambda b, j: (b, l, 0, 0)),
                  pl.BlockSpec((None, None, PAST_LEN, LANES), lambda b, j: (b, l, 0, 0)),
                  pl.BlockSpec(memory_space=pl.ANY)],
        out_specs=pl.BlockSpec((TM, SWA_OUT), lambda b, j: (_dec_row(b, j), 1)),
        out_shape=jax.ShapeDtypeStruct((N_TOK, D_MODEL), BF16),
        input_output_aliases={6: 0},
        compiler_params=_cparams(("arbitrary", "arbitrary")),
        name="swa_dec",
    )(sink, sw, sw, sw, ck, cv, o)


def _route(sc, sel):
    rows = [sel[e:e + 1, :] for e in range(N_EXPERTS)]

    def beats(a, ia, b, ib):
        return (a > b) | ((a == b) & (ia < ib)) if ia < ib else (a > b)

    in_top = []
    gscore = []
    for g in range(N_EXPERT_GROUPS):
        mem = list(range(g * EXPERTS_PER_GROUP, (g + 1) * EXPERTS_PER_GROUP))
        acc = None
        for e in mem:
            rank = sum(beats(rows[o], o, rows[e], e).astype(jnp.int32) for o in mem if o != e)
            top = rank < 2
            in_top.append(top)
            term = jnp.where(top, rows[e], 0.0)
            acc = term if acc is None else acc + term
        gscore.append(acc)
    gates, chosen = [], []
    for g in range(N_EXPERT_GROUPS):
        lost = sum(beats(gscore[o], o, gscore[g], g).astype(jnp.int32) for o in range(N_EXPERT_GROUPS) if o != g)
        best = lost == 0
        for e in range(g * EXPERTS_PER_GROUP, (g + 1) * EXPERTS_PER_GROUP):
            pick = best & in_top[e]
            chosen.append(pick.astype(F32))
            gates.append(jnp.where(pick, sc[e:e + 1, :], 0.0))
    gate = jnp.concatenate(gates, axis=0)
    return gate / gate.sum(axis=0, keepdims=True), jnp.concatenate(chosen, axis=0)


def _post_kernel(x_ref, o_ref, mod_ref, wout_ref, g_ref, wr_ref, br_ref,
                 y_ref, h2_ref, rt_ref, rtm_ref, cnt_ref):
    attn = jnp.dot(o_ref[...].astype(BF16), wout_ref[...], preferred_element_type=F32)
    y = x_ref[...] + mod_ref[2:3, :] * attn
    y_ref[...] = y
    h2 = _rms(y, g_ref[...]) * (1.0 + mod_ref[4:5, :]) + mod_ref[3:4, :]
    h_hi = h2.astype(BF16)
    h_lo = (h2 - h_hi.astype(F32)).astype(BF16)
    w = wr_ref[...]
    w_hi = w.astype(BF16)
    w_lo = (w - w_hi.astype(F32)).astype(BF16)
    logits = (jnp.dot(h_hi, w_hi, preferred_element_type=F32) + jnp.dot(h_lo, w_hi, preferred_element_type=F32)
              + jnp.dot(h_hi, w_lo, preferred_element_type=F32))
    logits = logits.T[:N_EXPERTS, :]
    sc = 1.0 / (1.0 + jnp.exp(-logits))
    gate, chosen = _route(sc, sc + br_ref[...])
    h2_ref[...] = h_hi
    rt = jnp.concatenate([gate, chosen], axis=0)
    rt_ref[...] = rt
    rtm_ref[...] = jnp.concatenate([rt, jnp.zeros((LANES - 2 * N_EXPERTS, TM), F32)], axis=0).T
    cnt_ref[...] = jnp.broadcast_to(jnp.sum(chosen, axis=1, keepdims=True), (N_EXPERTS, LANES))


def _post(l, x, o, mods, w_out_bf, g_ffn, w_router_p, b_router):
    tile = lambda w: pl.BlockSpec((TM, w), lambda i: (i, 0))
    return pl.pallas_call(
        _post_kernel,
        grid=(N_TILES,),
        in_specs=[tile(D_MODEL), tile(D_MODEL),
                  pl.BlockSpec((None, None, N_MOD, D_MODEL), lambda i: (l, _cond_row(i), 0, 0)),
                  pl.BlockSpec((None, D_MODEL, D_MODEL), lambda i: (l, 0, 0)),
                  pl.BlockSpec((None, 1, D_MODEL), lambda i: (l, 0, 0)),
                  pl.BlockSpec((D_MODEL, LANES), lambda i: (0, 0)),
                  pl.BlockSpec((N_EXPERTS, 1), lambda i: (0, 0))],
        out_specs=[tile(D_MODEL), tile(D_MODEL), pl.BlockSpec((2 * N_EXPERTS, TM), lambda i: (0, i)),
                   tile(LANES), pl.BlockSpec((None, N_EXPERTS, LANES), lambda i: (i, 0, 0))],
        out_shape=[jax.ShapeDtypeStruct((N_TOK, D_MODEL), F32), jax.ShapeDtypeStruct((N_TOK, D_MODEL), BF16),
                   jax.ShapeDtypeStruct((2 * N_EXPERTS, N_TOK), F32),
                   jax.ShapeDtypeStruct((N_TOK, LANES), F32),
                   jax.ShapeDtypeStruct((N_TILES, N_EXPERTS, LANES), F32)],
        compiler_params=_cparams(("arbitrary",)),
        name="post_attn",
    )(x, o, mods, w_out_bf, g_ffn.reshape(DEPTH, 1, D_MODEL), w_router_p, b_router.reshape(N_EXPERTS, 1))


def _shr(x, bits):
    return lax.shift_right_logical(x, jnp.int32(bits))


TR_BITS = TR.bit_length() - 1
ALIGN_BITS = RUN_ALIGN.bit_length() - 1


def _plan_rows(cnt_ref, off_ref, npass_ref, seg_ref):
    def per_expert(e, row0):
        def per_tile(bb, r):
            off_ref[bb * N_EXPERTS + e] = r
            return r + (_shr(cnt_ref[bb * N_EXPERTS + e] + (RUN_ALIGN - 1), ALIGN_BITS) << ALIGN_BITS)

        rows_end = lax.fori_loop(0, N_TILES, per_tile, row0)
        n = _shr(rows_end - row0 + (TR - 1), TR_BITS)
        seg_ref[e] = _shr(row0, TR_BITS)
        seg_ref[N_EXPERTS + e] = n
        seg_ref[2 * N_EXPERTS + e] = rows_end
        seg_ref[3 * N_EXPERTS + e] = row0 + (n << TR_BITS)
        return row0 + (n << TR_BITS)

    end_row = lax.fori_loop(0, N_EXPERTS, per_expert, jnp.int32(0))
    seg_ref[4 * N_EXPERTS] = _shr(end_row, TR_BITS)

    def longest(bb, carry):
        m = lax.fori_loop(0, N_EXPERTS, lambda e, m: jnp.maximum(m, cnt_ref[bb * N_EXPERTS + e]), jnp.int32(0))
        npass_ref[bb] = sum((m > k * WIN).astype(jnp.int32) for k in range(-(-TM // WIN)))
        return carry

    lax.fori_loop(0, N_TILES, longest, 0)


def _dispatch_kernel(cnt_ref, h_ref, rt_ref, et_ref, xs_in, xs_hbm, off_ref, npass_ref, seg_ref, zbuf, sem):
    del xs_in
    b = pl.program_id(0)
    slot = b % 2

    @pl.when(b == 0)
    def _():
        _plan_rows(cnt_ref, off_ref, npass_ref, seg_ref)

    key = jnp.dot(et_ref[...], _run_keys(rt_ref[...], 1).astype(BF16), preferred_element_type=F32)
    slot_j = _window_slot((N_EXPERTS * WIN, TM), 0)

    def run_copies(bb, sl, p, act):
        def one_expert(e, queue):
            left = cnt_ref[bb * N_EXPERTS + e] - p * WIN
            row = off_ref[bb * N_EXPERTS + e] + p * WIN
            whole = row + WIN <= seg_ref[2 * N_EXPERTS + e]

            @pl.when(whole & (left > 0))
            def _():
                src = pl.multiple_of(e * WIN, RUN_ALIGN)
                act(pltpu.make_async_copy(zbuf.at[sl, pl.ds(src, WIN)],
                                          xs_hbm.at[pl.ds(pl.multiple_of(row, RUN_ALIGN), WIN)], sem), queue)

            def piece(k, c):
                src = pl.multiple_of(e * WIN + k * RUN_ALIGN, RUN_ALIGN)
                dst = pl.multiple_of(row + k * RUN_ALIGN, RUN_ALIGN)
                act(pltpu.make_async_copy(zbuf.at[sl, pl.ds(src, RUN_ALIGN)],
                                          xs_hbm.at[pl.ds(dst, RUN_ALIGN)], sem), queue)
                return c

            pieces = jnp.minimum(_shr(jnp.maximum(left, 0) + (RUN_ALIGN - 1), ALIGN_BITS), WIN // RUN_ALIGN)
            lax.fori_loop(0, jnp.where(whole, 0, pieces), piece, 0)

        def expert_pair(e2, carry):
            one_expert(2 * e2, 0)
            one_expert(2 * e2 + 1, 1)
            return carry

        lax.fori_loop(0, N_EXPERTS // 2, expert_pair, 0)

    def fill_and_send(p):
        pick = jnp.where(key == slot_j + jnp.asarray(p * WIN, F32), 1.0, 0.0).astype(BF16)
        zbuf[slot] = jnp.dot(pick, h_ref[...], preferred_element_type=F32).astype(BF16)
        run_copies(b, slot, p, lambda cp, queue: cp.start(priority=queue))

    @pl.when(b > 0)
    def _():
        run_copies(b - 1, 1 - slot, npass_ref[jnp.maximum(b - 1, 0)] - 1, lambda cp, queue: cp.wait())

    fill_and_send(0)

    def more(p, carry):
        run_copies(b, slot, p - 1, lambda cp, queue: cp.wait())
        fill_and_send(p)
        return carry

    lax.fori_loop(1, npass_ref[b], more, 0)

    @pl.when(b == N_TILES - 1)
    def _():
        run_copies(b, slot, npass_ref[b] - 1, lambda cp, queue: cp.wait())


def _dispatch(cnt, h2, rt, et, xs_buf):
    smem = pl.BlockSpec(memory_space=pltpu.SMEM)
    return pl.pallas_call(
        _dispatch_kernel,
        grid_spec=pltpu.PrefetchScalarGridSpec(
            num_scalar_prefetch=1,
            grid=(N_TILES,),
            in_specs=[pl.BlockSpec((TM, D_MODEL), lambda i, *_: (i, 0)),
                      pl.BlockSpec((2 * N_EXPERTS, TM), lambda i, *_: (0, i)),
                      pl.BlockSpec((N_EXPERTS * WIN, 2 * N_EXPERTS), lambda i, *_: (0, 0)),
                      pl.BlockSpec(memory_space=pl.ANY)],
            out_specs=[pl.BlockSpec(memory_space=pl.ANY), smem, smem, smem],
            scratch_shapes=[pltpu.VMEM((2, N_EXPERTS * WIN, D_MODEL), BF16), pltpu.SemaphoreType.DMA(())]),
        out_shape=[jax.ShapeDtypeStruct((NT * TR, D_MODEL), BF16), jax.ShapeDtypeStruct((N_RUNS,), jnp.int32),
                   jax.ShapeDtypeStruct((N_TILES,), jnp.int32),
                   jax.ShapeDtypeStruct((4 * N_EXPERTS + 1,), jnp.int32)],
        input_output_aliases={4: 0},
        compiler_params=_cparams(("arbitrary",)),
        name="dispatch",
    )(cnt, h2, rt, et, xs_buf)


CAST_ROWS = 128


def _cast_rows(src_ref, dst_ref, n):
    def body(c, carry):
        rows = pl.ds(pl.multiple_of(c * CAST_ROWS, CAST_ROWS), CAST_ROWS)
        dst_ref[rows, :] = src_ref[rows, :].astype(BF16)
        return carry

    lax.fori_loop(0, n, body, 0)


def _experts_kernel(seg_ref, xs_hbm, wg_ref, wu_ref, wd_ref, ys_in, ys_hbm, wgb, wub, wdb, xbuf, ybuf, semx, semy):
    del ys_in
    e = pl.program_id(0)
    t0 = seg_ref[e]
    n = seg_ref[N_EXPERTS + e]
    _cast_rows(wg_ref, wgb, D_MODEL // CAST_ROWS)
    _cast_rows(wu_ref, wub, D_MODEL // CAST_ROWS)
    _cast_rows(wd_ref, wdb, D_EXPERT // CAST_ROWS)

    def rows(k):
        return pl.ds(pl.multiple_of((t0 + k) * TR, TR), TR)

    def fetch(k, s):
        return pltpu.make_async_copy(xs_hbm.at[rows(k)], xbuf.at[s], semx.at[s])

    def put(k, s):
        return pltpu.make_async_copy(ybuf.at[s], ys_hbm.at[rows(k)], semy.at[s])

    @pl.when(n > 0)
    def _():
        fetch(0, 0).start()

    def tile(k, carry):
        s = k % 2

        @pl.when(k + 1 < n)
        def _():
            fetch(k + 1, 1 - s).start()

        fetch(k, s).wait()

        @pl.when(k >= 2)
        def _():
            put(k - 2, s).wait()

        x = xbuf[s]
        hg = jnp.dot(x, wgb[...], preferred_element_type=F32)
        hu = jnp.dot(x, wub[...], preferred_element_type=F32)
        a = hg * (1.0 / (1.0 + jnp.exp(-hg))) * hu
        ybuf[s] = jnp.dot(a.astype(BF16), wdb[...], preferred_element_type=F32).astype(BF16)
        put(k, s).start(priority=1)
        return carry

    lax.fori_loop(0, n, tile, 0)

    @pl.when(n >= 2)
    def _():
        put(n - 2, n % 2).wait()

    @pl.when(n >= 1)
    def _():
        put(n - 1, (n - 1) % 2).wait()


def _experts(l, seg, xs, w_gate, w_up, w_down, ys_buf):
    wspec = lambda a, b: pl.BlockSpec((None, None, a, b), lambda e, seg: (l, e, 0, 0))
    tile_buf = pltpu.VMEM((2, TR, D_MODEL), BF16)
    return pl.pallas_call(
        _experts_kernel,
        grid_spec=pltpu.PrefetchScalarGridSpec(
            num_scalar_prefetch=1,
            grid=(N_EXPERTS,),
            in_specs=[pl.BlockSpec(memory_space=pl.ANY),
                      wspec(D_MODEL, D_EXPERT), wspec(D_MODEL, D_EXPERT), wspec(D_EXPERT, D_MODEL),
                      pl.BlockSpec(memory_space=pl.ANY)],
            out_specs=pl.BlockSpec(memory_space=pl.ANY),
            scratch_shapes=[pltpu.VMEM((D_MODEL, D_EXPERT), BF16), pltpu.VMEM((D_MODEL, D_EXPERT), BF16),
                            pltpu.VMEM((D_EXPERT, D_MODEL), BF16), tile_buf, tile_buf,
                            pltpu.SemaphoreType.DMA((2,)), pltpu.SemaphoreType.DMA((2,))]),
        out_shape=jax.ShapeDtypeStruct((NT * TR, D_MODEL), BF16),
        input_output_aliases={5: 0},
        compiler_params=_cparams(("arbitrary",)),
        name="experts",
    )(seg, xs, w_gate, w_up, w_down, ys_buf)


def _final_kernel(off_ref, npass_ref, y_ref, rtm_ref, ek_ref, eg_ref, ys_hbm, mod_ref, g_ref, op_ref, os_ref,
                  wbuf, sem):
    i = pl.program_id(0)
    moe = _combined_moe(off_ref, npass_ref, rtm_ref, ek_ref, eg_ref, ys_hbm, wbuf, sem)
    out = _rms(y_ref[...] + mod_ref[5:6, :] * moe, g_ref[...])

    @pl.when(i < CTX_TILES)
    def _():
        op_ref[...] = out

    @pl.when(i >= CTX_TILES)
    def _():
        os_ref[...] = out


def _final(off, npass, y, rtm, ek, eg, ys, mods, g_final):
    return pl.pallas_call(
        _final_kernel,
        grid_spec=pltpu.PrefetchScalarGridSpec(
            num_scalar_prefetch=2,
            grid=(N_TILES,),
            in_specs=_combine_specs() + [
                pl.BlockSpec((None, None, N_MOD, D_MODEL), lambda i, *_: (DEPTH - 1, _cond_row(i), 0, 0)),
                pl.BlockSpec((1, D_MODEL), lambda i, *_: (0, 0))],
            out_specs=[pl.BlockSpec((TM, D_MODEL), lambda i, *_: (jnp.minimum(i, CTX_TILES - 1), 0)),
                       pl.BlockSpec((TM, D_MODEL), lambda i, *_: (jnp.maximum(i - CTX_TILES, 0), 0))],
            scratch_shapes=_combine_scratch()),
        out_shape=[jax.ShapeDtypeStruct((N_CTX, D_MODEL), F32), jax.ShapeDtypeStruct((N_DEC, D_MODEL), F32)],
        compiler_params=_cparams(("arbitrary",)),
        name="final_norm",
    )(off, npass, y, rtm, ek, eg, ys, mods, g_final.reshape(1, D_MODEL))


def _prep_kernel(win_ref, wout_ref, wi_ref, wo_ref):
    w = win_ref[...]
    split = KPE_OFF + MLA_ROPE_DIM
    wi_ref[...] = jnp.concatenate([w[:, :split], jnp.zeros((CAST_ROWS, SW_OFF - split), F32), w[:, split:]],
                                  axis=1).astype(BF16)
    wo_ref[...] = wout_ref[...].astype(BF16)


def _prep_weights(w_in, w_out):
    n = D_MODEL // CAST_ROWS
    shift = NA_OUT // CAST_ROWS
    return pl.pallas_call(
        _prep_kernel,
        grid=(DEPTH, n),
        in_specs=[pl.BlockSpec((None, CAST_ROWS, IN_WIDTH), lambda l, j: (l, j, 0)),
                  pl.BlockSpec((None, CAST_ROWS, D_MODEL), lambda l, j: (l, (j + shift) % n, 0))],
        out_specs=[pl.BlockSpec((None, CAST_ROWS, IN_PAD), lambda l, j: (l, j, 0)),
                   pl.BlockSpec((None, CAST_ROWS, D_MODEL), lambda l, j: (l, j, 0))],
        out_shape=[jax.ShapeDtypeStruct((DEPTH, D_MODEL, IN_PAD), BF16),
                   jax.ShapeDtypeStruct((DEPTH, D_MODEL, D_MODEL), BF16)],
        compiler_params=_cparams(("arbitrary", "arbitrary")),
        name="prep_weights",
    )(w_in, w_out)


def _rope_tables(rot_dim):
    t = jnp.arange(DEC_SEQ, dtype=jnp.int32)
    row = (t // GRID_W).astype(F32)
    col = (t % GRID_W).astype(F32)
    per_axis = rot_dim // 2
    inv = ROPE_BASE ** (-jnp.arange(0, per_axis, 2, dtype=F32) / per_axis)
    ang = jnp.concatenate([row[:, None] * inv, col[:, None] * inv], axis=-1)
    cos, sin = jnp.cos(ang), jnp.sin(ang)
    zero = jnp.zeros_like(sin)
    rep = LANES // rot_dim
    tabs = [jnp.concatenate([cos, cos], -1), jnp.concatenate([zero, sin], -1), jnp.concatenate([-sin, zero], -1)]
    return jnp.stack([jnp.tile(a, (1, rep)) for a in tabs])


def kernel(x_prompt, x_sample, c, cache_na_k, cache_na_v, cache_mla_ckv, cache_mla_kpe, cache_swa_k, cache_swa_v,
           c_ctx, w_ada, b_ada, g_attn, w_in, g_mla_q, w_mla_qb, g_mla_kv, w_mla_kvb, na_rpb, swa_sink, w_out,
           g_ffn, w_router, b_router, w_gate, w_up, w_down, g_final):
    cond = jnp.concatenate([c_ctx[None], c, jnp.zeros((COND_ROWS - 1 - DEC_BATCH, D_MODEL), F32)], axis=0)
    mods = _ada(cond, w_ada, b_ada).reshape(DEPTH, COND_ROWS, N_MOD, D_MODEL)
    bias = _na_bias(na_rpb)
    t_mla = _rope_tables(MLA_ROPE_DIM)
    t_swa = _rope_tables(HEAD_DIM)

    w_in_p, w_out_p = _prep_weights(w_in, w_out)
    wq = w_mla_qb.reshape(DEPTH, MLA_Q_RANK, MLA_HEADS, MLA_QK_DIM)
    w_rope = jnp.pad(wq[..., MLA_NOPE_DIM:], ((0, 0), (0, 0), (0, 0), (0, LANES - MLA_ROPE_DIM)))
    w_qb_p = jnp.concatenate([wq[..., :MLA_NOPE_DIM].reshape(DEPTH, MLA_Q_RANK, MQ_NOPE),
                              w_rope.reshape(DEPTH, MLA_Q_RANK, MLA_HEADS * LANES)], axis=-1).astype(BF16)
    w_router_p = jnp.pad(w_router, ((0, 0), (0, LANES - N_EXPERTS)))
    et, ek, eg = _spread_consts()

    c_na_k = cache_na_k.reshape(DEC_BATCH, DEPTH, PAST_LEN, NA_OUT)
    c_na_v = cache_na_v.reshape(DEC_BATCH, DEPTH, PAST_LEN, NA_OUT)
    c_sw_k = cache_swa_k.reshape(DEC_BATCH, DEPTH, PAST_LEN, LANES)
    c_sw_v = cache_swa_v.reshape(DEC_BATCH, DEPTH, PAST_LEN, LANES)

    y = ys = rtm = off = npass = None
    caches = [jnp.zeros((BATCH, DEPTH, SEQ, w), F32) for w in CACHE_WIDTHS]
    o = jnp.zeros((N_TOK, D_MODEL), BF16)
    xs = jnp.zeros((NT * TR, D_MODEL), BF16)
    ys = jnp.zeros((NT * TR, D_MODEL), BF16)
    for l in range(DEPTH):
        if l == 0:
            srcs = [x_prompt.reshape(N_CTX, D_MODEL), x_sample.reshape(N_DEC, D_MODEL)]
        else:
            srcs = [off, npass, y, rtm, ek, eg, ys]
        x, na, mq, ckv, kpe, sw, *caches = _pre(l == 0, l, srcs, caches, mods, g_attn, w_in_p, g_mla_q, w_qb_p,
                                                g_mla_kv, t_mla, t_swa)
        o = _ctx_attn(l, swa_sink, na, mq, ckv, kpe, sw, w_mla_kvb, o)
        o = _mla_dec(l, mq, ckv, kpe, cache_mla_ckv, cache_mla_kpe, w_mla_kvb, o)
        o = _swa_dec(l, swa_sink, sw, c_sw_k, c_sw_v, o)
        o = _na_dec(l, na, c_na_k, c_na_v, bias, o)
        y, h2, rt, rtm, cnt = _post(l, x, o, mods, w_out_p, g_ffn, w_router_p, b_router)
        xs, off, npass, seg = _dispatch(cnt[:, :, 0].astype(jnp.int32).reshape(-1), h2, rt, et, xs)
        ys = _experts(l, seg, xs, w_gate, w_up, w_down, ys)
    y_prompt, y_sample = _final(off, npass, y, rtm, ek, eg, ys, mods, g_final)

    heads = lambda a, n: a.reshape(BATCH, DEPTH, SEQ, n, HEAD_DIM)
    return (y_prompt.reshape(BATCH, SEQ, D_MODEL), y_sample.reshape(DEC_BATCH, DEC_SEQ, D_MODEL),
            heads(caches[0], NA_HEADS), heads(caches[1], NA_HEADS), caches[2], caches[3],
            heads(caches[4], SWA_KV_HEADS), heads(caches[5], SWA_KV_HEADS))
```

```python
import functools

import jax
import jax.numpy as jnp
from jax import lax
from jax.experimental import pallas as pl
from jax.experimental.pallas import tpu as pltpu

D_MODEL = 1024
BATCH = 16
SEQ = 256
DEPTH = 4
DEC_BATCH = 2
DEC_SEQ = 2048
PAST_LEN = 256
GRID_W = 64
HEAD_DIM = 64
NA_HEADS = 4
NA_WIN_H = 8
NA_WIN_W = 16
MLA_HEADS = 6
MLA_Q_RANK = 256
MLA_KV_RANK = 128
MLA_NOPE_DIM = 64
MLA_ROPE_DIM = 32
MLA_V_DIM = 64
MLA_QK_DIM = MLA_NOPE_DIM + MLA_ROPE_DIM
SWA_HEADS = 6
SWA_KV_HEADS = 2
SWA_WINDOW = 128
ROPE_BASE = 10000.0
N_EXPERTS = 16
N_EXPERT_GROUPS = 4
EXPERTS_PER_GROUP = 4
D_EXPERT = 512
RMS_EPS = 1e-6
N_MOD = 6

NA_IN = 3 * NA_HEADS * HEAD_DIM
MLA_IN = MLA_Q_RANK + MLA_KV_RANK + MLA_ROPE_DIM
SWA_IN = (SWA_HEADS + 2 * SWA_KV_HEADS) * HEAD_DIM
IN_WIDTH = NA_IN + MLA_IN + SWA_IN
NA_OUT = NA_HEADS * HEAD_DIM
MLA_OUT = MLA_HEADS * MLA_V_DIM
SWA_OUT = SWA_HEADS * HEAD_DIM

LANES = 128
N_CTX = BATCH * SEQ
N_DEC = DEC_BATCH * DEC_SEQ
N_TOK = N_CTX + N_DEC
TM = 256
N_TILES = N_TOK // TM
CTX_TILES = N_CTX // TM
DEC_TILES_PER_BATCH = DEC_SEQ // TM
COND_ROWS = 8
KPE_OFF = NA_IN + MLA_Q_RANK + MLA_KV_RANK
SW_OFF = KPE_OFF + LANES
IN_PAD = SW_OFF + SWA_IN
MQ_NOPE = MLA_HEADS * MLA_NOPE_DIM
MQ_W = MQ_NOPE + MLA_HEADS * LANES
MLA_SCALE = MLA_QK_DIM ** -0.5
MLA_GROUP = 3
NA_ROWS_PER_STEP = TM // GRID_W
NA_KEY_ROWS = 12
NA_KEYS = NA_KEY_ROWS * GRID_W
SWA_KEYS = 512
NEG = -1e30
TR = 512
N_ASSIGN = 2 * N_TOK
RUN_ALIGN = 16
WIN = 80
N_RUNS = N_TILES * N_EXPERTS
NT = -(-(N_ASSIGN + N_RUNS * (RUN_ALIGN - 1) + N_EXPERTS * (TR - 1) + WIN) // TR)

F32 = jnp.float32
BF16 = jnp.bfloat16
VMEM_LIMIT = 56 * 1024 * 1024


def _cparams(sem):
    return pltpu.CompilerParams(dimension_semantics=sem, vmem_limit_bytes=VMEM_LIMIT)


def _cond_row(i):
    return jnp.where(i < CTX_TILES, 0, 1 + (i - CTX_TILES) // DEC_TILES_PER_BATCH)


def _rope_blk(i):
    return jnp.where(i < CTX_TILES, 0, (i - CTX_TILES) % DEC_TILES_PER_BATCH)


def _rms(x, g):
    ms = jnp.mean(x * x, axis=-1, keepdims=True)
    return x * lax.rsqrt(ms + RMS_EPS) * g


def _dot(a, b):
    return jnp.dot(a.astype(BF16), b.astype(BF16), preferred_element_type=F32)


def _dot_nt(a, b):
    return lax.dot_general(a.astype(BF16), b.astype(BF16), (((1,), (1,)), ((), ())),
                           preferred_element_type=F32)


def _ada_kernel(c_ref, w_ref, b_ref, o_ref):
    c = c_ref[...]
    s = c * (1.0 / (1.0 + jnp.exp(-c)))
    o_ref[...] = _dot(s, w_ref[...]) + b_ref[...]


def _ada(cond, w_ada, b_ada):
    tn = 1536
    n = N_MOD * D_MODEL
    return pl.pallas_call(
        _ada_kernel,
        grid=(DEPTH, n // tn),
        in_specs=[pl.BlockSpec((COND_ROWS, D_MODEL), lambda l, j: (0, 0)),
                  pl.BlockSpec((None, D_MODEL, tn), lambda l, j: (l, 0, j)),
                  pl.BlockSpec((None, 1, tn), lambda l, j: (l, 0, j))],
        out_specs=pl.BlockSpec((None, COND_ROWS, tn), lambda l, j: (l, 0, j)),
        out_shape=jax.ShapeDtypeStruct((DEPTH, COND_ROWS, n), F32),
        compiler_params=_cparams(("arbitrary", "arbitrary")),
        name="ada_mod",
    )(cond, w_ada, b_ada.reshape(DEPTH, 1, n))


def _bias_kernel(rpb_ref, o_ref):
    g = pl.program_id(0)
    base = g * ((2 * NA_WIN_H - 1) * (2 * NA_WIN_W - 1))
    qc = lax.broadcasted_iota(jnp.int32, (GRID_W, GRID_W), 0)
    kc = lax.broadcasted_iota(jnp.int32, (GRID_W, GRID_W), 1)
    dc = jnp.clip(kc - qc + (NA_WIN_W - 1), 0, 2 * NA_WIN_W - 2)
    cs = jnp.clip(qc - NA_WIN_W // 2, 0, GRID_W - NA_WIN_W)
    col_ok = (kc >= cs) & (kc < cs + NA_WIN_W)
    neg = jnp.full((GRID_W, GRID_W), NEG, F32)
    tabs = []
    for a in range(2 * NA_WIN_H - 1):
        t = jnp.zeros((GRID_W, GRID_W), F32)
        for b in range(2 * NA_WIN_W - 1):
            t = jnp.where(dc == b, rpb_ref[base + a * (2 * NA_WIN_W - 1) + b], t)
        tabs.append(jnp.where(col_ok, t, NEG))
    for p in range(3):
        for qi in range(NA_ROWS_PER_STEP):
            for kj in range(NA_KEY_ROWS):
                if p == 0:
                    ok, dr = kj < NA_WIN_H, kj - qi + 7
                elif p == 1:
                    ok, dr = qi <= kj < qi + NA_WIN_H, kj - qi + 3
                else:
                    ok, dr = kj >= NA_KEY_ROWS - NA_WIN_H, kj - qi - 1
                blk = tabs[dr] if ok else neg
                o_ref[p, qi * GRID_W:(qi + 1) * GRID_W, kj * GRID_W:(kj + 1) * GRID_W] = blk


def _na_bias(na_rpb):
    return pl.pallas_call(
        _bias_kernel,
        grid=(DEPTH * NA_HEADS,),
        in_specs=[pl.BlockSpec(memory_space=pltpu.SMEM)],
        out_specs=pl.BlockSpec((None, 3, None, TM, NA_KEYS),
                               lambda g: (g // NA_HEADS, 0, g % NA_HEADS, 0, 0)),
        out_shape=jax.ShapeDtypeStruct((DEPTH, 3, NA_HEADS, TM, NA_KEYS), F32),
        compiler_params=_cparams(("arbitrary",)),
        name="na_bias",
    )(na_rpb.reshape(-1))


def _rope128(x, t_ref, half):
    return (x * t_ref[0] + pltpu.roll(x, half, 1) * t_ref[1]
            + pltpu.roll(x, LANES - half, 1) * t_ref[2])


def _spread_consts():
    slot_e = jnp.arange(N_EXPERTS * WIN, dtype=jnp.int32) // WIN
    idx32 = jnp.arange(2 * N_EXPERTS, dtype=jnp.int32)
    idx128 = jnp.arange(LANES, dtype=jnp.int32)
    et = (idx32[None, :] == slot_e[:, None] + N_EXPERTS).astype(BF16)
    ek = (idx128[:, None] == slot_e[None, :] + N_EXPERTS).astype(BF16)
    eg = (idx128[:, None] == slot_e[None, :]).astype(BF16)
    return et, ek, eg


def _window_slot(shape, axis):
    r = lax.broadcasted_iota(jnp.int32, shape, axis).astype(F32)
    return r - WIN * jnp.floor((r + 0.5) * (1.0 / WIN))


def _run_keys(chosen, token_axis):
    a = lax.broadcasted_iota(jnp.int32, (TM, TM), 0)
    b = lax.broadcasted_iota(jnp.int32, (TM, TM), 1)
    if token_axis == 1:
        rank = jnp.dot(chosen.astype(BF16), (a < b).astype(BF16), preferred_element_type=F32)
    else:
        rank = jnp.dot((b < a).astype(BF16), chosen.astype(BF16), preferred_element_type=F32)
    return jnp.where(chosen > 0.5, rank, -1.0)


def _windows_start(i, slot, p, off_ref, ys_hbm, wbuf, sem):
    for e in range(N_EXPERTS):
        row = pl.multiple_of(off_ref[i * N_EXPERTS + e] + p * WIN, RUN_ALIGN)
        pltpu.make_async_copy(ys_hbm.at[pl.ds(row, WIN)], wbuf.at[slot, pl.ds(e * WIN, WIN)],
                              sem.at[slot]).start(priority=e % 2)


def _windows_wait(slot, ys_hbm, wbuf, sem):
    pltpu.make_async_copy(ys_hbm.at[pl.ds(0, N_EXPERTS * WIN)], wbuf.at[slot], sem.at[slot]).wait()


def _combined_moe(off_ref, npass_ref, rtm_ref, ek_ref, eg_ref, ys_hbm, wbuf, sem):
    i = pl.program_id(0)
    slot = i % 2

    @pl.when(i == 0)
    def _():
        _windows_start(0, 0, 0, off_ref, ys_hbm, wbuf, sem)

    @pl.when(i + 1 < N_TILES)
    def _():
        _windows_start(i + 1, 1 - slot, 0, off_ref, ys_hbm, wbuf, sem)

    r = rtm_ref[...]
    key = jnp.dot(_run_keys(r, 0).astype(BF16), ek_ref[...], preferred_element_type=F32)
    gate = jnp.dot(r.astype(BF16), eg_ref[...], preferred_element_type=F32)
    slot_j = _window_slot((TM, N_EXPERTS * WIN), 1)

    def contrib(p):
        g = jnp.where(key == slot_j + jnp.asarray(p * WIN, F32), gate, 0.0).astype(BF16)
        return jnp.dot(g, wbuf[slot], preferred_element_type=F32)

    _windows_wait(slot, ys_hbm, wbuf, sem)
    acc = contrib(0)

    def extra(p, acc):
        _windows_start(i, slot, p, off_ref, ys_hbm, wbuf, sem)
        _windows_wait(slot, ys_hbm, wbuf, sem)
        return acc + contrib(p)

    return lax.fori_loop(1, npass_ref[i], extra, acc)


def _pre_kernel(first, *refs):
    i = pl.program_id(0)
    if first:
        (xp_ref, xs_ref, mod_ref, g_ref, win_ref, gq_ref, wqb_ref, gkv_ref, tm_ref, ts_ref, _, _, _, _, _, _,
         xo_ref, na_ref, mq_ref, ckv_ref, kpe_ref, sw_ref, *cache_refs) = refs
        x = jnp.where(i < CTX_TILES, xp_ref[...], xs_ref[...])
    else:
        (off_ref, npass_ref, y_ref, rtm_ref, ek_ref, eg_ref, ys_hbm, modp_ref, mod_ref, g_ref, win_ref, gq_ref,
         wqb_ref, gkv_ref, tm_ref, ts_ref, _, _, _, _, _, _,
         xo_ref, na_ref, mq_ref, ckv_ref, kpe_ref, sw_ref, *rest) = refs
        *cache_refs, wbuf, sem = rest
        moe = _combined_moe(off_ref, npass_ref, rtm_ref, ek_ref, eg_ref, ys_hbm, wbuf, sem)
        x = y_ref[...] + modp_ref[5:6, :] * moe
    xo_ref[...] = x
    cnk_ref, cnv_ref, cckv_ref, ckpe_ref, csk_ref, csv_ref = cache_refs

    h = _rms(x, g_ref[...]) * (1.0 + mod_ref[1:2, :]) + mod_ref[0:1, :]
    z = jnp.dot(h.astype(BF16), win_ref[...], preferred_element_type=F32)
    na_ref[...] = z[:, :NA_IN].astype(BF16)
    cq = _rms(z[:, NA_IN:NA_IN + MLA_Q_RANK], gq_ref[...])
    ckv = _rms(z[:, NA_IN + MLA_Q_RANK:KPE_OFF], gkv_ref[...])
    ckv_ref[...] = ckv.astype(BF16)
    mq = jnp.dot(cq.astype(BF16), wqb_ref[...], preferred_element_type=F32)
    kpe = z[:, KPE_OFF:SW_OFF]
    sw = z[:, SW_OFF:IN_PAD]
    mq_ref[:, :MQ_NOPE] = mq[:, :MQ_NOPE].astype(BF16)
    sw_ref[:, SWA_OUT + LANES:] = sw[:, SWA_OUT + LANES:].astype(BF16)

    @pl.when(i < CTX_TILES)
    def _():
        mq_ref[:, MQ_NOPE:] = (mq[:, MQ_NOPE:] * MLA_SCALE).astype(BF16)
        kpe_ref[...] = kpe.astype(BF16)
        sw_ref[:, :SWA_OUT + LANES] = sw[:, :SWA_OUT + LANES].astype(BF16)
        cnk_ref[...] = z[:, NA_OUT:2 * NA_OUT]
        cnv_ref[...] = z[:, 2 * NA_OUT:NA_IN]
        cckv_ref[...] = ckv
        ckpe_ref[...] = kpe[:, :MLA_ROPE_DIM]
        csk_ref[...] = sw[:, SWA_OUT:SWA_OUT + LANES]
        csv_ref[...] = sw[:, SWA_OUT + LANES:]

    @pl.when(i >= CTX_TILES)
    def _():
        for c in range(MQ_NOPE // LANES, MQ_W // LANES):
            roped = _rope128(mq[:, c * LANES:(c + 1) * LANES], tm_ref, MLA_ROPE_DIM // 2)
            mq_ref[:, c * LANES:(c + 1) * LANES] = (roped * MLA_SCALE).astype(BF16)
        kpe_ref[...] = _rope128(kpe, tm_ref, MLA_ROPE_DIM // 2).astype(BF16)
        for c in range((SWA_OUT + LANES) // LANES):
            sw_ref[:, c * LANES:(c + 1) * LANES] = _rope128(sw[:, c * LANES:(c + 1) * LANES], ts_ref,
                                                            HEAD_DIM // 2).astype(BF16)


def _combine_scratch():
    return [pltpu.VMEM((2, N_EXPERTS * WIN, D_MODEL), BF16), pltpu.SemaphoreType.DMA((2,))]


def _combine_specs():
    spread = pl.BlockSpec((LANES, N_EXPERTS * WIN), lambda i, *_: (0, 0))
    return [pl.BlockSpec((TM, D_MODEL), lambda i, *_: (i, 0)),
            pl.BlockSpec((TM, LANES), lambda i, *_: (i, 0)),
            spread, spread, pl.BlockSpec(memory_space=pl.ANY)]


CACHE_WIDTHS = (NA_OUT, NA_OUT, MLA_KV_RANK, MLA_ROPE_DIM, LANES, LANES)


def _pre(first, l, xs, caches, mods, g_attn, w_in_p, g_mla_q, w_qb_p, g_mla_kv, t_mla, t_swa):
    tile = lambda w: pl.BlockSpec((TM, w), lambda i, *_: (i, 0))
    ctx_tile = lambda w: pl.BlockSpec((TM, w), lambda i, *_: (jnp.minimum(i, CTX_TILES - 1), 0))
    slab = lambda w: pl.BlockSpec((None, None, SEQ, w), lambda i, *_: (jnp.minimum(i, BATCH - 1), l, 0, 0))
    mod_spec = lambda ll: pl.BlockSpec((None, None, N_MOD, D_MODEL), lambda i, *_: (ll, _cond_row(i), 0, 0))
    vec = lambda w: pl.BlockSpec((None, 1, w), lambda i, *_: (l, 0, 0))
    if first:
        in_specs = [ctx_tile(D_MODEL), pl.BlockSpec((TM, D_MODEL), lambda i: (jnp.maximum(i - CTX_TILES, 0), 0))]
    else:
        in_specs = _combine_specs() + [mod_spec(l - 1)]
    in_specs += [mod_spec(l), vec(D_MODEL),
                 pl.BlockSpec((None, D_MODEL, IN_PAD), lambda i, *_: (l, 0, 0)),
                 vec(MLA_Q_RANK),
                 pl.BlockSpec((None, MLA_Q_RANK, MQ_W), lambda i, *_: (l, 0, 0)),
                 vec(MLA_KV_RANK),
                 pl.BlockSpec((3, TM, LANES), lambda i, *_: (0, _rope_blk(i), 0)),
                 pl.BlockSpec((3, TM, LANES), lambda i, *_: (0, _rope_blk(i), 0))]
    in_specs += [pl.BlockSpec(memory_space=pl.ANY)] * len(CACHE_WIDTHS)
    widths = [D_MODEL, NA_IN, MQ_W, MLA_KV_RANK, LANES, SWA_IN]
    args = list(xs) + ([mods] if not first else []) + [
        mods, g_attn.reshape(DEPTH, 1, D_MODEL), w_in_p, g_mla_q.reshape(DEPTH, 1, MLA_Q_RANK), w_qb_p,
        g_mla_kv.reshape(DEPTH, 1, MLA_KV_RANK), t_mla, t_swa]
    aliases = {len(args) + n: len(widths) + n for n in range(len(CACHE_WIDTHS))}
    return pl.pallas_call(
        functools.partial(_pre_kernel, first),
        grid_spec=pltpu.PrefetchScalarGridSpec(
            num_scalar_prefetch=0 if first else 2,
            grid=(N_TILES,),
            in_specs=in_specs,
            out_specs=[tile(w) for w in widths] + [slab(w) for w in CACHE_WIDTHS],
            scratch_shapes=[] if first else _combine_scratch()),
        out_shape=([jax.ShapeDtypeStruct((N_TOK, w), BF16 if n else F32) for n, w in enumerate(widths)]
                   + [jax.ShapeDtypeStruct((BATCH, DEPTH, SEQ, w), F32) for w in CACHE_WIDTHS]),
        input_output_aliases=aliases,
        compiler_params=_cparams(("arbitrary",)),
        name="pre_attn",
    )(*args, *caches)


def _softmax_parts(parts, sink=None):
    m = parts[0].max(axis=-1, keepdims=True)
    for s in parts[1:]:
        m = jnp.maximum(m, s.max(axis=-1, keepdims=True))
    if sink is not None:
        m = jnp.maximum(m, sink)
    ps = [jnp.exp(s - m) for s in parts]
    den = ps[0].sum(axis=-1, keepdims=True)
    for p in ps[1:]:
        den = den + p.sum(axis=-1, keepdims=True)
    if sink is not None:
        den = den + jnp.exp(sink - m)
    return ps, 1.0 / den


def _mla_heads(mq_ref, wkvb_ref, kcat, o_ref):
    rows = mq_ref.shape[0]
    for g in range(MLA_HEADS // MLA_GROUP):
        heads = range(g * MLA_GROUP, (g + 1) * MLA_GROUP)
        qs = []
        for h in heads:
            wk = wkvb_ref[:, h * 2 * HEAD_DIM:h * 2 * HEAD_DIM + MLA_NOPE_DIM]
            qa = _dot_nt(mq_ref[:, h * MLA_NOPE_DIM:(h + 1) * MLA_NOPE_DIM], wk) * MLA_SCALE
            qr = mq_ref[:, MQ_NOPE + h * LANES:MQ_NOPE + (h + 1) * LANES]
            qs.append(jnp.concatenate([qa.astype(BF16), qr], axis=1))
        s = _dot_nt(jnp.concatenate(qs, axis=0), kcat[...])
        p = jnp.exp(s - s.max(axis=-1, keepdims=True))
        inv = 1.0 / p.sum(axis=-1, keepdims=True)
        lat = _dot(p, kcat[:, :LANES]) * inv
        for n, h in enumerate(heads):
            wv = wkvb_ref[:, h * 2 * HEAD_DIM + MLA_NOPE_DIM:(h + 1) * 2 * HEAD_DIM]
            o_ref[:, h * MLA_V_DIM:(h + 1) * MLA_V_DIM] = _dot(lat[n * rows:(n + 1) * rows], wv).astype(o_ref.dtype)


def _ctx_attn_kernel(l, sink_ref, na_ref, mq_ref, ckv_ref, kpe_ref, sw_ref, wkvb_ref, oin_ref, o_ref):
    del oin_ref
    scale = HEAD_DIM ** -0.5
    _mla_heads(mq_ref, wkvb_ref, jnp.concatenate([ckv_ref[...], kpe_ref[...]], axis=1), o_ref)
    for h in range(SWA_HEADS):
        kh = h // (SWA_HEADS // SWA_KV_HEADS)
        q = sw_ref[:, h * HEAD_DIM:(h + 1) * HEAD_DIM] * scale
        k = sw_ref[:, SWA_OUT + kh * HEAD_DIM:SWA_OUT + (kh + 1) * HEAD_DIM]
        v = sw_ref[:, SWA_OUT + LANES + kh * HEAD_DIM:SWA_OUT + LANES + (kh + 1) * HEAD_DIM]
        (p,), inv = _softmax_parts([_dot_nt(q, k)], sink_ref[l, h])
        o_ref[:, MLA_OUT + h * HEAD_DIM:MLA_OUT + (h + 1) * HEAD_DIM] = (_dot(p, v) * inv).astype(o_ref.dtype)
    for h in range(NA_HEADS):
        q = na_ref[:, h * HEAD_DIM:(h + 1) * HEAD_DIM] * scale
        k = na_ref[:, NA_OUT + h * HEAD_DIM:NA_OUT + (h + 1) * HEAD_DIM]
        v = na_ref[:, 2 * NA_OUT + h * HEAD_DIM:2 * NA_OUT + (h + 1) * HEAD_DIM]
        (p,), inv = _softmax_parts([_dot_nt(q, k)])
        o_ref[:, MLA_OUT + SWA_OUT + h * HEAD_DIM:MLA_OUT + SWA_OUT + (h + 1) * HEAD_DIM] = (
            _dot(p, v) * inv).astype(o_ref.dtype)


def _ctx_attn(l, sink, na, mq, ckv, kpe, sw, w_kvb, o):
    tile = lambda w: pl.BlockSpec((SEQ, w), lambda b: (b, 0))
    return pl.pallas_call(
        functools.partial(_ctx_attn_kernel, l),
        grid=(BATCH,),
        in_specs=[pl.BlockSpec(memory_space=pltpu.SMEM), tile(NA_IN), tile(MQ_W), tile(MLA_KV_RANK), tile(LANES),
                  tile(SWA_IN), pl.BlockSpec((None, MLA_KV_RANK, MLA_HEADS * 2 * HEAD_DIM), lambda b: (l, 0, 0)),
                  pl.BlockSpec(memory_space=pl.ANY)],
        out_specs=tile(D_MODEL),
        out_shape=jax.ShapeDtypeStruct((N_TOK, D_MODEL), BF16),
        input_output_aliases={7: 0},
        compiler_params=_cparams(("arbitrary",)),
        name="ctx_attn",
    )(sink, na, mq, ckv, kpe, sw, w_kvb, o)


def _dec_row(b, j):
    return CTX_TILES + b * DEC_TILES_PER_BATCH + j


def _na_dec_kernel(q_ref, k_ref, v_ref, ck_ref, cv_ref, bias_ref, oin_ref, o_ref):
    del oin_ref
    scale = HEAD_DIM ** -0.5
    j = pl.program_id(1)
    w0 = jnp.clip(j * NA_ROWS_PER_STEP - NA_WIN_H // 2, 0, DEC_SEQ // GRID_W - NA_KEY_ROWS)
    start = pl.multiple_of(w0 * GRID_W, GRID_W)
    for h in range(NA_HEADS):
        sl = slice(h * HEAD_DIM, (h + 1) * HEAD_DIM)
        q = q_ref[:, sl] * scale
        k = k_ref[pl.ds(start, NA_KEYS), sl]
        v = v_ref[pl.ds(start, NA_KEYS), sl]
        s_nb = _dot_nt(q, k) + bias_ref[h]
        s_ctx = _dot_nt(q, ck_ref[:, sl])
        (p_nb, p_ctx), inv = _softmax_parts([s_nb, s_ctx])
        o_ref[:, sl] = ((_dot(p_nb, v) + _dot(p_ctx, cv_ref[:, sl])) * inv).astype(o_ref.dtype)


def _na_dec(l, na, ck, cv, bias, o):
    pat = lambda j: jnp.where(j == 0, 0, jnp.where(j == DEC_TILES_PER_BATCH - 1, 2, 1))
    return pl.pallas_call(
        _na_dec_kernel,
        grid=(DEC_BATCH, DEC_TILES_PER_BATCH),
        in_specs=[pl.BlockSpec((TM, NA_OUT), lambda b, j: (_dec_row(b, j), 0)),
                  pl.BlockSpec((DEC_SEQ, NA_OUT), lambda b, j: (N_CTX // DEC_SEQ + b, 1)),
                  pl.BlockSpec((DEC_SEQ, NA_OUT), lambda b, j: (N_CTX // DEC_SEQ + b, 2)),
                  pl.BlockSpec((None, None, PAST_LEN, NA_OUT), lambda b, j: (b, l, 0, 0)),
                  pl.BlockSpec((None, None, PAST_LEN, NA_OUT), lambda b, j: (b, l, 0, 0)),
                  pl.BlockSpec((None, None, NA_HEADS, TM, NA_KEYS), lambda b, j: (l, pat(j), 0, 0, 0)),
                  pl.BlockSpec(memory_space=pl.ANY)],
        out_specs=pl.BlockSpec((TM, NA_OUT), lambda b, j: (_dec_row(b, j), (MLA_OUT + SWA_OUT) // NA_OUT)),
        out_shape=jax.ShapeDtypeStruct((N_TOK, D_MODEL), BF16),
        input_output_aliases={6: 0},
        compiler_params=_cparams(("arbitrary", "arbitrary")),
        name="na_dec",
    )(na, na, na, ck, cv, bias, o)


KCAT_ROWS = 256


def _mla_dec_kernel(mq_ref, ckv_ref, kpe_ref, cckv_ref, ckpe_ref, wkvb_ref, oin_ref, o_ref, kcat):
    del oin_ref

    def assemble(c, carry):
        rows = pl.ds(pl.multiple_of(c * KCAT_ROWS, KCAT_ROWS), KCAT_ROWS)
        kcat[rows, :LANES] = ckv_ref[rows, :]
        kcat[rows, LANES:] = kpe_ref[rows, :]
        return carry

    first = pl.program_id(1) == 0
    lax.fori_loop(0, jnp.where(first, DEC_SEQ // KCAT_ROWS, 0), assemble, 0)

    def context(c, carry):
        kcat[DEC_SEQ:, :LANES] = cckv_ref[...].astype(BF16)
        kcat[DEC_SEQ:, LANES:] = jnp.concatenate(
            [ckpe_ref[...], jnp.zeros((PAST_LEN, LANES - MLA_ROPE_DIM), F32)], axis=1).astype(BF16)
        return carry

    lax.fori_loop(0, jnp.where(first, 1, 0), context, 0)
    _mla_heads(mq_ref, wkvb_ref, kcat, o_ref)


def _mla_dec(l, mq, ckv, kpe, cckv, ckpe, w_kvb, o):
    return pl.pallas_call(
        _mla_dec_kernel,
        grid=(DEC_BATCH, DEC_TILES_PER_BATCH),
        in_specs=[pl.BlockSpec((TM, MQ_W), lambda b, j: (_dec_row(b, j), 0)),
                  pl.BlockSpec((DEC_SEQ, MLA_KV_RANK), lambda b, j: (N_CTX // DEC_SEQ + b, 0)),
                  pl.BlockSpec((DEC_SEQ, LANES), lambda b, j: (N_CTX // DEC_SEQ + b, 0)),
                  pl.BlockSpec((None, None, PAST_LEN, MLA_KV_RANK), lambda b, j: (b, l, 0, 0)),
                  pl.BlockSpec((None, None, PAST_LEN, MLA_ROPE_DIM), lambda b, j: (b, l, 0, 0)),
                  pl.BlockSpec((None, MLA_KV_RANK, MLA_HEADS * 2 * HEAD_DIM), lambda b, j: (l, 0, 0)),
                  pl.BlockSpec(memory_space=pl.ANY)],
        out_specs=pl.BlockSpec((TM, MLA_OUT), lambda b, j: (_dec_row(b, j), 0)),
        out_shape=jax.ShapeDtypeStruct((N_TOK, D_MODEL), BF16),
        scratch_shapes=[pltpu.VMEM((DEC_SEQ + PAST_LEN, 2 * LANES), BF16)],
        input_output_aliases={6: 0},
        compiler_params=_cparams(("arbitrary", "arbitrary")),
        name="mla_dec",
    )(mq, ckv, kpe, cckv, ckpe, w_kvb, o)


def _swa_dec_kernel(l, sink_ref, q_ref, k_ref, v_ref, ck_ref, cv_ref, oin_ref, o_ref):
    del oin_ref
    scale = HEAD_DIM ** -0.5
    j = pl.program_id(1)
    start = pl.multiple_of(jnp.clip(j * TM - SWA_WINDOW, 0, DEC_SEQ - SWA_KEYS), SWA_WINDOW)
    qpos = j * TM + lax.broadcasted_iota(jnp.int32, (TM, SWA_KEYS), 0)
    kpos = start + lax.broadcasted_iota(jnp.int32, (TM, SWA_KEYS), 1)
    band = jnp.abs(qpos - kpos) <= SWA_WINDOW
    for h in range(SWA_HEADS):
        kh = h // (SWA_HEADS // SWA_KV_HEADS)
        sl = slice(kh * HEAD_DIM, (kh + 1) * HEAD_DIM)
        q = q_ref[:, h * HEAD_DIM:(h + 1) * HEAD_DIM] * scale
        s_loc = jnp.where(band, _dot_nt(q, k_ref[pl.ds(start, SWA_KEYS), sl]), NEG)
        s_ctx = _dot_nt(q, ck_ref[:, sl])
        (p_loc, p_ctx), inv = _softmax_parts([s_loc, s_ctx], sink_ref[l, h])
        o_ref[:, h * HEAD_DIM:(h + 1) * HEAD_DIM] = (
            (_dot(p_loc, v_ref[pl.ds(start, SWA_KEYS), sl]) + _dot(p_ctx, cv_ref[:, sl])) * inv).astype(o_ref.dtype)


def _swa_dec(l, sink, sw, ck, cv, o):
    return pl.pallas_call(
        functools.partial(_swa_dec_kernel, l),
        grid=(DEC_BATCH, DEC_TILES_PER_BATCH),
        in_specs=[pl.BlockSpec(memory_space=pltpu.SMEM),
                  pl.BlockSpec((TM, SWA_OUT), lambda b, j: (_dec_row(b, j), 0)),
                  pl.BlockSpec((DEC_SEQ, LANES), lambda b, j: (N_CTX // DEC_SEQ + b, SWA_OUT // LANES)),
                  pl.BlockSpec((DEC_SEQ, LANES), lambda b, j: (N_CTX // DEC_SEQ + b, SWA_OUT // LANES + 1)),
                  pl.BlockSpec((None, None, PAST_LEN, LANES), lambda b, j: (b, l, 0, 0)),
                  pl.BlockSpec((None, None, PAST_LEN, LANES), lambda b, j: (b, l, 0, 0)),
                  pl.BlockSpec(memory_space=pl.ANY)],
        out_specs=pl.BlockSpec((TM, SWA_OUT), lambda b, j: (_dec_row(b, j), 1)),
        out_shape=jax.ShapeDtypeStruct((N_TOK, D_MODEL), BF16),
        input_output_aliases={6: 0},
        compiler_params=_cparams(("arbitrary", "arbitrary")),
        name="swa_dec",
    )(sink, sw, sw, sw, ck, cv, o)


def _route(sc, sel):
    rows = [sel[e:e + 1, :] for e in range(N_EXPERTS)]

    def beats(a, ia, b, ib):
        return (a > b) | ((a == b) & (ia < ib)) if ia < ib else (a > b)

    in_top = []
    gscore = []
    for g in range(N_EXPERT_GROUPS):
        mem = list(range(g * EXPERTS_PER_GROUP, (g + 1) * EXPERTS_PER_GROUP))
        acc = None
        for e in mem:
            rank = sum(beats(rows[o], o, rows[e], e).astype(jnp.int32) for o in mem if o != e)
            top = rank < 2
            in_top.append(top)
            term = jnp.where(top, rows[e], 0.0)
            acc = term if acc is None else acc + term
        gscore.append(acc)
    gates, chosen = [], []
    for g in range(N_EXPERT_GROUPS):
        lost = sum(beats(gscore[o], o, gscore[g], g).astype(jnp.int32) for o in range(N_EXPERT_GROUPS) if o != g)
        best = lost == 0
        for e in range(g * EXPERTS_PER_GROUP, (g + 1) * EXPERTS_PER_GROUP):
            pick = best & in_top[e]
            chosen.append(pick.astype(F32))
            gates.append(jnp.where(pick, sc[e:e + 1, :], 0.0))
    gate = jnp.concatenate(gates, axis=0)
    return gate / gate.sum(axis=0, keepdims=True), jnp.concatenate(chosen, axis=0)


def _post_kernel(x_ref, o_ref, mod_ref, wout_ref, g_ref, wr_ref, br_ref,
                 y_ref, h2_ref, rt_ref, rtm_ref, cnt_ref):
    attn = jnp.dot(o_ref[...].astype(BF16), wout_ref[...], preferred_element_type=F32)
    y = x_ref[...] + mod_ref[2:3, :] * attn
    y_ref[...] = y
    h2 = _rms(y, g_ref[...]) * (1.0 + mod_ref[4:5, :]) + mod_ref[3:4, :]
    h_hi = h2.astype(BF16)
    h_lo = (h2 - h_hi.astype(F32)).astype(BF16)
    w = wr_ref[...]
    w_hi = w.astype(BF16)
    w_lo = (w - w_hi.astype(F32)).astype(BF16)
    logits = (jnp.dot(h_hi, w_hi, preferred_element_type=F32) + jnp.dot(h_lo, w_hi, preferred_element_type=F32)
              + jnp.dot(h_hi, w_lo, preferred_element_type=F32))
    logits = logits.T[:N_EXPERTS, :]
    sc = 1.0 / (1.0 + jnp.exp(-logits))
    gate, chosen = _route(sc, sc + br_ref[...])
    h2_ref[...] = h_hi
    rt = jnp.concatenate([gate, chosen], axis=0)
    rt_ref[...] = rt
    rtm_ref[...] = jnp.concatenate([rt, jnp.zeros((LANES - 2 * N_EXPERTS, TM), F32)], axis=0).T
    cnt_ref[...] = jnp.broadcast_to(jnp.sum(chosen, axis=1, keepdims=True), (N_EXPERTS, LANES))


def _post(l, x, o, mods, w_out_bf, g_ffn, w_router_p, b_router):
    tile = lambda w: pl.BlockSpec((TM, w), lambda i: (i, 0))
    return pl.pallas_call(
        _post_kernel,
        grid=(N_TILES,),
        in_specs=[tile(D_MODEL), tile(D_MODEL),
                  pl.BlockSpec((None, None, N_MOD, D_MODEL), lambda i: (l, _cond_row(i), 0, 0)),
                  pl.BlockSpec((None, D_MODEL, D_MODEL), lambda i: (l, 0, 0)),
                  pl.BlockSpec((None, 1, D_MODEL), lambda i: (l, 0, 0)),
                  pl.BlockSpec((D_MODEL, LANES), lambda i: (0, 0)),
                  pl.BlockSpec((N_EXPERTS, 1), lambda i: (0, 0))],
        out_specs=[tile(D_MODEL), tile(D_MODEL), pl.BlockSpec((2 * N_EXPERTS, TM), lambda i: (0, i)),
                   tile(LANES), pl.BlockSpec((None, N_EXPERTS, LANES), lambda i: (i, 0, 0))],
        out_shape=[jax.ShapeDtypeStruct((N_TOK, D_MODEL), F32), jax.ShapeDtypeStruct((N_TOK, D_MODEL), BF16),
                   jax.ShapeDtypeStruct((2 * N_EXPERTS, N_TOK), F32),
                   jax.ShapeDtypeStruct((N_TOK, LANES), F32),
                   jax.ShapeDtypeStruct((N_TILES, N_EXPERTS, LANES), F32)],
        compiler_params=_cparams(("arbitrary",)),
        name="post_attn",
    )(x, o, mods, w_out_bf, g_ffn.reshape(DEPTH, 1, D_MODEL), w_router_p, b_router.reshape(N_EXPERTS, 1))


def _shr(x, bits):
    return lax.shift_right_logical(x, jnp.int32(bits))


TR_BITS = TR.bit_length() - 1
ALIGN_BITS = RUN_ALIGN.bit_length() - 1


def _plan_rows(cnt_ref, off_ref, npass_ref, seg_ref):
    def per_expert(e, row0):
        def per_tile(bb, r):
            off_ref[bb * N_EXPERTS + e] = r
            return r + (_shr(cnt_ref[bb * N_EXPERTS + e] + (RUN_ALIGN - 1), ALIGN_BITS) << ALIGN_BITS)

        rows_end = lax.fori_loop(0, N_TILES, per_tile, row0)
        n = _shr(rows_end - row0 + (TR - 1), TR_BITS)
        seg_ref[e] = _shr(row0, TR_BITS)
        seg_ref[N_EXPERTS + e] = n
        seg_ref[2 * N_EXPERTS + e] = rows_end
        seg_ref[3 * N_EXPERTS + e] = row0 + (n << TR_BITS)
        return row0 + (n << TR_BITS)

    end_row = lax.fori_loop(0, N_EXPERTS, per_expert, jnp.int32(0))
    seg_ref[4 * N_EXPERTS] = _shr(end_row, TR_BITS)

    def longest(bb, carry):
        m = lax.fori_loop(0, N_EXPERTS, lambda e, m: jnp.maximum(m, cnt_ref[bb * N_EXPERTS + e]), jnp.int32(0))
        npass_ref[bb] = sum((m > k * WIN).astype(jnp.int32) for k in range(-(-TM // WIN)))
        return carry

    lax.fori_loop(0, N_TILES, longest, 0)


def _dispatch_kernel(cnt_ref, h_ref, rt_ref, et_ref, xs_in, xs_hbm, off_ref, npass_ref, seg_ref, zbuf, sem):
    del xs_in
    b = pl.program_id(0)
    slot = b % 2

    @pl.when(b == 0)
    def _():
        _plan_rows(cnt_ref, off_ref, npass_ref, seg_ref)

    key = jnp.dot(et_ref[...], _run_keys(rt_ref[...], 1).astype(BF16), preferred_element_type=F32)
    slot_j = _window_slot((N_EXPERTS * WIN, TM), 0)

    def run_copies(bb, sl, p, act):
        def one_expert(e, queue):
            left = cnt_ref[bb * N_EXPERTS + e] - p * WIN
            row = off_ref[bb * N_EXPERTS + e] + p * WIN
            whole = row + WIN <= seg_ref[2 * N_EXPERTS + e]

            @pl.when(whole & (left > 0))
            def _():
                src = pl.multiple_of(e * WIN, RUN_ALIGN)
                act(pltpu.make_async_copy(zbuf.at[sl, pl.ds(src, WIN)],
                                          xs_hbm.at[pl.ds(pl.multiple_of(row, RUN_ALIGN), WIN)], sem), queue)

            def piece(k, c):
                src = pl.multiple_of(e * WIN + k * RUN_ALIGN, RUN_ALIGN)
                dst = pl.multiple_of(row + k * RUN_ALIGN, RUN_ALIGN)
                act(pltpu.make_async_copy(zbuf.at[sl, pl.ds(src, RUN_ALIGN)],
                                          xs_hbm.at[pl.ds(dst, RUN_ALIGN)], sem), queue)
                return c

            pieces = jnp.minimum(_shr(jnp.maximum(left, 0) + (RUN_ALIGN - 1), ALIGN_BITS), WIN // RUN_ALIGN)
            lax.fori_loop(0, jnp.where(whole, 0, pieces), piece, 0)

        def expert_pair(e2, carry):
            one_expert(2 * e2, 0)
            one_expert(2 * e2 + 1, 1)
            return carry

        lax.fori_loop(0, N_EXPERTS // 2, expert_pair, 0)

    def fill_and_send(p):
        pick = jnp.where(key == slot_j + jnp.asarray(p * WIN, F32), 1.0, 0.0).astype(BF16)
        zbuf[slot] = jnp.dot(pick, h_ref[...], preferred_element_type=F32).astype(BF16)
        run_copies(b, slot, p, lambda cp, queue: cp.start(priority=queue))

    @pl.when(b > 0)
    def _():
        run_copies(b - 1, 1 - slot, npass_ref[jnp.maximum(b - 1, 0)] - 1, lambda cp, queue: cp.wait())

    fill_and_send(0)

    def more(p, carry):
        run_copies(b, slot, p - 1, lambda cp, queue: cp.wait())
        fill_and_send(p)
        return carry

    lax.fori_loop(1, npass_ref[b], more, 0)

    @pl.when(b == N_TILES - 1)
    def _():
        run_copies(b, slot, npass_ref[b] - 1, lambda cp, queue: cp.wait())


def _dispatch(cnt, h2, rt, et, xs_buf):
    smem = pl.BlockSpec(memory_space=pltpu.SMEM)
    return pl.pallas_call(
        _dispatch_kernel,
        grid_spec=pltpu.PrefetchScalarGridSpec(
            num_scalar_prefetch=1,
            grid=(N_TILES,),
            in_specs=[pl.BlockSpec((TM, D_MODEL), lambda i, *_: (i, 0)),
                      pl.BlockSpec((2 * N_EXPERTS, TM), lambda i, *_: (0, i)),
                      pl.BlockSpec((N_EXPERTS * WIN, 2 * N_EXPERTS), lambda i, *_: (0, 0)),
                      pl.BlockSpec(memory_space=pl.ANY)],
            out_specs=[pl.BlockSpec(memory_space=pl.ANY), smem, smem, smem],
            scratch_shapes=[pltpu.VMEM((2, N_EXPERTS * WIN, D_MODEL), BF16), pltpu.SemaphoreType.DMA(())]),
        out_shape=[jax.ShapeDtypeStruct((NT * TR, D_MODEL), BF16), jax.ShapeDtypeStruct((N_RUNS,), jnp.int32),
                   jax.ShapeDtypeStruct((N_TILES,), jnp.int32),
                   jax.ShapeDtypeStruct((4 * N_EXPERTS + 1,), jnp.int32)],
        input_output_aliases={4: 0},
        compiler_params=_cparams(("arbitrary",)),
        name="dispatch",
    )(cnt, h2, rt, et, xs_buf)


CAST_ROWS = 128
TILE_BUFS = 2


def _cast_rows(src_ref, dst_ref, n):
    def body(c, carry):
        rows = pl.ds(pl.multiple_of(c * CAST_ROWS, CAST_ROWS), CAST_ROWS)
        dst_ref[rows, :] = src_ref[rows, :].astype(BF16)
        return carry

    lax.fori_loop(0, n, body, 0)


def _experts_kernel(seg_ref, xs_hbm, wg_ref, wu_ref, wd_ref, ys_in, ys_hbm, wgb, wub, wdb, xbuf, ybuf, semx, semy):
    del ys_in
    e = pl.program_id(0)
    t0 = seg_ref[e]
    n = seg_ref[N_EXPERTS + e]
    _cast_rows(wg_ref, wgb, D_MODEL // CAST_ROWS)
    _cast_rows(wu_ref, wub, D_MODEL // CAST_ROWS)
    _cast_rows(wd_ref, wdb, D_EXPERT // CAST_ROWS)

    def rows(k):
        return pl.ds(pl.multiple_of((t0 + k) * TR, TR), TR)

    def fetch(k, s):
        return pltpu.make_async_copy(xs_hbm.at[rows(k)], xbuf.at[s], semx.at[s])

    def put(k, s):
        return pltpu.make_async_copy(ybuf.at[s], ys_hbm.at[rows(k)], semy.at[s])

    for j in range(TILE_BUFS - 1):
        @pl.when(j < n)
        def _():
            fetch(j, j).start()

    def tile(k, carry):
        s = k % TILE_BUFS
        ahead = k + (TILE_BUFS - 1)

        @pl.when(ahead < n)
        def _():
            fetch(ahead, ahead % TILE_BUFS).start()

        fetch(k, s).wait()

        @pl.when(k >= TILE_BUFS)
        def _():
            put(k - TILE_BUFS, s).wait()

        x = xbuf[s]
        hg = jnp.dot(x, wgb[...], preferred_element_type=F32)
        hu = jnp.dot(x, wub[...], preferred_element_type=F32)
        a = hg * (1.0 / (1.0 + jnp.exp(-hg))) * hu
        ybuf[s] = jnp.dot(a.astype(BF16), wdb[...], preferred_element_type=F32).astype(BF16)
        put(k, s).start(priority=1)
        return carry

    lax.fori_loop(0, n, tile, 0)

    for j in range(TILE_BUFS):
        last = n - TILE_BUFS + j

        @pl.when(last >= 0)
        def _():
            put(last, last % TILE_BUFS).wait()


def _experts(l, seg, xs, w_gate, w_up, w_down, ys_buf):
    wspec = lambda a, b: pl.BlockSpec((None, None, a, b), lambda e, seg: (l, e, 0, 0))
    tile_buf = pltpu.VMEM((TILE_BUFS, TR, D_MODEL), BF16)
    return pl.pallas_call(
        _experts_kernel,
        grid_spec=pltpu.PrefetchScalarGridSpec(
            num_scalar_prefetch=1,
            grid=(N_EXPERTS,),
            in_specs=[pl.BlockSpec(memory_space=pl.ANY),
                      wspec(D_MODEL, D_EXPERT), wspec(D_MODEL, D_EXPERT), wspec(D_EXPERT, D_MODEL),
                      pl.BlockSpec(memory_space=pl.ANY)],
            out_specs=pl.BlockSpec(memory_space=pl.ANY),
            scratch_shapes=[pltpu.VMEM((D_MODEL, D_EXPERT), BF16), pltpu.VMEM((D_MODEL, D_EXPERT), BF16),
                            pltpu.VMEM((D_EXPERT, D_MODEL), BF16), tile_buf, tile_buf,
                            pltpu.SemaphoreType.DMA((TILE_BUFS,)), pltpu.SemaphoreType.DMA((TILE_BUFS,))]),
        out_shape=jax.ShapeDtypeStruct((NT * TR, D_MODEL), BF16),
        input_output_aliases={5: 0},
        compiler_params=_cparams(("arbitrary",)),
        name="experts",
    )(seg, xs, w_gate, w_up, w_down, ys_buf)


def _final_kernel(off_ref, npass_ref, y_ref, rtm_ref, ek_ref, eg_ref, ys_hbm, mod_ref, g_ref, op_ref, os_ref,
                  wbuf, sem):
    i = pl.program_id(0)
    moe = _combined_moe(off_ref, npass_ref, rtm_ref, ek_ref, eg_ref, ys_hbm, wbuf, sem)
    out = _rms(y_ref[...] + mod_ref[5:6, :] * moe, g_ref[...])

    @pl.when(i < CTX_TILES)
    def _():
        op_ref[...] = out

    @pl.when(i >= CTX_TILES)
    def _():
        os_ref[...] = out


def _final(off, npass, y, rtm, ek, eg, ys, mods, g_final):
    return pl.pallas_call(
        _final_kernel,
        grid_spec=pltpu.PrefetchScalarGridSpec(
            num_scalar_prefetch=2,
            grid=(N_TILES,),
            in_specs=_combine_specs() + [
                pl.BlockSpec((None, None, N_MOD, D_MODEL), lambda i, *_: (DEPTH - 1, _cond_row(i), 0, 0)),
                pl.BlockSpec((1, D_MODEL), lambda i, *_: (0, 0))],
            out_specs=[pl.BlockSpec((TM, D_MODEL), lambda i, *_: (jnp.minimum(i, CTX_TILES - 1), 0)),
                       pl.BlockSpec((TM, D_MODEL), lambda i, *_: (jnp.maximum(i - CTX_TILES, 0), 0))],
            scratch_shapes=_combine_scratch()),
        out_shape=[jax.ShapeDtypeStruct((N_CTX, D_MODEL), F32), jax.ShapeDtypeStruct((N_DEC, D_MODEL), F32)],
        compiler_params=_cparams(("arbitrary",)),
        name="final_norm",
    )(off, npass, y, rtm, ek, eg, ys, mods, g_final.reshape(1, D_MODEL))


def _prep_kernel(win_ref, wout_ref, wi_ref, wo_ref):
    w = win_ref[...]
    split = KPE_OFF + MLA_ROPE_DIM
    wi_ref[...] = jnp.concatenate([w[:, :split], jnp.zeros((CAST_ROWS, SW_OFF - split), F32), w[:, split:]],
                                  axis=1).astype(BF16)
    wo_ref[...] = wout_ref[...].astype(BF16)


def _prep_weights(w_in, w_out):
    n = D_MODEL // CAST_ROWS
    shift = NA_OUT // CAST_ROWS
    return pl.pallas_call(
        _prep_kernel,
        grid=(DEPTH, n),
        in_specs=[pl.BlockSpec((None, CAST_ROWS, IN_WIDTH), lambda l, j: (l, j, 0)),
                  pl.BlockSpec((None, CAST_ROWS, D_MODEL), lambda l, j: (l, (j + shift) % n, 0))],
        out_specs=[pl.BlockSpec((None, CAST_ROWS, IN_PAD), lambda l, j: (l, j, 0)),
                   pl.BlockSpec((None, CAST_ROWS, D_MODEL), lambda l, j: (l, j, 0))],
        out_shape=[jax.ShapeDtypeStruct((DEPTH, D_MODEL, IN_PAD), BF16),
                   jax.ShapeDtypeStruct((DEPTH, D_MODEL, D_MODEL), BF16)],
        compiler_params=_cparams(("arbitrary", "arbitrary")),
        name="prep_weights",
    )(w_in, w_out)


def _rope_tables(rot_dim):
    t = jnp.arange(DEC_SEQ, dtype=jnp.int32)
    row = (t // GRID_W).astype(F32)
    col = (t % GRID_W).astype(F32)
    per_axis = rot_dim // 2
    inv = ROPE_BASE ** (-jnp.arange(0, per_axis, 2, dtype=F32) / per_axis)
    ang = jnp.concatenate([row[:, None] * inv, col[:, None] * inv], axis=-1)
    cos, sin = jnp.cos(ang), jnp.sin(ang)
    zero = jnp.zeros_like(sin)
    rep = LANES // rot_dim
    tabs = [jnp.concatenate([cos, cos], -1), jnp.concatenate([zero, sin], -1), jnp.concatenate([-sin, zero], -1)]
    return jnp.stack([jnp.tile(a, (1, rep)) for a in tabs])


def kernel(x_prompt, x_sample, c, cache_na_k, cache_na_v, cache_mla_ckv, cache_mla_kpe, cache_swa_k, cache_swa_v,
           c_ctx, w_ada, b_ada, g_attn, w_in, g_mla_q, w_mla_qb, g_mla_kv, w_mla_kvb, na_rpb, swa_sink, w_out,
           g_ffn, w_router, b_router, w_gate, w_up, w_down, g_final):
    cond = jnp.concatenate([c_ctx[None], c, jnp.zeros((COND_ROWS - 1 - DEC_BATCH, D_MODEL), F32)], axis=0)
    mods = _ada(cond, w_ada, b_ada).reshape(DEPTH, COND_ROWS, N_MOD, D_MODEL)
    bias = _na_bias(na_rpb)
    t_mla = _rope_tables(MLA_ROPE_DIM)
    t_swa = _rope_tables(HEAD_DIM)

    w_in_p, w_out_p = _prep_weights(w_in, w_out)
    wq = w_mla_qb.reshape(DEPTH, MLA_Q_RANK, MLA_HEADS, MLA_QK_DIM)
    w_rope = jnp.pad(wq[..., MLA_NOPE_DIM:], ((0, 0), (0, 0), (0, 0), (0, LANES - MLA_ROPE_DIM)))
    w_qb_p = jnp.concatenate([wq[..., :MLA_NOPE_DIM].reshape(DEPTH, MLA_Q_RANK, MQ_NOPE),
                              w_rope.reshape(DEPTH, MLA_Q_RANK, MLA_HEADS * LANES)], axis=-1).astype(BF16)
    w_router_p = jnp.pad(w_router, ((0, 0), (0, LANES - N_EXPERTS)))
    et, ek, eg = _spread_consts()

    c_na_k = cache_na_k.reshape(DEC_BATCH, DEPTH, PAST_LEN, NA_OUT)
    c_na_v = cache_na_v.reshape(DEC_BATCH, DEPTH, PAST_LEN, NA_OUT)
    c_sw_k = cache_swa_k.reshape(DEC_BATCH, DEPTH, PAST_LEN, LANES)
    c_sw_v = cache_swa_v.reshape(DEC_BATCH, DEPTH, PAST_LEN, LANES)

    y = ys = rtm = off = npass = None
    caches = [jnp.zeros((BATCH, DEPTH, SEQ, w), F32) for w in CACHE_WIDTHS]
    o = jnp.zeros((N_TOK, D_MODEL), BF16)
    xs = jnp.zeros((NT * TR, D_MODEL), BF16)
    ys = jnp.zeros((NT * TR, D_MODEL), BF16)
    for l in range(DEPTH):
        if l == 0:
            srcs = [x_prompt.reshape(N_CTX, D_MODEL), x_sample.reshape(N_DEC, D_MODEL)]
        else:
            srcs = [off, npass, y, rtm, ek, eg, ys]
        x, na, mq, ckv, kpe, sw, *caches = _pre(l == 0, l, srcs, caches, mods, g_attn, w_in_p, g_mla_q, w_qb_p,
                                                g_mla_kv, t_mla, t_swa)
        o = _ctx_attn(l, swa_sink, na, mq, ckv, kpe, sw, w_mla_kvb, o)
        o = _mla_dec(l, mq, ckv, kpe, cache_mla_ckv, cache_mla_kpe, w_mla_kvb, o)
        o = _swa_dec(l, swa_sink, sw, c_sw_k, c_sw_v, o)
        o = _na_dec(l, na, c_na_k, c_na_v, bias, o)
        y, h2, rt, rtm, cnt = _post(l, x, o, mods, w_out_p, g_ffn, w_router_p, b_router)
        xs, off, npass, seg = _dispatch(cnt[:, :, 0].astype(jnp.int32).reshape(-1), h2, rt, et, xs)
        ys = _experts(l, seg, xs, w_gate, w_up, w_down, ys)
    y_prompt, y_sample = _final(off, npass, y, rtm, ek, eg, ys, mods, g_final)

    heads = lambda a, n: a.reshape(BATCH, DEPTH, SEQ, n, HEAD_DIM)
    return (y_prompt.reshape(BATCH, SEQ, D_MODEL), y_sample.reshape(DEC_BATCH, DEC_SEQ, D_MODEL),
            heads(caches[0], NA_HEADS), heads(caches[1], NA_HEADS), caches[2], caches[3],
            heads(caches[4], SWA_KV_HEADS), heads(caches[5], SWA_KV_HEADS))
```

```python
import functools

import jax
import jax.numpy as jnp
from jax import lax
from jax.experimental import pallas as pl
from jax.experimental.pallas import tpu as pltpu

D_MODEL = 1024
BATCH = 16
SEQ = 256
DEPTH = 4
DEC_BATCH = 2
DEC_SEQ = 2048
PAST_LEN = 256
GRID_W = 64
HEAD_DIM = 64
NA_HEADS = 4
NA_WIN_H = 8
NA_WIN_W = 16
MLA_HEADS = 6
MLA_Q_RANK = 256
MLA_KV_RANK = 128
MLA_NOPE_DIM = 64
MLA_ROPE_DIM = 32
MLA_V_DIM = 64
MLA_QK_DIM = MLA_NOPE_DIM + MLA_ROPE_DIM
SWA_HEADS = 6
SWA_KV_HEADS = 2
SWA_WINDOW = 128
ROPE_BASE = 10000.0
N_EXPERTS = 16
N_EXPERT_GROUPS = 4
EXPERTS_PER_GROUP = 4
D_EXPERT = 512
RMS_EPS = 1e-6
N_MOD = 6

NA_IN = 3 * NA_HEADS * HEAD_DIM
MLA_IN = MLA_Q_RANK + MLA_KV_RANK + MLA_ROPE_DIM
SWA_IN = (SWA_HEADS + 2 * SWA_KV_HEADS) * HEAD_DIM
IN_WIDTH = NA_IN + MLA_IN + SWA_IN
NA_OUT = NA_HEADS * HEAD_DIM
MLA_OUT = MLA_HEADS * MLA_V_DIM
SWA_OUT = SWA_HEADS * HEAD_DIM

LANES = 128
N_CTX = BATCH * SEQ
N_DEC = DEC_BATCH * DEC_SEQ
N_TOK = N_CTX + N_DEC
TM = 256
N_TILES = N_TOK // TM
CTX_TILES = N_CTX // TM
DEC_TILES_PER_BATCH = DEC_SEQ // TM
COND_ROWS = 8
KPE_OFF = NA_IN + MLA_Q_RANK + MLA_KV_RANK
SW_OFF = KPE_OFF + LANES
IN_PAD = SW_OFF + SWA_IN
MQ_NOPE = MLA_HEADS * MLA_NOPE_DIM
MQ_W = MQ_NOPE + MLA_HEADS * LANES
MLA_SCALE = MLA_QK_DIM ** -0.5
MLA_GROUP = 3
NA_ROWS_PER_STEP = TM // GRID_W
NA_KEY_ROWS = 12
NA_KEYS = NA_KEY_ROWS * GRID_W
SWA_KEYS = 512
NEG = -1e30
TR = 512
N_ASSIGN = 2 * N_TOK
RUN_ALIGN = 16
WIN = 80
N_RUNS = N_TILES * N_EXPERTS
NT = -(-(N_ASSIGN + N_RUNS * (RUN_ALIGN - 1) + N_EXPERTS * (TR - 1) + WIN) // TR)

F32 = jnp.float32
BF16 = jnp.bfloat16
VMEM_LIMIT = 56 * 1024 * 1024


def _cparams(sem):
    return pltpu.CompilerParams(dimension_semantics=sem, vmem_limit_bytes=VMEM_LIMIT)


def _cond_row(i):
    return jnp.where(i < CTX_TILES, 0, 1 + (i - CTX_TILES) // DEC_TILES_PER_BATCH)


def _rope_blk(i):
    return jnp.where(i < CTX_TILES, 0, (i - CTX_TILES) % DEC_TILES_PER_BATCH)


def _rms(x, g):
    ms = jnp.mean(x * x, axis=-1, keepdims=True)
    return x * lax.rsqrt(ms + RMS_EPS) * g


def _dot(a, b):
    return jnp.dot(a.astype(BF16), b.astype(BF16), preferred_element_type=F32)


def _dot_nt(a, b):
    return lax.dot_general(a.astype(BF16), b.astype(BF16), (((1,), (1,)), ((), ())),
                           preferred_element_type=F32)


def _ada_kernel(c_ref, w_ref, b_ref, o_ref):
    c = c_ref[...]
    s = c * (1.0 / (1.0 + jnp.exp(-c)))
    o_ref[...] = _dot(s, w_ref[...]) + b_ref[...]


def _ada(cond, w_ada, b_ada):
    tn = 1536
    n = N_MOD * D_MODEL
    return pl.pallas_call(
        _ada_kernel,
        grid=(DEPTH, n // tn),
        in_specs=[pl.BlockSpec((COND_ROWS, D_MODEL), lambda l, j: (0, 0)),
                  pl.BlockSpec((None, D_MODEL, tn), lambda l, j: (l, 0, j)),
                  pl.BlockSpec((None, 1, tn), lambda l, j: (l, 0, j))],
        out_specs=pl.BlockSpec((None, COND_ROWS, tn), lambda l, j: (l, 0, j)),
        out_shape=jax.ShapeDtypeStruct((DEPTH, COND_ROWS, n), F32),
        compiler_params=_cparams(("arbitrary", "arbitrary")),
        name="ada_mod",
    )(cond, w_ada, b_ada.reshape(DEPTH, 1, n))


def _bias_kernel(rpb_ref, o_ref):
    g = pl.program_id(0)
    base = g * ((2 * NA_WIN_H - 1) * (2 * NA_WIN_W - 1))
    qc = lax.broadcasted_iota(jnp.int32, (GRID_W, GRID_W), 0)
    kc = lax.broadcasted_iota(jnp.int32, (GRID_W, GRID_W), 1)
    dc = jnp.clip(kc - qc + (NA_WIN_W - 1), 0, 2 * NA_WIN_W - 2)
    cs = jnp.clip(qc - NA_WIN_W // 2, 0, GRID_W - NA_WIN_W)
    col_ok = (kc >= cs) & (kc < cs + NA_WIN_W)
    neg = jnp.full((GRID_W, GRID_W), NEG, F32)
    tabs = []
    for a in range(2 * NA_WIN_H - 1):
        t = jnp.zeros((GRID_W, GRID_W), F32)
        for b in range(2 * NA_WIN_W - 1):
            t = jnp.where(dc == b, rpb_ref[base + a * (2 * NA_WIN_W - 1) + b], t)
        tabs.append(jnp.where(col_ok, t, NEG))
    for p in range(3):
        for qi in range(NA_ROWS_PER_STEP):
            for kj in range(NA_KEY_ROWS):
                if p == 0:
                    ok, dr = kj < NA_WIN_H, kj - qi + 7
                elif p == 1:
                    ok, dr = qi <= kj < qi + NA_WIN_H, kj - qi + 3
                else:
                    ok, dr = kj >= NA_KEY_ROWS - NA_WIN_H, kj - qi - 1
                blk = tabs[dr] if ok else neg
                o_ref[p, qi * GRID_W:(qi + 1) * GRID_W, kj * GRID_W:(kj + 1) * GRID_W] = blk


def _na_bias(na_rpb):
    return pl.pallas_call(
        _bias_kernel,
        grid=(DEPTH * NA_HEADS,),
        in_specs=[pl.BlockSpec(memory_space=pltpu.SMEM)],
        out_specs=pl.BlockSpec((None, 3, None, TM, NA_KEYS),
                               lambda g: (g // NA_HEADS, 0, g % NA_HEADS, 0, 0)),
        out_shape=jax.ShapeDtypeStruct((DEPTH, 3, NA_HEADS, TM, NA_KEYS), F32),
        compiler_params=_cparams(("arbitrary",)),
        name="na_bias",
    )(na_rpb.reshape(-1))


def _rope128(x, t_ref, half):
    return (x * t_ref[0] + pltpu.roll(x, half, 1) * t_ref[1]
            + pltpu.roll(x, LANES - half, 1) * t_ref[2])


def _spread_consts():
    slot_e = jnp.arange(N_EXPERTS * WIN, dtype=jnp.int32) // WIN
    idx32 = jnp.arange(2 * N_EXPERTS, dtype=jnp.int32)
    idx128 = jnp.arange(LANES, dtype=jnp.int32)
    et = (idx32[None, :] == slot_e[:, None] + N_EXPERTS).astype(BF16)
    ek = (idx128[:, None] == slot_e[None, :] + N_EXPERTS).astype(BF16)
    eg = (idx128[:, None] == slot_e[None, :]).astype(BF16)
    return et, ek, eg


def _window_slot(shape, axis):
    r = lax.broadcasted_iota(jnp.int32, shape, axis).astype(F32)
    return r - WIN * jnp.floor((r + 0.5) * (1.0 / WIN))


def _run_keys(chosen, token_axis):
    a = lax.broadcasted_iota(jnp.int32, (TM, TM), 0)
    b = lax.broadcasted_iota(jnp.int32, (TM, TM), 1)
    if token_axis == 1:
        rank = jnp.dot(chosen.astype(BF16), (a < b).astype(BF16), preferred_element_type=F32)
    else:
        rank = jnp.dot((b < a).astype(BF16), chosen.astype(BF16), preferred_element_type=F32)
    return jnp.where(chosen > 0.5, rank, -1.0)


def _windows_start(i, slot, p, off_ref, ys_hbm, wbuf, sem):
    for e in range(N_EXPERTS):
        row = pl.multiple_of(off_ref[i * N_EXPERTS + e] + p * WIN, RUN_ALIGN)
        pltpu.make_async_copy(ys_hbm.at[pl.ds(row, WIN)], wbuf.at[slot, pl.ds(e * WIN, WIN)],
                              sem.at[slot]).start(priority=e % 2)


def _windows_wait(slot, ys_hbm, wbuf, sem):
    pltpu.make_async_copy(ys_hbm.at[pl.ds(0, N_EXPERTS * WIN)], wbuf.at[slot], sem.at[slot]).wait()


def _combined_moe(off_ref, npass_ref, rtm_ref, ek_ref, eg_ref, ys_hbm, wbuf, sem):
    i = pl.program_id(0)
    slot = i % 2

    @pl.when(i == 0)
    def _():
        _windows_start(0, 0, 0, off_ref, ys_hbm, wbuf, sem)

    @pl.when(i + 1 < N_TILES)
    def _():
        _windows_start(i + 1, 1 - slot, 0, off_ref, ys_hbm, wbuf, sem)

    r = rtm_ref[...]
    key = jnp.dot(_run_keys(r, 0).astype(BF16), ek_ref[...], preferred_element_type=F32)
    gate = jnp.dot(r.astype(BF16), eg_ref[...], preferred_element_type=F32)
    slot_j = _window_slot((TM, N_EXPERTS * WIN), 1)

    def contrib(p):
        g = jnp.where(key == slot_j + jnp.asarray(p * WIN, F32), gate, 0.0).astype(BF16)
        return jnp.dot(g, wbuf[slot], preferred_element_type=F32)

    _windows_wait(slot, ys_hbm, wbuf, sem)
    acc = contrib(0)

    def extra(p, acc):
        _windows_start(i, slot, p, off_ref, ys_hbm, wbuf, sem)
        _windows_wait(slot, ys_hbm, wbuf, sem)
        return acc + contrib(p)

    return lax.fori_loop(1, npass_ref[i], extra, acc)


def _pre_kernel(first, *refs):
    i = pl.program_id(0)
    if first:
        (xp_ref, xs_ref, mod_ref, g_ref, win_ref, gq_ref, wqb_ref, gkv_ref, tm_ref, ts_ref, _, _, _, _, _, _,
         xo_ref, na_ref, mq_ref, ckv_ref, kpe_ref, sw_ref, *cache_refs) = refs
        x = jnp.where(i < CTX_TILES, xp_ref[...], xs_ref[...])
    else:
        (off_ref, npass_ref, y_ref, rtm_ref, ek_ref, eg_ref, ys_hbm, modp_ref, mod_ref, g_ref, win_ref, gq_ref,
         wqb_ref, gkv_ref, tm_ref, ts_ref, _, _, _, _, _, _,
         xo_ref, na_ref, mq_ref, ckv_ref, kpe_ref, sw_ref, *rest) = refs
        *cache_refs, wbuf, sem = rest
        moe = _combined_moe(off_ref, npass_ref, rtm_ref, ek_ref, eg_ref, ys_hbm, wbuf, sem)
        x = y_ref[...] + modp_ref[5:6, :] * moe
    xo_ref[...] = x
    cnk_ref, cnv_ref, cckv_ref, ckpe_ref, csk_ref, csv_ref = cache_refs

    h = _rms(x, g_ref[...]) * (1.0 + mod_ref[1:2, :]) + mod_ref[0:1, :]
    z = jnp.dot(h.astype(BF16), win_ref[...], preferred_element_type=F32)
    na_ref[...] = z[:, :NA_IN].astype(BF16)
    cq = _rms(z[:, NA_IN:NA_IN + MLA_Q_RANK], gq_ref[...])
    ckv = _rms(z[:, NA_IN + MLA_Q_RANK:KPE_OFF], gkv_ref[...])
    ckv_ref[...] = ckv.astype(BF16)
    mq = jnp.dot(cq.astype(BF16), wqb_ref[...], preferred_element_type=F32)
    kpe = z[:, KPE_OFF:SW_OFF]
    sw = z[:, SW_OFF:IN_PAD]
    mq_ref[:, :MQ_NOPE] = mq[:, :MQ_NOPE].astype(BF16)
    sw_ref[:, SWA_OUT + LANES:] = sw[:, SWA_OUT + LANES:].astype(BF16)

    @pl.when(i < CTX_TILES)
    def _():
        mq_ref[:, MQ_NOPE:] = (mq[:, MQ_NOPE:] * MLA_SCALE).astype(BF16)
        kpe_ref[...] = kpe.astype(BF16)
        sw_ref[:, :SWA_OUT + LANES] = sw[:, :SWA_OUT + LANES].astype(BF16)
        cnk_ref[...] = z[:, NA_OUT:2 * NA_OUT]
        cnv_ref[...] = z[:, 2 * NA_OUT:NA_IN]
        cckv_ref[...] = ckv
        ckpe_ref[...] = kpe[:, :MLA_ROPE_DIM]
        csk_ref[...] = sw[:, SWA_OUT:SWA_OUT + LANES]
        csv_ref[...] = sw[:, SWA_OUT + LANES:]

    @pl.when(i >= CTX_TILES)
    def _():
        for c in range(MQ_NOPE // LANES, MQ_W // LANES):
            roped = _rope128(mq[:, c * LANES:(c + 1) * LANES], tm_ref, MLA_ROPE_DIM // 2)
            mq_ref[:, c * LANES:(c + 1) * LANES] = (roped * MLA_SCALE).astype(BF16)
        kpe_ref[...] = _rope128(kpe, tm_ref, MLA_ROPE_DIM // 2).astype(BF16)
        for c in range((SWA_OUT + LANES) // LANES):
            sw_ref[:, c * LANES:(c + 1) * LANES] = _rope128(sw[:, c * LANES:(c + 1) * LANES], ts_ref,
                                                            HEAD_DIM // 2).astype(BF16)


def _combine_scratch():
    return [pltpu.VMEM((2, N_EXPERTS * WIN, D_MODEL), BF16), pltpu.SemaphoreType.DMA((2,))]


def _combine_specs():
    spread = pl.BlockSpec((LANES, N_EXPERTS * WIN), lambda i, *_: (0, 0))
    return [pl.BlockSpec((TM, D_MODEL), lambda i, *_: (i, 0)),
            pl.BlockSpec((TM, LANES), lambda i, *_: (i, 0)),
            spread, spread, pl.BlockSpec(memory_space=pl.ANY)]


CACHE_WIDTHS = (NA_OUT, NA_OUT, MLA_KV_RANK, MLA_ROPE_DIM, LANES, LANES)


def _pre(first, l, xs, caches, mods, g_attn, w_in_p, g_mla_q, w_qb_p, g_mla_kv, t_mla, t_swa):
    tile = lambda w: pl.BlockSpec((TM, w), lambda i, *_: (i, 0))
    ctx_tile = lambda w: pl.BlockSpec((TM, w), lambda i, *_: (jnp.minimum(i, CTX_TILES - 1), 0))
    slab = lambda w: pl.BlockSpec((None, None, SEQ, w), lambda i, *_: (jnp.minimum(i, BATCH - 1), l, 0, 0))
    mod_spec = lambda ll: pl.BlockSpec((None, None, N_MOD, D_MODEL), lambda i, *_: (ll, _cond_row(i), 0, 0))
    vec = lambda w: pl.BlockSpec((None, 1, w), lambda i, *_: (l, 0, 0))
    if first:
        in_specs = [ctx_tile(D_MODEL), pl.BlockSpec((TM, D_MODEL), lambda i: (jnp.maximum(i - CTX_TILES, 0), 0))]
    else:
        in_specs = _combine_specs() + [mod_spec(l - 1)]
    in_specs += [mod_spec(l), vec(D_MODEL),
                 pl.BlockSpec((None, D_MODEL, IN_PAD), lambda i, *_: (l, 0, 0)),
                 vec(MLA_Q_RANK),
                 pl.BlockSpec((None, MLA_Q_RANK, MQ_W), lambda i, *_: (l, 0, 0)),
                 vec(MLA_KV_RANK),
                 pl.BlockSpec((3, TM, LANES), lambda i, *_: (0, _rope_blk(i), 0)),
                 pl.BlockSpec((3, TM, LANES), lambda i, *_: (0, _rope_blk(i), 0))]
    in_specs += [pl.BlockSpec(memory_space=pl.ANY)] * len(CACHE_WIDTHS)
    widths = [D_MODEL, NA_IN, MQ_W, MLA_KV_RANK, LANES, SWA_IN]
    args = list(xs) + ([mods] if not first else []) + [
        mods, g_attn.reshape(DEPTH, 1, D_MODEL), w_in_p, g_mla_q.reshape(DEPTH, 1, MLA_Q_RANK), w_qb_p,
        g_mla_kv.reshape(DEPTH, 1, MLA_KV_RANK), t_mla, t_swa]
    aliases = {len(args) + n: len(widths) + n for n in range(len(CACHE_WIDTHS))}
    return pl.pallas_call(
        functools.partial(_pre_kernel, first),
        grid_spec=pltpu.PrefetchScalarGridSpec(
            num_scalar_prefetch=0 if first else 2,
            grid=(N_TILES,),
            in_specs=in_specs,
            out_specs=[tile(w) for w in widths] + [slab(w) for w in CACHE_WIDTHS],
            scratch_shapes=[] if first else _combine_scratch()),
        out_shape=([jax.ShapeDtypeStruct((N_TOK, w), BF16 if n else F32) for n, w in enumerate(widths)]
                   + [jax.ShapeDtypeStruct((BATCH, DEPTH, SEQ, w), F32) for w in CACHE_WIDTHS]),
        input_output_aliases=aliases,
        compiler_params=_cparams(("arbitrary",)),
        name="pre_attn",
    )(*args, *caches)


def _softmax_parts(parts, sink=None):
    m = parts[0].max(axis=-1, keepdims=True)
    for s in parts[1:]:
        m = jnp.maximum(m, s.max(axis=-1, keepdims=True))
    if sink is not None:
        m = jnp.maximum(m, sink)
    ps = [jnp.exp(s - m) for s in parts]
    den = ps[0].sum(axis=-1, keepdims=True)
    for p in ps[1:]:
        den = den + p.sum(axis=-1, keepdims=True)
    if sink is not None:
        den = den + jnp.exp(sink - m)
    return ps, 1.0 / den


def _mla_heads(mq_ref, wkvb_ref, kcat, o_ref):
    rows = mq_ref.shape[0]
    for g in range(MLA_HEADS // MLA_GROUP):
        heads = range(g * MLA_GROUP, (g + 1) * MLA_GROUP)
        qs = []
        for h in heads:
            wk = wkvb_ref[:, h * 2 * HEAD_DIM:h * 2 * HEAD_DIM + MLA_NOPE_DIM]
            qa = _dot_nt(mq_ref[:, h * MLA_NOPE_DIM:(h + 1) * MLA_NOPE_DIM], wk) * MLA_SCALE
            qr = mq_ref[:, MQ_NOPE + h * LANES:MQ_NOPE + (h + 1) * LANES]
            qs.append(jnp.concatenate([qa.astype(BF16), qr], axis=1))
        s = _dot_nt(jnp.concatenate(qs, axis=0), kcat[...])
        p = jnp.exp(s - s.max(axis=-1, keepdims=True))
        inv = 1.0 / p.sum(axis=-1, keepdims=True)
        lat = _dot(p, kcat[:, :LANES]) * inv
        for n, h in enumerate(heads):
            wv = wkvb_ref[:, h * 2 * HEAD_DIM + MLA_NOPE_DIM:(h + 1) * 2 * HEAD_DIM]
            o_ref[:, h * MLA_V_DIM:(h + 1) * MLA_V_DIM] = _dot(lat[n * rows:(n + 1) * rows], wv).astype(o_ref.dtype)


def _ctx_attn_kernel(l, sink_ref, na_ref, mq_ref, ckv_ref, kpe_ref, sw_ref, wkvb_ref, oin_ref, o_ref):
    del oin_ref
    scale = HEAD_DIM ** -0.5
    _mla_heads(mq_ref, wkvb_ref, jnp.concatenate([ckv_ref[...], kpe_ref[...]], axis=1), o_ref)
    for h in range(SWA_HEADS):
        kh = h // (SWA_HEADS // SWA_KV_HEADS)
        q = sw_ref[:, h * HEAD_DIM:(h + 1) * HEAD_DIM] * scale
        k = sw_ref[:, SWA_OUT + kh * HEAD_DIM:SWA_OUT + (kh + 1) * HEAD_DIM]
        v = sw_ref[:, SWA_OUT + LANES + kh * HEAD_DIM:SWA_OUT + LANES + (kh + 1) * HEAD_DIM]
        (p,), inv = _softmax_parts([_dot_nt(q, k)], sink_ref[l, h])
        o_ref[:, MLA_OUT + h * HEAD_DIM:MLA_OUT + (h + 1) * HEAD_DIM] = (_dot(p, v) * inv).astype(o_ref.dtype)
    for h in range(NA_HEADS):
        q = na_ref[:, h * HEAD_DIM:(h + 1) * HEAD_DIM] * scale
        k = na_ref[:, NA_OUT + h * HEAD_DIM:NA_OUT + (h + 1) * HEAD_DIM]
        v = na_ref[:, 2 * NA_OUT + h * HEAD_DIM:2 * NA_OUT + (h + 1) * HEAD_DIM]
        (p,), inv = _softmax_parts([_dot_nt(q, k)])
        o_ref[:, MLA_OUT + SWA_OUT + h * HEAD_DIM:MLA_OUT + SWA_OUT + (h + 1) * HEAD_DIM] = (
            _dot(p, v) * inv).astype(o_ref.dtype)


def _ctx_attn(l, sink, na, mq, ckv, kpe, sw, w_kvb, o):
    tile = lambda w: pl.BlockSpec((SEQ, w), lambda b: (b, 0))
    return pl.pallas_call(
        functools.partial(_ctx_attn_kernel, l),
        grid=(BATCH,),
        in_specs=[pl.BlockSpec(memory_space=pltpu.SMEM), tile(NA_IN), tile(MQ_W), tile(MLA_KV_RANK), tile(LANES),
                  tile(SWA_IN), pl.BlockSpec((None, MLA_KV_RANK, MLA_HEADS * 2 * HEAD_DIM), lambda b: (l, 0, 0)),
                  pl.BlockSpec(memory_space=pl.ANY)],
        out_specs=tile(D_MODEL),
        out_shape=jax.ShapeDtypeStruct((N_TOK, D_MODEL), BF16),
        input_output_aliases={7: 0},
        compiler_params=_cparams(("arbitrary",)),
        name="ctx_attn",
    )(sink, na, mq, ckv, kpe, sw, w_kvb, o)


def _dec_row(b, j):
    return CTX_TILES + b * DEC_TILES_PER_BATCH + j


def _na_dec_kernel(q_ref, k_ref, v_ref, ck_ref, cv_ref, bias_ref, oin_ref, o_ref):
    del oin_ref
    scale = HEAD_DIM ** -0.5
    j = pl.program_id(1)
    w0 = jnp.clip(j * NA_ROWS_PER_STEP - NA_WIN_H // 2, 0, DEC_SEQ // GRID_W - NA_KEY_ROWS)
    start = pl.multiple_of(w0 * GRID_W, GRID_W)
    for h in range(NA_HEADS):
        sl = slice(h * HEAD_DIM, (h + 1) * HEAD_DIM)
        q = q_ref[:, sl] * scale
        k = k_ref[pl.ds(start, NA_KEYS), sl]
        v = v_ref[pl.ds(start, NA_KEYS), sl]
        s_nb = _dot_nt(q, k) + bias_ref[h]
        s_ctx = _dot_nt(q, ck_ref[:, sl])
        (p_nb, p_ctx), inv = _softmax_parts([s_nb, s_ctx])
        o_ref[:, sl] = ((_dot(p_nb, v) + _dot(p_ctx, cv_ref[:, sl])) * inv).astype(o_ref.dtype)


def _na_dec(l, na, ck, cv, bias, o):
    pat = lambda j: jnp.where(j == 0, 0, jnp.where(j == DEC_TILES_PER_BATCH - 1, 2, 1))
    return pl.pallas_call(
        _na_dec_kernel,
        grid=(DEC_BATCH, DEC_TILES_PER_BATCH),
        in_specs=[pl.BlockSpec((TM, NA_OUT), lambda b, j: (_dec_row(b, j), 0)),
                  pl.BlockSpec((DEC_SEQ, NA_OUT), lambda b, j: (N_CTX // DEC_SEQ + b, 1)),
                  pl.BlockSpec((DEC_SEQ, NA_OUT), lambda b, j: (N_CTX // DEC_SEQ + b, 2)),
                  pl.BlockSpec((None, None, PAST_LEN, NA_OUT), lambda b, j: (b, l, 0, 0)),
                  pl.BlockSpec((None, None, PAST_LEN, NA_OUT), lambda b, j: (b, l, 0, 0)),
                  pl.BlockSpec((None, None, NA_HEADS, TM, NA_KEYS), lambda b, j: (l, pat(j), 0, 0, 0)),
                  pl.BlockSpec(memory_space=pl.ANY)],
        out_specs=pl.BlockSpec((TM, NA_OUT), lambda b, j: (_dec_row(b, j), (MLA_OUT + SWA_OUT) // NA_OUT)),
        out_shape=jax.ShapeDtypeStruct((N_TOK, D_MODEL), BF16),
        input_output_aliases={6: 0},
        compiler_params=_cparams(("arbitrary", "arbitrary")),
        name="na_dec",
    )(na, na, na, ck, cv, bias, o)


KCAT_ROWS = 256


def _mla_dec_kernel(mq_ref, ckv_ref, kpe_ref, cckv_ref, ckpe_ref, wkvb_ref, oin_ref, o_ref, kcat):
    del oin_ref

    def assemble(c, carry):
        rows = pl.ds(pl.multiple_of(c * KCAT_ROWS, KCAT_ROWS), KCAT_ROWS)
        kcat[rows, :LANES] = ckv_ref[rows, :]
        kcat[rows, LANES:] = kpe_ref[rows, :]
        return carry

    first = pl.program_id(1) == 0
    lax.fori_loop(0, jnp.where(first, DEC_SEQ // KCAT_ROWS, 0), assemble, 0)

    def context(c, carry):
        kcat[DEC_SEQ:, :LANES] = cckv_ref[...].astype(BF16)
        kcat[DEC_SEQ:, LANES:] = jnp.concatenate(
            [ckpe_ref[...], jnp.zeros((PAST_LEN, LANES - MLA_ROPE_DIM), F32)], axis=1).astype(BF16)
        return carry

    lax.fori_loop(0, jnp.where(first, 1, 0), context, 0)
    _mla_heads(mq_ref, wkvb_ref, kcat, o_ref)


def _mla_dec(l, mq, ckv, kpe, cckv, ckpe, w_kvb, o):
    return pl.pallas_call(
        _mla_dec_kernel,
        grid=(DEC_BATCH, DEC_TILES_PER_BATCH),
        in_specs=[pl.BlockSpec((TM, MQ_W), lambda b, j: (_dec_row(b, j), 0)),
                  pl.BlockSpec((DEC_SEQ, MLA_KV_RANK), lambda b, j: (N_CTX // DEC_SEQ + b, 0)),
                  pl.BlockSpec((DEC_SEQ, LANES), lambda b, j: (N_CTX // DEC_SEQ + b, 0)),
                  pl.BlockSpec((None, None, PAST_LEN, MLA_KV_RANK), lambda b, j: (b, l, 0, 0)),
                  pl.BlockSpec((None, None, PAST_LEN, MLA_ROPE_DIM), lambda b, j: (b, l, 0, 0)),
                  pl.BlockSpec((None, MLA_KV_RANK, MLA_HEADS * 2 * HEAD_DIM), lambda b, j: (l, 0, 0)),
                  pl.BlockSpec(memory_space=pl.ANY)],
        out_specs=pl.BlockSpec((TM, MLA_OUT), lambda b, j: (_dec_row(b, j), 0)),
        out_shape=jax.ShapeDtypeStruct((N_TOK, D_MODEL), BF16),
        scratch_shapes=[pltpu.VMEM((DEC_SEQ + PAST_LEN, 2 * LANES), BF16)],
        input_output_aliases={6: 0},
        compiler_params=_cparams(("arbitrary", "arbitrary")),
        name="mla_dec",
    )(mq, ckv, kpe, cckv, ckpe, w_kvb, o)


def _swa_dec_kernel(l, sink_ref, q_ref, k_ref, v_ref, ck_ref, cv_ref, oin_ref, o_ref):
    del oin_ref
    scale = HEAD_DIM ** -0.5
    j = pl.program_id(1)
    start = pl.multiple_of(jnp.clip(j * TM - SWA_WINDOW, 0, DEC_SEQ - SWA_KEYS), SWA_WINDOW)
    qpos = j * TM + lax.broadcasted_iota(jnp.int32, (TM, SWA_KEYS), 0)
    kpos = start + lax.broadcasted_iota(jnp.int32, (TM, SWA_KEYS), 1)
    band = jnp.abs(qpos - kpos) <= SWA_WINDOW
    for h in range(SWA_HEADS):
        kh = h // (SWA_HEADS // SWA_KV_HEADS)
        sl = slice(kh * HEAD_DIM, (kh + 1) * HEAD_DIM)
        q = q_ref[:, h * HEAD_DIM:(h + 1) * HEAD_DIM] * scale
        s_loc = jnp.where(band, _dot_nt(q, k_ref[pl.ds(start, SWA_KEYS), sl]), NEG)
        s_ctx = _dot_nt(q, ck_ref[:, sl])
        (p_loc, p_ctx), inv = _softmax_parts([s_loc, s_ctx], sink_ref[l, h])
        o_ref[:, h * HEAD_DIM:(h + 1) * HEAD_DIM] = (
            (_dot(p_loc, v_ref[pl.ds(start, SWA_KEYS), sl]) + _dot(p_ctx, cv_ref[:, sl])) * inv).astype(o_ref.dtype)


def _swa_dec(l, sink, sw, ck, cv, o):
    return pl.pallas_call(
        functools.partial(_swa_dec_kernel, l),
        grid=(DEC_BATCH, DEC_TILES_PER_BATCH),
        in_specs=[pl.BlockSpec(memory_space=pltpu.SMEM),
                  pl.BlockSpec((TM, SWA_OUT), lambda b, j: (_dec_row(b, j), 0)),
                  pl.BlockSpec((DEC_SEQ, LANES), lambda b, j: (N_CTX // DEC_SEQ + b, SWA_OUT // LANES)),
                  pl.BlockSpec((DEC_SEQ, LANES), lambda b, j: (N_CTX // DEC_SEQ + b, SWA_OUT // LANES + 1)),
                  pl.BlockSpec((None, None, PAST_LEN, LANES), lambda b, j: (b, l, 0, 0)),
                  pl.BlockSpec((None, None, PAST_LEN, LANES), lambda b, j: (b, l, 0, 0)),
                  pl.BlockSpec(memory_space=pl.ANY)],
        out_specs=pl.BlockSpec((TM, SWA_OUT), lambda b, j: (_dec_row(b, j), 1)),
        out_shape=jax.ShapeDtypeStruct((N_TOK, D_MODEL), BF16),
        input_output_aliases={6: 0},
        compiler_params=_cparams(("arbitrary", "arbitrary")),
        name="swa_dec",
    )(sink, sw, sw, sw, ck, cv, o)


def _route(sc, sel):
    rows = [sel[e:e + 1, :] for e in range(N_EXPERTS)]

    def beats(a, ia, b, ib):
        return (a > b) | ((a == b) & (ia < ib)) if ia < ib else (a > b)

    in_top = []
    gscore = []
    for g in range(N_EXPERT_GROUPS):
        mem = list(range(g * EXPERTS_PER_GROUP, (g + 1) * EXPERTS_PER_GROUP))
        acc = None
        for e in mem:
            rank = sum(beats(rows[o], o, rows[e], e).astype(jnp.int32) for o in mem if o != e)
            top = rank < 2
            in_top.append(top)
            term = jnp.where(top, rows[e], 0.0)
            acc = term if acc is None else acc + term
        gscore.append(acc)
    gates, chosen = [], []
    for g in range(N_EXPERT_GROUPS):
        lost = sum(beats(gscore[o], o, gscore[g], g).astype(jnp.int32) for o in range(N_EXPERT_GROUPS) if o != g)
        best = lost == 0
        for e in range(g * EXPERTS_PER_GROUP, (g + 1) * EXPERTS_PER_GROUP):
            pick = best & in_top[e]
            chosen.append(pick.astype(F32))
            gates.append(jnp.where(pick, sc[e:e + 1, :], 0.0))
    gate = jnp.concatenate(gates, axis=0)
    return gate / gate.sum(axis=0, keepdims=True), jnp.concatenate(chosen, axis=0)


def _post_kernel(x_ref, o_ref, mod_ref, wout_ref, g_ref, wr_ref, br_ref,
                 y_ref, h2_ref, rt_ref, rtm_ref, cnt_ref):
    attn = jnp.dot(o_ref[...].astype(BF16), wout_ref[...], preferred_element_type=F32)
    y = x_ref[...] + mod_ref[2:3, :] * attn
    y_ref[...] = y
    h2 = _rms(y, g_ref[...]) * (1.0 + mod_ref[4:5, :]) + mod_ref[3:4, :]
    h_hi = h2.astype(BF16)
    h_lo = (h2 - h_hi.astype(F32)).astype(BF16)
    w = wr_ref[...]
    w_hi = w.astype(BF16)
    w_lo = (w - w_hi.astype(F32)).astype(BF16)
    logits = (jnp.dot(h_hi, w_hi, preferred_element_type=F32) + jnp.dot(h_lo, w_hi, preferred_element_type=F32)
              + jnp.dot(h_hi, w_lo, preferred_element_type=F32))
    logits = logits.T[:N_EXPERTS, :]
    sc = 1.0 / (1.0 + jnp.exp(-logits))
    gate, chosen = _route(sc, sc + br_ref[...])
    h2_ref[...] = h_hi
    rt = jnp.concatenate([gate, chosen], axis=0)
    rt_ref[...] = rt
    rtm_ref[...] = jnp.concatenate([rt, jnp.zeros((LANES - 2 * N_EXPERTS, TM), F32)], axis=0).T
    cnt_ref[...] = jnp.broadcast_to(jnp.sum(chosen, axis=1, keepdims=True), (N_EXPERTS, LANES))


def _post(l, x, o, mods, w_out_bf, g_ffn, w_router_p, b_router):
    tile = lambda w: pl.BlockSpec((TM, w), lambda i: (i, 0))
    return pl.pallas_call(
        _post_kernel,
        grid=(N_TILES,),
        in_specs=[tile(D_MODEL), tile(D_MODEL),
                  pl.BlockSpec((None, None, N_MOD, D_MODEL), lambda i: (l, _cond_row(i), 0, 0)),
                  pl.BlockSpec((None, D_MODEL, D_MODEL), lambda i: (l, 0, 0)),
                  pl.BlockSpec((None, 1, D_MODEL), lambda i: (l, 0, 0)),
                  pl.BlockSpec((D_MODEL, LANES), lambda i: (0, 0)),
                  pl.BlockSpec((N_EXPERTS, 1), lambda i: (0, 0))],
        out_specs=[tile(D_MODEL), tile(D_MODEL), pl.BlockSpec((2 * N_EXPERTS, TM), lambda i: (0, i)),
                   tile(LANES), pl.BlockSpec((None, N_EXPERTS, LANES), lambda i: (i, 0, 0))],
        out_shape=[jax.ShapeDtypeStruct((N_TOK, D_MODEL), F32), jax.ShapeDtypeStruct((N_TOK, D_MODEL), BF16),
                   jax.ShapeDtypeStruct((2 * N_EXPERTS, N_TOK), F32),
                   jax.ShapeDtypeStruct((N_TOK, LANES), F32),
                   jax.ShapeDtypeStruct((N_TILES, N_EXPERTS, LANES), F32)],
        compiler_params=_cparams(("arbitrary",)),
        name="post_attn",
    )(x, o, mods, w_out_bf, g_ffn.reshape(DEPTH, 1, D_MODEL), w_router_p, b_router.reshape(N_EXPERTS, 1))


def _shr(x, bits):
    return lax.shift_right_logical(x, jnp.int32(bits))


TR_BITS = TR.bit_length() - 1
ALIGN_BITS = RUN_ALIGN.bit_length() - 1


def _plan_rows(cnt_ref, off_ref, npass_ref, seg_ref):
    def per_expert(e, row0):
        def per_tile(bb, r):
            off_ref[bb * N_EXPERTS + e] = r
            return r + (_shr(cnt_ref[bb * N_EXPERTS + e] + (RUN_ALIGN - 1), ALIGN_BITS) << ALIGN_BITS)

        rows_end = lax.fori_loop(0, N_TILES, per_tile, row0)
        n = _shr(rows_end - row0 + (TR - 1), TR_BITS)
        seg_ref[e] = _shr(row0, TR_BITS)
        seg_ref[N_EXPERTS + e] = n
        seg_ref[2 * N_EXPERTS + e] = rows_end
        seg_ref[3 * N_EXPERTS + e] = row0 + (n << TR_BITS)
        return row0 + (n << TR_BITS)

    end_row = lax.fori_loop(0, N_EXPERTS, per_expert, jnp.int32(0))
    seg_ref[4 * N_EXPERTS] = _shr(end_row, TR_BITS)

    def longest(bb, carry):
        m = lax.fori_loop(0, N_EXPERTS, lambda e, m: jnp.maximum(m, cnt_ref[bb * N_EXPERTS + e]), jnp.int32(0))
        npass_ref[bb] = sum((m > k * WIN).astype(jnp.int32) for k in range(-(-TM // WIN)))
        return carry

    lax.fori_loop(0, N_TILES, longest, 0)


def _dispatch_kernel(cnt_ref, h_ref, rt_ref, et_ref, xs_in, xs_hbm, off_ref, npass_ref, seg_ref, zbuf, sem):
    del xs_in
    b = pl.program_id(0)
    slot = b % 2

    @pl.when(b == 0)
    def _():
        _plan_rows(cnt_ref, off_ref, npass_ref, seg_ref)

    key = jnp.dot(et_ref[...], _run_keys(rt_ref[...], 1).astype(BF16), preferred_element_type=F32)
    slot_j = _window_slot((N_EXPERTS * WIN, TM), 0)

    def run_copies(bb, sl, p, act):
        def one_expert(e, queue):
            left = cnt_ref[bb * N_EXPERTS + e] - p * WIN
            row = off_ref[bb * N_EXPERTS + e] + p * WIN
            whole = row + WIN <= seg_ref[2 * N_EXPERTS + e]

            @pl.when(whole & (left > 0))
            def _():
                src = pl.multiple_of(e * WIN, RUN_ALIGN)
                act(pltpu.make_async_copy(zbuf.at[sl, pl.ds(src, WIN)],
                                          xs_hbm.at[pl.ds(pl.multiple_of(row, RUN_ALIGN), WIN)], sem), queue)

            def piece(k, c):
                src = pl.multiple_of(e * WIN + k * RUN_ALIGN, RUN_ALIGN)
                dst = pl.multiple_of(row + k * RUN_ALIGN, RUN_ALIGN)
                act(pltpu.make_async_copy(zbuf.at[sl, pl.ds(src, RUN_ALIGN)],
                                          xs_hbm.at[pl.ds(dst, RUN_ALIGN)], sem), queue)
                return c

            pieces = jnp.minimum(_shr(jnp.maximum(left, 0) + (RUN_ALIGN - 1), ALIGN_BITS), WIN // RUN_ALIGN)
            lax.fori_loop(0, jnp.where(whole, 0, pieces), piece, 0)

        def expert_pair(e2, carry):
            one_expert(2 * e2, 0)
            one_expert(2 * e2 + 1, 1)
            return carry

        lax.fori_loop(0, N_EXPERTS // 2, expert_pair, 0)

    def fill_and_send(p):
        pick = jnp.where(key == slot_j + jnp.asarray(p * WIN, F32), 1.0, 0.0).astype(BF16)
        zbuf[slot] = jnp.dot(pick, h_ref[...], preferred_element_type=F32).astype(BF16)
        run_copies(b, slot, p, lambda cp, queue: cp.start(priority=queue))

    @pl.when(b > 0)
    def _():
        run_copies(b - 1, 1 - slot, npass_ref[jnp.maximum(b - 1, 0)] - 1, lambda cp, queue: cp.wait())

    fill_and_send(0)

    def more(p, carry):
        run_copies(b, slot, p - 1, lambda cp, queue: cp.wait())
        fill_and_send(p)
        return carry

    lax.fori_loop(1, npass_ref[b], more, 0)

    @pl.when(b == N_TILES - 1)
    def _():
        run_copies(b, slot, npass_ref[b] - 1, lambda cp, queue: cp.wait())


def _dispatch(cnt, h2, rt, et, xs_buf):
    smem = pl.BlockSpec(memory_space=pltpu.SMEM)
    return pl.pallas_call(
        _dispatch_kernel,
        grid_spec=pltpu.PrefetchScalarGridSpec(
            num_scalar_prefetch=1,
            grid=(N_TILES,),
            in_specs=[pl.BlockSpec((TM, D_MODEL), lambda i, *_: (i, 0)),
                      pl.BlockSpec((2 * N_EXPERTS, TM), lambda i, *_: (0, i)),
                      pl.BlockSpec((N_EXPERTS * WIN, 2 * N_EXPERTS), lambda i, *_: (0, 0)),
                      pl.BlockSpec(memory_space=pl.ANY)],
            out_specs=[pl.BlockSpec(memory_space=pl.ANY), smem, smem, smem],
            scratch_shapes=[pltpu.VMEM((2, N_EXPERTS * WIN, D_MODEL), BF16), pltpu.SemaphoreType.DMA(())]),
        out_shape=[jax.ShapeDtypeStruct((NT * TR, D_MODEL), BF16), jax.ShapeDtypeStruct((N_RUNS,), jnp.int32),
                   jax.ShapeDtypeStruct((N_TILES,), jnp.int32),
                   jax.ShapeDtypeStruct((4 * N_EXPERTS + 1,), jnp.int32)],
        input_output_aliases={4: 0},
        compiler_params=_cparams(("arbitrary",)),
        name="dispatch",
    )(cnt, h2, rt, et, xs_buf)


CAST_ROWS = 128
TILE_BUFS = 2
WEIGHT_BUFS = 3


def _cast_rows(src_ref, dst_ref, n):
    def body(c, carry):
        rows = pl.ds(pl.multiple_of(c * CAST_ROWS, CAST_ROWS), CAST_ROWS)
        dst_ref[rows, :] = src_ref[rows, :].astype(BF16)
        return carry

    lax.fori_loop(0, n, body, 0)


def _experts_kernel(l, seg_ref, xs_hbm, wg_hbm, wu_hbm, wd_hbm, ys_in, ys_hbm,
                    wgf, wuf, wdf, wgb, wub, wdb, xbuf, ybuf, semw, semx, semy):
    del ys_in
    e = pl.program_id(0)
    t0 = seg_ref[e]
    n = seg_ref[N_EXPERTS + e]

    def weights(ee, slot):
        return [pltpu.make_async_copy(src.at[l, ee], dst.at[slot], semw.at[slot])
                for src, dst in ((wg_hbm, wgf), (wu_hbm, wuf), (wd_hbm, wdf))]

    @pl.when(e == 0)
    def _():
        for ee in range(WEIGHT_BUFS - 1):
            for cp in weights(ee, ee):
                cp.start()

    later = e + (WEIGHT_BUFS - 1)

    @pl.when(later < N_EXPERTS)
    def _():
        for cp in weights(later, later % WEIGHT_BUFS):
            cp.start()

    wslot = e % WEIGHT_BUFS
    for cp in weights(e, wslot):
        cp.wait()
    _cast_rows(wgf.at[wslot], wgb, D_MODEL // CAST_ROWS)
    _cast_rows(wuf.at[wslot], wub, D_MODEL // CAST_ROWS)
    _cast_rows(wdf.at[wslot], wdb, D_EXPERT // CAST_ROWS)

    def rows(k):
        return pl.ds(pl.multiple_of((t0 + k) * TR, TR), TR)

    def fetch(k, s):
        return pltpu.make_async_copy(xs_hbm.at[rows(k)], xbuf.at[s], semx.at[s])

    def put(k, s):
        return pltpu.make_async_copy(ybuf.at[s], ys_hbm.at[rows(k)], semy.at[s])

    for j in range(TILE_BUFS - 1):
        @pl.when(j < n)
        def _():
            fetch(j, j).start()

    def tile(k, carry):
        s = k % TILE_BUFS
        ahead = k + (TILE_BUFS - 1)

        @pl.when(ahead < n)
        def _():
            fetch(ahead, ahead % TILE_BUFS).start()

        fetch(k, s).wait()

        @pl.when(k >= TILE_BUFS)
        def _():
            put(k - TILE_BUFS, s).wait()

        x = xbuf[s]
        hg = jnp.dot(x, wgb[...], preferred_element_type=F32)
        hu = jnp.dot(x, wub[...], preferred_element_type=F32)
        a = hg * (1.0 / (1.0 + jnp.exp(-hg))) * hu
        ybuf[s] = jnp.dot(a.astype(BF16), wdb[...], preferred_element_type=F32).astype(BF16)
        put(k, s).start(priority=1)
        return carry

    lax.fori_loop(0, n, tile, 0)

    for j in range(TILE_BUFS):
        last = n - TILE_BUFS + j

        @pl.when(last >= 0)
        def _():
            put(last, last % TILE_BUFS).wait()


def _experts(l, seg, xs, w_gate, w_up, w_down, ys_buf):
    hbm = pl.BlockSpec(memory_space=pl.ANY)
    tile_buf = pltpu.VMEM((TILE_BUFS, TR, D_MODEL), BF16)
    up, down = (D_MODEL, D_EXPERT), (D_EXPERT, D_MODEL)
    return pl.pallas_call(
        functools.partial(_experts_kernel, l),
        grid_spec=pltpu.PrefetchScalarGridSpec(
            num_scalar_prefetch=1,
            grid=(N_EXPERTS,),
            in_specs=[hbm, hbm, hbm, hbm, hbm],
            out_specs=hbm,
            scratch_shapes=[pltpu.VMEM((WEIGHT_BUFS,) + up, F32), pltpu.VMEM((WEIGHT_BUFS,) + up, F32),
                            pltpu.VMEM((WEIGHT_BUFS,) + down, F32),
                            pltpu.VMEM(up, BF16), pltpu.VMEM(up, BF16), pltpu.VMEM(down, BF16), tile_buf, tile_buf,
                            pltpu.SemaphoreType.DMA((WEIGHT_BUFS,)),
                            pltpu.SemaphoreType.DMA((TILE_BUFS,)), pltpu.SemaphoreType.DMA((TILE_BUFS,))]),
        out_shape=jax.ShapeDtypeStruct((NT * TR, D_MODEL), BF16),
        input_output_aliases={5: 0},
        compiler_params=_cparams(("arbitrary",)),
        name="experts",
    )(seg, xs, w_gate, w_up, w_down, ys_buf)


def _final_kernel(off_ref, npass_ref, y_ref, rtm_ref, ek_ref, eg_ref, ys_hbm, mod_ref, g_ref, op_ref, os_ref,
                  wbuf, sem):
    i = pl.program_id(0)
    moe = _combined_moe(off_ref, npass_ref, rtm_ref, ek_ref, eg_ref, ys_hbm, wbuf, sem)
    out = _rms(y_ref[...] + mod_ref[5:6, :] * moe, g_ref[...])

    @pl.when(i < CTX_TILES)
    def _():
        op_ref[...] = out

    @pl.when(i >= CTX_TILES)
    def _():
        os_ref[...] = out


def _final(off, npass, y, rtm, ek, eg, ys, mods, g_final):
    return pl.pallas_call(
        _final_kernel,
        grid_spec=pltpu.PrefetchScalarGridSpec(
            num_scalar_prefetch=2,
            grid=(N_TILES,),
            in_specs=_combine_specs() + [
                pl.BlockSpec((None, None, N_MOD, D_MODEL), lambda i, *_: (DEPTH - 1, _cond_row(i), 0, 0)),
                pl.BlockSpec((1, D_MODEL), lambda i, *_: (0, 0))],
            out_specs=[pl.BlockSpec((TM, D_MODEL), lambda i, *_: (jnp.minimum(i, CTX_TILES - 1), 0)),
                       pl.BlockSpec((TM, D_MODEL), lambda i, *_: (jnp.maximum(i - CTX_TILES, 0), 0))],
            scratch_shapes=_combine_scratch()),
        out_shape=[jax.ShapeDtypeStruct((N_CTX, D_MODEL), F32), jax.ShapeDtypeStruct((N_DEC, D_MODEL), F32)],
        compiler_params=_cparams(("arbitrary",)),
        name="final_norm",
    )(off, npass, y, rtm, ek, eg, ys, mods, g_final.reshape(1, D_MODEL))


def _prep_kernel(win_ref, wout_ref, wi_ref, wo_ref):
    w = win_ref[...]
    split = KPE_OFF + MLA_ROPE_DIM
    wi_ref[...] = jnp.concatenate([w[:, :split], jnp.zeros((CAST_ROWS, SW_OFF - split), F32), w[:, split:]],
                                  axis=1).astype(BF16)
    wo_ref[...] = wout_ref[...].astype(BF16)


def _prep_weights(w_in, w_out):
    n = D_MODEL // CAST_ROWS
    shift = NA_OUT // CAST_ROWS
    return pl.pallas_call(
        _prep_kernel,
        grid=(DEPTH, n),
        in_specs=[pl.BlockSpec((None, CAST_ROWS, IN_WIDTH), lambda l, j: (l, j, 0)),
                  pl.BlockSpec((None, CAST_ROWS, D_MODEL), lambda l, j: (l, (j + shift) % n, 0))],
        out_specs=[pl.BlockSpec((None, CAST_ROWS, IN_PAD), lambda l, j: (l, j, 0)),
                   pl.BlockSpec((None, CAST_ROWS, D_MODEL), lambda l, j: (l, j, 0))],
        out_shape=[jax.ShapeDtypeStruct((DEPTH, D_MODEL, IN_PAD), BF16),
                   jax.ShapeDtypeStruct((DEPTH, D_MODEL, D_MODEL), BF16)],
        compiler_params=_cparams(("arbitrary", "arbitrary")),
        name="prep_weights",
    )(w_in, w_out)


def _rope_tables(rot_dim):
    t = jnp.arange(DEC_SEQ, dtype=jnp.int32)
    row = (t // GRID_W).astype(F32)
    col = (t % GRID_W).astype(F32)
    per_axis = rot_dim // 2
    inv = ROPE_BASE ** (-jnp.arange(0, per_axis, 2, dtype=F32) / per_axis)
    ang = jnp.concatenate([row[:, None] * inv, col[:, None] * inv], axis=-1)
    cos, sin = jnp.cos(ang), jnp.sin(ang)
    zero = jnp.zeros_like(sin)
    rep = LANES // rot_dim
    tabs = [jnp.concatenate([cos, cos], -1), jnp.concatenate([zero, sin], -1), jnp.concatenate([-sin, zero], -1)]
    return jnp.stack([jnp.tile(a, (1, rep)) for a in tabs])


def kernel(x_prompt, x_sample, c, cache_na_k, cache_na_v, cache_mla_ckv, cache_mla_kpe, cache_swa_k, cache_swa_v,
           c_ctx, w_ada, b_ada, g_attn, w_in, g_mla_q, w_mla_qb, g_mla_kv, w_mla_kvb, na_rpb, swa_sink, w_out,
           g_ffn, w_router, b_router, w_gate, w_up, w_down, g_final):
    cond = jnp.concatenate([c_ctx[None], c, jnp.zeros((COND_ROWS - 1 - DEC_BATCH, D_MODEL), F32)], axis=0)
    mods = _ada(cond, w_ada, b_ada).reshape(DEPTH, COND_ROWS, N_MOD, D_MODEL)
    bias = _na_bias(na_rpb)
    t_mla = _rope_tables(MLA_ROPE_DIM)
    t_swa = _rope_tables(HEAD_DIM)

    w_in_p, w_out_p = _prep_weights(w_in, w_out)
    wq = w_mla_qb.reshape(DEPTH, MLA_Q_RANK, MLA_HEADS, MLA_QK_DIM)
    w_rope = jnp.pad(wq[..., MLA_NOPE_DIM:], ((0, 0), (0, 0), (0, 0), (0, LANES - MLA_ROPE_DIM)))
    w_qb_p = jnp.concatenate([wq[..., :MLA_NOPE_DIM].reshape(DEPTH, MLA_Q_RANK, MQ_NOPE),
                              w_rope.reshape(DEPTH, MLA_Q_RANK, MLA_HEADS * LANES)], axis=-1).astype(BF16)
    w_router_p = jnp.pad(w_router, ((0, 0), (0, LANES - N_EXPERTS)))
    et, ek, eg = _spread_consts()

    c_na_k = cache_na_k.reshape(DEC_BATCH, DEPTH, PAST_LEN, NA_OUT)
    c_na_v = cache_na_v.reshape(DEC_BATCH, DEPTH, PAST_LEN, NA_OUT)
    c_sw_k = cache_swa_k.reshape(DEC_BATCH, DEPTH, PAST_LEN, LANES)
    c_sw_v = cache_swa_v.reshape(DEC_BATCH, DEPTH, PAST_LEN, LANES)

    y = ys = rtm = off = npass = None
    caches = [jnp.zeros((BATCH, DEPTH, SEQ, w), F32) for w in CACHE_WIDTHS]
    o = jnp.zeros((N_TOK, D_MODEL), BF16)
    xs = jnp.zeros((NT * TR, D_MODEL), BF16)
    ys = jnp.zeros((NT * TR, D_MODEL), BF16)
    for l in range(DEPTH):
        if l == 0:
            srcs = [x_prompt.reshape(N_CTX, D_MODEL), x_sample.reshape(N_DEC, D_MODEL)]
        else:
            srcs = [off, npass, y, rtm, ek, eg, ys]
        x, na, mq, ckv, kpe, sw, *caches = _pre(l == 0, l, srcs, caches, mods, g_attn, w_in_p, g_mla_q, w_qb_p,
                                                g_mla_kv, t_mla, t_swa)
        o = _ctx_attn(l, swa_sink, na, mq, ckv, kpe, sw, w_mla_kvb, o)
        o = _mla_dec(l, mq, ckv, kpe, cache_mla_ckv, cache_mla_kpe, w_mla_kvb, o)
        o = _swa_dec(l, swa_sink, sw, c_sw_k, c_sw_v, o)
        o = _na_dec(l, na, c_na_k, c_na_v, bias, o)
        y, h2, rt, rtm, cnt = _post(l, x, o, mods, w_out_p, g_ffn, w_router_p, b_router)
        xs, off, npass, seg = _dispatch(cnt[:, :, 0].astype(jnp.int32).reshape(-1), h2, rt, et, xs)
        ys = _experts(l, seg, xs, w_gate, w_up, w_down, ys)
    y_prompt, y_sample = _final(off, npass, y, rtm, ek, eg, ys, mods, g_final)

    heads = lambda a, n: a.reshape(BATCH, DEPTH, SEQ, n, HEAD_DIM)
    return (y_prompt.reshape(BATCH, SEQ, D_MODEL), y_sample.reshape(DEC_BATCH, DEC_SEQ, D_MODEL),
            heads(caches[0], NA_HEADS), heads(caches[1], NA_HEADS), caches[2], caches[3],
            heads(caches[4], SWA_KV_HEADS), heads(caches[5], SWA_KV_HEADS))
```

```python
import functools

import jax
import jax.numpy as jnp
from jax import lax
from jax.experimental import pallas as pl
from jax.experimental.pallas import tpu as pltpu

D_MODEL = 1024
BATCH = 16
SEQ = 256
DEPTH = 4
DEC_BATCH = 2
DEC_SEQ = 2048
PAST_LEN = 256
GRID_W = 64
HEAD_DIM = 64
NA_HEADS = 4
NA_WIN_H = 8
NA_WIN_W = 16
MLA_HEADS = 6
MLA_Q_RANK = 256
MLA_KV_RANK = 128
MLA_NOPE_DIM = 64
MLA_ROPE_DIM = 32
MLA_V_DIM = 64
MLA_QK_DIM = MLA_NOPE_DIM + MLA_ROPE_DIM
SWA_HEADS = 6
SWA_KV_HEADS = 2
SWA_WINDOW = 128
ROPE_BASE = 10000.0
N_EXPERTS = 16
N_EXPERT_GROUPS = 4
EXPERTS_PER_GROUP = 4
D_EXPERT = 512
RMS_EPS = 1e-6
N_MOD = 6

NA_IN = 3 * NA_HEADS * HEAD_DIM
MLA_IN = MLA_Q_RANK + MLA_KV_RANK + MLA_ROPE_DIM
SWA_IN = (SWA_HEADS + 2 * SWA_KV_HEADS) * HEAD_DIM
IN_WIDTH = NA_IN + MLA_IN + SWA_IN
NA_OUT = NA_HEADS * HEAD_DIM
MLA_OUT = MLA_HEADS * MLA_V_DIM
SWA_OUT = SWA_HEADS * HEAD_DIM

LANES = 128
N_CTX = BATCH * SEQ
N_DEC = DEC_BATCH * DEC_SEQ
N_TOK = N_CTX + N_DEC
TM = 256
N_TILES = N_TOK // TM
CTX_TILES = N_CTX // TM
DEC_TILES_PER_BATCH = DEC_SEQ // TM
COND_ROWS = 8
KPE_OFF = NA_IN + MLA_Q_RANK + MLA_KV_RANK
SW_OFF = KPE_OFF + LANES
IN_PAD = SW_OFF + SWA_IN
MQ_NOPE = MLA_HEADS * MLA_NOPE_DIM
MQ_W = MQ_NOPE + MLA_HEADS * LANES
MLA_SCALE = MLA_QK_DIM ** -0.5
MLA_GROUP = 3
NA_ROWS_PER_STEP = TM // GRID_W
NA_KEY_ROWS = 12
NA_KEYS = NA_KEY_ROWS * GRID_W
SWA_KEYS = 512
NEG = -1e30
TR = 512
N_ASSIGN = 2 * N_TOK
RUN_ALIGN = 16
WIN = 80
N_RUNS = N_TILES * N_EXPERTS
NT = -(-(N_ASSIGN + N_RUNS * (RUN_ALIGN - 1) + N_EXPERTS * (TR - 1) + WIN) // TR)

F32 = jnp.float32
BF16 = jnp.bfloat16
VMEM_LIMIT = 56 * 1024 * 1024


def _cparams(sem):
    return pltpu.CompilerParams(dimension_semantics=sem, vmem_limit_bytes=VMEM_LIMIT)


def _cond_row(i):
    return jnp.where(i < CTX_TILES, 0, 1 + (i - CTX_TILES) // DEC_TILES_PER_BATCH)


def _rope_blk(i):
    return jnp.where(i < CTX_TILES, 0, (i - CTX_TILES) % DEC_TILES_PER_BATCH)


def _rms(x, g):
    ms = jnp.mean(x * x, axis=-1, keepdims=True)
    return x * lax.rsqrt(ms + RMS_EPS) * g


def _dot(a, b):
    return jnp.dot(a.astype(BF16), b.astype(BF16), preferred_element_type=F32)


def _dot_nt(a, b):
    return lax.dot_general(a.astype(BF16), b.astype(BF16), (((1,), (1,)), ((), ())),
                           preferred_element_type=F32)


def _ada_kernel(c_ref, w_ref, b_ref, o_ref):
    c = c_ref[...]
    s = c * (1.0 / (1.0 + jnp.exp(-c)))
    o_ref[...] = _dot(s, w_ref[...]) + b_ref[...]


def _ada(cond, w_ada, b_ada):
    tn = 1536
    n = N_MOD * D_MODEL
    return pl.pallas_call(
        _ada_kernel,
        grid=(DEPTH, n // tn),
        in_specs=[pl.BlockSpec((COND_ROWS, D_MODEL), lambda l, j: (0, 0)),
                  pl.BlockSpec((None, D_MODEL, tn), lambda l, j: (l, 0, j)),
                  pl.BlockSpec((None, 1, tn), lambda l, j: (l, 0, j))],
        out_specs=pl.BlockSpec((None, COND_ROWS, tn), lambda l, j: (l, 0, j)),
        out_shape=jax.ShapeDtypeStruct((DEPTH, COND_ROWS, n), F32),
        compiler_params=_cparams(("arbitrary", "arbitrary")),
        name="ada_mod",
    )(cond, w_ada, b_ada.reshape(DEPTH, 1, n))


def _bias_kernel(rpb_ref, o_ref):
    g = pl.program_id(0)
    base = g * ((2 * NA_WIN_H - 1) * (2 * NA_WIN_W - 1))
    qc = lax.broadcasted_iota(jnp.int32, (GRID_W, GRID_W), 0)
    kc = lax.broadcasted_iota(jnp.int32, (GRID_W, GRID_W), 1)
    dc = jnp.clip(kc - qc + (NA_WIN_W - 1), 0, 2 * NA_WIN_W - 2)
    cs = jnp.clip(qc - NA_WIN_W // 2, 0, GRID_W - NA_WIN_W)
    col_ok = (kc >= cs) & (kc < cs + NA_WIN_W)
    neg = jnp.full((GRID_W, GRID_W), NEG, F32)
    tabs = []
    for a in range(2 * NA_WIN_H - 1):
        t = jnp.zeros((GRID_W, GRID_W), F32)
        for b in range(2 * NA_WIN_W - 1):
            t = jnp.where(dc == b, rpb_ref[base + a * (2 * NA_WIN_W - 1) + b], t)
        tabs.append(jnp.where(col_ok, t, NEG))
    for p in range(3):
        for qi in range(NA_ROWS_PER_STEP):
            for kj in range(NA_KEY_ROWS):
                if p == 0:
                    ok, dr = kj < NA_WIN_H, kj - qi + 7
                elif p == 1:
                    ok, dr = qi <= kj < qi + NA_WIN_H, kj - qi + 3
                else:
                    ok, dr = kj >= NA_KEY_ROWS - NA_WIN_H, kj - qi - 1
                blk = tabs[dr] if ok else neg
                o_ref[p, qi * GRID_W:(qi + 1) * GRID_W, kj * GRID_W:(kj + 1) * GRID_W] = blk


def _na_bias(na_rpb):
    return pl.pallas_call(
        _bias_kernel,
        grid=(DEPTH * NA_HEADS,),
        in_specs=[pl.BlockSpec(memory_space=pltpu.SMEM)],
        out_specs=pl.BlockSpec((None, 3, None, TM, NA_KEYS),
                               lambda g: (g // NA_HEADS, 0, g % NA_HEADS, 0, 0)),
        out_shape=jax.ShapeDtypeStruct((DEPTH, 3, NA_HEADS, TM, NA_KEYS), F32),
        compiler_params=_cparams(("arbitrary",)),
        name="na_bias",
    )(na_rpb.reshape(-1))


def _rope128(x, t_ref, half):
    return (x * t_ref[0] + pltpu.roll(x, half, 1) * t_ref[1]
            + pltpu.roll(x, LANES - half, 1) * t_ref[2])


def _spread_consts():
    slot_e = jnp.arange(N_EXPERTS * WIN, dtype=jnp.int32) // WIN
    idx32 = jnp.arange(2 * N_EXPERTS, dtype=jnp.int32)
    idx128 = jnp.arange(LANES, dtype=jnp.int32)
    et = (idx32[None, :] == slot_e[:, None] + N_EXPERTS).astype(BF16)
    ek = (idx128[:, None] == slot_e[None, :] + N_EXPERTS).astype(BF16)
    eg = (idx128[:, None] == slot_e[None, :]).astype(BF16)
    return et, ek, eg


def _window_slot(shape, axis):
    r = lax.broadcasted_iota(jnp.int32, shape, axis).astype(F32)
    return r - WIN * jnp.floor((r + 0.5) * (1.0 / WIN))


def _run_keys(chosen, token_axis):
    a = lax.broadcasted_iota(jnp.int32, (TM, TM), 0)
    b = lax.broadcasted_iota(jnp.int32, (TM, TM), 1)
    if token_axis == 1:
        rank = jnp.dot(chosen.astype(BF16), (a < b).astype(BF16), preferred_element_type=F32)
    else:
        rank = jnp.dot((b < a).astype(BF16), chosen.astype(BF16), preferred_element_type=F32)
    return jnp.where(chosen > 0.5, rank, -1.0)


def _windows_start(i, slot, p, off_ref, ys_hbm, wbuf, sem):
    for e in range(N_EXPERTS):
        row = pl.multiple_of(off_ref[i * N_EXPERTS + e] + p * WIN, RUN_ALIGN)
        pltpu.make_async_copy(ys_hbm.at[pl.ds(row, WIN)], wbuf.at[slot, pl.ds(e * WIN, WIN)],
                              sem.at[slot]).start(priority=e % 2)


def _windows_wait(slot, ys_hbm, wbuf, sem):
    pltpu.make_async_copy(ys_hbm.at[pl.ds(0, N_EXPERTS * WIN)], wbuf.at[slot], sem.at[slot]).wait()


def _combined_moe(off_ref, npass_ref, rtm_ref, ek_ref, eg_ref, ys_hbm, wbuf, sem):
    i = pl.program_id(0)
    slot = i % 2

    @pl.when(i == 0)
    def _():
        _windows_start(0, 0, 0, off_ref, ys_hbm, wbuf, sem)

    @pl.when(i + 1 < N_TILES)
    def _():
        _windows_start(i + 1, 1 - slot, 0, off_ref, ys_hbm, wbuf, sem)

    r = rtm_ref[...]
    key = jnp.dot(_run_keys(r, 0).astype(BF16), ek_ref[...], preferred_element_type=F32)
    gate = jnp.dot(r.astype(BF16), eg_ref[...], preferred_element_type=F32)
    slot_j = _window_slot((TM, N_EXPERTS * WIN), 1)

    def contrib(p):
        g = jnp.where(key == slot_j + jnp.asarray(p * WIN, F32), gate, 0.0).astype(BF16)
        return jnp.dot(g, wbuf[slot], preferred_element_type=F32)

    _windows_wait(slot, ys_hbm, wbuf, sem)
    acc = contrib(0)

    def extra(p, acc):
        _windows_start(i, slot, p, off_ref, ys_hbm, wbuf, sem)
        _windows_wait(slot, ys_hbm, wbuf, sem)
        return acc + contrib(p)

    return lax.fori_loop(1, npass_ref[i], extra, acc)


def _pre_kernel(first, *refs):
    i = pl.program_id(0)
    if first:
        (xp_ref, xs_ref, mod_ref, g_ref, win_ref, gq_ref, wqb_ref, gkv_ref, tm_ref, ts_ref, _, _, _, _, _, _,
         xo_ref, na_ref, mq_ref, ckv_ref, kpe_ref, sw_ref, *cache_refs) = refs
        x = jnp.where(i < CTX_TILES, xp_ref[...], xs_ref[...])
    else:
        (off_ref, npass_ref, y_ref, rtm_ref, ek_ref, eg_ref, ys_hbm, modp_ref, mod_ref, g_ref, win_ref, gq_ref,
         wqb_ref, gkv_ref, tm_ref, ts_ref, _, _, _, _, _, _,
         xo_ref, na_ref, mq_ref, ckv_ref, kpe_ref, sw_ref, *rest) = refs
        *cache_refs, wbuf, sem = rest
        moe = _combined_moe(off_ref, npass_ref, rtm_ref, ek_ref, eg_ref, ys_hbm, wbuf, sem)
        x = y_ref[...] + modp_ref[5:6, :] * moe
    xo_ref[...] = x
    cnk_ref, cnv_ref, cckv_ref, ckpe_ref, csk_ref, csv_ref = cache_refs

    h = _rms(x, g_ref[...]) * (1.0 + mod_ref[1:2, :]) + mod_ref[0:1, :]
    z = jnp.dot(h.astype(BF16), win_ref[...], preferred_element_type=F32)
    na_ref[...] = z[:, :NA_IN].astype(BF16)
    cq = _rms(z[:, NA_IN:NA_IN + MLA_Q_RANK], gq_ref[...])
    ckv = _rms(z[:, NA_IN + MLA_Q_RANK:KPE_OFF], gkv_ref[...])
    ckv_ref[...] = ckv.astype(BF16)
    mq = jnp.dot(cq.astype(BF16), wqb_ref[...], preferred_element_type=F32)
    kpe = z[:, KPE_OFF:SW_OFF]
    sw = z[:, SW_OFF:IN_PAD]
    mq_ref[:, :MQ_NOPE] = mq[:, :MQ_NOPE].astype(BF16)
    sw_ref[:, SWA_OUT + LANES:] = sw[:, SWA_OUT + LANES:].astype(BF16)

    @pl.when(i < CTX_TILES)
    def _():
        mq_ref[:, MQ_NOPE:] = (mq[:, MQ_NOPE:] * MLA_SCALE).astype(BF16)
        kpe_ref[...] = kpe.astype(BF16)
        sw_ref[:, :SWA_OUT + LANES] = sw[:, :SWA_OUT + LANES].astype(BF16)
        cnk_ref[...] = z[:, NA_OUT:2 * NA_OUT]
        cnv_ref[...] = z[:, 2 * NA_OUT:NA_IN]
        cckv_ref[...] = ckv
        ckpe_ref[...] = kpe[:, :MLA_ROPE_DIM]
        csk_ref[...] = sw[:, SWA_OUT:SWA_OUT + LANES]
        csv_ref[...] = sw[:, SWA_OUT + LANES:]

    @pl.when(i >= CTX_TILES)
    def _():
        for c in range(MQ_NOPE // LANES, MQ_W // LANES):
            roped = _rope128(mq[:, c * LANES:(c + 1) * LANES], tm_ref, MLA_ROPE_DIM // 2)
            mq_ref[:, c * LANES:(c + 1) * LANES] = (roped * MLA_SCALE).astype(BF16)
        kpe_ref[...] = _rope128(kpe, tm_ref, MLA_ROPE_DIM // 2).astype(BF16)
        for c in range((SWA_OUT + LANES) // LANES):
            sw_ref[:, c * LANES:(c + 1) * LANES] = _rope128(sw[:, c * LANES:(c + 1) * LANES], ts_ref,
                                                            HEAD_DIM // 2).astype(BF16)


def _combine_scratch():
    return [pltpu.VMEM((2, N_EXPERTS * WIN, D_MODEL), BF16), pltpu.SemaphoreType.DMA((2,))]


def _combine_specs():
    spread = pl.BlockSpec((LANES, N_EXPERTS * WIN), lambda i, *_: (0, 0))
    return [pl.BlockSpec((TM, D_MODEL), lambda i, *_: (i, 0)),
            pl.BlockSpec((TM, LANES), lambda i, *_: (i, 0)),
            spread, spread, pl.BlockSpec(memory_space=pl.ANY)]


CACHE_WIDTHS = (NA_OUT, NA_OUT, MLA_KV_RANK, MLA_ROPE_DIM, LANES, LANES)


def _pre(first, l, xs, caches, mods, g_attn, w_in_p, g_mla_q, w_qb_p, g_mla_kv, t_mla, t_swa):
    tile = lambda w: pl.BlockSpec((TM, w), lambda i, *_: (i, 0))
    ctx_tile = lambda w: pl.BlockSpec((TM, w), lambda i, *_: (jnp.minimum(i, CTX_TILES - 1), 0))
    slab = lambda w: pl.BlockSpec((None, None, SEQ, w), lambda i, *_: (jnp.minimum(i, BATCH - 1), l, 0, 0))
    mod_spec = lambda ll: pl.BlockSpec((None, None, N_MOD, D_MODEL), lambda i, *_: (ll, _cond_row(i), 0, 0))
    vec = lambda w: pl.BlockSpec((None, 1, w), lambda i, *_: (l, 0, 0))
    if first:
        in_specs = [ctx_tile(D_MODEL), pl.BlockSpec((TM, D_MODEL), lambda i: (jnp.maximum(i - CTX_TILES, 0), 0))]
    else:
        in_specs = _combine_specs() + [mod_spec(l - 1)]
    in_specs += [mod_spec(l), vec(D_MODEL),
                 pl.BlockSpec((None, D_MODEL, IN_PAD), lambda i, *_: (l, 0, 0)),
                 vec(MLA_Q_RANK),
                 pl.BlockSpec((None, MLA_Q_RANK, MQ_W), lambda i, *_: (l, 0, 0)),
                 vec(MLA_KV_RANK),
                 pl.BlockSpec((3, TM, LANES), lambda i, *_: (0, _rope_blk(i), 0)),
                 pl.BlockSpec((3, TM, LANES), lambda i, *_: (0, _rope_blk(i), 0))]
    in_specs += [pl.BlockSpec(memory_space=pl.ANY)] * len(CACHE_WIDTHS)
    widths = [D_MODEL, NA_IN, MQ_W, MLA_KV_RANK, LANES, SWA_IN]
    args = list(xs) + ([mods] if not first else []) + [
        mods, g_attn.reshape(DEPTH, 1, D_MODEL), w_in_p, g_mla_q.reshape(DEPTH, 1, MLA_Q_RANK), w_qb_p,
        g_mla_kv.reshape(DEPTH, 1, MLA_KV_RANK), t_mla, t_swa]
    aliases = {len(args) + n: len(widths) + n for n in range(len(CACHE_WIDTHS))}
    return pl.pallas_call(
        functools.partial(_pre_kernel, first),
        grid_spec=pltpu.PrefetchScalarGridSpec(
            num_scalar_prefetch=0 if first else 2,
            grid=(N_TILES,),
            in_specs=in_specs,
            out_specs=[tile(w) for w in widths] + [slab(w) for w in CACHE_WIDTHS],
            scratch_shapes=[] if first else _combine_scratch()),
        out_shape=([jax.ShapeDtypeStruct((N_TOK, w), BF16 if n else F32) for n, w in enumerate(widths)]
                   + [jax.ShapeDtypeStruct((BATCH, DEPTH, SEQ, w), F32) for w in CACHE_WIDTHS]),
        input_output_aliases=aliases,
        compiler_params=_cparams(("arbitrary",)),
        name="pre_attn",
    )(*args, *caches)


def _softmax_parts(parts, sink=None):
    m = parts[0].max(axis=-1, keepdims=True)
    for s in parts[1:]:
        m = jnp.maximum(m, s.max(axis=-1, keepdims=True))
    if sink is not None:
        m = jnp.maximum(m, sink)
    ps = [jnp.exp(s - m) for s in parts]
    den = ps[0].sum(axis=-1, keepdims=True)
    for p in ps[1:]:
        den = den + p.sum(axis=-1, keepdims=True)
    if sink is not None:
        den = den + jnp.exp(sink - m)
    return ps, 1.0 / den


def _mla_heads(mq_ref, wkvb_ref, kcat, o_ref):
    rows = mq_ref.shape[0]
    for g in range(MLA_HEADS // MLA_GROUP):
        heads = range(g * MLA_GROUP, (g + 1) * MLA_GROUP)
        qs = []
        for h in heads:
            wk = wkvb_ref[:, h * 2 * HEAD_DIM:h * 2 * HEAD_DIM + MLA_NOPE_DIM]
            qa = _dot_nt(mq_ref[:, h * MLA_NOPE_DIM:(h + 1) * MLA_NOPE_DIM], wk) * MLA_SCALE
            qr = mq_ref[:, MQ_NOPE + h * LANES:MQ_NOPE + (h + 1) * LANES]
            qs.append(jnp.concatenate([qa.astype(BF16), qr], axis=1))
        s = _dot_nt(jnp.concatenate(qs, axis=0), kcat[...])
        p = jnp.exp(s - s.max(axis=-1, keepdims=True))
        inv = 1.0 / p.sum(axis=-1, keepdims=True)
        lat = _dot(p, kcat[:, :LANES]) * inv
        for n, h in enumerate(heads):
            wv = wkvb_ref[:, h * 2 * HEAD_DIM + MLA_NOPE_DIM:(h + 1) * 2 * HEAD_DIM]
            o_ref[:, h * MLA_V_DIM:(h + 1) * MLA_V_DIM] = _dot(lat[n * rows:(n + 1) * rows], wv).astype(o_ref.dtype)


def _ctx_attn_kernel(l, sink_ref, na_ref, mq_ref, ckv_ref, kpe_ref, sw_ref, wkvb_ref, oin_ref, o_ref):
    del oin_ref
    scale = HEAD_DIM ** -0.5
    _mla_heads(mq_ref, wkvb_ref, jnp.concatenate([ckv_ref[...], kpe_ref[...]], axis=1), o_ref)
    for h in range(SWA_HEADS):
        kh = h // (SWA_HEADS // SWA_KV_HEADS)
        q = sw_ref[:, h * HEAD_DIM:(h + 1) * HEAD_DIM] * scale
        k = sw_ref[:, SWA_OUT + kh * HEAD_DIM:SWA_OUT + (kh + 1) * HEAD_DIM]
        v = sw_ref[:, SWA_OUT + LANES + kh * HEAD_DIM:SWA_OUT + LANES + (kh + 1) * HEAD_DIM]
        (p,), inv = _softmax_parts([_dot_nt(q, k)], sink_ref[l, h])
        o_ref[:, MLA_OUT + h * HEAD_DIM:MLA_OUT + (h + 1) * HEAD_DIM] = (_dot(p, v) * inv).astype(o_ref.dtype)
    for h in range(NA_HEADS):
        q = na_ref[:, h * HEAD_DIM:(h + 1) * HEAD_DIM] * scale
        k = na_ref[:, NA_OUT + h * HEAD_DIM:NA_OUT + (h + 1) * HEAD_DIM]
        v = na_ref[:, 2 * NA_OUT + h * HEAD_DIM:2 * NA_OUT + (h + 1) * HEAD_DIM]
        (p,), inv = _softmax_parts([_dot_nt(q, k)])
        o_ref[:, MLA_OUT + SWA_OUT + h * HEAD_DIM:MLA_OUT + SWA_OUT + (h + 1) * HEAD_DIM] = (
            _dot(p, v) * inv).astype(o_ref.dtype)


def _ctx_attn(l, sink, na, mq, ckv, kpe, sw, w_kvb, o):
    tile = lambda w: pl.BlockSpec((SEQ, w), lambda b: (b, 0))
    return pl.pallas_call(
        functools.partial(_ctx_attn_kernel, l),
        grid=(BATCH,),
        in_specs=[pl.BlockSpec(memory_space=pltpu.SMEM), tile(NA_IN), tile(MQ_W), tile(MLA_KV_RANK), tile(LANES),
                  tile(SWA_IN), pl.BlockSpec((None, MLA_KV_RANK, MLA_HEADS * 2 * HEAD_DIM), lambda b: (l, 0, 0)),
                  pl.BlockSpec(memory_space=pl.ANY)],
        out_specs=tile(D_MODEL),
        out_shape=jax.ShapeDtypeStruct((N_TOK, D_MODEL), BF16),
        input_output_aliases={7: 0},
        compiler_params=_cparams(("arbitrary",)),
        name="ctx_attn",
    )(sink, na, mq, ckv, kpe, sw, w_kvb, o)


def _dec_row(b, j):
    return CTX_TILES + b * DEC_TILES_PER_BATCH + j


def _na_dec_kernel(q_ref, k_ref, v_ref, ck_ref, cv_ref, bias_ref, oin_ref, o_ref):
    del oin_ref
    scale = HEAD_DIM ** -0.5
    j = pl.program_id(1)
    w0 = jnp.clip(j * NA_ROWS_PER_STEP - NA_WIN_H // 2, 0, DEC_SEQ // GRID_W - NA_KEY_ROWS)
    start = pl.multiple_of(w0 * GRID_W, GRID_W)
    for h in range(NA_HEADS):
        sl = slice(h * HEAD_DIM, (h + 1) * HEAD_DIM)
        q = q_ref[:, sl] * scale
        k = k_ref[pl.ds(start, NA_KEYS), sl]
        v = v_ref[pl.ds(start, NA_KEYS), sl]
        s_nb = _dot_nt(q, k) + bias_ref[h]
        s_ctx = _dot_nt(q, ck_ref[:, sl])
        (p_nb, p_ctx), inv = _softmax_parts([s_nb, s_ctx])
        o_ref[:, sl] = ((_dot(p_nb, v) + _dot(p_ctx, cv_ref[:, sl])) * inv).astype(o_ref.dtype)


def _na_dec(l, na, ck, cv, bias, o):
    pat = lambda j: jnp.where(j == 0, 0, jnp.where(j == DEC_TILES_PER_BATCH - 1, 2, 1))
    return pl.pallas_call(
        _na_dec_kernel,
        grid=(DEC_BATCH, DEC_TILES_PER_BATCH),
        in_specs=[pl.BlockSpec((TM, NA_OUT), lambda b, j: (_dec_row(b, j), 0)),
                  pl.BlockSpec((DEC_SEQ, NA_OUT), lambda b, j: (N_CTX // DEC_SEQ + b, 1)),
                  pl.BlockSpec((DEC_SEQ, NA_OUT), lambda b, j: (N_CTX // DEC_SEQ + b, 2)),
                  pl.BlockSpec((None, None, PAST_LEN, NA_OUT), lambda b, j: (b, l, 0, 0)),
                  pl.BlockSpec((None, None, PAST_LEN, NA_OUT), lambda b, j: (b, l, 0, 0)),
                  pl.BlockSpec((None, None, NA_HEADS, TM, NA_KEYS), lambda b, j: (l, pat(j), 0, 0, 0)),
                  pl.BlockSpec(memory_space=pl.ANY)],
        out_specs=pl.BlockSpec((TM, NA_OUT), lambda b, j: (_dec_row(b, j), (MLA_OUT + SWA_OUT) // NA_OUT)),
        out_shape=jax.ShapeDtypeStruct((N_TOK, D_MODEL), BF16),
        input_output_aliases={6: 0},
        compiler_params=_cparams(("arbitrary", "arbitrary")),
        name="na_dec",
    )(na, na, na, ck, cv, bias, o)


KCAT_ROWS = 256


def _mla_dec_kernel(mq_ref, ckv_ref, kpe_ref, cckv_ref, ckpe_ref, wkvb_ref, oin_ref, o_ref, kcat):
    del oin_ref

    def assemble(c, carry):
        rows = pl.ds(pl.multiple_of(c * KCAT_ROWS, KCAT_ROWS), KCAT_ROWS)
        kcat[rows, :LANES] = ckv_ref[rows, :]
        kcat[rows, LANES:] = kpe_ref[rows, :]
        return carry

    first = pl.program_id(1) == 0
    lax.fori_loop(0, jnp.where(first, DEC_SEQ // KCAT_ROWS, 0), assemble, 0)

    def context(c, carry):
        kcat[DEC_SEQ:, :LANES] = cckv_ref[...].astype(BF16)
        kcat[DEC_SEQ:, LANES:] = jnp.concatenate(
            [ckpe_ref[...], jnp.zeros((PAST_LEN, LANES - MLA_ROPE_DIM), F32)], axis=1).astype(BF16)
        return carry

    lax.fori_loop(0, jnp.where(first, 1, 0), context, 0)
    _mla_heads(mq_ref, wkvb_ref, kcat, o_ref)


def _mla_dec(l, mq, ckv, kpe, cckv, ckpe, w_kvb, o):
    return pl.pallas_call(
        _mla_dec_kernel,
        grid=(DEC_BATCH, DEC_TILES_PER_BATCH),
        in_specs=[pl.BlockSpec((TM, MQ_W), lambda b, j: (_dec_row(b, j), 0)),
                  pl.BlockSpec((DEC_SEQ, MLA_KV_RANK), lambda b, j: (N_CTX // DEC_SEQ + b, 0)),
                  pl.BlockSpec((DEC_SEQ, LANES), lambda b, j: (N_CTX // DEC_SEQ + b, 0)),
                  pl.BlockSpec((None, None, PAST_LEN, MLA_KV_RANK), lambda b, j: (b, l, 0, 0)),
                  pl.BlockSpec((None, None, PAST_LEN, MLA_ROPE_DIM), lambda b, j: (b, l, 0, 0)),
                  pl.BlockSpec((None, MLA_KV_RANK, MLA_HEADS * 2 * HEAD_DIM), lambda b, j: (l, 0, 0)),
                  pl.BlockSpec(memory_space=pl.ANY)],
        out_specs=pl.BlockSpec((TM, MLA_OUT), lambda b, j: (_dec_row(b, j), 0)),
        out_shape=jax.ShapeDtypeStruct((N_TOK, D_MODEL), BF16),
        scratch_shapes=[pltpu.VMEM((DEC_SEQ + PAST_LEN, 2 * LANES), BF16)],
        input_output_aliases={6: 0},
        compiler_params=_cparams(("arbitrary", "arbitrary")),
        name="mla_dec",
    )(mq, ckv, kpe, cckv, ckpe, w_kvb, o)


def _swa_dec_kernel(l, sink_ref, q_ref, k_ref, v_ref, ck_ref, cv_ref, oin_ref, o_ref):
    del oin_ref
    scale = HEAD_DIM ** -0.5
    j = pl.program_id(1)
    start = pl.multiple_of(jnp.clip(j * TM - SWA_WINDOW, 0, DEC_SEQ - SWA_KEYS), SWA_WINDOW)
    qpos = j * TM + lax.broadcasted_iota(jnp.int32, (TM, SWA_KEYS), 0)
    kpos = start + lax.broadcasted_iota(jnp.int32, (TM, SWA_KEYS), 1)
    band = jnp.abs(qpos - kpos) <= SWA_WINDOW
    for h in range(SWA_HEADS):
        kh = h // (SWA_HEADS // SWA_KV_HEADS)
        sl = slice(kh * HEAD_DIM, (kh + 1) * HEAD_DIM)
        q = q_ref[:, h * HEAD_DIM:(h + 1) * HEAD_DIM] * scale
        s_loc = jnp.where(band, _dot_nt(q, k_ref[pl.ds(start, SWA_KEYS), sl]), NEG)
        s_ctx = _dot_nt(q, ck_ref[:, sl])
        (p_loc, p_ctx), inv = _softmax_parts([s_loc, s_ctx], sink_ref[l, h])
        o_ref[:, h * HEAD_DIM:(h + 1) * HEAD_DIM] = (
            (_dot(p_loc, v_ref[pl.ds(start, SWA_KEYS), sl]) + _dot(p_ctx, cv_ref[:, sl])) * inv).astype(o_ref.dtype)


def _swa_dec(l, sink, sw, ck, cv, o):
    return pl.pallas_call(
        functools.partial(_swa_dec_kernel, l),
        grid=(DEC_BATCH, DEC_TILES_PER_BATCH),
        in_specs=[pl.BlockSpec(memory_space=pltpu.SMEM),
                  pl.BlockSpec((TM, SWA_OUT), lambda b, j: (_dec_row(b, j), 0)),
                  pl.BlockSpec((DEC_SEQ, LANES), lambda b, j: (N_CTX // DEC_SEQ + b, SWA_OUT // LANES)),
                  pl.BlockSpec((DEC_SEQ, LANES), lambda b, j: (N_CTX // DEC_SEQ + b, SWA_OUT // LANES + 1)),
                  pl.BlockSpec((None, None, PAST_LEN, LANES), lambda b, j: (b, l, 0, 0)),
                  pl.BlockSpec((None, None, PAST_LEN, LANES), lambda b, j: (b, l, 0, 0)),
                  pl.BlockSpec(memory_space=pl.ANY)],
        out_specs=pl.BlockSpec((TM, SWA_OUT), lambda b, j: (_dec_row(b, j), 1)),
        out_shape=jax.ShapeDtypeStruct((N_TOK, D_MODEL), BF16),
        input_output_aliases={6: 0},
        compiler_params=_cparams(("arbitrary", "arbitrary")),
        name="swa_dec",
    )(sink, sw, sw, sw, ck, cv, o)


def _route(sc, sel):
    rows = [sel[e:e + 1, :] for e in range(N_EXPERTS)]

    def beats(a, ia, b, ib):
        return (a > b) | ((a == b) & (ia < ib)) if ia < ib else (a > b)

    in_top = []
    gscore = []
    for g in range(N_EXPERT_GROUPS):
        mem = list(range(g * EXPERTS_PER_GROUP, (g + 1) * EXPERTS_PER_GROUP))
        acc = None
        for e in mem:
            rank = sum(beats(rows[o], o, rows[e], e).astype(jnp.int32) for o in mem if o != e)
            top = rank < 2
            in_top.append(top)
            term = jnp.where(top, rows[e], 0.0)
            acc = term if acc is None else acc + term
        gscore.append(acc)
    gates, chosen = [], []
    for g in range(N_EXPERT_GROUPS):
        lost = sum(beats(gscore[o], o, gscore[g], g).astype(jnp.int32) for o in range(N_EXPERT_GROUPS) if o != g)
        best = lost == 0
        for e in range(g * EXPERTS_PER_GROUP, (g + 1) * EXPERTS_PER_GROUP):
            pick = best & in_top[e]
            chosen.append(pick.astype(F32))
            gates.append(jnp.where(pick, sc[e:e + 1, :], 0.0))
    gate = jnp.concatenate(gates, axis=0)
    return gate / gate.sum(axis=0, keepdims=True), jnp.concatenate(chosen, axis=0)


def _post_kernel(x_ref, o_ref, mod_ref, wout_ref, g_ref, wr_ref, br_ref,
                 y_ref, h2_ref, rt_ref, rtm_ref, cnt_ref):
    attn = jnp.dot(o_ref[...].astype(BF16), wout_ref[...], preferred_element_type=F32)
    y = x_ref[...] + mod_ref[2:3, :] * attn
    y_ref[...] = y
    h2 = _rms(y, g_ref[...]) * (1.0 + mod_ref[4:5, :]) + mod_ref[3:4, :]
    h_hi = h2.astype(BF16)
    h_lo = (h2 - h_hi.astype(F32)).astype(BF16)
    w = wr_ref[...]
    w_hi = w.astype(BF16)
    w_lo = (w - w_hi.astype(F32)).astype(BF16)
    logits = (jnp.dot(h_hi, w_hi, preferred_element_type=F32) + jnp.dot(h_lo, w_hi, preferred_element_type=F32)
              + jnp.dot(h_hi, w_lo, preferred_element_type=F32))
    logits = logits.T[:N_EXPERTS, :]
    sc = 1.0 / (1.0 + jnp.exp(-logits))
    gate, chosen = _route(sc, sc + br_ref[...])
    h2_ref[...] = h_hi
    rt = jnp.concatenate([gate, chosen], axis=0)
    rt_ref[...] = rt
    rtm_ref[...] = jnp.concatenate([rt, jnp.zeros((LANES - 2 * N_EXPERTS, TM), F32)], axis=0).T
    cnt_ref[...] = jnp.broadcast_to(jnp.sum(chosen, axis=1, keepdims=True), (N_EXPERTS, LANES))


def _post(l, x, o, mods, w_out_bf, g_ffn, w_router_p, b_router):
    tile = lambda w: pl.BlockSpec((TM, w), lambda i: (i, 0))
    return pl.pallas_call(
        _post_kernel,
        grid=(N_TILES,),
        in_specs=[tile(D_MODEL), tile(D_MODEL),
                  pl.BlockSpec((None, None, N_MOD, D_MODEL), lambda i: (l, _cond_row(i), 0, 0)),
                  pl.BlockSpec((None, D_MODEL, D_MODEL), lambda i: (l, 0, 0)),
                  pl.BlockSpec((None, 1, D_MODEL), lambda i: (l, 0, 0)),
                  pl.BlockSpec((D_MODEL, LANES), lambda i: (0, 0)),
                  pl.BlockSpec((N_EXPERTS, 1), lambda i: (0, 0))],
        out_specs=[tile(D_MODEL), tile(D_MODEL), pl.BlockSpec((2 * N_EXPERTS, TM), lambda i: (0, i)),
                   tile(LANES), pl.BlockSpec((None, N_EXPERTS, LANES), lambda i: (i, 0, 0))],
        out_shape=[jax.ShapeDtypeStruct((N_TOK, D_MODEL), F32), jax.ShapeDtypeStruct((N_TOK, D_MODEL), BF16),
                   jax.ShapeDtypeStruct((2 * N_EXPERTS, N_TOK), F32),
                   jax.ShapeDtypeStruct((N_TOK, LANES), F32),
                   jax.ShapeDtypeStruct((N_TILES, N_EXPERTS, LANES), F32)],
        compiler_params=_cparams(("arbitrary",)),
        name="post_attn",
    )(x, o, mods, w_out_bf, g_ffn.reshape(DEPTH, 1, D_MODEL), w_router_p, b_router.reshape(N_EXPERTS, 1))


def _shr(x, bits):
    return lax.shift_right_logical(x, jnp.int32(bits))


TR_BITS = TR.bit_length() - 1
ALIGN_BITS = RUN_ALIGN.bit_length() - 1


def _plan_rows(cnt_ref, off_ref, npass_ref, seg_ref):
    def per_expert(e, row0):
        def per_tile(bb, r):
            off_ref[bb * N_EXPERTS + e] = r
            return r + (_shr(cnt_ref[bb * N_EXPERTS + e] + (RUN_ALIGN - 1), ALIGN_BITS) << ALIGN_BITS)

        rows_end = lax.fori_loop(0, N_TILES, per_tile, row0)
        n = _shr(rows_end - row0 + (TR - 1), TR_BITS)
        seg_ref[e] = _shr(row0, TR_BITS)
        seg_ref[N_EXPERTS + e] = n
        seg_ref[2 * N_EXPERTS + e] = rows_end
        seg_ref[3 * N_EXPERTS + e] = row0 + (n << TR_BITS)
        return row0 + (n << TR_BITS)

    end_row = lax.fori_loop(0, N_EXPERTS, per_expert, jnp.int32(0))
    seg_ref[4 * N_EXPERTS] = _shr(end_row, TR_BITS)

    def longest(bb, carry):
        m = lax.fori_loop(0, N_EXPERTS, lambda e, m: jnp.maximum(m, cnt_ref[bb * N_EXPERTS + e]), jnp.int32(0))
        npass_ref[bb] = sum((m > k * WIN).astype(jnp.int32) for k in range(-(-TM // WIN)))
        return carry

    lax.fori_loop(0, N_TILES, longest, 0)


def _dispatch_kernel(cnt_ref, h_ref, rt_ref, et_ref, xs_in, xs_hbm, off_ref, npass_ref, seg_ref, zbuf, sem):
    del xs_in
    b = pl.program_id(0)
    slot = b % 2

    @pl.when(b == 0)
    def _():
        _plan_rows(cnt_ref, off_ref, npass_ref, seg_ref)

    key = jnp.dot(et_ref[...], _run_keys(rt_ref[...], 1).astype(BF16), preferred_element_type=F32)
    slot_j = _window_slot((N_EXPERTS * WIN, TM), 0)

    def run_copies(bb, sl, p, act):
        def one_expert(e, queue):
            left = cnt_ref[bb * N_EXPERTS + e] - p * WIN
            row = off_ref[bb * N_EXPERTS + e] + p * WIN
            whole = row + WIN <= seg_ref[2 * N_EXPERTS + e]

            @pl.when(whole & (left > 0))
            def _():
                src = pl.multiple_of(e * WIN, RUN_ALIGN)
                act(pltpu.make_async_copy(zbuf.at[sl, pl.ds(src, WIN)],
                                          xs_hbm.at[pl.ds(pl.multiple_of(row, RUN_ALIGN), WIN)], sem), queue)

            def piece(k, c):
                src = pl.multiple_of(e * WIN + k * RUN_ALIGN, RUN_ALIGN)
                dst = pl.multiple_of(row + k * RUN_ALIGN, RUN_ALIGN)
                act(pltpu.make_async_copy(zbuf.at[sl, pl.ds(src, RUN_ALIGN)],
                                          xs_hbm.at[pl.ds(dst, RUN_ALIGN)], sem), queue)
                return c

            pieces = jnp.minimum(_shr(jnp.maximum(left, 0) + (RUN_ALIGN - 1), ALIGN_BITS), WIN // RUN_ALIGN)
            lax.fori_loop(0, jnp.where(whole, 0, pieces), piece, 0)

        def expert_pair(e2, carry):
            one_expert(2 * e2, 0)
            one_expert(2 * e2 + 1, 1)
            return carry

        lax.fori_loop(0, N_EXPERTS // 2, expert_pair, 0)

    def fill_and_send(p):
        pick = jnp.where(key == slot_j + jnp.asarray(p * WIN, F32), 1.0, 0.0).astype(BF16)
        zbuf[slot] = jnp.dot(pick, h_ref[...], preferred_element_type=F32).astype(BF16)
        run_copies(b, slot, p, lambda cp, queue: cp.start(priority=queue))

    @pl.when(b > 0)
    def _():
        run_copies(b - 1, 1 - slot, npass_ref[jnp.maximum(b - 1, 0)] - 1, lambda cp, queue: cp.wait())

    fill_and_send(0)

    def more(p, carry):
        run_copies(b, slot, p - 1, lambda cp, queue: cp.wait())
        fill_and_send(p)
        return carry

    lax.fori_loop(1, npass_ref[b], more, 0)

    @pl.when(b == N_TILES - 1)
    def _():
        run_copies(b, slot, npass_ref[b] - 1, lambda cp, queue: cp.wait())


def _dispatch(cnt, h2, rt, et, xs_buf):
    smem = pl.BlockSpec(memory_space=pltpu.SMEM)
    return pl.pallas_call(
        _dispatch_kernel,
        grid_spec=pltpu.PrefetchScalarGridSpec(
            num_scalar_prefetch=1,
            grid=(N_TILES,),
            in_specs=[pl.BlockSpec((TM, D_MODEL), lambda i, *_: (i, 0)),
                      pl.BlockSpec((2 * N_EXPERTS, TM), lambda i, *_: (0, i)),
                      pl.BlockSpec((N_EXPERTS * WIN, 2 * N_EXPERTS), lambda i, *_: (0, 0)),
                      pl.BlockSpec(memory_space=pl.ANY)],
            out_specs=[pl.BlockSpec(memory_space=pl.ANY), smem, smem, smem],
            scratch_shapes=[pltpu.VMEM((2, N_EXPERTS * WIN, D_MODEL), BF16), pltpu.SemaphoreType.DMA(())]),
        out_shape=[jax.ShapeDtypeStruct((NT * TR, D_MODEL), BF16), jax.ShapeDtypeStruct((N_RUNS,), jnp.int32),
                   jax.ShapeDtypeStruct((N_TILES,), jnp.int32),
                   jax.ShapeDtypeStruct((4 * N_EXPERTS + 1,), jnp.int32)],
        input_output_aliases={4: 0},
        compiler_params=_cparams(("arbitrary",)),
        name="dispatch",
    )(cnt, h2, rt, et, xs_buf)


CAST_ROWS = 128
TILE_BUFS = 2


def _cast_rows(src_ref, dst_ref, n):
    def body(c, carry):
        rows = pl.ds(pl.multiple_of(c * CAST_ROWS, CAST_ROWS), CAST_ROWS)
        dst_ref[rows, :] = src_ref[rows, :].astype(BF16)
        return carry

    lax.fori_loop(0, n, body, 0)


def _experts_kernel(seg_ref, xs_hbm, wg_ref, wu_ref, wd_ref, ys_in, ys_hbm, wgb, wub, wdb, xbuf, ybuf, semx, semy):
    del ys_in
    e = pl.program_id(0)
    t0 = seg_ref[e]
    n = seg_ref[N_EXPERTS + e]
    _cast_rows(wg_ref, wgb, D_MODEL // CAST_ROWS)
    _cast_rows(wu_ref, wub, D_MODEL // CAST_ROWS)
    _cast_rows(wd_ref, wdb, D_EXPERT // CAST_ROWS)

    def rows(k):
        return pl.ds(pl.multiple_of((t0 + k) * TR, TR), TR)

    def fetch(k, s):
        return pltpu.make_async_copy(xs_hbm.at[rows(k)], xbuf.at[s], semx.at[s])

    def put(k, s):
        return pltpu.make_async_copy(ybuf.at[s], ys_hbm.at[rows(k)], semy.at[s])

    for j in range(TILE_BUFS - 1):
        @pl.when(j < n)
        def _():
            fetch(j, j).start()

    def tile(k, carry):
        s = k % TILE_BUFS
        ahead = k + (TILE_BUFS - 1)

        @pl.when(ahead < n)
        def _():
            fetch(ahead, ahead % TILE_BUFS).start()

        fetch(k, s).wait()

        @pl.when(k >= TILE_BUFS)
        def _():
            put(k - TILE_BUFS, s).wait()

        x = xbuf[s]
        hg = jnp.dot(x, wgb[...], preferred_element_type=F32)
        hu = jnp.dot(x, wub[...], preferred_element_type=F32)
        a = hg * (1.0 / (1.0 + jnp.exp(-hg))) * hu
        ybuf[s] = jnp.dot(a.astype(BF16), wdb[...], preferred_element_type=F32).astype(BF16)
        put(k, s).start(priority=1)
        return carry

    lax.fori_loop(0, n, tile, 0)

    for j in range(TILE_BUFS):
        last = n - TILE_BUFS + j

        @pl.when(last >= 0)
        def _():
            put(last, last % TILE_BUFS).wait()


def _experts(l, seg, xs, w_gate, w_up, w_down, ys_buf):
    wspec = lambda a, b: pl.BlockSpec((None, None, a, b), lambda e, seg: (l, e, 0, 0))
    tile_buf = pltpu.VMEM((TILE_BUFS, TR, D_MODEL), BF16)
    return pl.pallas_call(
        _experts_kernel,
        grid_spec=pltpu.PrefetchScalarGridSpec(
            num_scalar_prefetch=1,
            grid=(N_EXPERTS,),
            in_specs=[pl.BlockSpec(memory_space=pl.ANY),
                      wspec(D_MODEL, D_EXPERT), wspec(D_MODEL, D_EXPERT), wspec(D_EXPERT, D_MODEL),
                      pl.BlockSpec(memory_space=pl.ANY)],
            out_specs=pl.BlockSpec(memory_space=pl.ANY),
            scratch_shapes=[pltpu.VMEM((D_MODEL, D_EXPERT), BF16), pltpu.VMEM((D_MODEL, D_EXPERT), BF16),
                            pltpu.VMEM((D_EXPERT, D_MODEL), BF16), tile_buf, tile_buf,
                            pltpu.SemaphoreType.DMA((TILE_BUFS,)), pltpu.SemaphoreType.DMA((TILE_BUFS,))]),
        out_shape=jax.ShapeDtypeStruct((NT * TR, D_MODEL), BF16),
        input_output_aliases={5: 0},
        compiler_params=_cparams(("arbitrary",)),
        name="experts",
    )(seg, xs, w_gate, w_up, w_down, ys_buf)


def _final_kernel(off_ref, npass_ref, y_ref, rtm_ref, ek_ref, eg_ref, ys_hbm, mod_ref, g_ref, op_ref, os_ref,
                  wbuf, sem):
    i = pl.program_id(0)
    moe = _combined_moe(off_ref, npass_ref, rtm_ref, ek_ref, eg_ref, ys_hbm, wbuf, sem)
    out = _rms(y_ref[...] + mod_ref[5:6, :] * moe, g_ref[...])

    @pl.when(i < CTX_TILES)
    def _():
        op_ref[...] = out

    @pl.when(i >= CTX_TILES)
    def _():
        os_ref[...] = out


def _final(off, npass, y, rtm, ek, eg, ys, mods, g_final):
    return pl.pallas_call(
        _final_kernel,
        grid_spec=pltpu.PrefetchScalarGridSpec(
            num_scalar_prefetch=2,
            grid=(N_TILES,),
            in_specs=_combine_specs() + [
                pl.BlockSpec((None, None, N_MOD, D_MODEL), lambda i, *_: (DEPTH - 1, _cond_row(i), 0, 0)),
                pl.BlockSpec((1, D_MODEL), lambda i, *_: (0, 0))],
            out_specs=[pl.BlockSpec((TM, D_MODEL), lambda i, *_: (jnp.minimum(i, CTX_TILES - 1), 0)),
                       pl.BlockSpec((TM, D_MODEL), lambda i, *_: (jnp.maximum(i - CTX_TILES, 0), 0))],
            scratch_shapes=_combine_scratch()),
        out_shape=[jax.ShapeDtypeStruct((N_CTX, D_MODEL), F32), jax.ShapeDtypeStruct((N_DEC, D_MODEL), F32)],
        compiler_params=_cparams(("arbitrary",)),
        name="final_norm",
    )(off, npass, y, rtm, ek, eg, ys, mods, g_final.reshape(1, D_MODEL))


def _prep_kernel(win_ref, wout_ref, wi_ref, wo_ref):
    w = win_ref[...]
    split = KPE_OFF + MLA_ROPE_DIM
    wi_ref[...] = jnp.concatenate([w[:, :split], jnp.zeros((CAST_ROWS, SW_OFF - split), F32), w[:, split:]],
                                  axis=1).astype(BF16)
    wo_ref[...] = wout_ref[...].astype(BF16)


def _prep_weights(w_in, w_out):
    n = D_MODEL // CAST_ROWS
    shift = NA_OUT // CAST_ROWS
    return pl.pallas_call(
        _prep_kernel,
        grid=(DEPTH, n),
        in_specs=[pl.BlockSpec((None, CAST_ROWS, IN_WIDTH), lambda l, j: (l, j, 0)),
                  pl.BlockSpec((None, CAST_ROWS, D_MODEL), lambda l, j: (l, (j + shift) % n, 0))],
        out_specs=[pl.BlockSpec((None, CAST_ROWS, IN_PAD), lambda l, j: (l, j, 0)),
                   pl.BlockSpec((None, CAST_ROWS, D_MODEL), lambda l, j: (l, j, 0))],
        out_shape=[jax.ShapeDtypeStruct((DEPTH, D_MODEL, IN_PAD), BF16),
                   jax.ShapeDtypeStruct((DEPTH, D_MODEL, D_MODEL), BF16)],
        compiler_params=_cparams(("arbitrary", "arbitrary")),
        name="prep_weights",
    )(w_in, w_out)


def _rope_tables(rot_dim):
    t = jnp.arange(DEC_SEQ, dtype=jnp.int32)
    row = (t // GRID_W).astype(F32)
    col = (t % GRID_W).astype(F32)
    per_axis = rot_dim // 2
    inv = ROPE_BASE ** (-jnp.arange(0, per_axis, 2, dtype=F32) / per_axis)
    ang = jnp.concatenate([row[:, None] * inv, col[:, None] * inv], axis=-1)
    cos, sin = jnp.cos(ang), jnp.sin(ang)
    zero = jnp.zeros_like(sin)
    rep = LANES // rot_dim
    tabs = [jnp.concatenate([cos, cos], -1), jnp.concatenate([zero, sin], -1), jnp.concatenate([-sin, zero], -1)]
    return jnp.stack([jnp.tile(a, (1, rep)) for a in tabs])


def kernel(x_prompt, x_sample, c, cache_na_k, cache_na_v, cache_mla_ckv, cache_mla_kpe, cache_swa_k, cache_swa_v,
           c_ctx, w_ada, b_ada, g_attn, w_in, g_mla_q, w_mla_qb, g_mla_kv, w_mla_kvb, na_rpb, swa_sink, w_out,
           g_ffn, w_router, b_router, w_gate, w_up, w_down, g_final):
    cond = jnp.concatenate([c_ctx[None], c, jnp.zeros((COND_ROWS - 1 - DEC_BATCH, D_MODEL), F32)], axis=0)
    mods = _ada(cond, w_ada, b_ada).reshape(DEPTH, COND_ROWS, N_MOD, D_MODEL)
    bias = _na_bias(na_rpb)
    t_mla = _rope_tables(MLA_ROPE_DIM)
    t_swa = _rope_tables(HEAD_DIM)

    w_in_p, w_out_p = _prep_weights(w_in, w_out)
    wq = w_mla_qb.reshape(DEPTH, MLA_Q_RANK, MLA_HEADS, MLA_QK_DIM)
    w_rope = jnp.pad(wq[..., MLA_NOPE_DIM:], ((0, 0), (0, 0), (0, 0), (0, LANES - MLA_ROPE_DIM)))
    w_qb_p = jnp.concatenate([wq[..., :MLA_NOPE_DIM].reshape(DEPTH, MLA_Q_RANK, MQ_NOPE),
                              w_rope.reshape(DEPTH, MLA_Q_RANK, MLA_HEADS * LANES)], axis=-1).astype(BF16)
    w_router_p = jnp.pad(w_router, ((0, 0), (0, LANES - N_EXPERTS)))
    et, ek, eg = _spread_consts()

    c_na_k = cache_na_k.reshape(DEC_BATCH, DEPTH, PAST_LEN, NA_OUT)
    c_na_v = cache_na_v.reshape(DEC_BATCH, DEPTH, PAST_LEN, NA_OUT)
    c_sw_k = cache_swa_k.reshape(DEC_BATCH, DEPTH, PAST_LEN, LANES)
    c_sw_v = cache_swa_v.reshape(DEC_BATCH, DEPTH, PAST_LEN, LANES)

    y = ys = rtm = off = npass = None
    caches = [jnp.zeros((BATCH, DEPTH, SEQ, w), F32) for w in CACHE_WIDTHS]
    o = jnp.zeros((N_TOK, D_MODEL), BF16)
    xs = jnp.zeros((NT * TR, D_MODEL), BF16)
    ys = jnp.zeros((NT * TR, D_MODEL), BF16)
    for l in range(DEPTH):
        if l == 0:
            srcs = [x_prompt.reshape(N_CTX, D_MODEL), x_sample.reshape(N_DEC, D_MODEL)]
        else:
            srcs = [off, npass, y, rtm, ek, eg, ys]
        x, na, mq, ckv, kpe, sw, *caches = _pre(l == 0, l, srcs, caches, mods, g_attn, w_in_p, g_mla_q, w_qb_p,
                                                g_mla_kv, t_mla, t_swa)
        o = _ctx_attn(l, swa_sink, na, mq, ckv, kpe, sw, w_mla_kvb, o)
        o = _mla_dec(l, mq, ckv, kpe, cache_mla_ckv, cache_mla_kpe, w_mla_kvb, o)
        o = _swa_dec(l, swa_sink, sw, c_sw_k, c_sw_v, o)
        o = _na_dec(l, na, c_na_k, c_na_v, bias, o)
        y, h2, rt, rtm, cnt = _post(l, x, o, mods, w_out_p, g_ffn, w_router_p, b_router)
        xs, off, npass, seg = _dispatch(cnt[:, :, 0].astype(jnp.int32).reshape(-1), h2, rt, et, xs)
        ys = _experts(l, seg, xs, w_gate, w_up, w_down, ys)
    y_prompt, y_sample = _final(off, npass, y, rtm, ek, eg, ys, mods, g_final)

    heads = lambda a, n: a.reshape(BATCH, DEPTH, SEQ, n, HEAD_DIM)
    return (y_prompt.reshape(BATCH, SEQ, D_MODEL), y_sample.reshape(DEC_BATCH, DEC_SEQ, D_MODEL),
            heads(caches[0], NA_HEADS), heads(caches[1], NA_HEADS), caches[2], caches[3],
            heads(caches[4], SWA_KV_HEADS), heads(caches[5], SWA_KV_HEADS))
```

```python
import functools

import jax
import jax.numpy as jnp
from jax import lax
from jax.experimental import pallas as pl
from jax.experimental.pallas import tpu as pltpu

D_MODEL = 1024
BATCH = 16
SEQ = 256
DEPTH = 4
DEC_BATCH = 2
DEC_SEQ = 2048
PAST_LEN = 256
GRID_W = 64
HEAD_DIM = 64
NA_HEADS = 4
NA_WIN_H = 8
NA_WIN_W = 16
MLA_HEADS = 6
MLA_Q_RANK = 256
MLA_KV_RANK = 128
MLA_NOPE_DIM = 64
MLA_ROPE_DIM = 32
MLA_V_DIM = 64
MLA_QK_DIM = MLA_NOPE_DIM + MLA_ROPE_DIM
SWA_HEADS = 6
SWA_KV_HEADS = 2
SWA_WINDOW = 128
ROPE_BASE = 10000.0
N_EXPERTS = 16
N_EXPERT_GROUPS = 4
EXPERTS_PER_GROUP = 4
D_EXPERT = 512
RMS_EPS = 1e-6
N_MOD = 6

NA_IN = 3 * NA_HEADS * HEAD_DIM
MLA_IN = MLA_Q_RANK + MLA_KV_RANK + MLA_ROPE_DIM
SWA_IN = (SWA_HEADS + 2 * SWA_KV_HEADS) * HEAD_DIM
IN_WIDTH = NA_IN + MLA_IN + SWA_IN
NA_OUT = NA_HEADS * HEAD_DIM
MLA_OUT = MLA_HEADS * MLA_V_DIM
SWA_OUT = SWA_HEADS * HEAD_DIM

LANES = 128
N_CTX = BATCH * SEQ
N_DEC = DEC_BATCH * DEC_SEQ
N_TOK = N_CTX + N_DEC
TM = 256
N_TILES = N_TOK // TM
CTX_TILES = N_CTX // TM
DEC_TILES_PER_BATCH = DEC_SEQ // TM
COND_ROWS = 8
KPE_OFF = NA_IN + MLA_Q_RANK + MLA_KV_RANK
SW_OFF = KPE_OFF + LANES
IN_PAD = SW_OFF + SWA_IN
MQ_NOPE = MLA_HEADS * MLA_NOPE_DIM
MQ_W = MQ_NOPE + MLA_HEADS * LANES
MLA_SCALE = MLA_QK_DIM ** -0.5
MLA_GROUP = 3
NA_ROWS_PER_STEP = TM // GRID_W
NA_KEY_ROWS = 12
NA_KEYS = NA_KEY_ROWS * GRID_W
SWA_KEYS = 512
NEG = -1e30
TR = 512
N_ASSIGN = 2 * N_TOK
RUN_ALIGN = 16
WIN = 80
N_RUNS = N_TILES * N_EXPERTS
NT = -(-(N_ASSIGN + N_RUNS * (RUN_ALIGN - 1) + N_EXPERTS * (TR - 1) + WIN) // TR)

F32 = jnp.float32
BF16 = jnp.bfloat16
VMEM_LIMIT = 56 * 1024 * 1024


def _cparams(sem):
    return pltpu.CompilerParams(dimension_semantics=sem, vmem_limit_bytes=VMEM_LIMIT)


def _cond_row(i):
    return jnp.where(i < CTX_TILES, 0, 1 + (i - CTX_TILES) // DEC_TILES_PER_BATCH)


def _rope_blk(i):
    return jnp.where(i < CTX_TILES, 0, (i - CTX_TILES) % DEC_TILES_PER_BATCH)


def _rms(x, g):
    ms = jnp.mean(x * x, axis=-1, keepdims=True)
    return x * lax.rsqrt(ms + RMS_EPS) * g


def _dot(a, b):
    return jnp.dot(a.astype(BF16), b.astype(BF16), preferred_element_type=F32)


def _dot_nt(a, b):
    return lax.dot_general(a.astype(BF16), b.astype(BF16), (((1,), (1,)), ((), ())),
                           preferred_element_type=F32)


def _ada_kernel(c_ref, w_ref, b_ref, o_ref):
    c = c_ref[...]
    s = c * (1.0 / (1.0 + jnp.exp(-c)))
    o_ref[...] = _dot(s, w_ref[...]) + b_ref[...]


def _ada(cond, w_ada, b_ada):
    tn = 1536
    n = N_MOD * D_MODEL
    return pl.pallas_call(
        _ada_kernel,
        grid=(DEPTH, n // tn),
        in_specs=[pl.BlockSpec((COND_ROWS, D_MODEL), lambda l, j: (0, 0)),
                  pl.BlockSpec((None, D_MODEL, tn), lambda l, j: (l, 0, j)),
                  pl.BlockSpec((None, 1, tn), lambda l, j: (l, 0, j))],
        out_specs=pl.BlockSpec((None, COND_ROWS, tn), lambda l, j: (l, 0, j)),
        out_shape=jax.ShapeDtypeStruct((DEPTH, COND_ROWS, n), F32),
        compiler_params=_cparams(("arbitrary", "arbitrary")),
        name="ada_mod",
    )(cond, w_ada, b_ada.reshape(DEPTH, 1, n))


def _bias_kernel(rpb_ref, o_ref):
    g = pl.program_id(0)
    base = g * ((2 * NA_WIN_H - 1) * (2 * NA_WIN_W - 1))
    qc = lax.broadcasted_iota(jnp.int32, (GRID_W, GRID_W), 0)
    kc = lax.broadcasted_iota(jnp.int32, (GRID_W, GRID_W), 1)
    dc = jnp.clip(kc - qc + (NA_WIN_W - 1), 0, 2 * NA_WIN_W - 2)
    cs = jnp.clip(qc - NA_WIN_W // 2, 0, GRID_W - NA_WIN_W)
    col_ok = (kc >= cs) & (kc < cs + NA_WIN_W)
    neg = jnp.full((GRID_W, GRID_W), NEG, F32)
    tabs = []
    for a in range(2 * NA_WIN_H - 1):
        t = jnp.zeros((GRID_W, GRID_W), F32)
        for b in range(2 * NA_WIN_W - 1):
            t = jnp.where(dc == b, rpb_ref[base + a * (2 * NA_WIN_W - 1) + b], t)
        tabs.append(jnp.where(col_ok, t, NEG))
    for p in range(3):
        for qi in range(NA_ROWS_PER_STEP):
            for kj in range(NA_KEY_ROWS):
                if p == 0:
                    ok, dr = kj < NA_WIN_H, kj - qi + 7
                elif p == 1:
                    ok, dr = qi <= kj < qi + NA_WIN_H, kj - qi + 3
                else:
                    ok, dr = kj >= NA_KEY_ROWS - NA_WIN_H, kj - qi - 1
                blk = tabs[dr] if ok else neg
                o_ref[p, qi * GRID_W:(qi + 1) * GRID_W, kj * GRID_W:(kj + 1) * GRID_W] = blk


def _na_bias(na_rpb):
    return pl.pallas_call(
        _bias_kernel,
        grid=(DEPTH * NA_HEADS,),
        in_specs=[pl.BlockSpec(memory_space=pltpu.SMEM)],
        out_specs=pl.BlockSpec((None, 3, None, TM, NA_KEYS),
                               lambda g: (g // NA_HEADS, 0, g % NA_HEADS, 0, 0)),
        out_shape=jax.ShapeDtypeStruct((DEPTH, 3, NA_HEADS, TM, NA_KEYS), F32),
        compiler_params=_cparams(("arbitrary",)),
        name="na_bias",
    )(na_rpb.reshape(-1))


def _rope128(x, t_ref, half):
    return (x * t_ref[0] + pltpu.roll(x, half, 1) * t_ref[1]
            + pltpu.roll(x, LANES - half, 1) * t_ref[2])


def _spread_consts():
    slot_e = jnp.arange(N_EXPERTS * WIN, dtype=jnp.int32) // WIN
    idx32 = jnp.arange(2 * N_EXPERTS, dtype=jnp.int32)
    idx128 = jnp.arange(LANES, dtype=jnp.int32)
    et = (idx32[None, :] == slot_e[:, None] + N_EXPERTS).astype(BF16)
    ek = (idx128[:, None] == slot_e[None, :] + N_EXPERTS).astype(BF16)
    eg = (idx128[:, None] == slot_e[None, :]).astype(BF16)
    return et, ek, eg


def _window_slot(shape, axis):
    r = lax.broadcasted_iota(jnp.int32, shape, axis).astype(F32)
    return r - WIN * jnp.floor((r + 0.5) * (1.0 / WIN))


def _run_keys(chosen, token_axis):
    a = lax.broadcasted_iota(jnp.int32, (TM, TM), 0)
    b = lax.broadcasted_iota(jnp.int32, (TM, TM), 1)
    if token_axis == 1:
        rank = jnp.dot(chosen.astype(BF16), (a < b).astype(BF16), preferred_element_type=F32)
    else:
        rank = jnp.dot((b < a).astype(BF16), chosen.astype(BF16), preferred_element_type=F32)
    return jnp.where(chosen > 0.5, rank, -1.0)


def _windows_start(i, slot, p, off_ref, ys_hbm, wbuf, sem):
    for e in range(N_EXPERTS):
        row = pl.multiple_of(off_ref[i * N_EXPERTS + e] + p * WIN, RUN_ALIGN)
        pltpu.make_async_copy(ys_hbm.at[pl.ds(row, WIN)], wbuf.at[slot, pl.ds(e * WIN, WIN)],
                              sem.at[slot]).start(priority=e % 2)


def _windows_wait(slot, ys_hbm, wbuf, sem):
    pltpu.make_async_copy(ys_hbm.at[pl.ds(0, N_EXPERTS * WIN)], wbuf.at[slot], sem.at[slot]).wait()


def _combined_moe(off_ref, npass_ref, rtm_ref, ek_ref, eg_ref, ys_hbm, wbuf, sem):
    i = pl.program_id(0)
    slot = i % 2

    @pl.when(i == 0)
    def _():
        _windows_start(0, 0, 0, off_ref, ys_hbm, wbuf, sem)

    @pl.when(i + 1 < N_TILES)
    def _():
        _windows_start(i + 1, 1 - slot, 0, off_ref, ys_hbm, wbuf, sem)

    r = rtm_ref[...]
    key = jnp.dot(_run_keys(r, 0).astype(BF16), ek_ref[...], preferred_element_type=F32)
    gate = jnp.dot(r.astype(BF16), eg_ref[...], preferred_element_type=F32)
    slot_j = _window_slot((TM, N_EXPERTS * WIN), 1)

    def contrib(p):
        g = jnp.where(key == slot_j + jnp.asarray(p * WIN, F32), gate, 0.0).astype(BF16)
        return jnp.dot(g, wbuf[slot], preferred_element_type=F32)

    _windows_wait(slot, ys_hbm, wbuf, sem)
    acc = contrib(0)

    def extra(p, acc):
        _windows_start(i, slot, p, off_ref, ys_hbm, wbuf, sem)
        _windows_wait(slot, ys_hbm, wbuf, sem)
        return acc + contrib(p)

    return lax.fori_loop(1, npass_ref[i], extra, acc)


def _pre_kernel(first, *refs):
    i = pl.program_id(0)
    if first:
        (xp_ref, xs_ref, mod_ref, g_ref, win_ref, gq_ref, wqb_ref, gkv_ref, tm_ref, ts_ref, _, _, _, _, _, _,
         xo_ref, na_ref, mq_ref, ckv_ref, kpe_ref, sw_ref, *cache_refs) = refs
        x = jnp.where(i < CTX_TILES, xp_ref[...], xs_ref[...])
    else:
        (off_ref, npass_ref, y_ref, rtm_ref, ek_ref, eg_ref, ys_hbm, modp_ref, mod_ref, g_ref, win_ref, gq_ref,
         wqb_ref, gkv_ref, tm_ref, ts_ref, _, _, _, _, _, _,
         xo_ref, na_ref, mq_ref, ckv_ref, kpe_ref, sw_ref, *rest) = refs
        *cache_refs, wbuf, sem = rest
        moe = _combined_moe(off_ref, npass_ref, rtm_ref, ek_ref, eg_ref, ys_hbm, wbuf, sem)
        x = y_ref[...] + modp_ref[5:6, :] * moe
    xo_ref[...] = x
    cnk_ref, cnv_ref, cckv_ref, ckpe_ref, csk_ref, csv_ref = cache_refs

    h = _rms(x, g_ref[...]) * (1.0 + mod_ref[1:2, :]) + mod_ref[0:1, :]
    z = jnp.dot(h.astype(BF16), win_ref[...], preferred_element_type=F32)
    na_ref[...] = z[:, :NA_IN].astype(BF16)
    cq = _rms(z[:, NA_IN:NA_IN + MLA_Q_RANK], gq_ref[...])
    ckv = _rms(z[:, NA_IN + MLA_Q_RANK:KPE_OFF], gkv_ref[...])
    ckv_ref[...] = ckv.astype(BF16)
    mq = jnp.dot(cq.astype(BF16), wqb_ref[...], preferred_element_type=F32)
    kpe = z[:, KPE_OFF:SW_OFF]
    sw = z[:, SW_OFF:IN_PAD]
    mq_ref[:, :MQ_NOPE] = mq[:, :MQ_NOPE].astype(BF16)
    sw_ref[:, SWA_OUT + LANES:] = sw[:, SWA_OUT + LANES:].astype(BF16)

    @pl.when(i < CTX_TILES)
    def _():
        mq_ref[:, MQ_NOPE:] = (mq[:, MQ_NOPE:] * MLA_SCALE).astype(BF16)
        kpe_ref[...] = kpe.astype(BF16)
        sw_ref[:, :SWA_OUT + LANES] = sw[:, :SWA_OUT + LANES].astype(BF16)
        cnk_ref[...] = z[:, NA_OUT:2 * NA_OUT]
        cnv_ref[...] = z[:, 2 * NA_OUT:NA_IN]
        cckv_ref[...] = ckv
        ckpe_ref[...] = kpe[:, :MLA_ROPE_DIM]
        csk_ref[...] = sw[:, SWA_OUT:SWA_OUT + LANES]
        csv_ref[...] = sw[:, SWA_OUT + LANES:]

    @pl.when(i >= CTX_TILES)
    def _():
        for c in range(MQ_NOPE // LANES, MQ_W // LANES):
            roped = _rope128(mq[:, c * LANES:(c + 1) * LANES], tm_ref, MLA_ROPE_DIM // 2)
            mq_ref[:, c * LANES:(c + 1) * LANES] = (roped * MLA_SCALE).astype(BF16)
        kpe_ref[...] = _rope128(kpe, tm_ref, MLA_ROPE_DIM // 2).astype(BF16)
        for c in range((SWA_OUT + LANES) // LANES):
            sw_ref[:, c * LANES:(c + 1) * LANES] = _rope128(sw[:, c * LANES:(c + 1) * LANES], ts_ref,
                                                            HEAD_DIM // 2).astype(BF16)


def _combine_scratch():
    return [pltpu.VMEM((2, N_EXPERTS * WIN, D_MODEL), BF16), pltpu.SemaphoreType.DMA((2,))]


def _combine_specs():
    spread = pl.BlockSpec((LANES, N_EXPERTS * WIN), lambda i, *_: (0, 0))
    return [pl.BlockSpec((TM, D_MODEL), lambda i, *_: (i, 0)),
            pl.BlockSpec((TM, LANES), lambda i, *_: (i, 0)),
            spread, spread, pl.BlockSpec(memory_space=pl.ANY)]


CACHE_WIDTHS = (NA_OUT, NA_OUT, MLA_KV_RANK, MLA_ROPE_DIM, LANES, LANES)


def _pre(first, l, xs, caches, mods, g_attn, w_in_p, g_mla_q, w_qb_p, g_mla_kv, t_mla, t_swa):
    tile = lambda w: pl.BlockSpec((TM, w), lambda i, *_: (i, 0))
    ctx_tile = lambda w: pl.BlockSpec((TM, w), lambda i, *_: (jnp.minimum(i, CTX_TILES - 1), 0))
    slab = lambda w: pl.BlockSpec((None, None, SEQ, w), lambda i, *_: (jnp.minimum(i, BATCH - 1), l, 0, 0))
    mod_spec = lambda ll: pl.BlockSpec((None, None, N_MOD, D_MODEL), lambda i, *_: (ll, _cond_row(i), 0, 0))
    vec = lambda w: pl.BlockSpec((None, 1, w), lambda i, *_: (l, 0, 0))
    if first:
        in_specs = [ctx_tile(D_MODEL), pl.BlockSpec((TM, D_MODEL), lambda i: (jnp.maximum(i - CTX_TILES, 0), 0))]
    else:
        in_specs = _combine_specs() + [mod_spec(l - 1)]
    in_specs += [mod_spec(l), vec(D_MODEL),
                 pl.BlockSpec((None, D_MODEL, IN_PAD), lambda i, *_: (l, 0, 0)),
                 vec(MLA_Q_RANK),
                 pl.BlockSpec((None, MLA_Q_RANK, MQ_W), lambda i, *_: (l, 0, 0)),
                 vec(MLA_KV_RANK),
                 pl.BlockSpec((3, TM, LANES), lambda i, *_: (0, _rope_blk(i), 0)),
                 pl.BlockSpec((3, TM, LANES), lambda i, *_: (0, _rope_blk(i), 0))]
    in_specs += [pl.BlockSpec(memory_space=pl.ANY)] * len(CACHE_WIDTHS)
    widths = [D_MODEL, NA_IN, MQ_W, MLA_KV_RANK, LANES, SWA_IN]
    args = list(xs) + ([mods] if not first else []) + [
        mods, g_attn.reshape(DEPTH, 1, D_MODEL), w_in_p, g_mla_q.reshape(DEPTH, 1, MLA_Q_RANK), w_qb_p,
        g_mla_kv.reshape(DEPTH, 1, MLA_KV_RANK), t_mla, t_swa]
    aliases = {len(args) + n: len(widths) + n for n in range(len(CACHE_WIDTHS))}
    return pl.pallas_call(
        functools.partial(_pre_kernel, first),
        grid_spec=pltpu.PrefetchScalarGridSpec(
            num_scalar_prefetch=0 if first else 2,
            grid=(N_TILES,),
            in_specs=in_specs,
            out_specs=[tile(w) for w in widths] + [slab(w) for w in CACHE_WIDTHS],
            scratch_shapes=[] if first else _combine_scratch()),
        out_shape=([jax.ShapeDtypeStruct((N_TOK, w), BF16 if n else F32) for n, w in enumerate(widths)]
                   + [jax.ShapeDtypeStruct((BATCH, DEPTH, SEQ, w), F32) for w in CACHE_WIDTHS]),
        input_output_aliases=aliases,
        compiler_params=_cparams(("arbitrary",)),
        name="pre_attn",
    )(*args, *caches)


def _softmax_parts(parts, sink=None):
    m = parts[0].max(axis=-1, keepdims=True)
    for s in parts[1:]:
        m = jnp.maximum(m, s.max(axis=-1, keepdims=True))
    if sink is not None:
        m = jnp.maximum(m, sink)
    ps = [jnp.exp(s - m) for s in parts]
    den = ps[0].sum(axis=-1, keepdims=True)
    for p in ps[1:]:
        den = den + p.sum(axis=-1, keepdims=True)
    if sink is not None:
        den = den + jnp.exp(sink - m)
    return ps, 1.0 / den


def _mla_heads(mq_ref, wkvb_ref, kcat, o_ref):
    rows = mq_ref.shape[0]
    for g in range(MLA_HEADS // MLA_GROUP):
        heads = range(g * MLA_GROUP, (g + 1) * MLA_GROUP)
        qs = []
        for h in heads:
            wk = wkvb_ref[:, h * 2 * HEAD_DIM:h * 2 * HEAD_DIM + MLA_NOPE_DIM]
            qa = _dot_nt(mq_ref[:, h * MLA_NOPE_DIM:(h + 1) * MLA_NOPE_DIM], wk) * MLA_SCALE
            qr = mq_ref[:, MQ_NOPE + h * LANES:MQ_NOPE + (h + 1) * LANES]
            qs.append(jnp.concatenate([qa.astype(BF16), qr], axis=1))
        s = _dot_nt(jnp.concatenate(qs, axis=0), kcat[...])
        p = jnp.exp(s - s.max(axis=-1, keepdims=True))
        inv = 1.0 / p.sum(axis=-1, keepdims=True)
        lat = _dot(p, kcat[:, :LANES]) * inv
        for n, h in enumerate(heads):
            wv = wkvb_ref[:, h * 2 * HEAD_DIM + MLA_NOPE_DIM:(h + 1) * 2 * HEAD_DIM]
            o_ref[:, h * MLA_V_DIM:(h + 1) * MLA_V_DIM] = _dot(lat[n * rows:(n + 1) * rows], wv).astype(o_ref.dtype)


def _ctx_attn_kernel(l, sink_ref, na_ref, mq_ref, ckv_ref, kpe_ref, sw_ref, wkvb_ref, oin_ref, o_ref):
    del oin_ref
    scale = HEAD_DIM ** -0.5
    _mla_heads(mq_ref, wkvb_ref, jnp.concatenate([ckv_ref[...], kpe_ref[...]], axis=1), o_ref)
    for h in range(SWA_HEADS):
        kh = h // (SWA_HEADS // SWA_KV_HEADS)
        q = sw_ref[:, h * HEAD_DIM:(h + 1) * HEAD_DIM] * scale
        k = sw_ref[:, SWA_OUT + kh * HEAD_DIM:SWA_OUT + (kh + 1) * HEAD_DIM]
        v = sw_ref[:, SWA_OUT + LANES + kh * HEAD_DIM:SWA_OUT + LANES + (kh + 1) * HEAD_DIM]
        (p,), inv = _softmax_parts([_dot_nt(q, k)], sink_ref[l, h])
        o_ref[:, MLA_OUT + h * HEAD_DIM:MLA_OUT + (h + 1) * HEAD_DIM] = (_dot(p, v) * inv).astype(o_ref.dtype)
    for h in range(NA_HEADS):
        q = na_ref[:, h * HEAD_DIM:(h + 1) * HEAD_DIM] * scale
        k = na_ref[:, NA_OUT + h * HEAD_DIM:NA_OUT + (h + 1) * HEAD_DIM]
        v = na_ref[:, 2 * NA_OUT + h * HEAD_DIM:2 * NA_OUT + (h + 1) * HEAD_DIM]
        (p,), inv = _softmax_parts([_dot_nt(q, k)])
        o_ref[:, MLA_OUT + SWA_OUT + h * HEAD_DIM:MLA_OUT + SWA_OUT + (h + 1) * HEAD_DIM] = (
            _dot(p, v) * inv).astype(o_ref.dtype)


def _ctx_attn(l, sink, na, mq, ckv, kpe, sw, w_kvb, o):
    tile = lambda w: pl.BlockSpec((SEQ, w), lambda b: (b, 0))
    return pl.pallas_call(
        functools.partial(_ctx_attn_kernel, l),
        grid=(BATCH,),
        in_specs=[pl.BlockSpec(memory_space=pltpu.SMEM), tile(NA_IN), tile(MQ_W), tile(MLA_KV_RANK), tile(LANES),
                  tile(SWA_IN), pl.BlockSpec((None, MLA_KV_RANK, MLA_HEADS * 2 * HEAD_DIM), lambda b: (l, 0, 0)),
                  pl.BlockSpec(memory_space=pl.ANY)],
        out_specs=tile(D_MODEL),
        out_shape=jax.ShapeDtypeStruct((N_TOK, D_MODEL), BF16),
        input_output_aliases={7: 0},
        compiler_params=_cparams(("arbitrary",)),
        name="ctx_attn",
    )(sink, na, mq, ckv, kpe, sw, w_kvb, o)


def _dec_row(b, j):
    return CTX_TILES + b * DEC_TILES_PER_BATCH + j


def _na_dec_kernel(q_ref, k_ref, v_ref, ck_ref, cv_ref, bias_ref, oin_ref, o_ref):
    del oin_ref
    scale = HEAD_DIM ** -0.5
    j = pl.program_id(1)
    w0 = jnp.clip(j * NA_ROWS_PER_STEP - NA_WIN_H // 2, 0, DEC_SEQ // GRID_W - NA_KEY_ROWS)
    start = pl.multiple_of(w0 * GRID_W, GRID_W)
    for h in range(NA_HEADS):
        sl = slice(h * HEAD_DIM, (h + 1) * HEAD_DIM)
        q = q_ref[:, sl] * scale
        k = k_ref[pl.ds(start, NA_KEYS), sl]
        v = v_ref[pl.ds(start, NA_KEYS), sl]
        s_nb = _dot_nt(q, k) + bias_ref[h]
        s_ctx = _dot_nt(q, ck_ref[:, sl])
        (p_nb, p_ctx), inv = _softmax_parts([s_nb, s_ctx])
        o_ref[:, sl] = ((_dot(p_nb, v) + _dot(p_ctx, cv_ref[:, sl])) * inv).astype(o_ref.dtype)


def _na_dec(l, na, ck, cv, bias, o):
    pat = lambda j: jnp.where(j == 0, 0, jnp.where(j == DEC_TILES_PER_BATCH - 1, 2, 1))
    return pl.pallas_call(
        _na_dec_kernel,
        grid=(DEC_BATCH, DEC_TILES_PER_BATCH),
        in_specs=[pl.BlockSpec((TM, NA_OUT), lambda b, j: (_dec_row(b, j), 0)),
                  pl.BlockSpec((DEC_SEQ, NA_OUT), lambda b, j: (N_CTX // DEC_SEQ + b, 1)),
                  pl.BlockSpec((DEC_SEQ, NA_OUT), lambda b, j: (N_CTX // DEC_SEQ + b, 2)),
                  pl.BlockSpec((None, None, PAST_LEN, NA_OUT), lambda b, j: (b, l, 0, 0)),
                  pl.BlockSpec((None, None, PAST_LEN, NA_OUT), lambda b, j: (b, l, 0, 0)),
                  pl.BlockSpec((None, None, NA_HEADS, TM, NA_KEYS), lambda b, j: (l, pat(j), 0, 0, 0)),
                  pl.BlockSpec(memory_space=pl.ANY)],
        out_specs=pl.BlockSpec((TM, NA_OUT), lambda b, j: (_dec_row(b, j), (MLA_OUT + SWA_OUT) // NA_OUT)),
        out_shape=jax.ShapeDtypeStruct((N_TOK, D_MODEL), BF16),
        input_output_aliases={6: 0},
        compiler_params=_cparams(("arbitrary", "arbitrary")),
        name="na_dec",
    )(na, na, na, ck, cv, bias, o)


KCAT_ROWS = 256


def _mla_dec_kernel(mq_ref, ckv_ref, kpe_ref, cckv_ref, ckpe_ref, wkvb_ref, oin_ref, o_ref, kcat):
    del oin_ref

    def assemble(c, carry):
        rows = pl.ds(pl.multiple_of(c * KCAT_ROWS, KCAT_ROWS), KCAT_ROWS)
        kcat[rows, :LANES] = ckv_ref[rows, :]
        kcat[rows, LANES:] = kpe_ref[rows, :]
        return carry

    first = pl.program_id(1) == 0
    lax.fori_loop(0, jnp.where(first, DEC_SEQ // KCAT_ROWS, 0), assemble, 0)

    def context(c, carry):
        kcat[DEC_SEQ:, :LANES] = cckv_ref[...].astype(BF16)
        kcat[DEC_SEQ:, LANES:] = jnp.concatenate(
            [ckpe_ref[...], jnp.zeros((PAST_LEN, LANES - MLA_ROPE_DIM), F32)], axis=1).astype(BF16)
        return carry

    lax.fori_loop(0, jnp.where(first, 1, 0), context, 0)
    _mla_heads(mq_ref, wkvb_ref, kcat, o_ref)


def _mla_dec(l, mq, ckv, kpe, cckv, ckpe, w_kvb, o):
    return pl.pallas_call(
        _mla_dec_kernel,
        grid=(DEC_BATCH, DEC_TILES_PER_BATCH),
        in_specs=[pl.BlockSpec((TM, MQ_W), lambda b, j: (_dec_row(b, j), 0)),
                  pl.BlockSpec((DEC_SEQ, MLA_KV_RANK), lambda b, j: (N_CTX // DEC_SEQ + b, 0)),
                  pl.BlockSpec((DEC_SEQ, LANES), lambda b, j: (N_CTX // DEC_SEQ + b, 0)),
                  pl.BlockSpec((None, None, PAST_LEN, MLA_KV_RANK), lambda b, j: (b, l, 0, 0)),
                  pl.BlockSpec((None, None, PAST_LEN, MLA_ROPE_DIM), lambda b, j: (b, l, 0, 0)),
                  pl.BlockSpec((None, MLA_KV_RANK, MLA_HEADS * 2 * HEAD_DIM), lambda b, j: (l, 0, 0)),
                  pl.BlockSpec(memory_space=pl.ANY)],
        out_specs=pl.BlockSpec((TM, MLA_OUT), lambda b, j: (_dec_row(b, j), 0)),
        out_shape=jax.ShapeDtypeStruct((N_TOK, D_MODEL), BF16),
        scratch_shapes=[pltpu.VMEM((DEC_SEQ + PAST_LEN, 2 * LANES), BF16)],
        input_output_aliases={6: 0},
        compiler_params=_cparams(("arbitrary", "arbitrary")),
        name="mla_dec",
    )(mq, ckv, kpe, cckv, ckpe, w_kvb, o)


def _swa_dec_kernel(l, sink_ref, q_ref, k_ref, v_ref, ck_ref, cv_ref, oin_ref, o_ref):
    del oin_ref
    scale = HEAD_DIM ** -0.5
    j = pl.program_id(1)
    start = pl.multiple_of(jnp.clip(j * TM - SWA_WINDOW, 0, DEC_SEQ - SWA_KEYS), SWA_WINDOW)
    qpos = j * TM + lax.broadcasted_iota(jnp.int32, (TM, SWA_KEYS), 0)
    kpos = start + lax.broadcasted_iota(jnp.int32, (TM, SWA_KEYS), 1)
    band = jnp.abs(qpos - kpos) <= SWA_WINDOW
    for h in range(SWA_HEADS):
        kh = h // (SWA_HEADS // SWA_KV_HEADS)
        sl = slice(kh * HEAD_DIM, (kh + 1) * HEAD_DIM)
        q = q_ref[:, h * HEAD_DIM:(h + 1) * HEAD_DIM] * scale
        s_loc = jnp.where(band, _dot_nt(q, k_ref[pl.ds(start, SWA_KEYS), sl]), NEG)
        s_ctx = _dot_nt(q, ck_ref[:, sl])
        (p_loc, p_ctx), inv = _softmax_parts([s_loc, s_ctx], sink_ref[l, h])
        o_ref[:, h * HEAD_DIM:(h + 1) * HEAD_DIM] = (
            (_dot(p_loc, v_ref[pl.ds(start, SWA_KEYS), sl]) + _dot(p_ctx, cv_ref[:, sl])) * inv).astype(o_ref.dtype)


def _swa_dec(l, sink, sw, ck, cv, o):
    return pl.pallas_call(
        functools.partial(_swa_dec_kernel, l),
        grid=(DEC_BATCH, DEC_TILES_PER_BATCH),
        in_specs=[pl.BlockSpec(memory_space=pltpu.SMEM),
                  pl.BlockSpec((TM, SWA_OUT), lambda b, j: (_dec_row(b, j), 0)),
                  pl.BlockSpec((DEC_SEQ, LANES), lambda b, j: (N_CTX // DEC_SEQ + b, SWA_OUT // LANES)),
                  pl.BlockSpec((DEC_SEQ, LANES), lambda b, j: (N_CTX // DEC_SEQ + b, SWA_OUT // LANES + 1)),
                  pl.BlockSpec((None, None, PAST_LEN, LANES), lambda b, j: (b, l, 0, 0)),
                  pl.BlockSpec((None, None, PAST_LEN, LANES), lambda b, j: (b, l, 0, 0)),
                  pl.BlockSpec(memory_space=pl.ANY)],
        out_specs=pl.BlockSpec((TM, SWA_OUT), lambda b, j: (_dec_row(b, j), 1)),
        out_shape=jax.ShapeDtypeStruct((N_TOK, D_MODEL), BF16),
        input_output_aliases={6: 0},
        compiler_params=_cparams(("arbitrary", "arbitrary")),
        name="swa_dec",
    )(sink, sw, sw, sw, ck, cv, o)


def _dec_attn_kernel(l, sink_ref, mq_ref, ckv_ref, kpe_ref, cckv_ref, ckpe_ref, wkvb_ref,
                     swq_ref, swk_ref, swv_ref, cswk_ref, cswv_ref,
                     naq_ref, nak_ref, nav_ref, cnak_ref, cnav_ref, bias_ref, oin_ref, o_ref, kcat):
    cols = lambda first, width: o_ref.at[:, pl.ds(first, width)]
    _mla_dec_kernel(mq_ref, ckv_ref, kpe_ref, cckv_ref, ckpe_ref, wkvb_ref, oin_ref, cols(0, MLA_OUT), kcat)
    _swa_dec_kernel(l, sink_ref, swq_ref, swk_ref, swv_ref, cswk_ref, cswv_ref, oin_ref, cols(MLA_OUT, SWA_OUT))
    _na_dec_kernel(naq_ref, nak_ref, nav_ref, cnak_ref, cnav_ref, bias_ref, oin_ref,
                   cols(MLA_OUT + SWA_OUT, NA_OUT))


def _dec_attn(l, sink, mq, ckv, kpe, cckv, ckpe, w_kvb, sw, cswk, cswv, na, cnak, cnav, bias, o):
    pat = lambda j: jnp.where(j == 0, 0, jnp.where(j == DEC_TILES_PER_BATCH - 1, 2, 1))
    tile = lambda w, c: pl.BlockSpec((TM, w), lambda b, j: (_dec_row(b, j), c))
    seq = lambda w, c: pl.BlockSpec((DEC_SEQ, w), lambda b, j: (N_CTX // DEC_SEQ + b, c))
    past = lambda w: pl.BlockSpec((None, None, PAST_LEN, w), lambda b, j: (b, l, 0, 0))
    return pl.pallas_call(
        functools.partial(_dec_attn_kernel, l),
        grid=(DEC_BATCH, DEC_TILES_PER_BATCH),
        in_specs=[pl.BlockSpec(memory_space=pltpu.SMEM),
                  tile(MQ_W, 0), seq(MLA_KV_RANK, 0), seq(LANES, 0), past(MLA_KV_RANK), past(MLA_ROPE_DIM),
                  pl.BlockSpec((None, MLA_KV_RANK, MLA_HEADS * 2 * HEAD_DIM), lambda b, j: (l, 0, 0)),
                  tile(SWA_OUT, 0), seq(LANES, SWA_OUT // LANES), seq(LANES, SWA_OUT // LANES + 1),
                  past(LANES), past(LANES),
                  tile(NA_OUT, 0), seq(NA_OUT, 1), seq(NA_OUT, 2), past(NA_OUT), past(NA_OUT),
                  pl.BlockSpec((None, None, NA_HEADS, TM, NA_KEYS), lambda b, j: (l, pat(j), 0, 0, 0)),
                  pl.BlockSpec(memory_space=pl.ANY)],
        out_specs=tile(D_MODEL, 0),
        out_shape=jax.ShapeDtypeStruct((N_TOK, D_MODEL), BF16),
        scratch_shapes=[pltpu.VMEM((DEC_SEQ + PAST_LEN, 2 * LANES), BF16)],
        input_output_aliases={18: 0},
        compiler_params=_cparams(("arbitrary", "arbitrary")),
        name="dec_attn",
    )(sink, mq, ckv, kpe, cckv, ckpe, w_kvb, sw, sw, sw, cswk, cswv, na, na, na, cnak, cnav, bias, o)


def _route(sc, sel):
    rows = [sel[e:e + 1, :] for e in range(N_EXPERTS)]

    def beats(a, ia, b, ib):
        return (a > b) | ((a == b) & (ia < ib)) if ia < ib else (a > b)

    in_top = []
    gscore = []
    for g in range(N_EXPERT_GROUPS):
        mem = list(range(g * EXPERTS_PER_GROUP, (g + 1) * EXPERTS_PER_GROUP))
        acc = None
        for e in mem:
            rank = sum(beats(rows[o], o, rows[e], e).astype(jnp.int32) for o in mem if o != e)
            top = rank < 2
            in_top.append(top)
            term = jnp.where(top, rows[e], 0.0)
            acc = term if acc is None else acc + term
        gscore.append(acc)
    gates, chosen = [], []
    for g in range(N_EXPERT_GROUPS):
        lost = sum(beats(gscore[o], o, gscore[g], g).astype(jnp.int32) for o in range(N_EXPERT_GROUPS) if o != g)
        best = lost == 0
        for e in range(g * EXPERTS_PER_GROUP, (g + 1) * EXPERTS_PER_GROUP):
            pick = best & in_top[e]
            chosen.append(pick.astype(F32))
            gates.append(jnp.where(pick, sc[e:e + 1, :], 0.0))
    gate = jnp.concatenate(gates, axis=0)
    return gate / gate.sum(axis=0, keepdims=True), jnp.concatenate(chosen, axis=0)


def _post_kernel(x_ref, o_ref, mod_ref, wout_ref, g_ref, wr_ref, br_ref,
                 y_ref, h2_ref, rt_ref, rtm_ref, cnt_ref):
    attn = jnp.dot(o_ref[...].astype(BF16), wout_ref[...], preferred_element_type=F32)
    y = x_ref[...] + mod_ref[2:3, :] * attn
    y_ref[...] = y
    h2 = _rms(y, g_ref[...]) * (1.0 + mod_ref[4:5, :]) + mod_ref[3:4, :]
    h_hi = h2.astype(BF16)
    h_lo = (h2 - h_hi.astype(F32)).astype(BF16)
    w = wr_ref[...]
    w_hi = w.astype(BF16)
    w_lo = (w - w_hi.astype(F32)).astype(BF16)
    logits = (jnp.dot(h_hi, w_hi, preferred_element_type=F32) + jnp.dot(h_lo, w_hi, preferred_element_type=F32)
              + jnp.dot(h_hi, w_lo, preferred_element_type=F32))
    logits = logits.T[:N_EXPERTS, :]
    sc = 1.0 / (1.0 + jnp.exp(-logits))
    gate, chosen = _route(sc, sc + br_ref[...])
    h2_ref[...] = h_hi
    rt = jnp.concatenate([gate, chosen], axis=0)
    rt_ref[...] = rt
    rtm_ref[...] = jnp.concatenate([rt, jnp.zeros((LANES - 2 * N_EXPERTS, TM), F32)], axis=0).T
    cnt_ref[...] = jnp.broadcast_to(jnp.sum(chosen, axis=1, keepdims=True), (N_EXPERTS, LANES))


def _post(l, x, o, mods, w_out_bf, g_ffn, w_router_p, b_router):
    tile = lambda w: pl.BlockSpec((TM, w), lambda i: (i, 0))
    return pl.pallas_call(
        _post_kernel,
        grid=(N_TILES,),
        in_specs=[tile(D_MODEL), tile(D_MODEL),
                  pl.BlockSpec((None, None, N_MOD, D_MODEL), lambda i: (l, _cond_row(i), 0, 0)),
                  pl.BlockSpec((None, D_MODEL, D_MODEL), lambda i: (l, 0, 0)),
                  pl.BlockSpec((None, 1, D_MODEL), lambda i: (l, 0, 0)),
                  pl.BlockSpec((D_MODEL, LANES), lambda i: (0, 0)),
                  pl.BlockSpec((N_EXPERTS, 1), lambda i: (0, 0))],
        out_specs=[tile(D_MODEL), tile(D_MODEL), pl.BlockSpec((2 * N_EXPERTS, TM), lambda i: (0, i)),
                   tile(LANES), pl.BlockSpec((None, N_EXPERTS, LANES), lambda i: (i, 0, 0))],
        out_shape=[jax.ShapeDtypeStruct((N_TOK, D_MODEL), F32), jax.ShapeDtypeStruct((N_TOK, D_MODEL), BF16),
                   jax.ShapeDtypeStruct((2 * N_EXPERTS, N_TOK), F32),
                   jax.ShapeDtypeStruct((N_TOK, LANES), F32),
                   jax.ShapeDtypeStruct((N_TILES, N_EXPERTS, LANES), F32)],
        compiler_params=_cparams(("arbitrary",)),
        name="post_attn",
    )(x, o, mods, w_out_bf, g_ffn.reshape(DEPTH, 1, D_MODEL), w_router_p, b_router.reshape(N_EXPERTS, 1))


def _shr(x, bits):
    return lax.shift_right_logical(x, jnp.int32(bits))


TR_BITS = TR.bit_length() - 1
ALIGN_BITS = RUN_ALIGN.bit_length() - 1


def _plan_rows(cnt_ref, off_ref, npass_ref, seg_ref):
    def per_expert(e, row0):
        def per_tile(bb, r):
            off_ref[bb * N_EXPERTS + e] = r
            return r + (_shr(cnt_ref[bb * N_EXPERTS + e] + (RUN_ALIGN - 1), ALIGN_BITS) << ALIGN_BITS)

        rows_end = lax.fori_loop(0, N_TILES, per_tile, row0)
        n = _shr(rows_end - row0 + (TR - 1), TR_BITS)
        seg_ref[e] = _shr(row0, TR_BITS)
        seg_ref[N_EXPERTS + e] = n
        seg_ref[2 * N_EXPERTS + e] = rows_end
        seg_ref[3 * N_EXPERTS + e] = row0 + (n << TR_BITS)
        return row0 + (n << TR_BITS)

    end_row = lax.fori_loop(0, N_EXPERTS, per_expert, jnp.int32(0))
    seg_ref[4 * N_EXPERTS] = _shr(end_row, TR_BITS)

    def longest(bb, carry):
        m = lax.fori_loop(0, N_EXPERTS, lambda e, m: jnp.maximum(m, cnt_ref[bb * N_EXPERTS + e]), jnp.int32(0))
        npass_ref[bb] = sum((m > k * WIN).astype(jnp.int32) for k in range(-(-TM // WIN)))
        return carry

    lax.fori_loop(0, N_TILES, longest, 0)


def _dispatch_kernel(cnt_ref, h_ref, rt_ref, et_ref, xs_in, xs_hbm, off_ref, npass_ref, seg_ref, zbuf, sem):
    del xs_in
    b = pl.program_id(0)
    slot = b % 2

    @pl.when(b == 0)
    def _():
        _plan_rows(cnt_ref, off_ref, npass_ref, seg_ref)

    key = jnp.dot(et_ref[...], _run_keys(rt_ref[...], 1).astype(BF16), preferred_element_type=F32)
    slot_j = _window_slot((N_EXPERTS * WIN, TM), 0)

    def run_copies(bb, sl, p, act):
        def one_expert(e, queue):
            left = cnt_ref[bb * N_EXPERTS + e] - p * WIN
            row = off_ref[bb * N_EXPERTS + e] + p * WIN
            whole = row + WIN <= seg_ref[2 * N_EXPERTS + e]

            @pl.when(whole & (left > 0))
            def _():
                src = pl.multiple_of(e * WIN, RUN_ALIGN)
                act(pltpu.make_async_copy(zbuf.at[sl, pl.ds(src, WIN)],
                                          xs_hbm.at[pl.ds(pl.multiple_of(row, RUN_ALIGN), WIN)], sem), queue)

            def piece(k, c):
                src = pl.multiple_of(e * WIN + k * RUN_ALIGN, RUN_ALIGN)
                dst = pl.multiple_of(row + k * RUN_ALIGN, RUN_ALIGN)
                act(pltpu.make_async_copy(zbuf.at[sl, pl.ds(src, RUN_ALIGN)],
                                          xs_hbm.at[pl.ds(dst, RUN_ALIGN)], sem), queue)
                return c

            pieces = jnp.minimum(_shr(jnp.maximum(left, 0) + (RUN_ALIGN - 1), ALIGN_BITS), WIN // RUN_ALIGN)
            lax.fori_loop(0, jnp.where(whole, 0, pieces), piece, 0)

        def expert_pair(e2, carry):
            one_expert(2 * e2, 0)
            one_expert(2 * e2 + 1, 1)
            return carry

        lax.fori_loop(0, N_EXPERTS // 2, expert_pair, 0)

    def fill_and_send(p):
        pick = jnp.where(key == slot_j + jnp.asarray(p * WIN, F32), 1.0, 0.0).astype(BF16)
        zbuf[slot] = jnp.dot(pick, h_ref[...], preferred_element_type=F32).astype(BF16)
        run_copies(b, slot, p, lambda cp, queue: cp.start(priority=queue))

    @pl.when(b > 0)
    def _():
        run_copies(b - 1, 1 - slot, npass_ref[jnp.maximum(b - 1, 0)] - 1, lambda cp, queue: cp.wait())

    fill_and_send(0)

    def more(p, carry):
        run_copies(b, slot, p - 1, lambda cp, queue: cp.wait())
        fill_and_send(p)
        return carry

    lax.fori_loop(1, npass_ref[b], more, 0)

    @pl.when(b == N_TILES - 1)
    def _():
        run_copies(b, slot, npass_ref[b] - 1, lambda cp, queue: cp.wait())


def _dispatch(cnt, h2, rt, et, xs_buf):
    smem = pl.BlockSpec(memory_space=pltpu.SMEM)
    return pl.pallas_call(
        _dispatch_kernel,
        grid_spec=pltpu.PrefetchScalarGridSpec(
            num_scalar_prefetch=1,
            grid=(N_TILES,),
            in_specs=[pl.BlockSpec((TM, D_MODEL), lambda i, *_: (i, 0)),
                      pl.BlockSpec((2 * N_EXPERTS, TM), lambda i, *_: (0, i)),
                      pl.BlockSpec((N_EXPERTS * WIN, 2 * N_EXPERTS), lambda i, *_: (0, 0)),
                      pl.BlockSpec(memory_space=pl.ANY)],
            out_specs=[pl.BlockSpec(memory_space=pl.ANY), smem, smem, smem],
            scratch_shapes=[pltpu.VMEM((2, N_EXPERTS * WIN, D_MODEL), BF16), pltpu.SemaphoreType.DMA(())]),
        out_shape=[jax.ShapeDtypeStruct((NT * TR, D_MODEL), BF16), jax.ShapeDtypeStruct((N_RUNS,), jnp.int32),
                   jax.ShapeDtypeStruct((N_TILES,), jnp.int32),
                   jax.ShapeDtypeStruct((4 * N_EXPERTS + 1,), jnp.int32)],
        input_output_aliases={4: 0},
        compiler_params=_cparams(("arbitrary",)),
        name="dispatch",
    )(cnt, h2, rt, et, xs_buf)


CAST_ROWS = 128
TILE_BUFS = 2


def _cast_rows(src_ref, dst_ref, n):
    def body(c, carry):
        rows = pl.ds(pl.multiple_of(c * CAST_ROWS, CAST_ROWS), CAST_ROWS)
        dst_ref[rows, :] = src_ref[rows, :].astype(BF16)
        return carry

    lax.fori_loop(0, n, body, 0)


def _experts_kernel(seg_ref, xs_hbm, wg_ref, wu_ref, wd_ref, ys_in, ys_hbm, wgb, wub, wdb, xbuf, ybuf, semx, semy):
    del ys_in
    e = pl.program_id(0)
    t0 = seg_ref[e]
    n = seg_ref[N_EXPERTS + e]
    _cast_rows(wg_ref, wgb, D_MODEL // CAST_ROWS)
    _cast_rows(wu_ref, wub, D_MODEL // CAST_ROWS)
    _cast_rows(wd_ref, wdb, D_EXPERT // CAST_ROWS)

    def rows(k):
        return pl.ds(pl.multiple_of((t0 + k) * TR, TR), TR)

    def fetch(k, s):
        return pltpu.make_async_copy(xs_hbm.at[rows(k)], xbuf.at[s], semx.at[s])

    def put(k, s):
        return pltpu.make_async_copy(ybuf.at[s], ys_hbm.at[rows(k)], semy.at[s])

    for j in range(TILE_BUFS - 1):
        @pl.when(j < n)
        def _():
            fetch(j, j).start()

    def tile(k, carry):
        s = k % TILE_BUFS
        ahead = k + (TILE_BUFS - 1)

        @pl.when(ahead < n)
        def _():
            fetch(ahead, ahead % TILE_BUFS).start()

        fetch(k, s).wait()

        @pl.when(k >= TILE_BUFS)
        def _():
            put(k - TILE_BUFS, s).wait()

        x = xbuf[s]
        hg = jnp.dot(x, wgb[...], preferred_element_type=F32)
        hu = jnp.dot(x, wub[...], preferred_element_type=F32)
        a = hg * (1.0 / (1.0 + jnp.exp(-hg))) * hu
        ybuf[s] = jnp.dot(a.astype(BF16), wdb[...], preferred_element_type=F32).astype(BF16)
        put(k, s).start(priority=1)
        return carry

    lax.fori_loop(0, n, tile, 0)

    for j in range(TILE_BUFS):
        last = n - TILE_BUFS + j

        @pl.when(last >= 0)
        def _():
            put(last, last % TILE_BUFS).wait()


def _experts(l, seg, xs, w_gate, w_up, w_down, ys_buf):
    wspec = lambda a, b: pl.BlockSpec((None, None, a, b), lambda e, seg: (l, e, 0, 0))
    tile_buf = pltpu.VMEM((TILE_BUFS, TR, D_MODEL), BF16)
    return pl.pallas_call(
        _experts_kernel,
        grid_spec=pltpu.PrefetchScalarGridSpec(
            num_scalar_prefetch=1,
            grid=(N_EXPERTS,),
            in_specs=[pl.BlockSpec(memory_space=pl.ANY),
                      wspec(D_MODEL, D_EXPERT), wspec(D_MODEL, D_EXPERT), wspec(D_EXPERT, D_MODEL),
                      pl.BlockSpec(memory_space=pl.ANY)],
            out_specs=pl.BlockSpec(memory_space=pl.ANY),
            scratch_shapes=[pltpu.VMEM((D_MODEL, D_EXPERT), BF16), pltpu.VMEM((D_MODEL, D_EXPERT), BF16),
                            pltpu.VMEM((D_EXPERT, D_MODEL), BF16), tile_buf, tile_buf,
                            pltpu.SemaphoreType.DMA((TILE_BUFS,)), pltpu.SemaphoreType.DMA((TILE_BUFS,))]),
        out_shape=jax.ShapeDtypeStruct((NT * TR, D_MODEL), BF16),
        input_output_aliases={5: 0},
        compiler_params=_cparams(("arbitrary",)),
        name="experts",
    )(seg, xs, w_gate, w_up, w_down, ys_buf)


def _final_kernel(off_ref, npass_ref, y_ref, rtm_ref, ek_ref, eg_ref, ys_hbm, mod_ref, g_ref, op_ref, os_ref,
                  wbuf, sem):
    i = pl.program_id(0)
    moe = _combined_moe(off_ref, npass_ref, rtm_ref, ek_ref, eg_ref, ys_hbm, wbuf, sem)
    out = _rms(y_ref[...] + mod_ref[5:6, :] * moe, g_ref[...])

    @pl.when(i < CTX_TILES)
    def _():
        op_ref[...] = out

    @pl.when(i >= CTX_TILES)
    def _():
        os_ref[...] = out


def _final(off, npass, y, rtm, ek, eg, ys, mods, g_final):
    return pl.pallas_call(
        _final_kernel,
        grid_spec=pltpu.PrefetchScalarGridSpec(
            num_scalar_prefetch=2,
            grid=(N_TILES,),
            in_specs=_combine_specs() + [
                pl.BlockSpec((None, None, N_MOD, D_MODEL), lambda i, *_: (DEPTH - 1, _cond_row(i), 0, 0)),
                pl.BlockSpec((1, D_MODEL), lambda i, *_: (0, 0))],
            out_specs=[pl.BlockSpec((TM, D_MODEL), lambda i, *_: (jnp.minimum(i, CTX_TILES - 1), 0)),
                       pl.BlockSpec((TM, D_MODEL), lambda i, *_: (jnp.maximum(i - CTX_TILES, 0), 0))],
            scratch_shapes=_combine_scratch()),
        out_shape=[jax.ShapeDtypeStruct((N_CTX, D_MODEL), F32), jax.ShapeDtypeStruct((N_DEC, D_MODEL), F32)],
        compiler_params=_cparams(("arbitrary",)),
        name="final_norm",
    )(off, npass, y, rtm, ek, eg, ys, mods, g_final.reshape(1, D_MODEL))


def _prep_kernel(win_ref, wout_ref, wi_ref, wo_ref):
    w = win_ref[...]
    split = KPE_OFF + MLA_ROPE_DIM
    wi_ref[...] = jnp.concatenate([w[:, :split], jnp.zeros((CAST_ROWS, SW_OFF - split), F32), w[:, split:]],
                                  axis=1).astype(BF16)
    wo_ref[...] = wout_ref[...].astype(BF16)


def _prep_weights(w_in, w_out):
    n = D_MODEL // CAST_ROWS
    shift = NA_OUT // CAST_ROWS
    return pl.pallas_call(
        _prep_kernel,
        grid=(DEPTH, n),
        in_specs=[pl.BlockSpec((None, CAST_ROWS, IN_WIDTH), lambda l, j: (l, j, 0)),
                  pl.BlockSpec((None, CAST_ROWS, D_MODEL), lambda l, j: (l, (j + shift) % n, 0))],
        out_specs=[pl.BlockSpec((None, CAST_ROWS, IN_PAD), lambda l, j: (l, j, 0)),
                   pl.BlockSpec((None, CAST_ROWS, D_MODEL), lambda l, j: (l, j, 0))],
        out_shape=[jax.ShapeDtypeStruct((DEPTH, D_MODEL, IN_PAD), BF16),
                   jax.ShapeDtypeStruct((DEPTH, D_MODEL, D_MODEL), BF16)],
        compiler_params=_cparams(("arbitrary", "arbitrary")),
        name="prep_weights",
    )(w_in, w_out)


def _rope_tables(rot_dim):
    t = jnp.arange(DEC_SEQ, dtype=jnp.int32)
    row = (t // GRID_W).astype(F32)
    col = (t % GRID_W).astype(F32)
    per_axis = rot_dim // 2
    inv = ROPE_BASE ** (-jnp.arange(0, per_axis, 2, dtype=F32) / per_axis)
    ang = jnp.concatenate([row[:, None] * inv, col[:, None] * inv], axis=-1)
    cos, sin = jnp.cos(ang), jnp.sin(ang)
    zero = jnp.zeros_like(sin)
    rep = LANES // rot_dim
    tabs = [jnp.concatenate([cos, cos], -1), jnp.concatenate([zero, sin], -1), jnp.concatenate([-sin, zero], -1)]
    return jnp.stack([jnp.tile(a, (1, rep)) for a in tabs])


def kernel(x_prompt, x_sample, c, cache_na_k, cache_na_v, cache_mla_ckv, cache_mla_kpe, cache_swa_k, cache_swa_v,
           c_ctx, w_ada, b_ada, g_attn, w_in, g_mla_q, w_mla_qb, g_mla_kv, w_mla_kvb, na_rpb, swa_sink, w_out,
           g_ffn, w_router, b_router, w_gate, w_up, w_down, g_final):
    cond = jnp.concatenate([c_ctx[None], c, jnp.zeros((COND_ROWS - 1 - DEC_BATCH, D_MODEL), F32)], axis=0)
    mods = _ada(cond, w_ada, b_ada).reshape(DEPTH, COND_ROWS, N_MOD, D_MODEL)
    bias = _na_bias(na_rpb)
    t_mla = _rope_tables(MLA_ROPE_DIM)
    t_swa = _rope_tables(HEAD_DIM)

    w_in_p, w_out_p = _prep_weights(w_in, w_out)
    wq = w_mla_qb.reshape(DEPTH, MLA_Q_RANK, MLA_HEADS, MLA_QK_DIM)
    w_rope = jnp.pad(wq[..., MLA_NOPE_DIM:], ((0, 0), (0, 0), (0, 0), (0, LANES - MLA_ROPE_DIM)))
    w_qb_p = jnp.concatenate([wq[..., :MLA_NOPE_DIM].reshape(DEPTH, MLA_Q_RANK, MQ_NOPE),
                              w_rope.reshape(DEPTH, MLA_Q_RANK, MLA_HEADS * LANES)], axis=-1).astype(BF16)
    w_router_p = jnp.pad(w_router, ((0, 0), (0, LANES - N_EXPERTS)))
    et, ek, eg = _spread_consts()

    c_na_k = cache_na_k.reshape(DEC_BATCH, DEPTH, PAST_LEN, NA_OUT)
    c_na_v = cache_na_v.reshape(DEC_BATCH, DEPTH, PAST_LEN, NA_OUT)
    c_sw_k = cache_swa_k.reshape(DEC_BATCH, DEPTH, PAST_LEN, LANES)
    c_sw_v = cache_swa_v.reshape(DEC_BATCH, DEPTH, PAST_LEN, LANES)

    y = ys = rtm = off = npass = None
    caches = [jnp.zeros((BATCH, DEPTH, SEQ, w), F32) for w in CACHE_WIDTHS]
    o = jnp.zeros((N_TOK, D_MODEL), BF16)
    xs = jnp.zeros((NT * TR, D_MODEL), BF16)
    ys = jnp.zeros((NT * TR, D_MODEL), BF16)
    for l in range(DEPTH):
        if l == 0:
            srcs = [x_prompt.reshape(N_CTX, D_MODEL), x_sample.reshape(N_DEC, D_MODEL)]
        else:
            srcs = [off, npass, y, rtm, ek, eg, ys]
        x, na, mq, ckv, kpe, sw, *caches = _pre(l == 0, l, srcs, caches, mods, g_attn, w_in_p, g_mla_q, w_qb_p,
                                                g_mla_kv, t_mla, t_swa)
        o = _ctx_attn(l, swa_sink, na, mq, ckv, kpe, sw, w_mla_kvb, o)
        o = _dec_attn(l, swa_sink, mq, ckv, kpe, cache_mla_ckv, cache_mla_kpe, w_mla_kvb, sw, c_sw_k, c_sw_v,
                      na, c_na_k, c_na_v, bias, o)
        y, h2, rt, rtm, cnt = _post(l, x, o, mods, w_out_p, g_ffn, w_router_p, b_router)
        xs, off, npass, seg = _dispatch(cnt[:, :, 0].astype(jnp.int32).reshape(-1), h2, rt, et, xs)
        ys = _experts(l, seg, xs, w_gate, w_up, w_down, ys)
    y_prompt, y_sample = _final(off, npass, y, rtm, ek, eg, ys, mods, g_final)

    heads = lambda a, n: a.reshape(BATCH, DEPTH, SEQ, n, HEAD_DIM)
    return (y_prompt.reshape(BATCH, SEQ, D_MODEL), y_sample.reshape(DEC_BATCH, DEC_SEQ, D_MODEL),
            heads(caches[0], NA_HEADS), heads(caches[1], NA_HEADS), caches[2], caches[3],
            heads(caches[4], SWA_KV_HEADS), heads(caches[5], SWA_KV_HEADS))
```
